```python
import math
import jax, jax.numpy as jnp
from jax import lax
import numpy as np

D_MODEL = 1024
BATCH = 4
SEQ = 8192
DEPTH = 4

N_MIXERS = 3
N_NSA = (DEPTH + 2) // 3
N_MLSTM = (DEPTH + 1) // 3
N_RWKV = DEPTH // 3
NORM_EPS = 1e-6
ROPE_THETA = 500000.0

NSA_HEAD_DIM = 64
NSA_HEADS = D_MODEL // NSA_HEAD_DIM
NSA_KV_GROUPS = 4
NSA_ROT_DIM = NSA_HEAD_DIM // 4
CMP_BLOCK = 32
CMP_STRIDE = 16
CMP_HIDDEN = 256
SEL_BLOCK = 64
SEL_TOPK = 16
WINDOW = 512
NSA_Q_BLOCK = 64
NSA_IN = NSA_HEADS * NSA_HEAD_DIM + 6 * NSA_KV_GROUPS * NSA_HEAD_DIM + 3 * NSA_HEADS
FORCE_SCORE = 1e6

MLSTM_HEADS = 8
MLSTM_QK_DIM = D_MODEL // (2 * MLSTM_HEADS)
MLSTM_V_DIM = D_MODEL // MLSTM_HEADS
MLSTM_CHUNK = 64
MLSTM_CONV = 4
MLSTM_IN = 2 * MLSTM_HEADS * MLSTM_QK_DIM + 2 * MLSTM_HEADS * MLSTM_V_DIM + 2 * MLSTM_HEADS

RWKV_HEAD_DIM = 64
RWKV_HEADS = D_MODEL // RWKV_HEAD_DIM
RWKV_DECAY_LORA = 64
RWKV_AAA_LORA = 64
RWKV_GATE_LORA = 128
RWKV_GN_EPS = 64e-5

FFN_DIM = 2816
FFN_CONV = 3

kernel_name = 'hybrid_nsa_mlstm_rwkv7_convffn'


def rmsnorm(x, g):
    xf = x.astype(jnp.float32)
    y = xf * lax.rsqrt(jnp.mean(xf * xf, axis=-1, keepdims=True) + NORM_EPS)
    return (y * g.astype(jnp.float32)).astype(x.dtype)


def causal_dwconv(x, w, b):
    k, c = w.shape
    y = lax.conv_general_dilated(x, w[:, None, :].astype(x.dtype), (1,), [(k - 1, 0)],
                                 dimension_numbers=('NWC', 'WIO', 'NWC'), feature_group_count=c)
    return y + b.astype(x.dtype)


def rope_tables(seq_len):
    half = NSA_ROT_DIM // 2
    inv_freq = ROPE_THETA ** (-jnp.arange(half, dtype=jnp.float32) / half)
    ang = jnp.arange(seq_len, dtype=jnp.float32)[:, None] * inv_freq[None, :]
    return jnp.cos(ang), jnp.sin(ang)


def partial_rope(x, cos, sin):
    half = NSA_ROT_DIM // 2
    x1 = x[..., :half].astype(jnp.float32)
    x2 = x[..., half:NSA_ROT_DIM].astype(jnp.float32)
    c = cos[None, :, None, :]
    s = sin[None, :, None, :]
    rot = jnp.concatenate([x1 * c - x2 * s, x1 * s + x2 * c], axis=-1).astype(x.dtype)
    return jnp.concatenate([rot, x[..., NSA_ROT_DIM:]], axis=-1)


def masked_softmax(s, mask):
    s = jnp.where(mask, s.astype(jnp.float32), -jnp.inf)
    m = jnp.max(s, axis=-1, keepdims=True)
    m = jnp.where(jnp.isfinite(m), m, 0.0)
    e = jnp.exp(s - m)
    return e / jnp.maximum(jnp.sum(e, axis=-1, keepdims=True), 1e-30)


def nsa_mixer(h, cos, sin, w_in, pe_k, pe_v, cmp_k_w1, cmp_k_w2, cmp_v_w1, cmp_v_w2, b_gate, w_out):
    B, T, _ = h.shape
    H, G, dh = NSA_HEADS, NSA_KV_GROUPS, NSA_HEAD_DIM
    R = H // G
    proj = h @ w_in
    cuts = np.cumsum([H * dh] + [G * dh] * 6).tolist()
    q, k_cmp, v_cmp, k_sel, v_sel, k_win, v_win, gate = jnp.split(proj, cuts, axis=-1)
    gate = jax.nn.sigmoid((gate + b_gate).astype(jnp.float32)).reshape(B, T, H, 3)
    q = q.reshape(B, T, H, dh)
    kv = lambda z: z.reshape(B, T, G, dh)
    k_cmp, v_cmp, k_sel, v_sel, k_win, v_win = kv(k_cmp), kv(v_cmp), kv(k_sel), kv(v_sel), kv(k_win), kv(v_win)
    q_rot = partial_rope(q, cos, sin)
    k_sel = partial_rope(k_sel, cos, sin)
    k_win = partial_rope(k_win, cos, sin)

    heads_first = lambda z: z.reshape(B, T, G, R, -1).transpose(0, 2, 3, 1, 4)
    q_p, q_r, gates = heads_first(q), heads_first(q_rot), heads_first(gate)

    ratio = CMP_BLOCK // CMP_STRIDE
    n_cmp = T // CMP_STRIDE - ratio + 1

    def compress(z, pe, w1, w2):
        chunks = z.reshape(B, T // CMP_STRIDE, CMP_STRIDE, G, dh)
        blocks = jnp.concatenate([chunks[:, r:r + n_cmp] for r in range(ratio)], axis=2)
        blocks = blocks + pe[None, None, :, None, :]
        flat = blocks.transpose(0, 1, 3, 2, 4).reshape(B, n_cmp, G, CMP_BLOCK * dh)
        return (jax.nn.gelu(flat @ w1) @ w2).transpose(0, 2, 1, 3)

    kc = compress(k_cmp, pe_k, cmp_k_w1, cmp_k_w2)
    vc = compress(v_cmp, pe_v, cmp_v_w1, cmp_v_w2)
    cmp_start = jnp.arange(n_cmp) * CMP_STRIDE
    cmp_end = cmp_start + CMP_BLOCK - 1

    n_sel = T // SEL_BLOCK
    k_top = min(SEL_TOPK, n_sel)
    sel_blocks = lambda z: z.transpose(0, 2, 1, 3).reshape(B, G, n_sel, SEL_BLOCK, dh)
    ks_blk, vs_blk = sel_blocks(k_sel), sel_blocks(v_sel)
    blk_start = jnp.arange(n_sel) * SEL_BLOCK
    overlap = ((cmp_end[:, None] >= blk_start[None, :]) &
               (cmp_start[:, None] <= blk_start[None, :] + SEL_BLOCK - 1)).astype(jnp.float32)
    b_idx = jnp.arange(B)[:, None, None, None]
    g_idx = jnp.arange(G)[None, :, None, None]

    pad_win = lambda z: jnp.pad(z.transpose(0, 2, 1, 3), ((0, 0), (0, 0), (WINDOW, 0), (0, 0)))
    kw_pad, vw_pad = pad_win(k_win), pad_win(v_win)

    scale = dh ** -0.5
    QB = NSA_Q_BLOCK

    def block(qi):
        q0 = qi * QB
        t = q0 + jnp.arange(QB)
        qp = lax.dynamic_slice_in_dim(q_p, q0, QB, axis=3)
        qr = lax.dynamic_slice_in_dim(q_r, q0, QB, axis=3)
        gt = lax.dynamic_slice_in_dim(gates, q0, QB, axis=3)
        s_c = jnp.einsum('bgrqd,bgcd->bgrqc', qp, kc) * scale
        p_c = masked_softmax(s_c, cmp_end[None, :] <= t[:, None])
        o_c = jnp.einsum('bgrqc,bgcd->bgrqd', p_c, vc)
        imp = jnp.einsum('bgrqc,cs->bgqs', p_c, overlap)
        tb = (t // SEL_BLOCK)[:, None]
        sid = jnp.arange(n_sel)[None, :]
        forced = (sid == 0) | (sid == tb) | (sid == tb - 1)
        imp = jnp.where(forced, FORCE_SCORE, jnp.where(sid <= tb, imp, -1.0))
        _, idx = lax.top_k(imp, k_top)
        ks = ks_blk[b_idx, g_idx, idx]
        vs = vs_blk[b_idx, g_idx, idx]
        s_s = jnp.einsum('bgrqd,bgqnkd->bgrqnk', qr, ks) * scale
        pos = idx[..., None] * SEL_BLOCK + jnp.arange(SEL_BLOCK)
        vis = (pos <= t[:, None, None])[:, :, None]
        p_s = masked_softmax(s_s.reshape(B, G, R, QB, -1), vis.reshape(B, G, 1, QB, -1))
        o_s = jnp.einsum('bgrqm,bgqmd->bgrqd', p_s, vs.reshape(B, G, QB, -1, dh))
        kw = lax.dynamic_slice_in_dim(kw_pad, q0, WINDOW + QB, axis=2)
        vw = lax.dynamic_slice_in_dim(vw_pad, q0, WINDOW + QB, axis=2)
        pw = q0 - WINDOW + jnp.arange(WINDOW + QB)
        vis_w = (pw[None, :] <= t[:, None]) & (pw[None, :] > t[:, None] - WINDOW) & (pw[None, :] >= 0)
        s_w = jnp.einsum('bgrqd,bgkd->bgrqk', qr, kw) * scale
        o_w = jnp.einsum('bgrqk,bgkd->bgrqd', masked_softmax(s_w, vis_w), vw)
        return gt[..., 0:1] * o_c + gt[..., 1:2] * o_s + gt[..., 2:3] * o_w

    o = lax.map(block, jnp.arange(T // QB))
    o = o.transpose(1, 0, 4, 2, 3, 5).reshape(B, T, H * dh).astype(h.dtype)
    return o @ w_out


def mlstm_mixer(h, w_in, conv_w, conv_b, b_gates, norm_g, w_out):
    B, T, _ = h.shape
    H, dk, dv, L = MLSTM_HEADS, MLSTM_QK_DIM, MLSTM_V_DIM, MLSTM_CHUNK
    nc = T // L
    f32 = jnp.float32
    proj = h @ w_in
    cuts = np.cumsum([2 * H * dk, H * dv, H, H]).tolist()
    qk, v, i_pre, f_pre, o_pre = jnp.split(proj, cuts, axis=-1)
    qk = jax.nn.silu(causal_dwconv(qk, conv_w, conv_b))
    q, k = jnp.split(qk, 2, axis=-1)
    chunk = lambda z: z.reshape(B, nc, L, H, -1).transpose(0, 3, 1, 2, 4).astype(f32)
    q, k, v = chunk(q), chunk(k) * (dk ** -0.5), chunk(v)
    gate_chunk = lambda z: z.astype(f32).reshape(B, nc, L, H).transpose(0, 3, 1, 2)
    log_i = gate_chunk(i_pre + b_gates[:H])
    log_f = jax.nn.log_sigmoid(gate_chunk(f_pre + b_gates[H:]))
    b = jnp.cumsum(log_f, axis=-1)
    b_end = b[..., -1]
    g_end = b_end[..., None] - b + log_i
    g_max = jnp.max(g_end, axis=-1)
    w_end = jnp.exp(g_end - g_max[..., None])
    c_loc = jnp.einsum('bhcs,bhcsk,bhcsv->bhckv', w_end, k, v)
    n_loc = jnp.einsum('bhcs,bhcsk->bhck', w_end, k)

    def step(carry, xs):
        c_st, n_st, m_st = carry
        bl, gm, cl, nl = xs
        m_new = jnp.maximum(bl + m_st, gm)
        a = jnp.exp(bl + m_st - m_new)
        s = jnp.exp(gm - m_new)
        c_new = a[..., None, None] * c_st + s[..., None, None] * cl
        n_new = a[..., None] * n_st + s[..., None] * nl
        return (c_new, n_new, m_new), (c_st, n_st, m_st)

    init = (jnp.zeros((B, H, dk, dv), f32), jnp.zeros((B, H, dk), f32), jnp.zeros((B, H), f32))
    front = lambda z: jnp.moveaxis(z, 2, 0)
    _, (c_prev, n_prev, m_prev) = lax.scan(step, init, (front(b_end), front(g_max), front(c_loc), front(n_loc)))
    c_prev = jnp.moveaxis(c_prev, 0, 2)
    n_prev = jnp.moveaxis(n_prev, 0, 2)
    m_prev = jnp.moveaxis(m_prev, 0, 2)
    causal = jnp.tril(jnp.ones((L, L), dtype=bool))
    d = jnp.where(causal, b[..., :, None] - b[..., None, :] + log_i[..., None, :], -jnp.inf)
    m_inter = b + m_prev[..., None]
    m_t = jnp.maximum(m_inter, jnp.max(d, axis=-1))
    att = jnp.exp(d - m_t[..., None]) * jnp.einsum('bhctk,bhcsk->bhcts', q, k)
    inter = jnp.exp(m_inter - m_t)
    num = jnp.einsum('bhcts,bhcsv->bhctv', att, v) + inter[..., None] * jnp.einsum('bhctk,bhckv->bhctv', q, c_prev)
    den = jnp.sum(att, axis=-1) + inter * jnp.einsum('bhctk,bhck->bhct', q, n_prev)
    h_t = num / jnp.maximum(jnp.abs(den), jnp.exp(-m_t))[..., None]
    h_t = h_t.transpose(0, 2, 3, 1, 4).reshape(B, T, H, dv)
    h_t = h_t * lax.rsqrt(jnp.mean(h_t * h_t, axis=-1, keepdims=True) + NORM_EPS) * norm_g.reshape(H, dv)
    out = h_t.reshape(B, T, H * dv) * jax.nn.sigmoid(o_pre.astype(f32))
    return out.astype(h.dtype) @ w_out


def rwkv7_mixer(h, mu, w_r, w_k, w_v, w_o, w0, w_w1, w_w2, a0, a_w1, a_w2, g_w1, g_w2,
                k_k, k_a, r_k, ln_w, ln_b):
    B, T, D = h.shape
    H, N = RWKV_HEADS, RWKV_HEAD_DIM
    f32 = jnp.float32
    xx = jnp.pad(h, ((0, 0), (1, 0), (0, 0)))[:, :-1] - h
    xr, xw, xk, xv, xa, xg = (h + xx * mu[j] for j in range(6))
    r = (xr @ w_r).astype(f32)
    k = (xk @ w_k).astype(f32)
    v = (xv @ w_v).astype(f32)
    w_log = -jax.nn.softplus(-(w0 + jnp.tanh(xw @ w_w1) @ w_w2).astype(f32)) - 0.5
    decay = jnp.exp(-jnp.exp(w_log))
    a = jax.nn.sigmoid((a0 + (xa @ a_w1) @ a_w2).astype(f32))
    g = jax.nn.sigmoid(xg @ g_w1) @ g_w2
    heads = lambda z: z.reshape(B, T, H, N)
    kk = heads(k * k_k)
    kk = kk / jnp.maximum(jnp.sqrt(jnp.sum(kk * kk, axis=-1, keepdims=True)), 1e-12)
    k = k * (1.0 + (a - 1.0) * k_a)
    r, decay, k, v, a = heads(r), heads(decay), heads(k), heads(v), heads(a)

    def step(state, inp):
        r_t, w_t, k_t, v_t, kk_t, a_t = inp
        removed = jnp.einsum('bhij,bhj->bhi', state, kk_t)
        state = (state * w_t[:, :, None, :] - removed[..., None] * (kk_t * a_t)[:, :, None, :]
                 + v_t[..., None] * k_t[:, :, None, :])
        return state, jnp.einsum('bhij,bhj->bhi', state, r_t)

    tm = lambda z: jnp.moveaxis(z, 1, 0)
    _, y = lax.scan(step, jnp.zeros((B, H, N, N), f32), (tm(r), tm(decay), tm(k), tm(v), tm(kk), tm(a)))
    y = jnp.moveaxis(y, 0, 1)
    mean = jnp.mean(y, axis=-1, keepdims=True)
    var = jnp.mean(jnp.square(y - mean), axis=-1, keepdims=True)
    y = ((y - mean) * lax.rsqrt(var + RWKV_GN_EPS)).reshape(B, T, D) * ln_w + ln_b
    bonus = jnp.sum(r * k * r_k, axis=-1, keepdims=True) * v
    y = y + bonus.reshape(B, T, D)
    return (y * g).astype(h.dtype) @ w_o


def conv_ffn(h, w_up, conv_w, conv_b, w_down):
    gate, val = jnp.split(h @ w_up, 2, axis=-1)
    gate = causal_dwconv(gate, conv_w, conv_b)
    return (jax.nn.silu(gate) * val) @ w_down


def setup_inputs(seed: int = 0) -> dict:
    key = jax.random.key(seed)
    ks = iter(jax.random.split(key, 64))
    f32 = jnp.float32

    def dense(shape, fan_in):
        return jax.random.normal(next(ks), shape, f32) * fan_in ** -0.5

    def gain(shape):
        return 1.0 + 0.02 * jax.random.normal(next(ks), shape, f32)

    def small(shape, s=0.02):
        return s * jax.random.normal(next(ks), shape, f32)

    D, H, dh, G = D_MODEL, NSA_HEADS, NSA_HEAD_DIM, NSA_KV_GROUPS
    MH, dk, dv = MLSTM_HEADS, MLSTM_QK_DIM, MLSTM_V_DIM
    inp = {}
    inp['x'] = jax.random.normal(next(ks), (BATCH, SEQ, D), f32)
    inp['norm_mixer'] = gain((DEPTH, D))
    inp['norm_ffn'] = gain((DEPTH, D))
    inp['ffn_w_up'] = dense((DEPTH, D, 2 * FFN_DIM), D)
    inp['ffn_conv_w'] = dense((DEPTH, FFN_CONV, FFN_DIM), FFN_CONV)
    inp['ffn_conv_b'] = small((DEPTH, FFN_DIM))
    inp['ffn_w_down'] = dense((DEPTH, FFN_DIM, D), FFN_DIM)
    inp['nsa_w_in'] = dense((N_NSA, D, NSA_IN), D)
    inp['nsa_pe_k'] = small((N_NSA, CMP_BLOCK, dh), 0.1)
    inp['nsa_pe_v'] = small((N_NSA, CMP_BLOCK, dh), 0.1)
    inp['nsa_cmp_k_w1'] = dense((N_NSA, CMP_BLOCK * dh, CMP_HIDDEN), CMP_BLOCK * dh)
    inp['nsa_cmp_k_w2'] = dense((N_NSA, CMP_HIDDEN, dh), CMP_HIDDEN)
    inp['nsa_cmp_v_w1'] = dense((N_NSA, CMP_BLOCK * dh, CMP_HIDDEN), CMP_BLOCK * dh)
    inp['nsa_cmp_v_w2'] = dense((N_NSA, CMP_HIDDEN, dh), CMP_HIDDEN)
    inp['nsa_b_gate'] = small((N_NSA, 3 * H), 0.1)
    inp['nsa_w_out'] = dense((N_NSA, H * dh, D), H * dh)
    inp['mlstm_w_in'] = dense((N_MLSTM, D, MLSTM_IN), D)
    inp['mlstm_conv_w'] = dense((N_MLSTM, MLSTM_CONV, 2 * MH * dk), MLSTM_CONV)
    inp['mlstm_conv_b'] = small((N_MLSTM, 2 * MH * dk))
    b_i = small((N_MLSTM, MH), 0.1)
    b_f = jnp.linspace(3.0, 6.0, MH, dtype=f32)[None, :] + small((N_MLSTM, MH), 0.1)
    inp['mlstm_b_gates'] = jnp.concatenate([b_i, b_f], axis=-1)
    inp['mlstm_norm'] = gain((N_MLSTM, MH * dv))
    inp['mlstm_w_out'] = dense((N_MLSTM, MH * dv, D), MH * dv)
    inp['rwkv_mu'] = jax.random.uniform(next(ks), (N_RWKV, 6, D), f32)
    inp['rwkv_w_r'] = dense((N_RWKV, D, D), D)
    inp['rwkv_w_k'] = dense((N_RWKV, D, D), D)
    inp['rwkv_w_v'] = dense((N_RWKV, D, D), D)
    inp['rwkv_w_o'] = dense((N_RWKV, D, D), D)
    inp['rwkv_w0'] = jax.random.uniform(next(ks), (N_RWKV, D), f32, -6.0, -1.0)
    inp['rwkv_w_w1'] = dense((N_RWKV, D, RWKV_DECAY_LORA), D)
    inp['rwkv_w_w2'] = dense((N_RWKV, RWKV_DECAY_LORA, D), RWKV_DECAY_LORA)
    inp['rwkv_a0'] = small((N_RWKV, D), 0.1)
    inp['rwkv_a_w1'] = dense((N_RWKV, D, RWKV_AAA_LORA), D)
    inp['rwkv_a_w2'] = dense((N_RWKV, RWKV_AAA_LORA, D), RWKV_AAA_LORA)
    inp['rwkv_g_w1'] = dense((N_RWKV, D, RWKV_GATE_LORA), D)
    inp['rwkv_g_w2'] = dense((N_RWKV, RWKV_GATE_LORA, D), RWKV_GATE_LORA)
    inp['rwkv_k_k'] = 0.85 + small((N_RWKV, D))
    inp['rwkv_k_a'] = 1.0 + small((N_RWKV, D))
    inp['rwkv_r_k'] = small((N_RWKV, RWKV_HEADS, RWKV_HEAD_DIM), 0.1)
    inp['rwkv_ln_w'] = gain((N_RWKV, D))
    inp['rwkv_ln_b'] = small((N_RWKV, D))
    inp['final_norm'] = gain((D,))
    return inp


def reference(x, norm_mixer, norm_ffn, ffn_w_up, ffn_conv_w, ffn_conv_b, ffn_w_down,
              nsa_w_in, nsa_pe_k, nsa_pe_v, nsa_cmp_k_w1, nsa_cmp_k_w2, nsa_cmp_v_w1, nsa_cmp_v_w2,
              nsa_b_gate, nsa_w_out,
              mlstm_w_in, mlstm_conv_w, mlstm_conv_b, mlstm_b_gates, mlstm_norm, mlstm_w_out,
              rwkv_mu, rwkv_w_r, rwkv_w_k, rwkv_w_v, rwkv_w_o, rwkv_w0, rwkv_w_w1, rwkv_w_w2,
              rwkv_a0, rwkv_a_w1, rwkv_a_w2, rwkv_g_w1, rwkv_g_w2, rwkv_k_k, rwkv_k_a, rwkv_r_k,
              rwkv_ln_w, rwkv_ln_b, final_norm):
    T = x.shape[1]
    cos, sin = rope_tables(T)
    for i in range(DEPTH):
        kind, j = i % N_MIXERS, i // N_MIXERS
        h = rmsnorm(x, norm_mixer[i])
        if kind == 0:
            y = nsa_mixer(h, cos, sin, nsa_w_in[j], nsa_pe_k[j], nsa_pe_v[j], nsa_cmp_k_w1[j],
                          nsa_cmp_k_w2[j], nsa_cmp_v_w1[j], nsa_cmp_v_w2[j], nsa_b_gate[j], nsa_w_out[j])
        elif kind == 1:
            y = mlstm_mixer(h, mlstm_w_in[j], mlstm_conv_w[j], mlstm_conv_b[j], mlstm_b_gates[j],
                            mlstm_norm[j], mlstm_w_out[j])
        else:
            y = rwkv7_mixer(h, rwkv_mu[j], rwkv_w_r[j], rwkv_w_k[j], rwkv_w_v[j], rwkv_w_o[j],
                            rwkv_w0[j], rwkv_w_w1[j], rwkv_w_w2[j], rwkv_a0[j], rwkv_a_w1[j],
                            rwkv_a_w2[j], rwkv_g_w1[j], rwkv_g_w2[j], rwkv_k_k[j], rwkv_k_a[j],
                            rwkv_r_k[j], rwkv_ln_w[j], rwkv_ln_b[j])
        x = x + y.astype(x.dtype)
        h = rmsnorm(x, norm_ffn[i])
        x = x + conv_ffn(h, ffn_w_up[i], ffn_conv_w[i], ffn_conv_b[i], ffn_w_down[i]).astype(x.dtype)
    return rmsnorm(x, final_norm)
```

```python
import functools
import math

import jax
import jax.numpy as jnp
import numpy as np
from jax import lax
from jax.experimental import pallas as pl
from jax.experimental.pallas import tpu as pltpu

F32 = jnp.float32
BF16 = jnp.bfloat16

D_MODEL = 1024
DEPTH = 4
NORM_EPS = 1e-6
ROPE_THETA = 500000.0

NSA_HEAD_DIM = 64
NSA_HEADS = 16
NSA_GROUPS = 4
NSA_REP = NSA_HEADS // NSA_GROUPS
NSA_ROT_DIM = 16
CMP_BLOCK = 32
CMP_STRIDE = 16
CMP_HIDDEN = 256
SEL_BLOCK = 64
SEL_TOPK = 16
WINDOW = 512
FORCE_SCORE = 1e6
NSA_KV = NSA_GROUPS * NSA_HEAD_DIM

MLSTM_HEADS = 8
MLSTM_QK_DIM = 64
MLSTM_V_DIM = 128
MLSTM_CHUNK = 64
MLSTM_CONV = 4

RWKV_HEAD_DIM = 64
RWKV_HEADS = 16
RWKV_GN_EPS = 64e-5
RWKV_CHUNK = 64
RWKV_INV_BLOCK = 16

FFN_DIM = 2816
FFN_CONV = 3
FFN_CHUNK = 256

LANES = 128
SUBLANES = 8
VMEM_LIMIT = 56 * 1024 * 1024

HIGHEST = lax.Precision.HIGHEST


def _dot(a, b):
    return jnp.dot(a, b, preferred_element_type=F32)


def _dot_nt(a, b):
    return lax.dot_general(a, b, (((1,), (1,)), ((), ())), preferred_element_type=F32)


def _dot_tn(a, b):
    return lax.dot_general(a, b, (((0,), (0,)), ((), ())), preferred_element_type=F32)


def _dot_f32(a, b):
    return jnp.dot(a, b, preferred_element_type=F32, precision=HIGHEST)


def _rms(x, g):
    ms = jnp.mean(x * x, axis=-1, keepdims=True)
    return x * lax.rsqrt(ms + NORM_EPS) * g


def _sigmoid(x):
    return 1.0 / (1.0 + jnp.exp(-x))


def _cparams(sem):
    return pltpu.CompilerParams(dimension_semantics=sem, vmem_limit_bytes=VMEM_LIMIT)


def _const_spec(shape):
    n = len(shape)
    return pl.BlockSpec(shape, lambda *_: (0,) * n)


def _matmul_res_kernel(a_ref, w_ref, r_ref, o_ref):
    o_ref[...] = r_ref[...] + _dot(a_ref[...], w_ref[...])


def _matmul_res(a, w, res, tm=512):
    m, k = a.shape
    n = w.shape[1]
    return pl.pallas_call(
        _matmul_res_kernel,
        out_shape=jax.ShapeDtypeStruct((m, n), F32),
        grid=(m // tm,),
        in_specs=[pl.BlockSpec((tm, k), lambda i: (i, 0)), _const_spec((k, n)),
                  pl.BlockSpec((tm, n), lambda i: (i, 0))],
        out_specs=pl.BlockSpec((tm, n), lambda i: (i, 0)),
        compiler_params=_cparams(("parallel",)),
        name="matmul_res",
    )(a, w, res)


def _final_norm_kernel(x_ref, g_ref, o_ref):
    o_ref[...] = _rms(x_ref[...], g_ref[...])


def _final_norm(x, g, tm=1024):
    m, d = x.shape
    return pl.pallas_call(
        _final_norm_kernel,
        out_shape=jax.ShapeDtypeStruct((m, d), F32),
        grid=(m // tm,),
        in_specs=[pl.BlockSpec((tm, d), lambda i: (i, 0)), _const_spec((1, d))],
        out_specs=pl.BlockSpec((tm, d), lambda i: (i, 0)),
        compiler_params=_cparams(("parallel",)),
        name="final_norm",
    )(x, g.reshape(1, d))


def _ffn_kernel(x_ref, g_ref, wg_ref, wv_ref, cw_ref, cb_ref, wd_ref, o_ref,
                h_scr, carry_scr, acc_scr, *, tiles_per_seq, n_chunks):
    tm = x_ref.shape[0]
    x = x_ref[...]
    h_scr[...] = _rms(x, g_ref[...]).astype(BF16)
    seq_start = (pl.program_id(0) % tiles_per_seq) == 0
    rows = lax.broadcasted_iota(jnp.int32, (tm, FFN_CHUNK), 0)

    def chunk(c, carry):
        h = h_scr[...]
        gate = _dot(h, wg_ref[c])
        val = _dot(h, wv_ref[c])
        prev = carry_scr[c]
        prev = jnp.where(seq_start, 0.0, prev)
        p1 = prev[SUBLANES - 1:SUBLANES, :]
        p2 = prev[SUBLANES - 2:SUBLANES - 1, :]
        carry_scr[c] = gate[tm - SUBLANES:, :]
        g1 = jnp.where(rows == 0, p1, pltpu.roll(gate, 1, 0))
        g2 = jnp.where(rows == 0, p2, jnp.where(rows == 1, p1, pltpu.roll(gate, 2, 0)))
        cw = cw_ref[c]
        y = cw[2:3, :] * gate + cw[1:2, :] * g1 + cw[0:1, :] * g2 + cb_ref[c]
        act = (y * _sigmoid(y) * val).astype(BF16)
        contrib = _dot(act, wd_ref[c])

        @pl.when(c == 0)
        def _():
            acc_scr[...] = contrib

        @pl.when(c > 0)
        def _():
            acc_scr[...] += contrib

        return carry

    lax.fori_loop(0, n_chunks, chunk, 0)
    o_ref[...] = x + acc_scr[...]


def _ffn(x, g, w_up, conv_w, conv_b, w_down, seq_len, tm=512):
    m, d = x.shape
    nc = FFN_DIM // FFN_CHUNK
    wg = w_up[:, :FFN_DIM].astype(BF16).reshape(d, nc, FFN_CHUNK).transpose(1, 0, 2)
    wv = w_up[:, FFN_DIM:].astype(BF16).reshape(d, nc, FFN_CHUNK).transpose(1, 0, 2)
    cw = jnp.pad(conv_w, ((0, SUBLANES - FFN_CONV), (0, 0)))
    cw = cw.reshape(SUBLANES, nc, FFN_CHUNK).transpose(1, 0, 2)
    cb = conv_b.reshape(nc, 1, FFN_CHUNK)
    wd = w_down.astype(BF16).reshape(nc, FFN_CHUNK, d)
    kern = functools.partial(_ffn_kernel, tiles_per_seq=seq_len // tm, n_chunks=nc)
    return pl.pallas_call(
        kern,
        out_shape=jax.ShapeDtypeStruct((m, d), F32),
        grid=(m // tm,),
        in_specs=[pl.BlockSpec((tm, d), lambda i: (i, 0)), _const_spec((1, d)),
                  _const_spec((nc, d, FFN_CHUNK)), _const_spec((nc, d, FFN_CHUNK)),
                  _const_spec((nc, SUBLANES, FFN_CHUNK)), _const_spec((nc, 1, FFN_CHUNK)),
                  _const_spec((nc, FFN_CHUNK, d))],
        out_specs=pl.BlockSpec((tm, d), lambda i: (i, 0)),
        scratch_shapes=[pltpu.VMEM((tm, d), BF16),
                        pltpu.VMEM((nc, SUBLANES, FFN_CHUNK), F32),
                        pltpu.VMEM((tm, d), F32)],
        compiler_params=_cparams(("arbitrary",)),
        name="conv_ffn",
    )(x, g.reshape(1, d), wg, wv, cw, cb, wd)


def _nsa_proj_kernel(x_ref, g_ref, w_ref, bg_ref, rc_ref, rs1_ref, rs2_ref,
                     qp_ref, qr_ref, kc_ref, vc_ref, ks_ref, vs_ref, kw_ref, vw_ref, gate_ref):
    hn = _rms(x_ref[...], g_ref[...]).astype(BF16)
    y = _dot(hn, w_ref[...])
    rc, rs1, rs2 = rc_ref[...], rs1_ref[...], rs2_ref[...]
    dh = NSA_HEAD_DIM

    def rope(z):
        return z * rc + pltpu.roll(z, 8, 1) * rs1 + pltpu.roll(z, LANES - 8, 1) * rs2

    scale = dh ** -0.5
    for j in range(D_MODEL // LANES):
        q = y[:, j * LANES:(j + 1) * LANES] * scale
        qp_ref[:, j * LANES:(j + 1) * LANES] = q.astype(BF16)
        qr_ref[:, j * LANES:(j + 1) * LANES] = rope(q).astype(BF16)

    def kv_chunk(idx):
        return y[:, D_MODEL + idx * NSA_KV:D_MODEL + (idx + 1) * NSA_KV]

    def split_groups(z, ref, dtype):
        for g in range(NSA_GROUPS):
            ref[g] = z[:, g * dh:(g + 1) * dh].astype(dtype)

    def rope256(z):
        return jnp.concatenate([rope(z[:, :LANES]), rope(z[:, LANES:])], axis=1)

    split_groups(kv_chunk(0), kc_ref, F32)
    split_groups(kv_chunk(1), vc_ref, F32)
    split_groups(rope256(kv_chunk(2)), ks_ref, BF16)
    split_groups(kv_chunk(3), vs_ref, BF16)
    split_groups(rope256(kv_chunk(4)), kw_ref, BF16)
    split_groups(kv_chunk(5), vw_ref, BF16)
    gate = y[:, D_MODEL + 6 * NSA_KV:] + bg_ref[...]
    gate_ref[...] = _sigmoid(gate)


def _rope_tables(seq_len):
    half = NSA_ROT_DIM // 2
    inv_freq = ROPE_THETA ** (-jnp.arange(half, dtype=F32) / half)
    ang = jnp.arange(seq_len, dtype=F32)[:, None] * inv_freq[None, :]
    cos, sin = jnp.cos(ang), jnp.sin(ang)
    zeros = jnp.zeros((seq_len, NSA_HEAD_DIM - NSA_ROT_DIM), F32)
    z8 = jnp.zeros((seq_len, half), F32)
    rc = jnp.concatenate([cos, cos, zeros + 1.0], axis=1)
    rs1 = jnp.concatenate([z8, sin, zeros], axis=1)
    rs2 = jnp.concatenate([-sin, z8, zeros], axis=1)
    two = lambda t: jnp.concatenate([t, t], axis=1)
    return two(rc), two(rs1), two(rs2)


def _nsa_proj(x, g, w_in, b_gate, rope, seq_len, tm=512):
    m, d = x.shape
    n_kv = 6 * NSA_KV
    wg = w_in[:, D_MODEL + n_kv:].reshape(d, NSA_GROUPS, NSA_REP * 3)
    wg = jnp.pad(wg, ((0, 0), (0, 0), (0, LANES - NSA_REP * 3))).reshape(d, NSA_GROUPS * LANES)
    w = jnp.concatenate([w_in[:, :D_MODEL + n_kv], wg], axis=1).astype(BF16)
    bg = jnp.pad(b_gate.reshape(NSA_GROUPS, NSA_REP * 3), ((0, 0), (0, LANES - NSA_REP * 3)))
    bg = bg.reshape(1, NSA_GROUPS * LANES)
    n = w.shape[1]
    tps = seq_len // tm
    row = lambda i: (i, 0)
    rope_spec = pl.BlockSpec((tm, LANES), lambda i: (i % tps, 0))
    g_out = lambda dt: jax.ShapeDtypeStruct((NSA_GROUPS, m, NSA_HEAD_DIM), dt)
    g_spec = pl.BlockSpec((NSA_GROUPS, tm, NSA_HEAD_DIM), lambda i: (0, i, 0))
    return pl.pallas_call(
        _nsa_proj_kernel,
        out_shape=(jax.ShapeDtypeStruct((m, d), BF16), jax.ShapeDtypeStruct((m, d), BF16),
                   g_out(F32), g_out(F32), g_out(BF16), g_out(BF16), g_out(BF16), g_out(BF16),
                   jax.ShapeDtypeStruct((m, NSA_GROUPS * LANES), F32)),
        grid=(m // tm,),
        in_specs=[pl.BlockSpec((tm, d), row), _const_spec((1, d)), _const_spec((d, n)),
                  _const_spec((1, NSA_GROUPS * LANES)), rope_spec, rope_spec, rope_spec],
        out_specs=(pl.BlockSpec((tm, d), row), pl.BlockSpec((tm, d), row),
                   g_spec, g_spec, g_spec, g_spec, g_spec, g_spec,
                   pl.BlockSpec((tm, NSA_GROUPS * LANES), row)),
        compiler_params=_cparams(("parallel",)),
        name="nsa_proj",
    )(x, g.reshape(1, d), w, bg, *rope)


def _gelu_tanh(x):
    return 0.5 * x * (1.0 + jnp.tanh(math.sqrt(2.0 / math.pi) * (x + 0.044715 * (x * x * x))))


def _compress_kernel(zk_ref, zv_ref, pek_ref, pev_ref, w1k_ref, w2k_ref, w1v_ref, w2v_ref,
                     kc_ref, vc_ref):
    nrow = zk_ref.shape[0]
    rows = lax.broadcasted_iota(jnp.int32, (nrow, NSA_HEAD_DIM), 0)

    def one(z_ref, pe_ref, w1_ref, w2_ref, o_ref):
        z = z_ref[...]
        a = _dot((z + pe_ref[0:1, :]).astype(BF16), w1_ref[0])
        b = _dot((z + pe_ref[1:2, :]).astype(BF16), w1_ref[1])
        hid = a + pltpu.roll(b, nrow - 1, 0)
        out = _dot(_gelu_tanh(hid).astype(BF16), w2_ref[...])
        o_ref[...] = jnp.where(rows == nrow - 1, 0.0, out).astype(o_ref.dtype)

    one(zk_ref, pek_ref, w1k_ref, w2k_ref, kc_ref)
    one(zv_ref, pev_ref, w1v_ref, w2v_ref, vc_ref)


def _compress(kc_raw, vc_raw, pe_k, pe_v, w1k, w2k, w1v, w2v, seq_len):
    g, m, dh = kc_raw.shape
    half = CMP_STRIDE * dh
    nchunk = seq_len // CMP_STRIDE
    zk = kc_raw.reshape(g * m // CMP_STRIDE, half)
    zv = vc_raw.reshape(g * m // CMP_STRIDE, half)
    pe2 = lambda pe: pe.reshape(2, half)
    w1 = lambda w: w.astype(BF16).reshape(2, half, CMP_HIDDEN)
    nblk = zk.shape[0] // nchunk
    row = lambda i: (i, 0)
    return pl.pallas_call(
        _compress_kernel,
        out_shape=(jax.ShapeDtypeStruct((zk.shape[0], dh), BF16),
                   jax.ShapeDtypeStruct((zk.shape[0], dh), BF16)),
        grid=(nblk,),
        in_specs=[pl.BlockSpec((nchunk, half), row), pl.BlockSpec((nchunk, half), row),
                  _const_spec((2, half)), _const_spec((2, half)),
                  _const_spec((2, half, CMP_HIDDEN)), _const_spec((CMP_HIDDEN, dh)),
                  _const_spec((2, half, CMP_HIDDEN)), _const_spec((CMP_HIDDEN, dh))],
        out_specs=(pl.BlockSpec((nchunk, dh), row), pl.BlockSpec((nchunk, dh), row)),
        compiler_params=_cparams(("parallel",)),
        name="nsa_compress",
    )(zk, zv, pe2(pe_k), pe2(pe_v), w1(w1k), w2k.astype(BF16), w1(w1v), w2v.astype(BF16))


def _softmax_rows(s, mask):
    s = jnp.where(mask, s, -jnp.inf)
    mx = jnp.max(s, axis=-1, keepdims=True)
    mx = jnp.where(mx == -jnp.inf, 0.0, mx)
    e = jnp.exp(s - mx)
    return e / jnp.maximum(jnp.sum(e, axis=-1, keepdims=True), 1e-30)


def _nsa_cmp_kernel(q_ref, kc_ref, vc_ref, gate_ref, ov_ref, oc_ref, sel_ref, *, tq):
    qi = pl.program_id(2)
    dh = NSA_HEAD_DIM
    q = q_ref[...]
    qs = jnp.concatenate([q[:, r * dh:(r + 1) * dh] for r in range(NSA_REP)], axis=0)
    kc, vc = kc_ref[...], vc_ref[...]
    ncmp = kc.shape[0]
    s = _dot_nt(qs, kc).reshape(NSA_REP, tq, ncmp)
    t = qi * tq + lax.broadcasted_iota(jnp.int32, (tq, ncmp), 0)
    cmp_end = lax.broadcasted_iota(jnp.int32, (tq, ncmp), 1) * CMP_STRIDE + (CMP_BLOCK - 1)
    p = _softmax_rows(s, (cmp_end <= t)[None])
    o = _dot(p.reshape(NSA_REP * tq, ncmp).astype(BF16), vc)
    gate = gate_ref[...]
    for r in range(NSA_REP):
        oc_ref[:, r * dh:(r + 1) * dh] = o[r * tq:(r + 1) * tq] * gate[:, 3 * r:3 * r + 1]
    psum = jnp.sum(p, axis=0)
    imp = _dot_f32(psum, ov_ref[...])
    nsel = imp.shape[1]
    lane = lax.broadcasted_iota(jnp.int32, (tq, nsel), 1)
    tb = (qi * tq + lax.broadcasted_iota(jnp.int32, (tq, nsel), 0)) // SEL_BLOCK
    forced = (lane == 0) | (lane == tb) | (lane == tb - 1)
    vals = jnp.where(forced, FORCE_SCORE, jnp.where(lane <= tb, imp, -1.0))
    lane_f = lane.astype(F32)
    sel = jnp.zeros((tq, nsel), F32)
    for _ in range(SEL_TOPK):
        mx = jnp.max(vals, axis=-1, keepdims=True)
        first = jnp.min(jnp.where(vals == mx, lane_f, float(nsel)), axis=-1, keepdims=True)
        pick = lane_f == first
        sel = jnp.where(pick, 1.0, sel)
        vals = jnp.where(pick, -jnp.inf, vals)
    sel_ref[...] = sel.astype(sel_ref.dtype)


def _overlap_matrix(ncmp_pad, nsel):
    c = np.arange(ncmp_pad)[:, None]
    s = np.arange(nsel)[None, :]
    cmp_start = c * CMP_STRIDE
    cmp_end = cmp_start + CMP_BLOCK - 1
    blk_start = s * SEL_BLOCK
    ov = (cmp_end >= blk_start) & (cmp_start <= blk_start + SEL_BLOCK - 1)
    return jnp.asarray(ov.astype(np.float32))


def _nsa_cmp(qp, kc, vc, gates, batch, seq_len, tq=256):
    m, d = qp.shape
    nq = seq_len // tq
    ncmp = seq_len // CMP_STRIDE
    nsel = seq_len // SEL_BLOCK
    ov = _overlap_matrix(ncmp, nsel)
    qmap = lambda b, g, i: (b * nq + i, g)
    kmap = lambda b, g, i: (g * batch + b, 0)
    return pl.pallas_call(
        functools.partial(_nsa_cmp_kernel, tq=tq),
        out_shape=(jax.ShapeDtypeStruct((m, d), F32),
                   jax.ShapeDtypeStruct((NSA_GROUPS, m, nsel), BF16)),
        grid=(batch, NSA_GROUPS, nq),
        in_specs=[pl.BlockSpec((tq, NSA_KV), qmap),
                  pl.BlockSpec((ncmp, NSA_HEAD_DIM), kmap), pl.BlockSpec((ncmp, NSA_HEAD_DIM), kmap),
                  pl.BlockSpec((tq, LANES), qmap), _const_spec((ncmp, nsel))],
        out_specs=(pl.BlockSpec((tq, NSA_KV), qmap),
                   pl.BlockSpec((None, tq, nsel), lambda b, g, i: (g, b * nq + i, 0))),
        compiler_params=_cparams(("parallel", "parallel", "parallel")),
        name="nsa_cmp_topk",
    )(qp, kc, vc, gates, ov)


def _nsa_sel_kernel(q_ref, ks_ref, vs_ref, kw_ref, vw_ref, sel_ref, gate_ref, oc_ref, o_ref,
                    m_scr, l_scr, acc_scr, *, tq, tk):
    qi = pl.program_id(2)
    dh = NSA_HEAD_DIM
    rows = NSA_REP * tq
    q = q_ref[...]
    qs = jnp.concatenate([q[:, r * dh:(r + 1) * dh] for r in range(NSA_REP)], axis=0)
    sel = sel_ref[...]
    nsel = sel.shape[1]
    q0 = qi * tq
    t_pos = q0 + lax.broadcasted_iota(jnp.int32, (tq, tk), 0)
    col = lax.broadcasted_iota(jnp.int32, (tq, tk), 1)
    e_row = lax.broadcasted_iota(jnp.int32, (nsel, tk), 0)
    e_col = lax.broadcasted_iota(jnp.int32, (nsel, tk), 1) // SEL_BLOCK

    m_scr[...] = jnp.full((rows, 1), -jnp.inf, F32)
    l_scr[...] = jnp.zeros((rows, 1), F32)
    acc_scr[...] = jnp.zeros((rows, dh), F32)

    def ktile(ki, carry):
        k0 = pl.multiple_of(ki * tk, tk)
        k = ks_ref[pl.ds(k0, tk), :]
        v = vs_ref[pl.ds(k0, tk), :]
        expand = (e_row == ki * (tk // SEL_BLOCK) + e_col).astype(BF16)
        picked = _dot(sel, expand) > 0.5
        mask = picked & (k0 + col <= t_pos)
        s = _dot_nt(qs, k).reshape(NSA_REP, tq, tk)
        s = jnp.where(mask[None], s, -jnp.inf).reshape(rows, tk)
        m_old = m_scr[...]
        m_new = jnp.maximum(m_old, jnp.max(s, axis=-1, keepdims=True))
        m_safe = jnp.where(m_new == -jnp.inf, 0.0, m_new)
        alpha = jnp.exp(m_old - m_safe)
        p = jnp.exp(s - m_safe)
        l_scr[...] = alpha * l_scr[...] + jnp.sum(p, axis=-1, keepdims=True)
        acc_scr[...] = alpha * acc_scr[...] + _dot(p.astype(BF16), v)
        m_scr[...] = m_new
        return carry

    lax.fori_loop(0, (q0 + tq - 1) // tk + 1, ktile, 0)
    o_sel = acc_scr[...] / jnp.maximum(l_scr[...], 1e-30)

    wlen = tq + WINDOW
    w0 = pl.multiple_of(jnp.maximum(q0 - WINDOW, 0), tq)
    kw = kw_ref[pl.ds(w0, wlen), :]
    vw = vw_ref[pl.ds(w0, wlen), :]
    tw = q0 + lax.broadcasted_iota(jnp.int32, (tq, wlen), 0)
    pw = w0 + lax.broadcasted_iota(jnp.int32, (tq, wlen), 1)
    vis = (pw <= tw) & (pw > tw - WINDOW)
    sw = _dot_nt(qs, kw).reshape(NSA_REP, tq, wlen)
    pwin = _softmax_rows(sw, vis[None]).reshape(rows, wlen)
    o_win = _dot(pwin.astype(BF16), vw)

    gate = gate_ref[...]
    oc = oc_ref[...]
    for r in range(NSA_REP):
        sl = slice(r * tq, (r + 1) * tq)
        tot = (oc[:, r * dh:(r + 1) * dh] + o_sel[sl] * gate[:, 3 * r + 1:3 * r + 2]
               + o_win[sl] * gate[:, 3 * r + 2:3 * r + 3])
        o_ref[:, r * dh:(r + 1) * dh] = tot.astype(o_ref.dtype)


def _nsa_sel(qr, ks, vs, kw, vw, sel, gates, oc, batch, seq_len, tq=256, tk=512):
    m, d = qr.shape
    nq = seq_len // tq
    nsel = seq_len // SEL_BLOCK
    qmap = lambda b, g, i: (b * nq + i, g)
    kvmap = lambda b, g, i: (g, b, 0)
    kv_spec = pl.BlockSpec((None, seq_len, NSA_HEAD_DIM), kvmap)
    rows = NSA_REP * tq
    return pl.pallas_call(
        functools.partial(_nsa_sel_kernel, tq=tq, tk=tk),
        out_shape=jax.ShapeDtypeStruct((m, d), BF16),
        grid=(batch, NSA_GROUPS, nq),
        in_specs=[pl.BlockSpec((tq, NSA_KV), qmap), kv_spec, kv_spec, kv_spec, kv_spec,
                  pl.BlockSpec((None, tq, nsel), lambda b, g, i: (g, b * nq + i, 0)),
                  pl.BlockSpec((tq, LANES), qmap), pl.BlockSpec((tq, NSA_KV), qmap)],
        out_specs=pl.BlockSpec((tq, NSA_KV), qmap),
        scratch_shapes=[pltpu.VMEM((rows, 1), F32), pltpu.VMEM((rows, 1), F32),
                        pltpu.VMEM((rows, NSA_HEAD_DIM), F32)],
        compiler_params=_cparams(("parallel", "parallel", "parallel")),
        name="nsa_sel_win",
    )(qr, ks, vs, kw, vw, sel, gates, oc)


def _nsa_layer(x, norm_g, w_in, pe_k, pe_v, w1k, w2k, w1v, w2v, b_gate, w_out, rope, batch, seq_len):
    qp, qr, kc_raw, vc_raw, ks, vs, kw, vw, gates = _nsa_proj(x, norm_g, w_in, b_gate, rope, seq_len)
    kc, vc = _compress(kc_raw, vc_raw, pe_k, pe_v, w1k, w2k, w1v, w2v, seq_len)
    oc, sel = _nsa_cmp(qp, kc, vc, gates, batch, seq_len)
    o = _nsa_sel(qr, ks, vs, kw, vw, sel, gates, oc, batch, seq_len)
    return _matmul_res(o, w_out.astype(BF16), x)


def _mlstm_proj_kernel(x_ref, g_ref, w_ref, wt_ref, bcol_ref, brow_ref,
                       qk_ref, v_ref, o_ref, gc_ref, gr_ref):
    hn = _rms(x_ref[...], g_ref[...]).astype(BF16)
    y = _dot(hn, w_ref[...])
    d = D_MODEL
    qk_ref[...] = y[:, :d]
    v_ref[...] = y[:, d:2 * d].astype(BF16)
    o_ref[...] = _sigmoid(y[:, 2 * d:3 * d])
    gc_ref[...] = y[:, 3 * d:] + bcol_ref[...]
    gr_ref[...] = _dot_nt(wt_ref[...], hn) + brow_ref[...]


def _mlstm_proj(x, g, w_in, b_gates, tm=512):
    m, d = x.shape
    h, dk = MLSTM_HEADS, MLSTM_QK_DIM
    wq = w_in[:, :h * dk].reshape(d, h, dk)
    wk = w_in[:, h * dk:2 * h * dk].reshape(d, h, dk)
    wqk = jnp.concatenate([wq, wk], axis=2).reshape(d, 2 * h * dk)
    wv = w_in[:, d:2 * d]
    wif = w_in[:, 2 * d:2 * d + 2 * h]
    wo = w_in[:, 2 * d + 2 * h:]
    w = jnp.concatenate([wqk, wv, wo, jnp.pad(wif, ((0, 0), (0, LANES - 2 * h)))], axis=1).astype(BF16)
    wt = wif.T.astype(BF16)
    bcol = jnp.pad(b_gates, (0, LANES - 2 * h)).reshape(1, LANES)
    brow = b_gates.reshape(2 * h, 1)
    n = w.shape[1]
    row = lambda i: (i, 0)
    return pl.pallas_call(
        _mlstm_proj_kernel,
        out_shape=(jax.ShapeDtypeStruct((m, d), F32), jax.ShapeDtypeStruct((m, d), BF16),
                   jax.ShapeDtypeStruct((m, d), F32), jax.ShapeDtypeStruct((m, LANES), F32),
                   jax.ShapeDtypeStruct((2 * h, m), F32)),
        grid=(m // tm,),
        in_specs=[pl.BlockSpec((tm, d), row), _const_spec((1, d)), _const_spec((d, n)),
                  _const_spec((2 * h, d)), _const_spec((1, LANES)), _const_spec((2 * h, 1))],
        out_specs=(pl.BlockSpec((tm, d), row), pl.BlockSpec((tm, d), row), pl.BlockSpec((tm, d), row),
                   pl.BlockSpec((tm, LANES), row), pl.BlockSpec((2 * h, tm), lambda i: (0, i))),
        compiler_params=_cparams(("parallel",)),
        name="mlstm_proj",
    )(x, g.reshape(1, d), w, wt, bcol, brow)


def _log_sigmoid(x):
    return jnp.minimum(x, 0.0) - jnp.log(1.0 + jnp.exp(-jnp.abs(x)))


def _mlstm_core_kernel(qk_ref, v_ref, og_ref, gc_ref, gr_ref, cw_ref, cb_ref, ng_ref, o_ref,
                       qkc_scr, prev_scr, c_scr, n_scr, m_scr, *, tt):
    L = MLSTM_CHUNK
    H, dk, dv = MLSTM_HEADS, MLSTM_QK_DIM, MLSTM_V_DIM
    seq_start = pl.program_id(1) == 0

    @pl.when(seq_start)
    def _():
        c_scr[...] = jnp.zeros_like(c_scr)
        n_scr[...] = jnp.zeros_like(n_scr)
        m_scr[...] = jnp.zeros_like(m_scr)
        prev_scr[...] = jnp.zeros_like(prev_scr)

    raw = qk_ref[...]
    prev = prev_scr[...]
    prev_scr[...] = raw[tt - SUBLANES:, :]
    rows = lax.broadcasted_iota(jnp.int32, raw.shape, 0)
    cw = cw_ref[...]
    acc = cw[MLSTM_CONV - 1:MLSTM_CONV, :] * raw + cb_ref[...]
    for s in range(1, MLSTM_CONV):
        shifted = pltpu.roll(raw, s, 0)
        for j in range(s):
            shifted = jnp.where(rows == j, prev[SUBLANES - s + j:SUBLANES - s + j + 1, :], shifted)
        acc = acc + cw[MLSTM_CONV - 1 - s:MLSTM_CONV - s, :] * shifted
    act = acc * _sigmoid(acc)
    lane = lax.broadcasted_iota(jnp.int32, raw.shape, 1)
    qkc_scr[...] = jnp.where(lane % (2 * dk) >= dk, act * dk ** -0.5, act)

    tri_r = lax.broadcasted_iota(jnp.int32, (L, L), 0)
    tri_c = lax.broadcasted_iota(jnp.int32, (L, L), 1)
    tri_l = tri_r >= tri_c
    tri_lf = tri_l.astype(F32)
    tri_uf = (tri_r <= tri_c).astype(F32)

    def chunk(c, carry):
        r0 = pl.multiple_of(c * L, L)
        gcol = gc_ref[pl.ds(r0, L), :]
        grow = gr_ref[c]
        b_col = _dot_f32(tri_lf, _log_sigmoid(gcol))
        b_row = _dot_f32(_log_sigmoid(grow), tri_uf)
        for h in range(H):
            qk = qkc_scr[pl.ds(r0, L), h * 2 * dk:(h + 1) * 2 * dk]
            q = qk[:, :dk].astype(BF16)
            k = qk[:, dk:]
            v = v_ref[pl.ds(r0, L), h * dv:(h + 1) * dv]
            li_c, b_c = gcol[:, h:h + 1], b_col[:, H + h:H + h + 1]
            li_r, b_r = grow[h:h + 1, :], b_row[H + h:H + h + 1, :]
            b_end = b_r[:, L - 1:L]
            m_prev = m_scr[h:h + 1, 0:1]
            c_prev = c_scr[h]
            n_prev = n_scr[h:h + 1, :]
            dmat = jnp.where(tri_l, b_c - b_r + li_r, -jnp.inf)
            m_inter = b_c + m_prev
            m_t = jnp.maximum(m_inter, jnp.max(dmat, axis=-1, keepdims=True))
            att = jnp.exp(dmat - m_t) * _dot_nt(q, k.astype(BF16))
            inter = jnp.exp(m_inter - m_t)
            num = _dot(att.astype(BF16), v) + inter * _dot(q, c_prev.astype(BF16))
            qf = qk[:, :dk]
            den = jnp.sum(att, axis=-1, keepdims=True) + inter * jnp.sum(qf * n_prev, axis=-1, keepdims=True)
            h_t = num / jnp.maximum(jnp.abs(den), jnp.exp(-m_t))
            h_t = h_t * lax.rsqrt(jnp.mean(h_t * h_t, axis=-1, keepdims=True) + NORM_EPS)
            h_t = h_t * ng_ref[:, h * dv:(h + 1) * dv] * og_ref[pl.ds(r0, L), h * dv:(h + 1) * dv]
            o_ref[pl.ds(r0, L), h * dv:(h + 1) * dv] = h_t.astype(o_ref.dtype)
            g_end_r = b_end - b_r + li_r
            g_max = jnp.max(g_end_r, axis=-1, keepdims=True)
            w_end_c = jnp.exp(b_end - b_c + li_c - g_max)
            kw = k * w_end_c
            c_loc = _dot_tn(kw.astype(BF16), v)
            n_loc = jnp.sum(kw, axis=0, keepdims=True)
            m_new = jnp.maximum(b_end + m_prev, g_max)
            a = jnp.exp(b_end + m_prev - m_new)
            sc = jnp.exp(g_max - m_new)
            c_scr[h] = a * c_prev + sc * c_loc
            n_scr[h:h + 1, :] = a * n_prev + sc * n_loc
            m_scr[h:h + 1, :] = jnp.broadcast_to(m_new, (1, LANES))
        return carry

    lax.fori_loop(0, tt // L, chunk, 0)


def _mlstm_core(qk, v, og, gcol, grow3, conv_w, conv_b, norm_g, batch, seq_len, tt=256):
    m, d = qk.shape
    H, dk, dv = MLSTM_HEADS, MLSTM_QK_DIM, MLSTM_V_DIM
    nt = seq_len // tt
    ncs = tt // MLSTM_CHUNK
    row = lambda b, i: (b * nt + i, 0)
    return pl.pallas_call(
        functools.partial(_mlstm_core_kernel, tt=tt),
        out_shape=jax.ShapeDtypeStruct((m, d), BF16),
        grid=(batch, nt),
        in_specs=[pl.BlockSpec((tt, d), row), pl.BlockSpec((tt, d), row), pl.BlockSpec((tt, d), row),
                  pl.BlockSpec((tt, LANES), row),
                  pl.BlockSpec((ncs, 2 * H, MLSTM_CHUNK), lambda b, i: (b * nt + i, 0, 0)),
                  _const_spec((SUBLANES, d)), _const_spec((1, d)), _const_spec((1, d))],
        out_specs=pl.BlockSpec((tt, d), row),
        scratch_shapes=[pltpu.VMEM((tt, d), F32), pltpu.VMEM((SUBLANES, d), F32),
                        pltpu.VMEM((H, dk, dv), F32), pltpu.VMEM((H, dk), F32),
                        pltpu.VMEM((H, LANES), F32)],
        compiler_params=_cparams(("arbitrary", "arbitrary")),
        name="mlstm_core",
    )(qk, v, og, gcol, grow3, conv_w, conv_b, norm_g)


def _mlstm_layer(x, norm_g, w_in, conv_w, conv_b, b_gates, hnorm, w_out, batch, seq_len):
    H, dk = MLSTM_HEADS, MLSTM_QK_DIM
    qk, v, og, gcol, grow = _mlstm_proj(x, norm_g, w_in, b_gates)
    m = x.shape[0]
    grow3 = grow.reshape(2 * H, m // MLSTM_CHUNK, MLSTM_CHUNK).transpose(1, 0, 2)

    def perm(z):
        lead = z.shape[:-1]
        zq = z[..., :H * dk].reshape(*lead, H, dk)
        zk = z[..., H * dk:].reshape(*lead, H, dk)
        return jnp.concatenate([zq, zk], axis=-1).reshape(*lead, 2 * H * dk)

    cw = jnp.pad(perm(conv_w), ((0, SUBLANES - MLSTM_CONV), (0, 0)))
    o = _mlstm_core(qk, v, og, gcol, grow3, cw, perm(conv_b).reshape(1, -1), hnorm.reshape(1, -1),
                    batch, seq_len)
    return _matmul_res(o, w_out.astype(BF16), x)


def _softplus(x):
    return jnp.maximum(x, 0.0) + jnp.log(1.0 + jnp.exp(-jnp.abs(x)))


def _rwkv_proj_kernel(x_ref, g_ref, mu_ref, wr_ref, wk_ref, wv_ref, ww1_ref, ww2_ref, w0_ref,
                      aw1_ref, aw2_ref, a0_ref, gw1_ref, gw2_ref,
                      r_ref, k_ref, v_ref, lw_ref, a_ref, go_ref, prev_scr, *, tiles_per_seq):
    tm = x_ref.shape[0]
    h = _rms(x_ref[...], g_ref[...])
    seq_start = (pl.program_id(0) % tiles_per_seq) == 0
    prev = jnp.where(seq_start, 0.0, prev_scr[SUBLANES - 1:SUBLANES, :])
    prev_scr[...] = h[tm - SUBLANES:, :]
    rows = lax.broadcasted_iota(jnp.int32, h.shape, 0)
    xx = jnp.where(rows == 0, prev, pltpu.roll(h, 1, 0)) - h
    mix = lambda j: (h + xx * mu_ref[j:j + 1, :]).astype(BF16)
    r_ref[...] = _dot(mix(0), wr_ref[...])
    k_ref[...] = _dot(mix(2), wk_ref[...])
    v_ref[...] = _dot(mix(3), wv_ref[...])
    wl = _dot(jnp.tanh(_dot(mix(1), ww1_ref[...])).astype(BF16), ww2_ref[...]) + w0_ref[...]
    w_log = -_softplus(-wl) - 0.5
    lw_ref[...] = -jnp.exp(w_log)
    al = _dot(_dot(mix(4), aw1_ref[...]).astype(BF16), aw2_ref[...]) + a0_ref[...]
    a_ref[...] = _sigmoid(al)
    go_ref[...] = _dot(_sigmoid(_dot(mix(5), gw1_ref[...])).astype(BF16), gw2_ref[...])


def _rwkv_proj(x, g, mu, w_r, w_k, w_v, w0, w_w1, w_w2, a0, a_w1, a_w2, g_w1, g_w2, seq_len, tm=256):
    m, d = x.shape
    row = lambda i: (i, 0)
    bf = lambda w: w.astype(BF16)
    mu8 = jnp.pad(mu, ((0, SUBLANES - mu.shape[0]), (0, 0)))
    consts = [g.reshape(1, d), mu8, bf(w_r), bf(w_k), bf(w_v), bf(w_w1), bf(w_w2), w0.reshape(1, d),
              bf(a_w1), bf(a_w2), a0.reshape(1, d), bf(g_w1), bf(g_w2)]
    out = jax.ShapeDtypeStruct((m, d), F32)
    return pl.pallas_call(
        functools.partial(_rwkv_proj_kernel, tiles_per_seq=seq_len // tm),
        out_shape=(out,) * 6,
        grid=(m // tm,),
        in_specs=[pl.BlockSpec((tm, d), row)] + [_const_spec(c.shape) for c in consts],
        out_specs=(pl.BlockSpec((tm, d), row),) * 6,
        scratch_shapes=[pltpu.VMEM((SUBLANES, d), F32)],
        compiler_params=_cparams(("arbitrary",)),
        name="rwkv_proj",
    )(x, *consts)


def _unit_lower_inverse(a_strict, eye, blk_diag):
    n1 = jnp.where(blk_diag, -a_strict, 0.0)
    n2 = _dot_f32(n1, n1)
    n4 = _dot_f32(n2, n2)
    n8 = _dot_f32(n4, n4)
    p = eye + n1
    p = p + _dot_f32(p, n2)
    p = p + _dot_f32(p, n4)
    d_inv = p + _dot_f32(p, n8)
    l_off = jnp.where(blk_diag, 0.0, a_strict)
    e1 = -_dot_f32(d_inv, l_off)
    e2 = _dot_f32(e1, e1)
    q = eye + e1
    q = q + _dot_f32(q, e2)
    return _dot_f32(q, d_inv)


def _rwkv_core_kernel(r_ref, k_ref, v_ref, lw_ref, a_ref, go_ref, kk_ref, ka_ref, rk_ref,
                      lnw_ref, lnb_ref, o_ref, st_scr, *, tt):
    L, N = RWKV_CHUNK, RWKV_HEAD_DIM

    @pl.when(pl.program_id(2) == 0)
    def _():
        st_scr[...] = jnp.zeros_like(st_scr)

    ri = lax.broadcasted_iota(jnp.int32, (L, L), 0)
    ci = lax.broadcasted_iota(jnp.int32, (L, L), 1)
    lower_incl = ri >= ci
    lower_strict = ri > ci
    blk_diag = (ri // RWKV_INV_BLOCK) == (ci // RWKV_INV_BLOCK)
    eye = (ri == ci).astype(F32)
    tri_f = lower_incl.astype(F32)

    def chunk(c, carry):
        r0 = pl.multiple_of(c * L, L)
        for hh in range(2):
            sl = slice(hh * N, (hh + 1) * N)
            r = r_ref[pl.ds(r0, L), sl]
            k = k_ref[pl.ds(r0, L), sl]
            v = v_ref[pl.ds(r0, L), sl]
            lw = lw_ref[pl.ds(r0, L), sl]
            a = a_ref[pl.ds(r0, L), sl]
            kk = k * kk_ref[:, sl]
            kk = kk / jnp.maximum(jnp.sqrt(jnp.sum(kk * kk, axis=-1, keepdims=True)), 1e-12)
            km = k * (1.0 + (a - 1.0) * ka_ref[:, sl])
            bv = kk * a
            cum = _dot_f32(tri_f, lw)
            cum_end = cum[L - 1:L, :]
            w_in = jnp.exp(cum)
            w_inv = jnp.exp(-cum)
            w_out = jnp.exp(cum_end - cum)
            kk_h = (kk * jnp.exp(cum - lw)).astype(BF16)
            r_h = (r * w_in).astype(BF16)
            b_t = (bv * w_inv).astype(BF16)
            k_t = (km * w_inv).astype(BF16)
            vb = v.astype(BF16)
            lhs = jnp.concatenate([kk_h, r_h], axis=0)
            rhs = jnp.concatenate([b_t, k_t], axis=0)
            amat = _dot_nt(lhs, rhs)
            a_ub = jnp.where(lower_strict, amat[:L, :L], 0.0)
            a_uk = jnp.where(lower_strict, amat[:L, L:], 0.0)
            a_rb = jnp.where(lower_incl, amat[L:, :L], 0.0)
            a_rk = jnp.where(lower_incl, amat[L:, L:], 0.0)
            t_inv = _unit_lower_inverse(a_ub, eye, blk_diag)
            st = st_scr[hh]
            stb = st.astype(BF16)
            rhs_u = _dot_nt(kk_h, stb) + _dot(a_uk.astype(BF16), vb)
            u = -_dot_f32(t_inv, rhs_u)
            ub = u.astype(BF16)
            y = _dot_nt(r_h, stb) + _dot(a_rb.astype(BF16), ub) + _dot(a_rk.astype(BF16), vb)
            st_scr[hh] = (st * jnp.exp(cum_end)
                          + _dot_tn(ub, (bv * w_out).astype(BF16))
                          + _dot_tn(vb, (km * w_out).astype(BF16)))
            mean = jnp.mean(y, axis=-1, keepdims=True)
            yc = y - mean
            var = jnp.mean(yc * yc, axis=-1, keepdims=True)
            yn = yc * lax.rsqrt(var + RWKV_GN_EPS) * lnw_ref[:, sl] + lnb_ref[:, sl]
            bonus = jnp.sum(r * km * rk_ref[:, sl], axis=-1, keepdims=True) * v
            o_ref[pl.ds(r0, L), sl] = ((yn + bonus) * go_ref[pl.ds(r0, L), sl]).astype(o_ref.dtype)
        return carry

    lax.fori_loop(0, tt // L, chunk, 0)


def _rwkv_core(r, k, v, lw, a, go, k_k, k_a, r_k, ln_w, ln_b, batch, seq_len, tt=256):
    m, d = r.shape
    nt = seq_len // tt
    npair = d // LANES
    blk = pl.BlockSpec((tt, LANES), lambda b, p, i: (b * nt + i, p))
    par = pl.BlockSpec((1, LANES), lambda b, p, i: (0, p))
    row1 = lambda z: z.reshape(1, d)
    return pl.pallas_call(
        functools.partial(_rwkv_core_kernel, tt=tt),
        out_shape=jax.ShapeDtypeStruct((m, d), BF16),
        grid=(batch, npair, nt),
        in_specs=[blk] * 6 + [par] * 5,
        out_specs=blk,
        scratch_shapes=[pltpu.VMEM((2, RWKV_HEAD_DIM, RWKV_HEAD_DIM), F32)],
        compiler_params=_cparams(("arbitrary", "arbitrary", "arbitrary")),
        name="rwkv_core",
    )(r, k, v, lw, a, go, row1(k_k), row1(k_a), row1(r_k), row1(ln_w), row1(ln_b))


def _rwkv_layer(x, norm_g, mu, w_r, w_k, w_v, w_o, w0, w_w1, w_w2, a0, a_w1, a_w2, g_w1, g_w2,
                k_k, k_a, r_k, ln_w, ln_b, batch, seq_len):
    r, k, v, lw, a, go = _rwkv_proj(x, norm_g, mu, w_r, w_k, w_v, w0, w_w1, w_w2, a0, a_w1, a_w2,
                                    g_w1, g_w2, seq_len)
    o = _rwkv_core(r, k, v, lw, a, go, k_k, k_a, r_k, ln_w, ln_b, batch, seq_len)
    return _matmul_res(o, w_o.astype(BF16), x)


def kernel(x, norm_mixer, norm_ffn, ffn_w_up, ffn_conv_w, ffn_conv_b, ffn_w_down, nsa_w_in, nsa_pe_k, nsa_pe_v, nsa_cmp_k_w1, nsa_cmp_k_w2, nsa_cmp_v_w1, nsa_cmp_v_w2, nsa_b_gate, nsa_w_out, mlstm_w_in, mlstm_conv_w, mlstm_conv_b, mlstm_b_gates, mlstm_norm, mlstm_w_out, rwkv_mu, rwkv_w_r, rwkv_w_k, rwkv_w_v, rwkv_w_o, rwkv_w0, rwkv_w_w1, rwkv_w_w2, rwkv_a0, rwkv_a_w1, rwkv_a_w2, rwkv_g_w1, rwkv_g_w2, rwkv_k_k, rwkv_k_a, rwkv_r_k, rwkv_ln_w, rwkv_ln_b, final_norm):
    batch, seq_len, d = x.shape
    depth = norm_mixer.shape[0]
    rope = _rope_tables(seq_len)
    xf = x.reshape(batch * seq_len, d)
    for i in range(depth):
        kind, j = i % 3, i // 3
        if kind == 0:
            xf = _nsa_layer(xf, norm_mixer[i], nsa_w_in[j], nsa_pe_k[j], nsa_pe_v[j], nsa_cmp_k_w1[j],
                            nsa_cmp_k_w2[j], nsa_cmp_v_w1[j], nsa_cmp_v_w2[j], nsa_b_gate[j],
                            nsa_w_out[j], rope, batch, seq_len)
        elif kind == 1:
            xf = _mlstm_layer(xf, norm_mixer[i], mlstm_w_in[j], mlstm_conv_w[j], mlstm_conv_b[j],
                              mlstm_b_gates[j], mlstm_norm[j], mlstm_w_out[j], batch, seq_len)
        else:
            xf = _rwkv_layer(xf, norm_mixer[i], rwkv_mu[j], rwkv_w_r[j], rwkv_w_k[j], rwkv_w_v[j],
                             rwkv_w_o[j], rwkv_w0[j], rwkv_w_w1[j], rwkv_w_w2[j], rwkv_a0[j],
                             rwkv_a_w1[j], rwkv_a_w2[j], rwkv_g_w1[j], rwkv_g_w2[j], rwkv_k_k[j],
                             rwkv_k_a[j], rwkv_r_k[j], rwkv_ln_w[j], rwkv_ln_b[j], batch, seq_len)
        xf = _ffn(xf, norm_ffn[i], ffn_w_up[i], ffn_conv_w[i], ffn_conv_b[i], ffn_w_down[i], seq_len)
    return _final_norm(xf, final_norm).reshape(batch, seq_len, d)
```

```python
import functools
import math

import jax
import jax.numpy as jnp
import numpy as np
from jax import lax
from jax.experimental import pallas as pl
from jax.experimental.pallas import tpu as pltpu

F32 = jnp.float32
BF16 = jnp.bfloat16

D_MODEL = 1024
DEPTH = 4
NORM_EPS = 1e-6
ROPE_THETA = 500000.0

NSA_HEAD_DIM = 64
NSA_HEADS = 16
NSA_GROUPS = 4
NSA_REP = NSA_HEADS // NSA_GROUPS
NSA_ROT_DIM = 16
CMP_BLOCK = 32
CMP_STRIDE = 16
CMP_HIDDEN = 256
SEL_BLOCK = 64
SEL_TOPK = 16
WINDOW = 512
FORCE_SCORE = 1e6
NSA_KV = NSA_GROUPS * NSA_HEAD_DIM

MLSTM_HEADS = 8
MLSTM_QK_DIM = 64
MLSTM_V_DIM = 128
MLSTM_CHUNK = 64
MLSTM_CONV = 4

RWKV_HEAD_DIM = 64
RWKV_HEADS = 16
RWKV_GN_EPS = 64e-5
RWKV_CHUNK = 64
RWKV_INV_BLOCK = 16

FFN_DIM = 2816
FFN_CONV = 3
FFN_CHUNK = 256

LANES = 128
SUBLANES = 8
VMEM_LIMIT = 56 * 1024 * 1024


def _dot(a, b):
    return jnp.dot(a, b, preferred_element_type=F32)


def _dot_nt(a, b):
    return lax.dot_general(a, b, (((1,), (1,)), ((), ())), preferred_element_type=F32)


def _dot_tn(a, b):
    return lax.dot_general(a, b, (((0,), (0,)), ((), ())), preferred_element_type=F32)


def _split2(x):
    hi = x.astype(BF16)
    return hi, (x - hi.astype(F32)).astype(BF16)


def _split3(x):
    hi = x.astype(BF16)
    r1 = x - hi.astype(F32)
    mid = r1.astype(BF16)
    return hi, mid, (r1 - mid.astype(F32)).astype(BF16)


def _cumsum_rows(tri3, x):
    return _dot(tri3, jnp.concatenate(_split3(x), axis=0))


def _cumsum_lanes(x, triu3):
    return _dot(jnp.concatenate(_split3(x), axis=1), triu3)


def _rms(x, g):
    ms = jnp.mean(x * x, axis=-1, keepdims=True)
    return x * lax.rsqrt(ms + NORM_EPS) * g


def _sigmoid(x):
    return 1.0 / (1.0 + jnp.exp(-x))


def _cparams(sem):
    return pltpu.CompilerParams(dimension_semantics=sem, vmem_limit_bytes=VMEM_LIMIT)


def _const_spec(shape):
    n = len(shape)
    return pl.BlockSpec(shape, lambda *_: (0,) * n)


def _matmul_res_kernel(a_ref, w_ref, r_ref, o_ref):
    o_ref[...] = r_ref[...] + _dot(a_ref[...], w_ref[...])


def _matmul_res(a, w, res, tm=512):
    m, k = a.shape
    n = w.shape[1]
    return pl.pallas_call(
        _matmul_res_kernel,
        out_shape=jax.ShapeDtypeStruct((m, n), F32),
        grid=(m // tm,),
        in_specs=[pl.BlockSpec((tm, k), lambda i: (i, 0)), _const_spec((k, n)),
                  pl.BlockSpec((tm, n), lambda i: (i, 0))],
        out_specs=pl.BlockSpec((tm, n), lambda i: (i, 0)),
        compiler_params=_cparams(("parallel",)),
        name="matmul_res",
    )(a, w, res)


def _final_norm_kernel(x_ref, g_ref, o_ref):
    o_ref[...] = _rms(x_ref[...], g_ref[...])


def _final_norm(x, g, tm=1024):
    m, d = x.shape
    return pl.pallas_call(
        _final_norm_kernel,
        out_shape=jax.ShapeDtypeStruct((m, d), F32),
        grid=(m // tm,),
        in_specs=[pl.BlockSpec((tm, d), lambda i: (i, 0)), _const_spec((1, d))],
        out_specs=pl.BlockSpec((tm, d), lambda i: (i, 0)),
        compiler_params=_cparams(("parallel",)),
        name="final_norm",
    )(x, g.reshape(1, d))


def _ffn_kernel(x_ref, g_ref, wg_ref, wv_ref, cw_ref, cb_ref, wd_ref, o_ref,
                h_scr, carry_scr, acc_scr, *, tiles_per_seq, n_chunks):
    tm = x_ref.shape[0]
    x = x_ref[...]
    h_scr[...] = _rms(x, g_ref[...]).astype(BF16)
    seq_start = (pl.program_id(0) % tiles_per_seq) == 0
    rows = lax.broadcasted_iota(jnp.int32, (tm, FFN_CHUNK), 0)

    def chunk(c, carry):
        h = h_scr[...]
        gate = _dot(h, wg_ref[c])
        val = _dot(h, wv_ref[c])
        prev = carry_scr[c]
        prev = jnp.where(seq_start, 0.0, prev)
        p1 = prev[SUBLANES - 1:SUBLANES, :]
        p2 = prev[SUBLANES - 2:SUBLANES - 1, :]
        carry_scr[c] = gate[tm - SUBLANES:, :]
        g1 = jnp.where(rows == 0, p1, pltpu.roll(gate, 1, 0))
        g2 = jnp.where(rows == 0, p2, jnp.where(rows == 1, p1, pltpu.roll(gate, 2, 0)))
        cw = cw_ref[c]
        y = cw[2:3, :] * gate + cw[1:2, :] * g1 + cw[0:1, :] * g2 + cb_ref[c]
        act = (y * _sigmoid(y) * val).astype(BF16)
        contrib = _dot(act, wd_ref[c])

        @pl.when(c == 0)
        def _():
            acc_scr[...] = contrib

        @pl.when(c > 0)
        def _():
            acc_scr[...] += contrib

        return carry

    lax.fori_loop(0, n_chunks, chunk, 0)
    o_ref[...] = x + acc_scr[...]


def _ffn(x, g, w_up, conv_w, conv_b, w_down, seq_len, tm=512):
    m, d = x.shape
    nc = FFN_DIM // FFN_CHUNK
    wg = w_up[:, :FFN_DIM].astype(BF16).reshape(d, nc, FFN_CHUNK).transpose(1, 0, 2)
    wv = w_up[:, FFN_DIM:].astype(BF16).reshape(d, nc, FFN_CHUNK).transpose(1, 0, 2)
    cw = jnp.pad(conv_w, ((0, SUBLANES - FFN_CONV), (0, 0)))
    cw = cw.reshape(SUBLANES, nc, FFN_CHUNK).transpose(1, 0, 2)
    cb = conv_b.reshape(nc, 1, FFN_CHUNK)
    wd = w_down.astype(BF16).reshape(nc, FFN_CHUNK, d)
    kern = functools.partial(_ffn_kernel, tiles_per_seq=seq_len // tm, n_chunks=nc)
    return pl.pallas_call(
        kern,
        out_shape=jax.ShapeDtypeStruct((m, d), F32),
        grid=(m // tm,),
        in_specs=[pl.BlockSpec((tm, d), lambda i: (i, 0)), _const_spec((1, d)),
                  _const_spec((nc, d, FFN_CHUNK)), _const_spec((nc, d, FFN_CHUNK)),
                  _const_spec((nc, SUBLANES, FFN_CHUNK)), _const_spec((nc, 1, FFN_CHUNK)),
                  _const_spec((nc, FFN_CHUNK, d))],
        out_specs=pl.BlockSpec((tm, d), lambda i: (i, 0)),
        scratch_shapes=[pltpu.VMEM((tm, d), BF16),
                        pltpu.VMEM((nc, SUBLANES, FFN_CHUNK), F32),
                        pltpu.VMEM((tm, d), F32)],
        compiler_params=_cparams(("arbitrary",)),
        name="conv_ffn",
    )(x, g.reshape(1, d), wg, wv, cw, cb, wd)


def _nsa_proj_kernel(x_ref, g_ref, w_ref, bg_ref, rc_ref, rs1_ref, rs2_ref,
                     qp_ref, qr_ref, kc_ref, vc_ref, ks_ref, vs_ref, kw_ref, vw_ref, gate_ref):
    hn = _rms(x_ref[...], g_ref[...]).astype(BF16)
    y = _dot(hn, w_ref[...])
    rc, rs1, rs2 = rc_ref[...], rs1_ref[...], rs2_ref[...]
    dh = NSA_HEAD_DIM

    def rope(z):
        return z * rc + pltpu.roll(z, 8, 1) * rs1 + pltpu.roll(z, LANES - 8, 1) * rs2

    scale = dh ** -0.5
    for j in range(D_MODEL // LANES):
        q = y[:, j * LANES:(j + 1) * LANES] * scale
        qp_ref[:, j * LANES:(j + 1) * LANES] = q.astype(BF16)
        qr_ref[:, j * LANES:(j + 1) * LANES] = rope(q).astype(BF16)

    def kv_chunk(idx):
        return y[:, D_MODEL + idx * NSA_KV:D_MODEL + (idx + 1) * NSA_KV]

    def split_groups(z, ref, dtype):
        for g in range(NSA_GROUPS):
            ref[g] = z[:, g * dh:(g + 1) * dh].astype(dtype)

    def rope256(z):
        return jnp.concatenate([rope(z[:, :LANES]), rope(z[:, LANES:])], axis=1)

    split_groups(kv_chunk(0), kc_ref, F32)
    split_groups(kv_chunk(1), vc_ref, F32)
    split_groups(rope256(kv_chunk(2)), ks_ref, BF16)
    split_groups(kv_chunk(3), vs_ref, BF16)
    split_groups(rope256(kv_chunk(4)), kw_ref, BF16)
    split_groups(kv_chunk(5), vw_ref, BF16)
    gate = y[:, D_MODEL + 6 * NSA_KV:] + bg_ref[...]
    gate_ref[...] = _sigmoid(gate)


def _rope_tables(seq_len):
    half = NSA_ROT_DIM // 2
    inv_freq = ROPE_THETA ** (-jnp.arange(half, dtype=F32) / half)
    ang = jnp.arange(seq_len, dtype=F32)[:, None] * inv_freq[None, :]
    cos, sin = jnp.cos(ang), jnp.sin(ang)
    zeros = jnp.zeros((seq_len, NSA_HEAD_DIM - NSA_ROT_DIM), F32)
    z8 = jnp.zeros((seq_len, half), F32)
    rc = jnp.concatenate([cos, cos, zeros + 1.0], axis=1)
    rs1 = jnp.concatenate([z8, sin, zeros], axis=1)
    rs2 = jnp.concatenate([-sin, z8, zeros], axis=1)
    two = lambda t: jnp.concatenate([t, t], axis=1)
    return two(rc), two(rs1), two(rs2)


def _nsa_proj(x, g, w_in, b_gate, rope, seq_len, tm=512):
    m, d = x.shape
    n_kv = 6 * NSA_KV
    wg = w_in[:, D_MODEL + n_kv:].reshape(d, NSA_GROUPS, NSA_REP * 3)
    wg = jnp.pad(wg, ((0, 0), (0, 0), (0, LANES - NSA_REP * 3))).reshape(d, NSA_GROUPS * LANES)
    w = jnp.concatenate([w_in[:, :D_MODEL + n_kv], wg], axis=1).astype(BF16)
    bg = jnp.pad(b_gate.reshape(NSA_GROUPS, NSA_REP * 3), ((0, 0), (0, LANES - NSA_REP * 3)))
    bg = bg.reshape(1, NSA_GROUPS * LANES)
    n = w.shape[1]
    tps = seq_len // tm
    row = lambda i: (i, 0)
    rope_spec = pl.BlockSpec((tm, LANES), lambda i: (i % tps, 0))
    g_out = lambda dt: jax.ShapeDtypeStruct((NSA_GROUPS, m, NSA_HEAD_DIM), dt)
    g_spec = pl.BlockSpec((NSA_GROUPS, tm, NSA_HEAD_DIM), lambda i: (0, i, 0))
    return pl.pallas_call(
        _nsa_proj_kernel,
        out_shape=(jax.ShapeDtypeStruct((m, d), BF16), jax.ShapeDtypeStruct((m, d), BF16),
                   g_out(F32), g_out(F32), g_out(BF16), g_out(BF16), g_out(BF16), g_out(BF16),
                   jax.ShapeDtypeStruct((m, NSA_GROUPS * LANES), F32)),
        grid=(m // tm,),
        in_specs=[pl.BlockSpec((tm, d), row), _const_spec((1, d)), _const_spec((d, n)),
                  _const_spec((1, NSA_GROUPS * LANES)), rope_spec, rope_spec, rope_spec],
        out_specs=(pl.BlockSpec((tm, d), row), pl.BlockSpec((tm, d), row),
                   g_spec, g_spec, g_spec, g_spec, g_spec, g_spec,
                   pl.BlockSpec((tm, NSA_GROUPS * LANES), row)),
        compiler_params=_cparams(("parallel",)),
        name="nsa_proj",
    )(x, g.reshape(1, d), w, bg, *rope)


def _gelu_tanh(x):
    return 0.5 * x * (1.0 + jnp.tanh(math.sqrt(2.0 / math.pi) * (x + 0.044715 * (x * x * x))))


def _compress_kernel(zk_ref, zv_ref, pek_ref, pev_ref, w1k_ref, w2k_ref, w1v_ref, w2v_ref,
                     kc_ref, vc_ref):
    nrow = zk_ref.shape[0]
    rows = lax.broadcasted_iota(jnp.int32, (nrow, NSA_HEAD_DIM), 0)

    def one(z_ref, pe_ref, w1_ref, w2_ref, o_ref):
        z = z_ref[...]
        a = _dot((z + pe_ref[0:1, :]).astype(BF16), w1_ref[0])
        b = _dot((z + pe_ref[1:2, :]).astype(BF16), w1_ref[1])
        hid = a + pltpu.roll(b, nrow - 1, 0)
        out = _dot(_gelu_tanh(hid).astype(BF16), w2_ref[...])
        o_ref[...] = jnp.where(rows == nrow - 1, 0.0, out).astype(o_ref.dtype)

    one(zk_ref, pek_ref, w1k_ref, w2k_ref, kc_ref)
    one(zv_ref, pev_ref, w1v_ref, w2v_ref, vc_ref)


def _compress(kc_raw, vc_raw, pe_k, pe_v, w1k, w2k, w1v, w2v, seq_len):
    g, m, dh = kc_raw.shape
    half = CMP_STRIDE * dh
    nchunk = seq_len // CMP_STRIDE
    zk = kc_raw.reshape(g * m // CMP_STRIDE, half)
    zv = vc_raw.reshape(g * m // CMP_STRIDE, half)
    pe2 = lambda pe: pe.reshape(2, half)
    w1 = lambda w: w.astype(BF16).reshape(2, half, CMP_HIDDEN)
    nblk = zk.shape[0] // nchunk
    row = lambda i: (i, 0)
    return pl.pallas_call(
        _compress_kernel,
        out_shape=(jax.ShapeDtypeStruct((zk.shape[0], dh), BF16),
                   jax.ShapeDtypeStruct((zk.shape[0], dh), BF16)),
        grid=(nblk,),
        in_specs=[pl.BlockSpec((nchunk, half), row), pl.BlockSpec((nchunk, half), row),
                  _const_spec((2, half)), _const_spec((2, half)),
                  _const_spec((2, half, CMP_HIDDEN)), _const_spec((CMP_HIDDEN, dh)),
                  _const_spec((2, half, CMP_HIDDEN)), _const_spec((CMP_HIDDEN, dh))],
        out_specs=(pl.BlockSpec((nchunk, dh), row), pl.BlockSpec((nchunk, dh), row)),
        compiler_params=_cparams(("parallel",)),
        name="nsa_compress",
    )(zk, zv, pe2(pe_k), pe2(pe_v), w1(w1k), w2k.astype(BF16), w1(w1v), w2v.astype(BF16))


def _nsa_cmp_kernel(q_ref, kc_ref, vc_ref, gate_ref, ovt_ref, oc_ref, selt_ref, *, tq):
    qi = pl.program_id(2)
    dh = NSA_HEAD_DIM
    q = q_ref[...]
    qs = jnp.concatenate([q[:, r * dh:(r + 1) * dh] for r in range(NSA_REP)], axis=0)
    kc, vc = kc_ref[...], vc_ref[...]
    ncmp = kc.shape[0]
    t = qi * tq + lax.broadcasted_iota(jnp.int32, (ncmp, tq), 1)
    cmp_end = lax.broadcasted_iota(jnp.int32, (ncmp, tq), 0) * CMP_STRIDE + (CMP_BLOCK - 1)
    neg = jnp.where(cmp_end <= t, 0.0, -jnp.inf)
    st = _dot_nt(kc, qs) + jnp.concatenate([neg] * NSA_REP, axis=1)
    mx = jnp.max(st, axis=0, keepdims=True)
    mx = jnp.where(mx == -jnp.inf, 0.0, mx)
    e = jnp.exp(st - mx)
    p = e / jnp.maximum(jnp.sum(e, axis=0, keepdims=True), 1e-30)
    o_t = _dot_tn(vc, p.astype(BF16))
    o_t2 = jnp.concatenate([o_t, o_t], axis=0)
    gate = gate_ref[...]
    for r in range(NSA_REP):
        o_r = o_t2[:, r * tq:(r + 1) * tq].T[:, :dh]
        oc_ref[:, r * dh:(r + 1) * dh] = o_r * gate[:, 3 * r:3 * r + 1]
    psum = sum(p[:, r * tq:(r + 1) * tq] for r in range(NSA_REP))
    imp = _dot(ovt_ref[...], jnp.concatenate(_split3(psum), axis=0))
    nsel = imp.shape[0]
    blk = lax.broadcasted_iota(jnp.int32, (nsel, tq), 0)
    tb = (qi * tq + lax.broadcasted_iota(jnp.int32, (nsel, tq), 1)) // SEL_BLOCK
    forced = (blk == 0) | (blk == tb) | (blk == tb - 1)
    vals = jnp.where(forced, FORCE_SCORE, jnp.where(blk <= tb, imp, -1.0))
    blk_f = blk.astype(F32)
    sel = jnp.zeros((nsel, tq), F32)
    for _ in range(SEL_TOPK):
        top = jnp.max(vals, axis=0, keepdims=True)
        first = jnp.min(jnp.where(vals == top, blk_f, float(nsel)), axis=0, keepdims=True)
        pick = blk_f == first
        sel = jnp.where(pick, 1.0, sel)
        vals = jnp.where(pick, -jnp.inf, vals)
    selt_ref[...] = sel.astype(selt_ref.dtype)


def _overlap_matrix_t3(ncmp_pad, nsel):
    c = np.arange(ncmp_pad)[None, :]
    s = np.arange(nsel)[:, None]
    cmp_start = c * CMP_STRIDE
    cmp_end = cmp_start + CMP_BLOCK - 1
    blk_start = s * SEL_BLOCK
    ov = ((cmp_end >= blk_start) & (cmp_start <= blk_start + SEL_BLOCK - 1)).astype(np.float32)
    return jnp.asarray(np.concatenate([ov, ov, ov], axis=1), dtype=BF16)


def _nsa_cmp(qp, kc, vc, gates, batch, seq_len, tq=256):
    m, d = qp.shape
    nq = seq_len // tq
    ncmp = seq_len // CMP_STRIDE
    nsel = seq_len // SEL_BLOCK
    ovt3 = _overlap_matrix_t3(ncmp, nsel)
    qmap = lambda b, g, i: (b * nq + i, g)
    kmap = lambda b, g, i: (g * batch + b, 0)
    return pl.pallas_call(
        functools.partial(_nsa_cmp_kernel, tq=tq),
        out_shape=(jax.ShapeDtypeStruct((m, d), F32),
                   jax.ShapeDtypeStruct((NSA_GROUPS, nsel, m), BF16)),
        grid=(batch, NSA_GROUPS, nq),
        in_specs=[pl.BlockSpec((tq, NSA_KV), qmap),
                  pl.BlockSpec((ncmp, NSA_HEAD_DIM), kmap), pl.BlockSpec((ncmp, NSA_HEAD_DIM), kmap),
                  pl.BlockSpec((tq, LANES), qmap), _const_spec((nsel, 3 * ncmp))],
        out_specs=(pl.BlockSpec((tq, NSA_KV), qmap),
                   pl.BlockSpec((None, nsel, tq), lambda b, g, i: (g, 0, b * nq + i))),
        compiler_params=_cparams(("parallel", "parallel", "parallel")),
        name="nsa_cmp_topk",
    )(qp, kc, vc, gates, ovt3)


def _nsa_sel_kernel(q_ref, ks_ref, vs_ref, kw_ref, vw_ref, sel_ref, gate_ref, oc_ref, o_ref,
                    selt_scr, m_scr, l_scr, acc_scr, *, tq, tk):
    qi = pl.program_id(2)
    dh = NSA_HEAD_DIM
    cols = NSA_REP * tq
    q = q_ref[...]
    qs = jnp.concatenate([q[:, r * dh:(r + 1) * dh] for r in range(NSA_REP)], axis=0)
    selt_scr[...] = sel_ref[...].astype(F32)
    q0 = qi * tq
    key_iota = lax.broadcasted_iota(jnp.int32, (tk, tq), 0)
    t_pos = q0 + lax.broadcasted_iota(jnp.int32, (tk, tq), 1)
    nblk = tk // SEL_BLOCK

    m_scr[...] = jnp.full((1, cols), -jnp.inf, F32)
    l_scr[...] = jnp.zeros((1, cols), F32)
    acc_scr[...] = jnp.zeros((dh, cols), F32)

    def ktile(ki, carry):
        k0 = pl.multiple_of(ki * tk, tk)
        k = ks_ref[pl.ds(k0, tk), :]
        v = vs_ref[pl.ds(k0, tk), :]
        srow = selt_scr[pl.ds(pl.multiple_of(ki * nblk, nblk), nblk), :]
        picked = jnp.concatenate(
            [jnp.broadcast_to(srow[j:j + 1, :], (SEL_BLOCK, tq)) for j in range(nblk)], axis=0)
        neg = jnp.where(k0 + key_iota <= t_pos, jnp.where(picked > 0.5, 0.0, -jnp.inf), -jnp.inf)
        st = _dot_nt(k, qs) + jnp.concatenate([neg] * NSA_REP, axis=1)
        m_old = m_scr[...]
        m_new = jnp.maximum(m_old, jnp.max(st, axis=0, keepdims=True))
        m_safe = jnp.where(m_new == -jnp.inf, 0.0, m_new)
        alpha = jnp.exp(m_old - m_safe)
        p = jnp.exp(st - m_safe)
        l_scr[...] = alpha * l_scr[...] + jnp.sum(p, axis=0, keepdims=True)
        acc_scr[...] = alpha * acc_scr[...] + _dot_tn(v, p.astype(BF16))
        m_scr[...] = m_new
        return carry

    lax.fori_loop(0, (q0 + tq - 1) // tk + 1, ktile, 0)
    o_sel = acc_scr[...] / jnp.maximum(l_scr[...], 1e-30)

    wlen = tq + WINDOW
    w0 = pl.multiple_of(jnp.maximum(q0 - WINDOW, 0), tq)
    kw = kw_ref[pl.ds(w0, wlen), :]
    vw = vw_ref[pl.ds(w0, wlen), :]
    pw = w0 + lax.broadcasted_iota(jnp.int32, (wlen, tq), 0)
    tw = q0 + lax.broadcasted_iota(jnp.int32, (wlen, tq), 1)
    neg_w = jnp.where((pw <= tw) & (pw > tw - WINDOW), 0.0, -jnp.inf)
    sw = _dot_nt(kw, qs) + jnp.concatenate([neg_w] * NSA_REP, axis=1)
    mw = jnp.max(sw, axis=0, keepdims=True)
    mw = jnp.where(mw == -jnp.inf, 0.0, mw)
    ew = jnp.exp(sw - mw)
    o_win = _dot_tn(vw, ew.astype(BF16)) / jnp.maximum(jnp.sum(ew, axis=0, keepdims=True), 1e-30)

    o2 = jnp.concatenate([o_sel, o_win], axis=0)
    gate = gate_ref[...]
    oc = oc_ref[...]
    for r in range(NSA_REP):
        o2r = o2[:, r * tq:(r + 1) * tq].T
        tot = (oc[:, r * dh:(r + 1) * dh] + o2r[:, :dh] * gate[:, 3 * r + 1:3 * r + 2]
               + o2r[:, dh:] * gate[:, 3 * r + 2:3 * r + 3])
        o_ref[:, r * dh:(r + 1) * dh] = tot.astype(o_ref.dtype)


def _nsa_sel(qr, ks, vs, kw, vw, sel, gates, oc, batch, seq_len, tq=256, tk=512):
    m, d = qr.shape
    nq = seq_len // tq
    nsel = seq_len // SEL_BLOCK
    qmap = lambda b, g, i: (b * nq + i, g)
    kvmap = lambda b, g, i: (g, b, 0)
    kv_spec = pl.BlockSpec((None, seq_len, NSA_HEAD_DIM), kvmap)
    cols = NSA_REP * tq
    return pl.pallas_call(
        functools.partial(_nsa_sel_kernel, tq=tq, tk=tk),
        out_shape=jax.ShapeDtypeStruct((m, d), BF16),
        grid=(batch, NSA_GROUPS, nq),
        in_specs=[pl.BlockSpec((tq, NSA_KV), qmap), kv_spec, kv_spec, kv_spec, kv_spec,
                  pl.BlockSpec((None, nsel, tq), lambda b, g, i: (g, 0, b * nq + i)),
                  pl.BlockSpec((tq, LANES), qmap), pl.BlockSpec((tq, NSA_KV), qmap)],
        out_specs=pl.BlockSpec((tq, NSA_KV), qmap),
        scratch_shapes=[pltpu.VMEM((nsel, tq), F32), pltpu.VMEM((1, cols), F32),
                        pltpu.VMEM((1, cols), F32), pltpu.VMEM((NSA_HEAD_DIM, cols), F32)],
        compiler_params=_cparams(("parallel", "parallel", "parallel")),
        name="nsa_sel_win",
    )(qr, ks, vs, kw, vw, sel, gates, oc)


def _nsa_layer(x, norm_g, w_in, pe_k, pe_v, w1k, w2k, w1v, w2v, b_gate, w_out, rope, batch, seq_len):
    qp, qr, kc_raw, vc_raw, ks, vs, kw, vw, gates = _nsa_proj(x, norm_g, w_in, b_gate, rope, seq_len)
    kc, vc = _compress(kc_raw, vc_raw, pe_k, pe_v, w1k, w2k, w1v, w2v, seq_len)
    oc, sel = _nsa_cmp(qp, kc, vc, gates, batch, seq_len)
    o = _nsa_sel(qr, ks, vs, kw, vw, sel, gates, oc, batch, seq_len)
    return _matmul_res(o, w_out.astype(BF16), x)


def _mlstm_proj_kernel(x_ref, g_ref, w_ref, wt_ref, bcol_ref, brow_ref,
                       qk_ref, v_ref, o_ref, gc_ref, gr_ref):
    hn = _rms(x_ref[...], g_ref[...]).astype(BF16)
    y = _dot(hn, w_ref[...])
    d = D_MODEL
    qk_ref[...] = y[:, :d]
    v_ref[...] = y[:, d:2 * d].astype(BF16)
    o_ref[...] = _sigmoid(y[:, 2 * d:3 * d])
    gc_ref[...] = y[:, 3 * d:] + bcol_ref[...]
    gr_ref[...] = _dot_nt(wt_ref[...], hn) + brow_ref[...]


def _mlstm_proj(x, g, w_in, b_gates, tm=512):
    m, d = x.shape
    h, dk = MLSTM_HEADS, MLSTM_QK_DIM
    wq = w_in[:, :h * dk].reshape(d, h, dk)
    wk = w_in[:, h * dk:2 * h * dk].reshape(d, h, dk)
    wqk = jnp.concatenate([wq, wk], axis=2).reshape(d, 2 * h * dk)
    wv = w_in[:, d:2 * d]
    wif = w_in[:, 2 * d:2 * d + 2 * h]
    wo = w_in[:, 2 * d + 2 * h:]
    w = jnp.concatenate([wqk, wv, wo, jnp.pad(wif, ((0, 0), (0, LANES - 2 * h)))], axis=1).astype(BF16)
    wt = wif.T.astype(BF16)
    bcol = jnp.pad(b_gates, (0, LANES - 2 * h)).reshape(1, LANES)
    brow = b_gates.reshape(2 * h, 1)
    n = w.shape[1]
    row = lambda i: (i, 0)
    return pl.pallas_call(
        _mlstm_proj_kernel,
        out_shape=(jax.ShapeDtypeStruct((m, d), F32), jax.ShapeDtypeStruct((m, d), BF16),
                   jax.ShapeDtypeStruct((m, d), F32), jax.ShapeDtypeStruct((m, LANES), F32),
                   jax.ShapeDtypeStruct((2 * h, m), F32)),
        grid=(m // tm,),
        in_specs=[pl.BlockSpec((tm, d), row), _const_spec((1, d)), _const_spec((d, n)),
                  _const_spec((2 * h, d)), _const_spec((1, LANES)), _const_spec((2 * h, 1))],
        out_specs=(pl.BlockSpec((tm, d), row), pl.BlockSpec((tm, d), row), pl.BlockSpec((tm, d), row),
                   pl.BlockSpec((tm, LANES), row), pl.BlockSpec((2 * h, tm), lambda i: (0, i))),
        compiler_params=_cparams(("parallel",)),
        name="mlstm_proj",
    )(x, g.reshape(1, d), w, wt, bcol, brow)


def _log_sigmoid(x):
    return jnp.minimum(x, 0.0) - jnp.log(1.0 + jnp.exp(-jnp.abs(x)))


def _mlstm_core_kernel(qk_ref, v_ref, og_ref, gc_ref, gr_ref, cw_ref, cb_ref, ng_ref, o_ref,
                       qkc_scr, prev_scr, c_scr, n_scr, m_scr, *, tt):
    L = MLSTM_CHUNK
    H, dk, dv = MLSTM_HEADS, MLSTM_QK_DIM, MLSTM_V_DIM
    seq_start = pl.program_id(1) == 0

    @pl.when(seq_start)
    def _():
        c_scr[...] = jnp.zeros_like(c_scr)
        n_scr[...] = jnp.zeros_like(n_scr)
        m_scr[...] = jnp.zeros_like(m_scr)
        prev_scr[...] = jnp.zeros_like(prev_scr)

    raw = qk_ref[...]
    prev = prev_scr[...]
    prev_scr[...] = raw[tt - SUBLANES:, :]
    rows = lax.broadcasted_iota(jnp.int32, raw.shape, 0)
    cw = cw_ref[...]
    acc = cw[MLSTM_CONV - 1:MLSTM_CONV, :] * raw + cb_ref[...]
    for s in range(1, MLSTM_CONV):
        shifted = pltpu.roll(raw, s, 0)
        for j in range(s):
            shifted = jnp.where(rows == j, prev[SUBLANES - s + j:SUBLANES - s + j + 1, :], shifted)
        acc = acc + cw[MLSTM_CONV - 1 - s:MLSTM_CONV - s, :] * shifted
    act = acc * _sigmoid(acc)
    lane = lax.broadcasted_iota(jnp.int32, raw.shape, 1)
    qkc_scr[...] = jnp.where(lane % (2 * dk) >= dk, act * dk ** -0.5, act)

    tri_r = lax.broadcasted_iota(jnp.int32, (L, L), 0)
    tri_c = lax.broadcasted_iota(jnp.int32, (L, L), 1)
    tri_l = tri_r >= tri_c
    tri3 = jnp.concatenate([tri_l.astype(BF16)] * 3, axis=1)
    triu3 = jnp.concatenate([(tri_r <= tri_c).astype(BF16)] * 3, axis=0)

    def chunk(c, carry):
        r0 = pl.multiple_of(c * L, L)
        gcol = gc_ref[pl.ds(r0, L), :]
        grow = gr_ref[c]
        b_col = _cumsum_rows(tri3, _log_sigmoid(gcol))
        b_row = _cumsum_lanes(_log_sigmoid(grow), triu3)
        hs = range(H)
        vsl = [slice(h * dv, (h + 1) * dv) for h in hs]
        qk = [qkc_scr[pl.ds(r0, L), h * 2 * dk:(h + 1) * 2 * dk] for h in hs]
        qf = [qk[h][:, :dk] for h in hs]
        q = [qf[h].astype(BF16) for h in hs]
        k = [qk[h][:, dk:] for h in hs]
        v = [v_ref[pl.ds(r0, L), vsl[h]] for h in hs]
        li_c = [gcol[:, h:h + 1] for h in hs]
        b_c = [b_col[:, H + h:H + h + 1] for h in hs]
        li_r = [grow[h:h + 1, :] for h in hs]
        b_r = [b_row[H + h:H + h + 1, :] for h in hs]
        b_end = [b_r[h][:, L - 1:L] for h in hs]
        m_prev = [m_scr[h:h + 1, 0:1] for h in hs]
        c_prev = [c_scr[h] for h in hs]
        n_prev = [n_scr[h:h + 1, :] for h in hs]
        dmat = [jnp.where(tri_l, b_c[h] - b_r[h] + li_r[h], -jnp.inf) for h in hs]
        m_inter = [b_c[h] + m_prev[h] for h in hs]
        m_t = [jnp.maximum(m_inter[h], jnp.max(dmat[h], axis=-1, keepdims=True)) for h in hs]
        s_qk = [_dot_nt(q[h], k[h].astype(BF16)) for h in hs]
        q_c = [_dot(q[h], c_prev[h].astype(BF16)) for h in hs]
        att = [jnp.exp(dmat[h] - m_t[h]) * s_qk[h] for h in hs]
        inter = [jnp.exp(m_inter[h] - m_t[h]) for h in hs]
        num = [_dot(att[h].astype(BF16), v[h]) + inter[h] * q_c[h] for h in hs]
        den = [jnp.sum(att[h], axis=-1, keepdims=True)
               + inter[h] * jnp.sum(qf[h] * n_prev[h], axis=-1, keepdims=True) for h in hs]
        h_t = [num[h] / jnp.maximum(jnp.abs(den[h]), jnp.exp(-m_t[h])) for h in hs]
        h_t = [h_t[h] * lax.rsqrt(jnp.mean(h_t[h] * h_t[h], axis=-1, keepdims=True) + NORM_EPS) for h in hs]
        g_end_r = [b_end[h] - b_r[h] + li_r[h] for h in hs]
        g_max = [jnp.max(g_end_r[h], axis=-1, keepdims=True) for h in hs]
        kw = [k[h] * jnp.exp(b_end[h] - b_c[h] + li_c[h] - g_max[h]) for h in hs]
        c_loc = [_dot_tn(kw[h].astype(BF16), v[h]) for h in hs]
        n_loc = [jnp.sum(kw[h], axis=0, keepdims=True) for h in hs]
        m_new = [jnp.maximum(b_end[h] + m_prev[h], g_max[h]) for h in hs]
        a = [jnp.exp(b_end[h] + m_prev[h] - m_new[h]) for h in hs]
        sc = [jnp.exp(g_max[h] - m_new[h]) for h in hs]
        for h in hs:
            out = h_t[h] * ng_ref[:, vsl[h]] * og_ref[pl.ds(r0, L), vsl[h]]
            o_ref[pl.ds(r0, L), vsl[h]] = out.astype(o_ref.dtype)
            c_scr[h] = a[h] * c_prev[h] + sc[h] * c_loc[h]
            n_scr[h:h + 1, :] = a[h] * n_prev[h] + sc[h] * n_loc[h]
            m_scr[h:h + 1, :] = jnp.broadcast_to(m_new[h], (1, LANES))
        return carry

    lax.fori_loop(0, tt // L, chunk, 0)


def _mlstm_core(qk, v, og, gcol, grow3, conv_w, conv_b, norm_g, batch, seq_len, tt=256):
    m, d = qk.shape
    H, dk, dv = MLSTM_HEADS, MLSTM_QK_DIM, MLSTM_V_DIM
    nt = seq_len // tt
    ncs = tt // MLSTM_CHUNK
    row = lambda b, i: (b * nt + i, 0)
    return pl.pallas_call(
        functools.partial(_mlstm_core_kernel, tt=tt),
        out_shape=jax.ShapeDtypeStruct((m, d), BF16),
        grid=(batch, nt),
        in_specs=[pl.BlockSpec((tt, d), row), pl.BlockSpec((tt, d), row), pl.BlockSpec((tt, d), row),
                  pl.BlockSpec((tt, LANES), row),
                  pl.BlockSpec((ncs, 2 * H, MLSTM_CHUNK), lambda b, i: (b * nt + i, 0, 0)),
                  _const_spec((SUBLANES, d)), _const_spec((1, d)), _const_spec((1, d))],
        out_specs=pl.BlockSpec((tt, d), row),
        scratch_shapes=[pltpu.VMEM((tt, d), F32), pltpu.VMEM((SUBLANES, d), F32),
                        pltpu.VMEM((H, dk, dv), F32), pltpu.VMEM((H, dk), F32),
                        pltpu.VMEM((H, LANES), F32)],
        compiler_params=_cparams(("arbitrary", "arbitrary")),
        name="mlstm_core",
    )(qk, v, og, gcol, grow3, conv_w, conv_b, norm_g)


def _mlstm_layer(x, norm_g, w_in, conv_w, conv_b, b_gates, hnorm, w_out, batch, seq_len):
    H, dk = MLSTM_HEADS, MLSTM_QK_DIM
    qk, v, og, gcol, grow = _mlstm_proj(x, norm_g, w_in, b_gates)
    m = x.shape[0]
    grow3 = grow.reshape(2 * H, m // MLSTM_CHUNK, MLSTM_CHUNK).transpose(1, 0, 2)

    def perm(z):
        lead = z.shape[:-1]
        zq = z[..., :H * dk].reshape(*lead, H, dk)
        zk = z[..., H * dk:].reshape(*lead, H, dk)
        return jnp.concatenate([zq, zk], axis=-1).reshape(*lead, 2 * H * dk)

    cw = jnp.pad(perm(conv_w), ((0, SUBLANES - MLSTM_CONV), (0, 0)))
    o = _mlstm_core(qk, v, og, gcol, grow3, cw, perm(conv_b).reshape(1, -1), hnorm.reshape(1, -1),
                    batch, seq_len)
    return _matmul_res(o, w_out.astype(BF16), x)


def _softplus(x):
    return jnp.maximum(x, 0.0) + jnp.log(1.0 + jnp.exp(-jnp.abs(x)))


def _rwkv_proj_kernel(x_ref, g_ref, mu_ref, wr_ref, wk_ref, wv_ref, ww1_ref, ww2_ref, w0_ref,
                      aw1_ref, aw2_ref, a0_ref, gw1_ref, gw2_ref,
                      r_ref, k_ref, v_ref, lw_ref, a_ref, go_ref, prev_scr, *, tiles_per_seq):
    tm = x_ref.shape[0]
    h = _rms(x_ref[...], g_ref[...])
    seq_start = (pl.program_id(0) % tiles_per_seq) == 0
    prev = jnp.where(seq_start, 0.0, prev_scr[SUBLANES - 1:SUBLANES, :])
    prev_scr[...] = h[tm - SUBLANES:, :]
    rows = lax.broadcasted_iota(jnp.int32, h.shape, 0)
    xx = jnp.where(rows == 0, prev, pltpu.roll(h, 1, 0)) - h
    mix = lambda j: (h + xx * mu_ref[j:j + 1, :]).astype(BF16)
    r_ref[...] = _dot(mix(0), wr_ref[...])
    k_ref[...] = _dot(mix(2), wk_ref[...])
    v_ref[...] = _dot(mix(3), wv_ref[...])
    wl = _dot(jnp.tanh(_dot(mix(1), ww1_ref[...])).astype(BF16), ww2_ref[...]) + w0_ref[...]
    w_log = -_softplus(-wl) - 0.5
    lw_ref[...] = -jnp.exp(w_log)
    al = _dot(_dot(mix(4), aw1_ref[...]).astype(BF16), aw2_ref[...]) + a0_ref[...]
    a_ref[...] = _sigmoid(al)
    go_ref[...] = _dot(_sigmoid(_dot(mix(5), gw1_ref[...])).astype(BF16), gw2_ref[...])


def _rwkv_proj(x, g, mu, w_r, w_k, w_v, w0, w_w1, w_w2, a0, a_w1, a_w2, g_w1, g_w2, seq_len, tm=256):
    m, d = x.shape
    row = lambda i: (i, 0)
    bf = lambda w: w.astype(BF16)
    mu8 = jnp.pad(mu, ((0, SUBLANES - mu.shape[0]), (0, 0)))
    consts = [g.reshape(1, d), mu8, bf(w_r), bf(w_k), bf(w_v), bf(w_w1), bf(w_w2), w0.reshape(1, d),
              bf(a_w1), bf(a_w2), a0.reshape(1, d), bf(g_w1), bf(g_w2)]
    out = jax.ShapeDtypeStruct((m, d), F32)
    return pl.pallas_call(
        functools.partial(_rwkv_proj_kernel, tiles_per_seq=seq_len // tm),
        out_shape=(out,) * 6,
        grid=(m // tm,),
        in_specs=[pl.BlockSpec((tm, d), row)] + [_const_spec(c.shape) for c in consts],
        out_specs=(pl.BlockSpec((tm, d), row),) * 6,
        scratch_shapes=[pltpu.VMEM((SUBLANES, d), F32)],
        compiler_params=_cparams(("arbitrary",)),
        name="rwkv_proj",
    )(x, *consts)


def _block_diag(x, lo):
    zero = jnp.zeros_like(x)
    return jnp.concatenate([jnp.where(lo, x, zero), jnp.where(lo, zero, x)], axis=0)


class _PairMat:
    def __init__(self, x, lo):
        self.x, self.lo = x, lo
        self._lhs = self._rhs = None

    def lhs(self):
        if self._lhs is None:
            hi, lo = _split2(self.x)
            self._lhs = jnp.concatenate([hi, lo, hi], axis=1)
        return self._lhs

    def rhs(self):
        if self._rhs is None:
            hi, lo = _split2(self.x)
            bh = _block_diag(hi, self.lo)
            self._rhs = jnp.concatenate([bh, bh, _block_diag(lo, self.lo)], axis=0)
        return self._rhs


def _pair_mm(p, q):
    return _dot(p.lhs(), q.rhs())


def _rwkv_core_kernel(r_ref, k_ref, v_ref, lw_ref, a_ref, go_ref, kk_ref, ka_ref, rk_ref,
                      lnw_ref, lnb_ref, o_ref, z_scr, *, tt):
    L, N = RWKV_CHUNK, RWKV_HEAD_DIM
    npair = z_scr.shape[0]
    pairs = range(npair)

    @pl.when(pl.program_id(1) == 0)
    def _():
        z_scr[...] = jnp.zeros_like(z_scr)

    ri = lax.broadcasted_iota(jnp.int32, (L, LANES), 0)
    ln = lax.broadcasted_iota(jnp.int32, (L, LANES), 1)
    si = ln % N
    lo = ln < N
    lower_incl = ri >= si
    lower_strict = ri > si
    blk_diag = (ri // RWKV_INV_BLOCK) == (si // RWKV_INV_BLOCK)
    eye = (ri == si).astype(F32)
    tri = (lax.broadcasted_iota(jnp.int32, (L, L), 0) >= lax.broadcasted_iota(jnp.int32, (L, L), 1))
    tri3 = jnp.concatenate([tri.astype(BF16)] * 3, axis=1)
    z_mask = ((lax.broadcasted_iota(jnp.int32, (2 * N, LANES), 0) // N)
              == (lax.broadcasted_iota(jnp.int32, (2 * N, LANES), 1) // N))

    def half_sum(x):
        s0 = jnp.sum(jnp.where(lo, x, 0.0), axis=-1, keepdims=True)
        s1 = jnp.sum(jnp.where(lo, 0.0, x), axis=-1, keepdims=True)
        return jnp.where(lo, s0, s1)

    bd = lambda x: _block_diag(x, lo)
    mk = lambda xs: [_PairMat(x, lo) for x in xs]
    mm = lambda ps, qs: [_pair_mm(p, q) for p, q in zip(ps, qs)]

    def chunk(c, carry):
        r0 = pl.multiple_of(c * L, L)
        sl = [slice(p * LANES, (p + 1) * LANES) for p in pairs]
        ld = lambda ref: [ref[pl.ds(r0, L), s] for s in sl]
        r, k, v, lw, a = ld(r_ref), ld(k_ref), ld(v_ref), ld(lw_ref), ld(a_ref)
        kk = [k[p] * kk_ref[:, sl[p]] for p in pairs]
        kk = [kk[p] / jnp.maximum(jnp.sqrt(half_sum(kk[p] * kk[p])), 1e-12) for p in pairs]
        km = [k[p] * (1.0 + (a[p] - 1.0) * ka_ref[:, sl[p]]) for p in pairs]
        bv = [kk[p] * a[p] for p in pairs]
        cum = [_cumsum_rows(tri3, lw[p]) for p in pairs]
        cum_end = [cum[p][L - 1:L, :] for p in pairs]
        w_inv = [jnp.exp(-cum[p]) for p in pairs]
        w_out = [jnp.exp(cum_end[p] - cum[p]) for p in pairs]
        kk_h = [(kk[p] * jnp.exp(cum[p] - lw[p])).astype(BF16) for p in pairs]
        r_h = [(r[p] * jnp.exp(cum[p])).astype(BF16) for p in pairs]
        b_t = [(bv[p] * w_inv[p]).astype(BF16) for p in pairs]
        k_t = [(km[p] * w_inv[p]).astype(BF16) for p in pairs]
        bbar = [(bv[p] * w_out[p]).astype(BF16) for p in pairs]
        kbar = [(km[p] * w_out[p]).astype(BF16) for p in pairs]
        vb = [v[p].astype(BF16) for p in pairs]
        lhs = [jnp.concatenate([kk_h[p], r_h[p]], axis=0) for p in pairs]
        ab = [_dot_nt(lhs[p], bd(b_t[p])) for p in pairs]
        ak = [_dot_nt(lhs[p], bd(k_t[p])) for p in pairs]
        a_ub = [jnp.where(lower_strict, ab[p][:L], 0.0) for p in pairs]
        a_rb = [jnp.where(lower_incl, ab[p][L:], 0.0).astype(BF16) for p in pairs]
        a_uk = [jnp.where(lower_strict, ak[p][:L], 0.0).astype(BF16) for p in pairs]
        a_rk = [jnp.where(lower_incl, ak[p][L:], 0.0).astype(BF16) for p in pairs]

        n1 = mk([jnp.where(blk_diag, -a_ub[p], 0.0) for p in pairs])
        n2 = mk(mm(n1, n1))
        n4 = mk(mm(n2, n2))
        n8 = mk(mm(n4, n4))
        acc = mk([eye + n1[p].x for p in pairs])
        for nk in (n2, n4):
            prod = mm(acc, nk)
            acc = mk([acc[p].x + prod[p] for p in pairs])
        prod = mm(acc, n8)
        d_inv = mk([acc[p].x + prod[p] for p in pairs])
        l_off = mk([jnp.where(blk_diag, 0.0, a_ub[p]) for p in pairs])
        e1 = mk([-x for x in mm(d_inv, l_off)])
        e2 = mk(mm(e1, e1))
        qm = mk([eye + e1[p].x for p in pairs])
        prod = mm(qm, e2)
        qm = mk([qm[p].x + prod[p] for p in pairs])
        t_inv = mk(mm(qm, d_inv))

        z = [z_scr[p] for p in pairs]
        zb = [z[p].astype(BF16) for p in pairs]
        bdv = [bd(vb[p]) for p in pairs]
        rhs_u = mk([_dot_nt(kk_h[p], zb[p]) + _dot(a_uk[p], bdv[p]) for p in pairs])
        u = [-x for x in mm(t_inv, rhs_u)]
        ub = [u[p].astype(BF16) for p in pairs]
        y = [_dot_nt(r_h[p], zb[p])
             + _dot(jnp.concatenate([a_rb[p], a_rk[p]], axis=1),
                    jnp.concatenate([bd(ub[p]), bdv[p]], axis=0)) for p in pairs]
        upd = [_dot_tn(jnp.concatenate([ub[p], vb[p]], axis=0),
                       jnp.concatenate([bbar[p], kbar[p]], axis=0)) for p in pairs]
        for p in pairs:
            z_scr[p] = z[p] * jnp.exp(cum_end[p]) + jnp.where(z_mask, upd[p], 0.0)

        inv_n = 1.0 / N
        mean = [half_sum(y[p]) * inv_n for p in pairs]
        yc = [y[p] - mean[p] for p in pairs]
        var = [half_sum(yc[p] * yc[p]) * inv_n for p in pairs]
        bonus = [half_sum(r[p] * km[p] * rk_ref[:, sl[p]]) * v[p] for p in pairs]
        for p in pairs:
            yn = yc[p] * lax.rsqrt(var[p] + RWKV_GN_EPS) * lnw_ref[:, sl[p]] + lnb_ref[:, sl[p]]
            o_ref[pl.ds(r0, L), sl[p]] = ((yn + bonus[p]) * go_ref[pl.ds(r0, L), sl[p]]).astype(o_ref.dtype)
        return carry

    lax.fori_loop(0, tt // L, chunk, 0)


def _rwkv_core(r, k, v, lw, a, go, k_k, k_a, r_k, ln_w, ln_b, batch, seq_len, tt=256):
    m, d = r.shape
    nt = seq_len // tt
    npair = d // LANES
    blk = pl.BlockSpec((tt, d), lambda b, i: (b * nt + i, 0))
    par = _const_spec((1, d))
    row1 = lambda z: z.reshape(1, d)
    return pl.pallas_call(
        functools.partial(_rwkv_core_kernel, tt=tt),
        out_shape=jax.ShapeDtypeStruct((m, d), BF16),
        grid=(batch, nt),
        in_specs=[blk] * 6 + [par] * 5,
        out_specs=blk,
        scratch_shapes=[pltpu.VMEM((npair, 2 * RWKV_HEAD_DIM, LANES), F32)],
        compiler_params=_cparams(("arbitrary", "arbitrary")),
        name="rwkv_core",
    )(r, k, v, lw, a, go, row1(k_k), row1(k_a), row1(r_k), row1(ln_w), row1(ln_b))


def _rwkv_layer(x, norm_g, mu, w_r, w_k, w_v, w_o, w0, w_w1, w_w2, a0, a_w1, a_w2, g_w1, g_w2,
                k_k, k_a, r_k, ln_w, ln_b, batch, seq_len):
    r, k, v, lw, a, go = _rwkv_proj(x, norm_g, mu, w_r, w_k, w_v, w0, w_w1, w_w2, a0, a_w1, a_w2,
                                    g_w1, g_w2, seq_len)
    o = _rwkv_core(r, k, v, lw, a, go, k_k, k_a, r_k, ln_w, ln_b, batch, seq_len)
    return _matmul_res(o, w_o.astype(BF16), x)


def kernel(x, norm_mixer, norm_ffn, ffn_w_up, ffn_conv_w, ffn_conv_b, ffn_w_down, nsa_w_in, nsa_pe_k, nsa_pe_v, nsa_cmp_k_w1, nsa_cmp_k_w2, nsa_cmp_v_w1, nsa_cmp_v_w2, nsa_b_gate, nsa_w_out, mlstm_w_in, mlstm_conv_w, mlstm_conv_b, mlstm_b_gates, mlstm_norm, mlstm_w_out, rwkv_mu, rwkv_w_r, rwkv_w_k, rwkv_w_v, rwkv_w_o, rwkv_w0, rwkv_w_w1, rwkv_w_w2, rwkv_a0, rwkv_a_w1, rwkv_a_w2, rwkv_g_w1, rwkv_g_w2, rwkv_k_k, rwkv_k_a, rwkv_r_k, rwkv_ln_w, rwkv_ln_b, final_norm):
    batch, seq_len, d = x.shape
    depth = norm_mixer.shape[0]
    rope = _rope_tables(seq_len)
    xf = x.reshape(batch * seq_len, d)
    for i in range(depth):
        kind, j = i % 3, i // 3
        if kind == 0:
            xf = _nsa_layer(xf, norm_mixer[i], nsa_w_in[j], nsa_pe_k[j], nsa_pe_v[j], nsa_cmp_k_w1[j],
                            nsa_cmp_k_w2[j], nsa_cmp_v_w1[j], nsa_cmp_v_w2[j], nsa_b_gate[j],
                            nsa_w_out[j], rope, batch, seq_len)
        elif kind == 1:
            xf = _mlstm_layer(xf, norm_mixer[i], mlstm_w_in[j], mlstm_conv_w[j], mlstm_conv_b[j],
                              mlstm_b_gates[j], mlstm_norm[j], mlstm_w_out[j], batch, seq_len)
        else:
            xf = _rwkv_layer(xf, norm_mixer[i], rwkv_mu[j], rwkv_w_r[j], rwkv_w_k[j], rwkv_w_v[j],
                             rwkv_w_o[j], rwkv_w0[j], rwkv_w_w1[j], rwkv_w_w2[j], rwkv_a0[j],
                             rwkv_a_w1[j], rwkv_a_w2[j], rwkv_g_w1[j], rwkv_g_w2[j], rwkv_k_k[j],
                             rwkv_k_a[j], rwkv_r_k[j], rwkv_ln_w[j], rwkv_ln_b[j], batch, seq_len)
        xf = _ffn(xf, norm_ffn[i], ffn_w_up[i], ffn_conv_w[i], ffn_conv_b[i], ffn_w_down[i], seq_len)
    return _final_norm(xf, final_norm).reshape(batch, seq_len, d)
```

```python
import functools
import math

import jax
import jax.numpy as jnp
import numpy as np
from jax import lax
from jax.experimental import pallas as pl
from jax.experimental.pallas import tpu as pltpu

F32 = jnp.float32
BF16 = jnp.bfloat16

D_MODEL = 1024
DEPTH = 4
NORM_EPS = 1e-6
ROPE_THETA = 500000.0

NSA_HEAD_DIM = 64
NSA_HEADS = 16
NSA_GROUPS = 4
NSA_REP = NSA_HEADS // NSA_GROUPS
NSA_ROT_DIM = 16
CMP_BLOCK = 32
CMP_STRIDE = 16
CMP_HIDDEN = 256
SEL_BLOCK = 64
SEL_TOPK = 16
WINDOW = 512
FORCE_SCORE = 1e6
NSA_KV = NSA_GROUPS * NSA_HEAD_DIM

MLSTM_HEADS = 8
MLSTM_QK_DIM = 64
MLSTM_V_DIM = 128
MLSTM_CHUNK = 64
MLSTM_CONV = 4

RWKV_HEAD_DIM = 64
RWKV_HEADS = 16
RWKV_GN_EPS = 64e-5
RWKV_CHUNK = 64
RWKV_INV_BLOCK = 16

FFN_DIM = 2816
FFN_CONV = 3
FFN_CHUNK = 256

LOG2E = math.log2(math.e)
MASKED = -1e30

LANES = 128
SUBLANES = 8
VMEM_LIMIT = 56 * 1024 * 1024


def _dot(a, b):
    return jnp.dot(a, b, preferred_element_type=F32)


def _dot_nt(a, b):
    return lax.dot_general(a, b, (((1,), (1,)), ((), ())), preferred_element_type=F32)


def _dot_tn(a, b):
    return lax.dot_general(a, b, (((0,), (0,)), ((), ())), preferred_element_type=F32)


def _split2(x):
    hi = x.astype(BF16)
    return hi, (x - hi.astype(F32)).astype(BF16)


def _split3(x):
    hi = x.astype(BF16)
    r1 = x - hi.astype(F32)
    mid = r1.astype(BF16)
    return hi, mid, (r1 - mid.astype(F32)).astype(BF16)


def _cumsum_rows(tri3, x):
    return _dot(tri3, jnp.concatenate(_split3(x), axis=0))


def _cumsum_lanes(x, triu3):
    return _dot(jnp.concatenate(_split3(x), axis=1), triu3)


def _rms(x, g):
    ms = jnp.mean(x * x, axis=-1, keepdims=True)
    return x * lax.rsqrt(ms + NORM_EPS) * g


def _sigmoid(x):
    return 1.0 / (1.0 + jnp.exp(-x))


def _cparams(sem):
    return pltpu.CompilerParams(dimension_semantics=sem, vmem_limit_bytes=VMEM_LIMIT)


def _const_spec(shape):
    n = len(shape)
    return pl.BlockSpec(shape, lambda *_: (0,) * n, pipeline_mode=pl.Buffered(1))


def _matmul_res_kernel(a_ref, w_ref, r_ref, o_ref):
    o_ref[...] = r_ref[...] + _dot(a_ref[...], w_ref[...])


def _matmul_res(a, w, res, tm=512):
    m, k = a.shape
    n = w.shape[1]
    return pl.pallas_call(
        _matmul_res_kernel,
        out_shape=jax.ShapeDtypeStruct((m, n), F32),
        grid=(m // tm,),
        in_specs=[pl.BlockSpec((tm, k), lambda i: (i, 0)), _const_spec((k, n)),
                  pl.BlockSpec((tm, n), lambda i: (i, 0))],
        out_specs=pl.BlockSpec((tm, n), lambda i: (i, 0)),
        compiler_params=_cparams(("parallel",)),
        name="matmul_res",
    )(a, w, res)


def _final_norm_kernel(x_ref, g_ref, o_ref):
    o_ref[...] = _rms(x_ref[...], g_ref[...])


def _final_norm(x, g, tm=1024):
    m, d = x.shape
    return pl.pallas_call(
        _final_norm_kernel,
        out_shape=jax.ShapeDtypeStruct((m, d), F32),
        grid=(m // tm,),
        in_specs=[pl.BlockSpec((tm, d), lambda i: (i, 0)), _const_spec((1, d))],
        out_specs=pl.BlockSpec((tm, d), lambda i: (i, 0)),
        compiler_params=_cparams(("parallel",)),
        name="final_norm",
    )(x, g.reshape(1, d))


def _ffn_kernel(x_ref, g_ref, wg_ref, wv_ref, cw_ref, cb_ref, wd_ref, o_ref,
                h_scr, carry_scr, *, tiles_per_seq, n_chunks):
    tm = x_ref.shape[0]
    x = x_ref[...]
    h_scr[...] = _rms(x, g_ref[...]).astype(BF16)
    o_ref[...] = x
    seq_start = (pl.program_id(0) % tiles_per_seq) == 0
    rows = lax.broadcasted_iota(jnp.int32, (tm, FFN_CHUNK), 0)

    def up(c):
        h = h_scr[...]
        return _dot(h, wg_ref[c]), _dot(h, wv_ref[c])

    def down(c, gate, val):
        prev = carry_scr[c]
        prev = jnp.where(seq_start, 0.0, prev)
        p1 = prev[SUBLANES - 1:SUBLANES, :]
        p2 = prev[SUBLANES - 2:SUBLANES - 1, :]
        carry_scr[c] = gate[tm - SUBLANES:, :]
        g1 = jnp.where(rows == 0, p1, pltpu.roll(gate, 1, 0))
        g2 = jnp.where(rows == 0, p2, jnp.where(rows == 1, p1, pltpu.roll(gate, 2, 0)))
        cw = cw_ref[c]
        y = cw[2:3, :] * gate + cw[1:2, :] * g1 + cw[0:1, :] * g2 + cb_ref[c]
        act = (y * _sigmoid(y) * val).astype(BF16)
        o_ref[...] += _dot(act, wd_ref[c])

    def step(c, carry):
        nxt = up(c + 1)
        down(c, *carry)
        return nxt

    last = lax.fori_loop(0, n_chunks - 1, step, up(0))
    down(n_chunks - 1, *last)


def _ffn(x, g, w_up, conv_w, conv_b, w_down, seq_len, tm=1024):
    m, d = x.shape
    nc = FFN_DIM // FFN_CHUNK
    wg = w_up[:, :FFN_DIM].astype(BF16).reshape(d, nc, FFN_CHUNK).transpose(1, 0, 2)
    wv = w_up[:, FFN_DIM:].astype(BF16).reshape(d, nc, FFN_CHUNK).transpose(1, 0, 2)
    cw = jnp.pad(conv_w, ((0, SUBLANES - FFN_CONV), (0, 0)))
    cw = cw.reshape(SUBLANES, nc, FFN_CHUNK).transpose(1, 0, 2)
    cb = conv_b.reshape(nc, 1, FFN_CHUNK)
    wd = w_down.astype(BF16).reshape(nc, FFN_CHUNK, d)
    kern = functools.partial(_ffn_kernel, tiles_per_seq=seq_len // tm, n_chunks=nc)
    return pl.pallas_call(
        kern,
        out_shape=jax.ShapeDtypeStruct((m, d), F32),
        grid=(m // tm,),
        in_specs=[pl.BlockSpec((tm, d), lambda i: (i, 0)), _const_spec((1, d)),
                  _const_spec((nc, d, FFN_CHUNK)), _const_spec((nc, d, FFN_CHUNK)),
                  _const_spec((nc, SUBLANES, FFN_CHUNK)), _const_spec((nc, 1, FFN_CHUNK)),
                  _const_spec((nc, FFN_CHUNK, d))],
        out_specs=pl.BlockSpec((tm, d), lambda i: (i, 0)),
        scratch_shapes=[pltpu.VMEM((tm, d), BF16),
                        pltpu.VMEM((nc, SUBLANES, FFN_CHUNK), F32)],
        compiler_params=_cparams(("arbitrary",)),
        name="conv_ffn",
    )(x, g.reshape(1, d), wg, wv, cw, cb, wd)


def _nsa_proj_kernel(x_ref, g_ref, w_ref, bg_ref, rc_ref, rs1_ref, rs2_ref,
                     qp_ref, qr_ref, kc_ref, vc_ref, ks_ref, vs_ref, kw_ref, vw_ref, gate_ref,
                     *, tiles_per_seq):
    hn = _rms(x_ref[...], g_ref[...]).astype(BF16)
    y = _dot(hn, w_ref[...])
    rc, rs1, rs2 = rc_ref[...], rs1_ref[...], rs2_ref[...]
    dh = NSA_HEAD_DIM

    def rope(z):
        return z * rc + pltpu.roll(z, 8, 1) * rs1 + pltpu.roll(z, LANES - 8, 1) * rs2

    scale = dh ** -0.5 * LOG2E
    for j in range(D_MODEL // LANES):
        q = y[:, j * LANES:(j + 1) * LANES] * scale
        qp_ref[:, j * LANES:(j + 1) * LANES] = q.astype(BF16)
        qr_ref[:, j * LANES:(j + 1) * LANES] = rope(q).astype(BF16)

    def kv_chunk(idx):
        return y[:, D_MODEL + idx * NSA_KV:D_MODEL + (idx + 1) * NSA_KV]

    def split_groups(z, ref, dtype):
        for g in range(NSA_GROUPS):
            ref[g] = z[:, g * dh:(g + 1) * dh].astype(dtype)

    def rope256(z):
        return jnp.concatenate([rope(z[:, :LANES]), rope(z[:, LANES:])], axis=1)

    split_groups(kv_chunk(0), kc_ref, F32)
    split_groups(kv_chunk(1), vc_ref, F32)
    tm = y.shape[0]
    t_pos = (pl.program_id(0) % tiles_per_seq) * tm + lax.broadcasted_iota(jnp.int32, (tm, LANES), 0)
    onehot = (lax.broadcasted_iota(jnp.int32, (tm, LANES), 1) == t_pos // SEL_BLOCK).astype(F32)
    ksel = rope256(kv_chunk(2))
    zpad = jnp.zeros((tm, LANES - dh), F32)
    for g in range(NSA_GROUPS):
        ks_ref[g] = jnp.concatenate([ksel[:, g * dh:(g + 1) * dh], zpad, onehot], axis=1).astype(BF16)
    split_groups(kv_chunk(3), vs_ref, BF16)
    split_groups(rope256(kv_chunk(4)), kw_ref, BF16)
    split_groups(kv_chunk(5), vw_ref, BF16)
    gate = y[:, D_MODEL + 6 * NSA_KV:] + bg_ref[...]
    gate_ref[...] = _sigmoid(gate)


def _rope_tables(seq_len):
    half = NSA_ROT_DIM // 2
    inv_freq = ROPE_THETA ** (-jnp.arange(half, dtype=F32) / half)
    ang = jnp.arange(seq_len, dtype=F32)[:, None] * inv_freq[None, :]
    cos, sin = jnp.cos(ang), jnp.sin(ang)
    zeros = jnp.zeros((seq_len, NSA_HEAD_DIM - NSA_ROT_DIM), F32)
    z8 = jnp.zeros((seq_len, half), F32)
    rc = jnp.concatenate([cos, cos, zeros + 1.0], axis=1)
    rs1 = jnp.concatenate([z8, sin, zeros], axis=1)
    rs2 = jnp.concatenate([-sin, z8, zeros], axis=1)
    two = lambda t: jnp.concatenate([t, t], axis=1)
    return two(rc), two(rs1), two(rs2)


def _nsa_proj(x, g, w_in, b_gate, rope, seq_len, tm=512):
    m, d = x.shape
    n_kv = 6 * NSA_KV
    wg = w_in[:, D_MODEL + n_kv:].reshape(d, NSA_GROUPS, NSA_REP * 3)
    wg = jnp.pad(wg, ((0, 0), (0, 0), (0, LANES - NSA_REP * 3))).reshape(d, NSA_GROUPS * LANES)
    w = jnp.concatenate([w_in[:, :D_MODEL + n_kv], wg], axis=1).astype(BF16)
    bg = jnp.pad(b_gate.reshape(NSA_GROUPS, NSA_REP * 3), ((0, 0), (0, LANES - NSA_REP * 3)))
    bg = bg.reshape(1, NSA_GROUPS * LANES)
    n = w.shape[1]
    tps = seq_len // tm
    row = lambda i: (i, 0)
    rope_spec = pl.BlockSpec((tm, LANES), lambda i: (i % tps, 0))
    assert seq_len // SEL_BLOCK <= LANES
    g_out = lambda dt, w=NSA_HEAD_DIM: jax.ShapeDtypeStruct((NSA_GROUPS, m, w), dt)
    g_spec = pl.BlockSpec((NSA_GROUPS, tm, NSA_HEAD_DIM), lambda i: (0, i, 0))
    ks_spec = pl.BlockSpec((NSA_GROUPS, tm, 2 * LANES), lambda i: (0, i, 0))
    return pl.pallas_call(
        functools.partial(_nsa_proj_kernel, tiles_per_seq=tps),
        out_shape=(jax.ShapeDtypeStruct((m, d), BF16), jax.ShapeDtypeStruct((m, d), BF16),
                   g_out(F32), g_out(F32), g_out(BF16, 2 * LANES), g_out(BF16), g_out(BF16), g_out(BF16),
                   jax.ShapeDtypeStruct((m, NSA_GROUPS * LANES), F32)),
        grid=(m // tm,),
        in_specs=[pl.BlockSpec((tm, d), row), _const_spec((1, d)), _const_spec((d, n)),
                  _const_spec((1, NSA_GROUPS * LANES)), rope_spec, rope_spec, rope_spec],
        out_specs=(pl.BlockSpec((tm, d), row), pl.BlockSpec((tm, d), row),
                   g_spec, g_spec, ks_spec, g_spec, g_spec, g_spec,
                   pl.BlockSpec((tm, NSA_GROUPS * LANES), row)),
        compiler_params=_cparams(("parallel",)),
        name="nsa_proj",
    )(x, g.reshape(1, d), w, bg, *rope)


def _gelu_tanh(x):
    return 0.5 * x * (1.0 + jnp.tanh(math.sqrt(2.0 / math.pi) * (x + 0.044715 * (x * x * x))))


def _compress_kernel(zk_ref, zv_ref, pek_ref, pev_ref, w1k_ref, w2k_ref, w1v_ref, w2v_ref,
                     kc_ref, vc_ref):
    nrow = zk_ref.shape[0]
    rows = lax.broadcasted_iota(jnp.int32, (nrow, NSA_HEAD_DIM), 0)

    def one(z_ref, pe_ref, w1_ref, w2_ref, o_ref):
        z = z_ref[...]
        a = _dot((z + pe_ref[0:1, :]).astype(BF16), w1_ref[0])
        b = _dot((z + pe_ref[1:2, :]).astype(BF16), w1_ref[1])
        hid = a + pltpu.roll(b, nrow - 1, 0)
        out = _dot(_gelu_tanh(hid).astype(BF16), w2_ref[...])
        o_ref[...] = jnp.where(rows == nrow - 1, 0.0, out).astype(o_ref.dtype)

    one(zk_ref, pek_ref, w1k_ref, w2k_ref, kc_ref)
    one(zv_ref, pev_ref, w1v_ref, w2v_ref, vc_ref)


def _compress(kc_raw, vc_raw, pe_k, pe_v, w1k, w2k, w1v, w2v, seq_len):
    g, m, dh = kc_raw.shape
    half = CMP_STRIDE * dh
    nchunk = seq_len // CMP_STRIDE
    zk = kc_raw.reshape(g * m // CMP_STRIDE, half)
    zv = vc_raw.reshape(g * m // CMP_STRIDE, half)
    pe2 = lambda pe: pe.reshape(2, half)
    w1 = lambda w: w.astype(BF16).reshape(2, half, CMP_HIDDEN)
    nblk = zk.shape[0] // nchunk
    row = lambda i: (i, 0)
    return pl.pallas_call(
        _compress_kernel,
        out_shape=(jax.ShapeDtypeStruct((zk.shape[0], dh), BF16),
                   jax.ShapeDtypeStruct((zk.shape[0], dh), BF16)),
        grid=(nblk,),
        in_specs=[pl.BlockSpec((nchunk, half), row), pl.BlockSpec((nchunk, half), row),
                  _const_spec((2, half)), _const_spec((2, half)),
                  _const_spec((2, half, CMP_HIDDEN)), _const_spec((CMP_HIDDEN, dh)),
                  _const_spec((2, half, CMP_HIDDEN)), _const_spec((CMP_HIDDEN, dh))],
        out_specs=(pl.BlockSpec((nchunk, dh), row), pl.BlockSpec((nchunk, dh), row)),
        compiler_params=_cparams(("parallel",)),
        name="nsa_compress",
    )(zk, zv, pe2(pe_k), pe2(pe_v), w1(w1k), w2k.astype(BF16), w1(w1v), w2v.astype(BF16))


def _nsa_cmp_kernel(q_ref, kc_ref, vc_ref, gate_ref, ovt_ref, oc_ref, selt_ref, *, tq):
    qi = pl.program_id(2)
    dh = NSA_HEAD_DIM
    q = q_ref[...]
    qs = jnp.concatenate([q[:, r * dh:(r + 1) * dh] for r in range(NSA_REP)], axis=0)
    kc, vc = kc_ref[...], vc_ref[...]
    ncmp = kc.shape[0]
    t = qi * tq + lax.broadcasted_iota(jnp.int32, (ncmp, tq), 1)
    cmp_end = lax.broadcasted_iota(jnp.int32, (ncmp, tq), 0) * CMP_STRIDE + (CMP_BLOCK - 1)
    neg = jnp.where(cmp_end <= t, 0.0, -jnp.inf)
    heads = range(NSA_REP)
    st = [_dot_nt(kc, qs[r * tq:(r + 1) * tq]) + neg for r in heads]
    mx = [jnp.max(st[r], axis=0, keepdims=True) for r in heads]
    mx = [jnp.where(mx[r] == -jnp.inf, 0.0, mx[r]) for r in heads]
    e = [jnp.exp2(st[r] - mx[r]) for r in heads]
    p = [e[r] / jnp.maximum(jnp.sum(e[r], axis=0, keepdims=True), 1e-30) for r in heads]
    o_t = [_dot_tn(vc, p[r].astype(BF16)) for r in heads]
    gate = gate_ref[...]
    for r in heads:
        o_r = jnp.concatenate([o_t[r], o_t[r]], axis=0).T[:, :dh]
        oc_ref[:, r * dh:(r + 1) * dh] = o_r * gate[:, 3 * r:3 * r + 1]
    psum = sum(p)
    imp = _dot(ovt_ref[...], jnp.concatenate(_split3(psum), axis=0))
    nsel = imp.shape[0]
    blk = lax.broadcasted_iota(jnp.int32, (nsel, tq), 0)
    tb = (qi * tq + lax.broadcasted_iota(jnp.int32, (nsel, tq), 1)) // SEL_BLOCK
    forced = (blk == 0) | (blk == tb) | (blk == tb - 1)
    vals = jnp.where(forced, FORCE_SCORE, jnp.where(blk <= tb, imp, -1.0))
    blk_f = blk.astype(F32)
    sel = jnp.zeros((nsel, tq), F32)
    for _ in range(SEL_TOPK):
        top = jnp.max(vals, axis=0, keepdims=True)
        first = jnp.min(jnp.where(vals == top, blk_f, float(nsel)), axis=0, keepdims=True)
        pick = blk_f == first
        sel = jnp.where(pick, 1.0, sel)
        vals = jnp.where(pick, -jnp.inf, vals)
    selt_ref[...] = sel.astype(selt_ref.dtype)


def _overlap_matrix_t3(ncmp_pad, nsel):
    c = np.arange(ncmp_pad)[None, :]
    s = np.arange(nsel)[:, None]
    cmp_start = c * CMP_STRIDE
    cmp_end = cmp_start + CMP_BLOCK - 1
    blk_start = s * SEL_BLOCK
    ov = ((cmp_end >= blk_start) & (cmp_start <= blk_start + SEL_BLOCK - 1)).astype(np.float32)
    return jnp.asarray(np.concatenate([ov, ov, ov], axis=1), dtype=BF16)


def _nsa_cmp(qp, kc, vc, gates, batch, seq_len, tq=256):
    m, d = qp.shape
    nq = seq_len // tq
    ncmp = seq_len // CMP_STRIDE
    nsel = seq_len // SEL_BLOCK
    ovt3 = _overlap_matrix_t3(ncmp, nsel)
    qmap = lambda b, g, i: (b * nq + i, g)
    kmap = lambda b, g, i: (g * batch + b, 0)
    return pl.pallas_call(
        functools.partial(_nsa_cmp_kernel, tq=tq),
        out_shape=(jax.ShapeDtypeStruct((m, d), F32),
                   jax.ShapeDtypeStruct((NSA_GROUPS, nsel, m), BF16)),
        grid=(batch, NSA_GROUPS, nq),
        in_specs=[pl.BlockSpec((tq, NSA_KV), qmap),
                  pl.BlockSpec((ncmp, NSA_HEAD_DIM), kmap), pl.BlockSpec((ncmp, NSA_HEAD_DIM), kmap),
                  pl.BlockSpec((tq, LANES), qmap), _const_spec((nsel, 3 * ncmp))],
        out_specs=(pl.BlockSpec((tq, NSA_KV), qmap),
                   pl.BlockSpec((None, nsel, tq), lambda b, g, i: (g, 0, b * nq + i))),
        compiler_params=_cparams(("parallel", "parallel", "parallel")),
        name="nsa_cmp_topk",
    )(qp, kc, vc, gates, ovt3)


def _nsa_sel_kernel(q_ref, ks_ref, vs_ref, kw_ref, vw_ref, sel_ref, gate_ref, oc_ref, o_ref,
                    m_scr, l_scr, acc_scr, sta_scr, stb_scr, *, tq, tk):
    qi = pl.program_id(2)
    dh = NSA_HEAD_DIM
    cols = NSA_REP * tq
    q = q_ref[...]
    qs = jnp.concatenate([q[:, r * dh:(r + 1) * dh] for r in range(NSA_REP)], axis=0)
    selt = sel_ref[...].astype(F32)
    nsel = selt.shape[0]
    bmask_t = jnp.where(selt > 0.5, 0.0, MASKED)
    if nsel < LANES:
        bmask_t = jnp.concatenate([bmask_t, jnp.zeros((LANES - nsel, tq), F32)], axis=0)
    bmask = bmask_t.T.astype(BF16)
    zpad = jnp.zeros((tq, LANES - dh), BF16)
    qa = jnp.concatenate([jnp.concatenate([q[:, r * dh:(r + 1) * dh], zpad, bmask], axis=1)
                          for r in range(NSA_REP)], axis=0)
    q0 = qi * tq
    key_iota = lax.broadcasted_iota(jnp.int32, (tk, tq), 0)
    t_pos = q0 + lax.broadcasted_iota(jnp.int32, (tk, tq), 1)

    m_scr[...] = jnp.full((1, cols), -jnp.inf, F32)
    l_scr[...] = jnp.zeros((1, cols), F32)
    acc_scr[...] = jnp.zeros((dh, cols), F32)

    heads = range(NSA_REP)
    hsl = [slice(r * tq, (r + 1) * tq) for r in heads]

    def put_scores(scr, ki):
        k = ks_ref[pl.ds(pl.multiple_of(ki * tk, tk), tk), :]
        for r in heads:
            scr[r] = _dot_nt(k, qa[hsl[r]])

    def get_scores(scr):
        return [scr[r] for r in heads]

    def consume(ki, st, causal):
        k0 = pl.multiple_of(ki * tk, tk)
        v = vs_ref[pl.ds(k0, tk), :]
        if causal:
            neg = jnp.where(k0 + key_iota <= t_pos, 0.0, -jnp.inf)
            st = [s + neg for s in st]
        m_old = [m_scr[:, hsl[r]] for r in heads]
        m_new = [jnp.maximum(m_old[r], jnp.max(st[r], axis=0, keepdims=True)) for r in heads]
        alpha = [jnp.exp2(m_old[r] - m_new[r]) for r in heads]
        p = [jnp.exp2(st[r] - m_new[r]) for r in heads]
        psum = [jnp.sum(p[r], axis=0, keepdims=True) for r in heads]
        pv = [_dot_tn(v, p[r].astype(BF16)) for r in heads]
        for r in heads:
            l_scr[:, hsl[r]] = alpha[r] * l_scr[:, hsl[r]] + psum[r]
            acc_scr[:, hsl[r]] = alpha[r] * acc_scr[:, hsl[r]] + pv[r]
            m_scr[:, hsl[r]] = m_new[r]

    def tile_pair(j, carry):
        even = get_scores(sta_scr)
        put_scores(stb_scr, 2 * j + 1)
        consume(2 * j, even, False)
        odd = get_scores(stb_scr)
        put_scores(sta_scr, 2 * j + 2)
        consume(2 * j + 1, odd, False)
        return carry

    last = (q0 + tq - 1) // tk
    put_scores(sta_scr, 0)
    lax.fori_loop(0, last // 2, tile_pair, 0)

    @pl.when(last % 2 == 0)
    def _():
        consume(last, get_scores(sta_scr), True)

    @pl.when(last % 2 == 1)
    def _():
        even = get_scores(sta_scr)
        put_scores(stb_scr, last)
        consume(last - 1, even, False)
        consume(last, get_scores(stb_scr), True)
    o_sel = acc_scr[...] / jnp.maximum(l_scr[...], 1e-30)

    wlen = tq + WINDOW
    w0 = pl.multiple_of(jnp.maximum(q0 - WINDOW, 0), tq)
    kw = kw_ref[pl.ds(w0, wlen), :]
    vw = vw_ref[pl.ds(w0, wlen), :]
    pw = w0 + lax.broadcasted_iota(jnp.int32, (wlen, tq), 0)
    tw = q0 + lax.broadcasted_iota(jnp.int32, (wlen, tq), 1)
    neg_w = jnp.where((pw <= tw) & (pw > tw - WINDOW), 0.0, -jnp.inf)
    sw = [_dot_nt(kw, qs[hsl[r]]) + neg_w for r in heads]
    mw = [jnp.max(sw[r], axis=0, keepdims=True) for r in heads]
    mw = [jnp.where(mw[r] == -jnp.inf, 0.0, mw[r]) for r in heads]
    ew = [jnp.exp2(sw[r] - mw[r]) for r in heads]
    o_win = [_dot_tn(vw, ew[r].astype(BF16)) / jnp.maximum(jnp.sum(ew[r], axis=0, keepdims=True), 1e-30)
             for r in heads]

    gate = gate_ref[...]
    oc = oc_ref[...]
    for r in heads:
        o2r = jnp.concatenate([o_sel[:, hsl[r]], o_win[r]], axis=0).T
        tot = (oc[:, r * dh:(r + 1) * dh] + o2r[:, :dh] * gate[:, 3 * r + 1:3 * r + 2]
               + o2r[:, dh:] * gate[:, 3 * r + 2:3 * r + 3])
        o_ref[:, r * dh:(r + 1) * dh] = tot.astype(o_ref.dtype)


def _nsa_sel(qr, ks, vs, kw, vw, sel, gates, oc, batch, seq_len, tq=256, tk=512):
    m, d = qr.shape
    nq = seq_len // tq
    nsel = seq_len // SEL_BLOCK
    qmap = lambda b, g, i: (b * nq + i, g)
    kvmap = lambda b, g, i: (g, b, 0)
    kv_spec = pl.BlockSpec((None, seq_len, NSA_HEAD_DIM), kvmap)
    ks_spec = pl.BlockSpec((None, seq_len, 2 * LANES), kvmap)
    cols = NSA_REP * tq
    assert tk % tq == 0
    return pl.pallas_call(
        functools.partial(_nsa_sel_kernel, tq=tq, tk=tk),
        out_shape=jax.ShapeDtypeStruct((m, d), BF16),
        grid=(batch, NSA_GROUPS, nq),
        in_specs=[pl.BlockSpec((tq, NSA_KV), qmap), ks_spec, kv_spec, kv_spec, kv_spec,
                  pl.BlockSpec((None, nsel, tq), lambda b, g, i: (g, 0, b * nq + i)),
                  pl.BlockSpec((tq, LANES), qmap), pl.BlockSpec((tq, NSA_KV), qmap)],
        out_specs=pl.BlockSpec((tq, NSA_KV), qmap),
        scratch_shapes=[pltpu.VMEM((1, cols), F32),
                        pltpu.VMEM((1, cols), F32), pltpu.VMEM((NSA_HEAD_DIM, cols), F32),
                        pltpu.VMEM((NSA_REP, tk, tq), F32), pltpu.VMEM((NSA_REP, tk, tq), F32)],
        compiler_params=_cparams(("parallel", "parallel", "parallel")),
        name="nsa_sel_win",
    )(qr, ks, vs, kw, vw, sel, gates, oc)


def _nsa_layer(x, norm_g, w_in, pe_k, pe_v, w1k, w2k, w1v, w2v, b_gate, w_out, rope, batch, seq_len):
    qp, qr, kc_raw, vc_raw, ks, vs, kw, vw, gates = _nsa_proj(x, norm_g, w_in, b_gate, rope, seq_len)
    kc, vc = _compress(kc_raw, vc_raw, pe_k, pe_v, w1k, w2k, w1v, w2v, seq_len)
    oc, sel = _nsa_cmp(qp, kc, vc, gates, batch, seq_len)
    o = _nsa_sel(qr, ks, vs, kw, vw, sel, gates, oc, batch, seq_len)
    return _matmul_res(o, w_out.astype(BF16), x)


def _mlstm_proj_kernel(x_ref, g_ref, w_ref, wt_ref, bcol_ref, brow_ref,
                       qk_ref, v_ref, o_ref, gc_ref, gr_ref):
    hn = _rms(x_ref[...], g_ref[...]).astype(BF16)
    y = _dot(hn, w_ref[...])
    d = D_MODEL
    qk_ref[...] = y[:, :d]
    v_ref[...] = y[:, d:2 * d].astype(BF16)
    o_ref[...] = _sigmoid(y[:, 2 * d:3 * d])
    gc_ref[...] = y[:, 3 * d:] + bcol_ref[...]
    gr_ref[...] = _dot_nt(wt_ref[...], hn) + brow_ref[...]


def _mlstm_proj(x, g, w_in, b_gates, tm=512):
    m, d = x.shape
    h, dk = MLSTM_HEADS, MLSTM_QK_DIM
    wq = w_in[:, :h * dk].reshape(d, h, dk)
    wk = w_in[:, h * dk:2 * h * dk].reshape(d, h, dk)
    wqk = jnp.concatenate([wq, wk], axis=2).reshape(d, 2 * h * dk)
    wv = w_in[:, d:2 * d]
    wif = w_in[:, 2 * d:2 * d + 2 * h]
    wo = w_in[:, 2 * d + 2 * h:]
    w = jnp.concatenate([wqk, wv, wo, jnp.pad(wif, ((0, 0), (0, LANES - 2 * h)))], axis=1).astype(BF16)
    wt = wif.T.astype(BF16)
    bcol = jnp.pad(b_gates, (0, LANES - 2 * h)).reshape(1, LANES)
    brow = b_gates.reshape(2 * h, 1)
    n = w.shape[1]
    row = lambda i: (i, 0)
    return pl.pallas_call(
        _mlstm_proj_kernel,
        out_shape=(jax.ShapeDtypeStruct((m, d), F32), jax.ShapeDtypeStruct((m, d), BF16),
                   jax.ShapeDtypeStruct((m, d), F32), jax.ShapeDtypeStruct((m, LANES), F32),
                   jax.ShapeDtypeStruct((2 * h, m), F32)),
        grid=(m // tm,),
        in_specs=[pl.BlockSpec((tm, d), row), _const_spec((1, d)), _const_spec((d, n)),
                  _const_spec((2 * h, d)), _const_spec((1, LANES)), _const_spec((2 * h, 1))],
        out_specs=(pl.BlockSpec((tm, d), row), pl.BlockSpec((tm, d), row), pl.BlockSpec((tm, d), row),
                   pl.BlockSpec((tm, LANES), row), pl.BlockSpec((2 * h, tm), lambda i: (0, i))),
        compiler_params=_cparams(("parallel",)),
        name="mlstm_proj",
    )(x, g.reshape(1, d), w, wt, bcol, brow)


def _log_sigmoid(x):
    return jnp.minimum(x, 0.0) - jnp.log(1.0 + jnp.exp(-jnp.abs(x)))


def _mlstm_core_kernel(qk_ref, v_ref, og_ref, gc_ref, gr_ref, cw_ref, cb_ref, ng_ref, o_ref,
                       qkc_scr, prev_scr, c_scr, n_scr, m_scr, *, tt):
    L = MLSTM_CHUNK
    H, dk, dv = MLSTM_HEADS, MLSTM_QK_DIM, MLSTM_V_DIM
    seq_start = pl.program_id(1) == 0

    @pl.when(seq_start)
    def _():
        c_scr[...] = jnp.zeros_like(c_scr)
        n_scr[...] = jnp.zeros_like(n_scr)
        m_scr[...] = jnp.zeros_like(m_scr)
        prev_scr[...] = jnp.zeros_like(prev_scr)

    raw = qk_ref[...]
    prev = prev_scr[...]
    prev_scr[...] = raw[tt - SUBLANES:, :]
    rows = lax.broadcasted_iota(jnp.int32, raw.shape, 0)
    cw = cw_ref[...]
    acc = cw[MLSTM_CONV - 1:MLSTM_CONV, :] * raw + cb_ref[...]
    for s in range(1, MLSTM_CONV):
        shifted = pltpu.roll(raw, s, 0)
        for j in range(s):
            shifted = jnp.where(rows == j, prev[SUBLANES - s + j:SUBLANES - s + j + 1, :], shifted)
        acc = acc + cw[MLSTM_CONV - 1 - s:MLSTM_CONV - s, :] * shifted
    act = acc * _sigmoid(acc)
    lane = lax.broadcasted_iota(jnp.int32, raw.shape, 1)
    qkc_scr[...] = jnp.where(lane % (2 * dk) >= dk, act * dk ** -0.5, act)

    tri_r = lax.broadcasted_iota(jnp.int32, (L, L), 0)
    tri_c = lax.broadcasted_iota(jnp.int32, (L, L), 1)
    tri_l = tri_r >= tri_c
    tri3 = jnp.concatenate([tri_l.astype(BF16)] * 3, axis=1)
    triu3 = jnp.concatenate([(tri_r <= tri_c).astype(BF16)] * 3, axis=0)

    def chunk(c, carry):
        r0 = pl.multiple_of(c * L, L)
        gcol = gc_ref[pl.ds(r0, L), :]
        grow = gr_ref[c]
        b_col = _cumsum_rows(tri3, _log_sigmoid(gcol))
        b_row = _cumsum_lanes(_log_sigmoid(grow), triu3)
        hs = range(H)
        vsl = [slice(h * dv, (h + 1) * dv) for h in hs]
        qk = [qkc_scr[pl.ds(r0, L), h * 2 * dk:(h + 1) * 2 * dk] for h in hs]
        qf = [qk[h][:, :dk] for h in hs]
        q = [qf[h].astype(BF16) for h in hs]
        k = [qk[h][:, dk:] for h in hs]
        v = [v_ref[pl.ds(r0, L), vsl[h]] for h in hs]
        li_c = [gcol[:, h:h + 1] for h in hs]
        b_c = [b_col[:, H + h:H + h + 1] for h in hs]
        li_r = [grow[h:h + 1, :] for h in hs]
        b_r = [b_row[H + h:H + h + 1, :] for h in hs]
        b_end = [b_r[h][:, L - 1:L] for h in hs]
        m_prev = [m_scr[h:h + 1, 0:1] for h in hs]
        c_prev = [c_scr[h] for h in hs]
        n_prev = [n_scr[h:h + 1, :] for h in hs]
        dmat = [jnp.where(tri_l, b_c[h] - b_r[h] + li_r[h], -jnp.inf) for h in hs]
        m_inter = [b_c[h] + m_prev[h] for h in hs]
        m_t = [jnp.maximum(m_inter[h], jnp.max(dmat[h], axis=-1, keepdims=True)) for h in hs]
        s_qk = [_dot_nt(q[h], k[h].astype(BF16)) for h in hs]
        q_c = [_dot(q[h], c_prev[h].astype(BF16)) for h in hs]
        att = [jnp.exp(dmat[h] - m_t[h]) * s_qk[h] for h in hs]
        inter = [jnp.exp(m_inter[h] - m_t[h]) for h in hs]
        num = [_dot(att[h].astype(BF16), v[h]) + inter[h] * q_c[h] for h in hs]
        den = [jnp.sum(att[h], axis=-1, keepdims=True)
               + inter[h] * jnp.sum(qf[h] * n_prev[h], axis=-1, keepdims=True) for h in hs]
        h_t = [num[h] / jnp.maximum(jnp.abs(den[h]), jnp.exp(-m_t[h])) for h in hs]
        h_t = [h_t[h] * lax.rsqrt(jnp.mean(h_t[h] * h_t[h], axis=-1, keepdims=True) + NORM_EPS) for h in hs]
        g_end_r = [b_end[h] - b_r[h] + li_r[h] for h in hs]
        g_max = [jnp.max(g_end_r[h], axis=-1, keepdims=True) for h in hs]
        kw = [k[h] * jnp.exp(b_end[h] - b_c[h] + li_c[h] - g_max[h]) for h in hs]
        c_loc = [_dot_tn(kw[h].astype(BF16), v[h]) for h in hs]
        n_loc = [jnp.sum(kw[h], axis=0, keepdims=True) for h in hs]
        m_new = [jnp.maximum(b_end[h] + m_prev[h], g_max[h]) for h in hs]
        a = [jnp.exp(b_end[h] + m_prev[h] - m_new[h]) for h in hs]
        sc = [jnp.exp(g_max[h] - m_new[h]) for h in hs]
        for h in hs:
            out = h_t[h] * ng_ref[:, vsl[h]] * og_ref[pl.ds(r0, L), vsl[h]]
            o_ref[pl.ds(r0, L), vsl[h]] = out.astype(o_ref.dtype)
            c_scr[h] = a[h] * c_prev[h] + sc[h] * c_loc[h]
            n_scr[h:h + 1, :] = a[h] * n_prev[h] + sc[h] * n_loc[h]
            m_scr[h:h + 1, :] = jnp.broadcast_to(m_new[h], (1, LANES))
        return carry

    lax.fori_loop(0, tt // L, chunk, 0)


def _mlstm_core(qk, v, og, gcol, grow3, conv_w, conv_b, norm_g, batch, seq_len, tt=256):
    m, d = qk.shape
    H, dk, dv = MLSTM_HEADS, MLSTM_QK_DIM, MLSTM_V_DIM
    nt = seq_len // tt
    ncs = tt // MLSTM_CHUNK
    row = lambda b, i: (b * nt + i, 0)
    return pl.pallas_call(
        functools.partial(_mlstm_core_kernel, tt=tt),
        out_shape=jax.ShapeDtypeStruct((m, d), BF16),
        grid=(batch, nt),
        in_specs=[pl.BlockSpec((tt, d), row), pl.BlockSpec((tt, d), row), pl.BlockSpec((tt, d), row),
                  pl.BlockSpec((tt, LANES), row),
                  pl.BlockSpec((ncs, 2 * H, MLSTM_CHUNK), lambda b, i: (b * nt + i, 0, 0)),
                  _const_spec((SUBLANES, d)), _const_spec((1, d)), _const_spec((1, d))],
        out_specs=pl.BlockSpec((tt, d), row),
        scratch_shapes=[pltpu.VMEM((tt, d), F32), pltpu.VMEM((SUBLANES, d), F32),
                        pltpu.VMEM((H, dk, dv), F32), pltpu.VMEM((H, dk), F32),
                        pltpu.VMEM((H, LANES), F32)],
        compiler_params=_cparams(("arbitrary", "arbitrary")),
        name="mlstm_core",
    )(qk, v, og, gcol, grow3, conv_w, conv_b, norm_g)


def _mlstm_layer(x, norm_g, w_in, conv_w, conv_b, b_gates, hnorm, w_out, batch, seq_len):
    H, dk = MLSTM_HEADS, MLSTM_QK_DIM
    qk, v, og, gcol, grow = _mlstm_proj(x, norm_g, w_in, b_gates)
    m = x.shape[0]
    grow3 = grow.reshape(2 * H, m // MLSTM_CHUNK, MLSTM_CHUNK).transpose(1, 0, 2)

    def perm(z):
        lead = z.shape[:-1]
        zq = z[..., :H * dk].reshape(*lead, H, dk)
        zk = z[..., H * dk:].reshape(*lead, H, dk)
        return jnp.concatenate([zq, zk], axis=-1).reshape(*lead, 2 * H * dk)

    cw = jnp.pad(perm(conv_w), ((0, SUBLANES - MLSTM_CONV), (0, 0)))
    o = _mlstm_core(qk, v, og, gcol, grow3, cw, perm(conv_b).reshape(1, -1), hnorm.reshape(1, -1),
                    batch, seq_len)
    return _matmul_res(o, w_out.astype(BF16), x)


def _softplus(x):
    return jnp.maximum(x, 0.0) + jnp.log(1.0 + jnp.exp(-jnp.abs(x)))


def _rwkv_proj_kernel(x_ref, g_ref, mu_ref, wr_ref, wk_ref, wv_ref, ww1_ref, ww2_ref, w0_ref,
                      aw1_ref, aw2_ref, a0_ref, gw1_ref, gw2_ref,
                      r_ref, k_ref, v_ref, lw_ref, a_ref, go_ref, prev_scr, *, tiles_per_seq):
    tm = x_ref.shape[0]
    h = _rms(x_ref[...], g_ref[...])
    seq_start = (pl.program_id(0) % tiles_per_seq) == 0
    prev = jnp.where(seq_start, 0.0, prev_scr[SUBLANES - 1:SUBLANES, :])
    prev_scr[...] = h[tm - SUBLANES:, :]
    rows = lax.broadcasted_iota(jnp.int32, h.shape, 0)
    xx = jnp.where(rows == 0, prev, pltpu.roll(h, 1, 0)) - h
    mix = lambda j: (h + xx * mu_ref[j:j + 1, :]).astype(BF16)
    r_ref[...] = _dot(mix(0), wr_ref[...])
    k_ref[...] = _dot(mix(2), wk_ref[...])
    v_ref[...] = _dot(mix(3), wv_ref[...])
    wl = _dot(jnp.tanh(_dot(mix(1), ww1_ref[...])).astype(BF16), ww2_ref[...]) + w0_ref[...]
    w_log = -_softplus(-wl) - 0.5
    lw_ref[...] = -jnp.exp(w_log)
    al = _dot(_dot(mix(4), aw1_ref[...]).astype(BF16), aw2_ref[...]) + a0_ref[...]
    a_ref[...] = _sigmoid(al)
    go_ref[...] = _dot(_sigmoid(_dot(mix(5), gw1_ref[...])).astype(BF16), gw2_ref[...])


def _rwkv_proj(x, g, mu, w_r, w_k, w_v, w0, w_w1, w_w2, a0, a_w1, a_w2, g_w1, g_w2, seq_len, tm=256):
    m, d = x.shape
    row = lambda i: (i, 0)
    bf = lambda w: w.astype(BF16)
    mu8 = jnp.pad(mu, ((0, SUBLANES - mu.shape[0]), (0, 0)))
    consts = [g.reshape(1, d), mu8, bf(w_r), bf(w_k), bf(w_v), bf(w_w1), bf(w_w2), w0.reshape(1, d),
              bf(a_w1), bf(a_w2), a0.reshape(1, d), bf(g_w1), bf(g_w2)]
    out = jax.ShapeDtypeStruct((m, d), F32)
    return pl.pallas_call(
        functools.partial(_rwkv_proj_kernel, tiles_per_seq=seq_len // tm),
        out_shape=(out,) * 6,
        grid=(m // tm,),
        in_specs=[pl.BlockSpec((tm, d), row)] + [_const_spec(c.shape) for c in consts],
        out_specs=(pl.BlockSpec((tm, d), row),) * 6,
        scratch_shapes=[pltpu.VMEM((SUBLANES, d), F32)],
        compiler_params=_cparams(("arbitrary",)),
        name="rwkv_proj",
    )(x, *consts)


def _block_diag(x, lo):
    zero = jnp.zeros_like(x)
    return jnp.concatenate([jnp.where(lo, x, zero), jnp.where(lo, zero, x)], axis=0)


class _PairMat:
    def __init__(self, x, lo):
        self.x, self.lo = x, lo
        self._lhs = self._rhs = None

    def lhs(self):
        if self._lhs is None:
            hi, lo = _split2(self.x)
            self._lhs = jnp.concatenate([hi, lo, hi], axis=1)
        return self._lhs

    def rhs(self):
        if self._rhs is None:
            hi, lo = _split2(self.x)
            bh = _block_diag(hi, self.lo)
            self._rhs = jnp.concatenate([bh, bh, _block_diag(lo, self.lo)], axis=0)
        return self._rhs


def _pair_mm(p, q):
    return _dot(p.lhs(), q.rhs())


def _rwkv_core_kernel(r_ref, k_ref, v_ref, lw_ref, a_ref, go_ref, kk_ref, ka_ref, rk_ref,
                      lnw_ref, lnb_ref, o_ref, z_scr, *, tt):
    L, N = RWKV_CHUNK, RWKV_HEAD_DIM
    npair = z_scr.shape[0]
    pairs = range(npair)

    @pl.when(pl.program_id(1) == 0)
    def _():
        z_scr[...] = jnp.zeros_like(z_scr)

    ri = lax.broadcasted_iota(jnp.int32, (L, LANES), 0)
    ln = lax.broadcasted_iota(jnp.int32, (L, LANES), 1)
    si = ln % N
    lo = ln < N
    lower_incl = ri >= si
    lower_strict = ri > si
    blk_diag = (ri // RWKV_INV_BLOCK) == (si // RWKV_INV_BLOCK)
    eye = (ri == si).astype(F32)
    tri = (lax.broadcasted_iota(jnp.int32, (L, L), 0) >= lax.broadcasted_iota(jnp.int32, (L, L), 1))
    tri3 = jnp.concatenate([tri.astype(BF16)] * 3, axis=1)
    z_mask = ((lax.broadcasted_iota(jnp.int32, (2 * N, LANES), 0) // N)
              == (lax.broadcasted_iota(jnp.int32, (2 * N, LANES), 1) // N))

    def half_sum(x):
        s0 = jnp.sum(jnp.where(lo, x, 0.0), axis=-1, keepdims=True)
        s1 = jnp.sum(jnp.where(lo, 0.0, x), axis=-1, keepdims=True)
        return jnp.where(lo, s0, s1)

    bd = lambda x: _block_diag(x, lo)
    mk = lambda xs: [_PairMat(x, lo) for x in xs]
    mm = lambda ps, qs: [_pair_mm(p, q) for p, q in zip(ps, qs)]

    def chunk(c, carry):
        r0 = pl.multiple_of(c * L, L)
        sl = [slice(p * LANES, (p + 1) * LANES) for p in pairs]
        ld = lambda ref: [ref[pl.ds(r0, L), s] for s in sl]
        r, k, v, lw, a = ld(r_ref), ld(k_ref), ld(v_ref), ld(lw_ref), ld(a_ref)
        kk = [k[p] * kk_ref[:, sl[p]] for p in pairs]
        kk = [kk[p] / jnp.maximum(jnp.sqrt(half_sum(kk[p] * kk[p])), 1e-12) for p in pairs]
        km = [k[p] * (1.0 + (a[p] - 1.0) * ka_ref[:, sl[p]]) for p in pairs]
        bv = [kk[p] * a[p] for p in pairs]
        cum = [_cumsum_rows(tri3, lw[p]) for p in pairs]
        cum_end = [cum[p][L - 1:L, :] for p in pairs]
        w_inv = [jnp.exp(-cum[p]) for p in pairs]
        w_out = [jnp.exp(cum_end[p] - cum[p]) for p in pairs]
        kk_h = [(kk[p] * jnp.exp(cum[p] - lw[p])).astype(BF16) for p in pairs]
        r_h = [(r[p] * jnp.exp(cum[p])).astype(BF16) for p in pairs]
        b_t = [(bv[p] * w_inv[p]).astype(BF16) for p in pairs]
        k_t = [(km[p] * w_inv[p]).astype(BF16) for p in pairs]
        bbar = [(bv[p] * w_out[p]).astype(BF16) for p in pairs]
        kbar = [(km[p] * w_out[p]).astype(BF16) for p in pairs]
        vb = [v[p].astype(BF16) for p in pairs]
        lhs = [jnp.concatenate([kk_h[p], r_h[p]], axis=0) for p in pairs]
        ab = [_dot_nt(lhs[p], bd(b_t[p])) for p in pairs]
        ak = [_dot_nt(lhs[p], bd(k_t[p])) for p in pairs]
        a_ub = [jnp.where(lower_strict, ab[p][:L], 0.0) for p in pairs]
        a_rb = [jnp.where(lower_incl, ab[p][L:], 0.0).astype(BF16) for p in pairs]
        a_uk = [jnp.where(lower_strict, ak[p][:L], 0.0).astype(BF16) for p in pairs]
        a_rk = [jnp.where(lower_incl, ak[p][L:], 0.0).astype(BF16) for p in pairs]

        n1 = mk([jnp.where(blk_diag, -a_ub[p], 0.0) for p in pairs])
        n2 = mk(mm(n1, n1))
        n4 = mk(mm(n2, n2))
        n8 = mk(mm(n4, n4))
        acc = mk([eye + n1[p].x for p in pairs])
        for nk in (n2, n4):
            prod = mm(acc, nk)
            acc = mk([acc[p].x + prod[p] for p in pairs])
        prod = mm(acc, n8)
        d_inv = mk([acc[p].x + prod[p] for p in pairs])
        l_off = mk([jnp.where(blk_diag, 0.0, a_ub[p]) for p in pairs])
        e1 = mk([-x for x in mm(d_inv, l_off)])
        e2 = mk(mm(e1, e1))
        qm = mk([eye + e1[p].x for p in pairs])
        prod = mm(qm, e2)
        qm = mk([qm[p].x + prod[p] for p in pairs])
        t_inv = mk(mm(qm, d_inv))

        z = [z_scr[p] for p in pairs]
        zb = [z[p].astype(BF16) for p in pairs]
        bdv = [bd(vb[p]) for p in pairs]
        rhs_u = mk([_dot_nt(kk_h[p], zb[p]) + _dot(a_uk[p], bdv[p]) for p in pairs])
        u = [-x for x in mm(t_inv, rhs_u)]
        ub = [u[p].astype(BF16) for p in pairs]
        y = [_dot_nt(r_h[p], zb[p])
             + _dot(jnp.concatenate([a_rb[p], a_rk[p]], axis=1),
                    jnp.concatenate([bd(ub[p]), bdv[p]], axis=0)) for p in pairs]
        upd = [_dot_tn(jnp.concatenate([ub[p], vb[p]], axis=0),
                       jnp.concatenate([bbar[p], kbar[p]], axis=0)) for p in pairs]
        for p in pairs:
            z_scr[p] = z[p] * jnp.exp(cum_end[p]) + jnp.where(z_mask, upd[p], 0.0)

        inv_n = 1.0 / N
        mean = [half_sum(y[p]) * inv_n for p in pairs]
        yc = [y[p] - mean[p] for p in pairs]
        var = [half_sum(yc[p] * yc[p]) * inv_n for p in pairs]
        bonus = [half_sum(r[p] * km[p] * rk_ref[:, sl[p]]) * v[p] for p in pairs]
        for p in pairs:
            yn = yc[p] * lax.rsqrt(var[p] + RWKV_GN_EPS) * lnw_ref[:, sl[p]] + lnb_ref[:, sl[p]]
            o_ref[pl.ds(r0, L), sl[p]] = ((yn + bonus[p]) * go_ref[pl.ds(r0, L), sl[p]]).astype(o_ref.dtype)
        return carry

    lax.fori_loop(0, tt // L, chunk, 0)


def _rwkv_core(r, k, v, lw, a, go, k_k, k_a, r_k, ln_w, ln_b, batch, seq_len, tt=256):
    m, d = r.shape
    nt = seq_len // tt
    npair = d // LANES
    blk = pl.BlockSpec((tt, d), lambda b, i: (b * nt + i, 0))
    par = _const_spec((1, d))
    row1 = lambda z: z.reshape(1, d)
    return pl.pallas_call(
        functools.partial(_rwkv_core_kernel, tt=tt),
        out_shape=jax.ShapeDtypeStruct((m, d), BF16),
        grid=(batch, nt),
        in_specs=[blk] * 6 + [par] * 5,
        out_specs=blk,
        scratch_shapes=[pltpu.VMEM((npair, 2 * RWKV_HEAD_DIM, LANES), F32)],
        compiler_params=_cparams(("arbitrary", "arbitrary")),
        name="rwkv_core",
    )(r, k, v, lw, a, go, row1(k_k), row1(k_a), row1(r_k), row1(ln_w), row1(ln_b))


def _rwkv_layer(x, norm_g, mu, w_r, w_k, w_v, w_o, w0, w_w1, w_w2, a0, a_w1, a_w2, g_w1, g_w2,
                k_k, k_a, r_k, ln_w, ln_b, batch, seq_len):
    r, k, v, lw, a, go = _rwkv_proj(x, norm_g, mu, w_r, w_k, w_v, w0, w_w1, w_w2, a0, a_w1, a_w2,
                                    g_w1, g_w2, seq_len)
    o = _rwkv_core(r, k, v, lw, a, go, k_k, k_a, r_k, ln_w, ln_b, batch, seq_len)
    return _matmul_res(o, w_o.astype(BF16), x)


def kernel(x, norm_mixer, norm_ffn, ffn_w_up, ffn_conv_w, ffn_conv_b, ffn_w_down, nsa_w_in, nsa_pe_k, nsa_pe_v, nsa_cmp_k_w1, nsa_cmp_k_w2, nsa_cmp_v_w1, nsa_cmp_v_w2, nsa_b_gate, nsa_w_out, mlstm_w_in, mlstm_conv_w, mlstm_conv_b, mlstm_b_gates, mlstm_norm, mlstm_w_out, rwkv_mu, rwkv_w_r, rwkv_w_k, rwkv_w_v, rwkv_w_o, rwkv_w0, rwkv_w_w1, rwkv_w_w2, rwkv_a0, rwkv_a_w1, rwkv_a_w2, rwkv_g_w1, rwkv_g_w2, rwkv_k_k, rwkv_k_a, rwkv_r_k, rwkv_ln_w, rwkv_ln_b, final_norm):
    batch, seq_len, d = x.shape
    depth = norm_mixer.shape[0]
    rope = _rope_tables(seq_len)
    xf = x.reshape(batch * seq_len, d)
    for i in range(depth):
        kind, j = i % 3, i // 3
        if kind == 0:
            xf = _nsa_layer(xf, norm_mixer[i], nsa_w_in[j], nsa_pe_k[j], nsa_pe_v[j], nsa_cmp_k_w1[j],
                            nsa_cmp_k_w2[j], nsa_cmp_v_w1[j], nsa_cmp_v_w2[j], nsa_b_gate[j],
                            nsa_w_out[j], rope, batch, seq_len)
        elif kind == 1:
            xf = _mlstm_layer(xf, norm_mixer[i], mlstm_w_in[j], mlstm_conv_w[j], mlstm_conv_b[j],
                              mlstm_b_gates[j], mlstm_norm[j], mlstm_w_out[j], batch, seq_len)
        else:
            xf = _rwkv_layer(xf, norm_mixer[i], rwkv_mu[j], rwkv_w_r[j], rwkv_w_k[j], rwkv_w_v[j],
                             rwkv_w_o[j], rwkv_w0[j], rwkv_w_w1[j], rwkv_w_w2[j], rwkv_a0[j],
                             rwkv_a_w1[j], rwkv_a_w2[j], rwkv_g_w1[j], rwkv_g_w2[j], rwkv_k_k[j],
                             rwkv_k_a[j], rwkv_r_k[j], rwkv_ln_w[j], rwkv_ln_b[j], batch, seq_len)
        xf = _ffn(xf, norm_ffn[i], ffn_w_up[i], ffn_conv_w[i], ffn_conv_b[i], ffn_w_down[i], seq_len)
    return _final_norm(xf, final_norm).reshape(batch, seq_len, d)
```

```python
import functools
import math

import jax
import jax.numpy as jnp
import numpy as np
from jax import lax
from jax.experimental import pallas as pl
from jax.experimental.pallas import tpu as pltpu

F32 = jnp.float32
BF16 = jnp.bfloat16

D_MODEL = 1024
DEPTH = 4
NORM_EPS = 1e-6
ROPE_THETA = 500000.0

NSA_HEAD_DIM = 64
NSA_HEADS = 16
NSA_GROUPS = 4
NSA_REP = NSA_HEADS // NSA_GROUPS
NSA_ROT_DIM = 16
CMP_BLOCK = 32
CMP_STRIDE = 16
CMP_HIDDEN = 256
SEL_BLOCK = 64
SEL_TOPK = 16
WINDOW = 512
NSA_KV = NSA_GROUPS * NSA_HEAD_DIM

MLSTM_HEADS = 8
MLSTM_QK_DIM = 64
MLSTM_V_DIM = 128
MLSTM_CHUNK = 64
MLSTM_CONV = 4

RWKV_HEAD_DIM = 64
RWKV_HEADS = 16
RWKV_GN_EPS = 64e-5
RWKV_CHUNK = 64
RWKV_INV_BLOCK = 16

FFN_DIM = 2816
FFN_CONV = 3
FFN_CHUNK = 256

LOG2E = math.log2(math.e)
MASKED = -1e30

LANES = 128
SUBLANES = 8
VMEM_LIMIT = 56 * 1024 * 1024


def _dot(a, b):
    return jnp.dot(a, b, preferred_element_type=F32)


def _dot_nt(a, b):
    return lax.dot_general(a, b, (((1,), (1,)), ((), ())), preferred_element_type=F32)


def _dot_tn(a, b):
    return lax.dot_general(a, b, (((0,), (0,)), ((), ())), preferred_element_type=F32)


def _split2(x):
    hi = x.astype(BF16)
    return hi, (x - hi.astype(F32)).astype(BF16)


def _split3(x):
    hi = x.astype(BF16)
    r1 = x - hi.astype(F32)
    mid = r1.astype(BF16)
    return hi, mid, (r1 - mid.astype(F32)).astype(BF16)


def _cumsum_rows(tri3, x):
    return _dot(tri3, jnp.concatenate(_split3(x), axis=0))


def _cumsum_lanes(x, triu3):
    return _dot(jnp.concatenate(_split3(x), axis=1), triu3)


def _rms(x, g):
    ms = jnp.mean(x * x, axis=-1, keepdims=True)
    return x * lax.rsqrt(ms + NORM_EPS) * g


def _sigmoid(x):
    return 1.0 / (1.0 + jnp.exp(-x))


def _cparams(sem):
    return pltpu.CompilerParams(dimension_semantics=sem, vmem_limit_bytes=VMEM_LIMIT)


def _const_spec(shape):
    n = len(shape)
    return pl.BlockSpec(shape, lambda *_: (0,) * n, pipeline_mode=pl.Buffered(1))


def _ffn_kernel(res_ref, a_ref, wo_ref, g_ref, wu_ref, cw_ref, cb_ref, wd_ref, fg_ref, o_ref,
                h_scr, carry_scr, ga_scr, va_scr, gb_scr, vb_scr, *, tiles_per_seq, n_chunks, final_norm):
    tm = res_ref.shape[0]
    fc = FFN_CHUNK
    x = res_ref[...] + _dot(a_ref[...], wo_ref[...])
    h_scr[...] = _rms(x, g_ref[...]).astype(BF16)
    o_ref[...] = x
    seq_start = (pl.program_id(0) % tiles_per_seq) == 0
    rows = lax.broadcasted_iota(jnp.int32, (tm, fc), 0)
    cols = lambda c, base=0: pl.ds(pl.multiple_of(base + c * fc, LANES), fc)

    def up(c, g_scr, v_scr):
        h = h_scr[...]
        g_scr[...] = _dot(h, wu_ref[:, cols(c)])
        v_scr[...] = _dot(h, wu_ref[:, cols(c, FFN_DIM)])

    def down(c, g_scr, v_scr):
        gate, val = g_scr[...], v_scr[...]
        prev = carry_scr[:, cols(c)]
        prev = jnp.where(seq_start, 0.0, prev)
        p1 = prev[SUBLANES - 1:SUBLANES, :]
        p2 = prev[SUBLANES - 2:SUBLANES - 1, :]
        carry_scr[:, cols(c)] = gate[tm - SUBLANES:, :]
        g1 = jnp.where(rows == 0, p1, pltpu.roll(gate, 1, 0))
        g2 = jnp.where(rows == 0, p2, jnp.where(rows == 1, p1, pltpu.roll(gate, 2, 0)))
        cw = cw_ref[:, cols(c)]
        y = cw[2:3, :] * gate + cw[1:2, :] * g1 + cw[0:1, :] * g2 + cb_ref[:, cols(c)]
        act = (y * _sigmoid(y) * val).astype(BF16)
        o_ref[...] += _dot(act, wd_ref[pl.ds(pl.multiple_of(c * fc, fc), fc), :])

    assert n_chunks % 2 == 1

    def chunk_pair(j, carry):
        up(2 * j + 1, gb_scr, vb_scr)
        down(2 * j, ga_scr, va_scr)
        up(2 * j + 2, ga_scr, va_scr)
        down(2 * j + 1, gb_scr, vb_scr)
        return carry

    up(0, ga_scr, va_scr)
    lax.fori_loop(0, n_chunks // 2, chunk_pair, 0)
    down(n_chunks - 1, ga_scr, va_scr)
    if final_norm:
        o_ref[...] = _rms(o_ref[...], fg_ref[...])


def _ffn(res, a, w_o, g, w_up, conv_w, conv_b, w_down, seq_len, final_g=None, tm=1024):
    m, d = res.shape
    nc = FFN_DIM // FFN_CHUNK
    cw = jnp.pad(conv_w, ((0, SUBLANES - FFN_CONV), (0, 0)))
    row = lambda i: (i, 0)
    fg = jnp.ones((d,), F32) if final_g is None else final_g
    kern = functools.partial(_ffn_kernel, tiles_per_seq=seq_len // tm, n_chunks=nc,
                             final_norm=final_g is not None)
    return pl.pallas_call(
        kern,
        out_shape=jax.ShapeDtypeStruct((m, d), F32),
        grid=(m // tm,),
        in_specs=[pl.BlockSpec((tm, d), row), pl.BlockSpec((tm, d), row), _const_spec((d, d)),
                  _const_spec((1, d)), _const_spec((d, 2 * FFN_DIM)),
                  _const_spec((SUBLANES, FFN_DIM)), _const_spec((1, FFN_DIM)),
                  _const_spec((FFN_DIM, d)), _const_spec((1, d))],
        out_specs=pl.BlockSpec((tm, d), row),
        scratch_shapes=[pltpu.VMEM((tm, d), BF16),
                        pltpu.VMEM((SUBLANES, FFN_DIM), F32)]
                       + [pltpu.VMEM((tm, FFN_CHUNK), F32)] * 4,
        compiler_params=_cparams(("arbitrary",)),
        name="conv_ffn",
    )(res, a, w_o.astype(BF16), g.reshape(1, d), w_up.astype(BF16), cw, conv_b.reshape(1, FFN_DIM),
      w_down.astype(BF16), fg.reshape(1, d))


def _nsa_proj_kernel(x_ref, g_ref, w_ref, bg_ref, rc_ref, rs1_ref, rs2_ref,
                     qp_ref, qr_ref, kc_ref, vc_ref, ks_ref, vs_ref, kw_ref, vw_ref, gate_ref,
                     *, tiles_per_seq):
    hn = _rms(x_ref[...], g_ref[...]).astype(BF16)
    y = _dot(hn, w_ref[...])
    rc, rs1, rs2 = rc_ref[...], rs1_ref[...], rs2_ref[...]
    dh = NSA_HEAD_DIM

    def rope(z):
        return z * rc + pltpu.roll(z, 8, 1) * rs1 + pltpu.roll(z, LANES - 8, 1) * rs2

    scale = dh ** -0.5 * LOG2E
    for j in range(D_MODEL // LANES):
        q = y[:, j * LANES:(j + 1) * LANES] * scale
        qp_ref[:, j * LANES:(j + 1) * LANES] = q.astype(BF16)
        qr_ref[:, j * LANES:(j + 1) * LANES] = rope(q).astype(BF16)

    def kv_chunk(idx):
        return y[:, D_MODEL + idx * NSA_KV:D_MODEL + (idx + 1) * NSA_KV]

    def split_groups(z, ref, dtype):
        for g in range(NSA_GROUPS):
            ref[g] = z[:, g * dh:(g + 1) * dh].astype(dtype)

    def rope256(z):
        return jnp.concatenate([rope(z[:, :LANES]), rope(z[:, LANES:])], axis=1)

    split_groups(kv_chunk(0), kc_ref, F32)
    split_groups(kv_chunk(1), vc_ref, F32)
    tm = y.shape[0]
    t_pos = (pl.program_id(0) % tiles_per_seq) * tm + lax.broadcasted_iota(jnp.int32, (tm, LANES), 0)
    onehot = (lax.broadcasted_iota(jnp.int32, (tm, LANES), 1) == t_pos // SEL_BLOCK).astype(F32)
    ksel = rope256(kv_chunk(2))
    zpad = jnp.zeros((tm, LANES - dh), F32)
    for g in range(NSA_GROUPS):
        ks_ref[g] = jnp.concatenate([ksel[:, g * dh:(g + 1) * dh], zpad, onehot], axis=1).astype(BF16)
    split_groups(kv_chunk(3), vs_ref, BF16)
    split_groups(rope256(kv_chunk(4)), kw_ref, BF16)
    split_groups(kv_chunk(5), vw_ref, BF16)
    gate = y[:, D_MODEL + 6 * NSA_KV:] + bg_ref[...]
    gate_ref[...] = _sigmoid(gate)


def _rope_tables(seq_len):
    half = NSA_ROT_DIM // 2
    inv_freq = ROPE_THETA ** (-jnp.arange(half, dtype=F32) / half)
    ang = jnp.arange(seq_len, dtype=F32)[:, None] * inv_freq[None, :]
    cos, sin = jnp.cos(ang), jnp.sin(ang)
    zeros = jnp.zeros((seq_len, NSA_HEAD_DIM - NSA_ROT_DIM), F32)
    z8 = jnp.zeros((seq_len, half), F32)
    rc = jnp.concatenate([cos, cos, zeros + 1.0], axis=1)
    rs1 = jnp.concatenate([z8, sin, zeros], axis=1)
    rs2 = jnp.concatenate([-sin, z8, zeros], axis=1)
    two = lambda t: jnp.concatenate([t, t], axis=1)
    return two(rc), two(rs1), two(rs2)


def _nsa_proj(x, g, w_in, b_gate, rope, seq_len, tm=512):
    m, d = x.shape
    n_kv = 6 * NSA_KV
    wg = w_in[:, D_MODEL + n_kv:].reshape(d, NSA_GROUPS, NSA_REP * 3)
    wg = jnp.pad(wg, ((0, 0), (0, 0), (0, LANES - NSA_REP * 3))).reshape(d, NSA_GROUPS * LANES)
    w = jnp.concatenate([w_in[:, :D_MODEL + n_kv], wg], axis=1).astype(BF16)
    bg = jnp.pad(b_gate.reshape(NSA_GROUPS, NSA_REP * 3), ((0, 0), (0, LANES - NSA_REP * 3)))
    bg = bg.reshape(1, NSA_GROUPS * LANES)
    n = w.shape[1]
    tps = seq_len // tm
    row = lambda i: (i, 0)
    rope_spec = pl.BlockSpec((tm, LANES), lambda i: (i % tps, 0))
    assert seq_len // SEL_BLOCK <= LANES
    g_out = lambda dt, w=NSA_HEAD_DIM: jax.ShapeDtypeStruct((NSA_GROUPS, m, w), dt)
    g_spec = pl.BlockSpec((NSA_GROUPS, tm, NSA_HEAD_DIM), lambda i: (0, i, 0))
    ks_spec = pl.BlockSpec((NSA_GROUPS, tm, 2 * LANES), lambda i: (0, i, 0))
    return pl.pallas_call(
        functools.partial(_nsa_proj_kernel, tiles_per_seq=tps),
        out_shape=(jax.ShapeDtypeStruct((m, d), BF16), jax.ShapeDtypeStruct((m, d), BF16),
                   g_out(F32), g_out(F32), g_out(BF16, 2 * LANES), g_out(BF16), g_out(BF16), g_out(BF16),
                   jax.ShapeDtypeStruct((m, NSA_GROUPS * LANES), F32)),
        grid=(m // tm,),
        in_specs=[pl.BlockSpec((tm, d), row), _const_spec((1, d)), _const_spec((d, n)),
                  _const_spec((1, NSA_GROUPS * LANES)), rope_spec, rope_spec, rope_spec],
        out_specs=(pl.BlockSpec((tm, d), row), pl.BlockSpec((tm, d), row),
                   g_spec, g_spec, ks_spec, g_spec, g_spec, g_spec,
                   pl.BlockSpec((tm, NSA_GROUPS * LANES), row)),
        compiler_params=_cparams(("parallel",)),
        name="nsa_proj",
    )(x, g.reshape(1, d), w, bg, *rope)


def _gelu_tanh(x):
    return 0.5 * x * (1.0 + jnp.tanh(math.sqrt(2.0 / math.pi) * (x + 0.044715 * (x * x * x))))


def _compress_kernel(zk_ref, zv_ref, pek_ref, pev_ref, w1k_ref, w2k_ref, w1v_ref, w2v_ref,
                     kc_ref, vc_ref):
    nrow = zk_ref.shape[0]
    rows = lax.broadcasted_iota(jnp.int32, (nrow, NSA_HEAD_DIM), 0)

    def one(z_ref, pe_ref, w1_ref, w2_ref, o_ref):
        z = z_ref[...]
        a = _dot((z + pe_ref[0:1, :]).astype(BF16), w1_ref[0])
        b = _dot((z + pe_ref[1:2, :]).astype(BF16), w1_ref[1])
        hid = a + pltpu.roll(b, nrow - 1, 0)
        out = _dot(_gelu_tanh(hid).astype(BF16), w2_ref[...])
        o_ref[...] = jnp.where(rows == nrow - 1, 0.0, out).astype(o_ref.dtype)

    one(zk_ref, pek_ref, w1k_ref, w2k_ref, kc_ref)
    one(zv_ref, pev_ref, w1v_ref, w2v_ref, vc_ref)


def _compress(kc_raw, vc_raw, pe_k, pe_v, w1k, w2k, w1v, w2v, seq_len):
    g, m, dh = kc_raw.shape
    half = CMP_STRIDE * dh
    nchunk = seq_len // CMP_STRIDE
    zk = kc_raw.reshape(g * m // CMP_STRIDE, half)
    zv = vc_raw.reshape(g * m // CMP_STRIDE, half)
    pe2 = lambda pe: pe.reshape(2, half)
    w1 = lambda w: w.astype(BF16).reshape(2, half, CMP_HIDDEN)
    nblk = zk.shape[0] // nchunk
    row = lambda i: (i, 0)
    return pl.pallas_call(
        _compress_kernel,
        out_shape=(jax.ShapeDtypeStruct((zk.shape[0], dh), BF16),
                   jax.ShapeDtypeStruct((zk.shape[0], dh), BF16)),
        grid=(nblk,),
        in_specs=[pl.BlockSpec((nchunk, half), row), pl.BlockSpec((nchunk, half), row),
                  _const_spec((2, half)), _const_spec((2, half)),
                  _const_spec((2, half, CMP_HIDDEN)), _const_spec((CMP_HIDDEN, dh)),
                  _const_spec((2, half, CMP_HIDDEN)), _const_spec((CMP_HIDDEN, dh))],
        out_specs=(pl.BlockSpec((nchunk, dh), row), pl.BlockSpec((nchunk, dh), row)),
        compiler_params=_cparams(("parallel",)),
        name="nsa_compress",
    )(zk, zv, pe2(pe_k), pe2(pe_v), w1(w1k), w2k.astype(BF16), w1(w1v), w2v.astype(BF16))


def _nsa_cmp_kernel(q_ref, kc_ref, vc_ref, gate_ref, ovt_ref, oc_ref, selt_ref, *, tq):
    qi = pl.program_id(2)
    dh = NSA_HEAD_DIM
    q = q_ref[...]
    qs = jnp.concatenate([q[:, r * dh:(r + 1) * dh] for r in range(NSA_REP)], axis=0)
    kc, vc = kc_ref[...], vc_ref[...]
    ncmp = kc.shape[0]
    t = qi * tq + lax.broadcasted_iota(jnp.int32, (ncmp, tq), 1)
    cmp_end = lax.broadcasted_iota(jnp.int32, (ncmp, tq), 0) * CMP_STRIDE + (CMP_BLOCK - 1)
    neg = jnp.where(cmp_end <= t, 0.0, -jnp.inf)
    heads = range(NSA_REP)
    st = [_dot_nt(kc, qs[r * tq:(r + 1) * tq]) + neg for r in heads]
    mx = [jnp.max(st[r], axis=0, keepdims=True) for r in heads]
    mx = [jnp.where(mx[r] == -jnp.inf, 0.0, mx[r]) for r in heads]
    e = [jnp.exp2(st[r] - mx[r]) for r in heads]
    p = [e[r] / jnp.maximum(jnp.sum(e[r], axis=0, keepdims=True), 1e-30) for r in heads]
    o_t = [_dot_tn(vc, p[r].astype(BF16)) for r in heads]
    gate = gate_ref[...]
    for r in heads:
        o_r = jnp.concatenate([o_t[r], o_t[r]], axis=0).T[:, :dh]
        oc_ref[:, r * dh:(r + 1) * dh] = o_r * gate[:, 3 * r:3 * r + 1]
    psum = sum(p)
    imp = _dot(ovt_ref[...], jnp.concatenate(_split3(psum), axis=0))
    nsel = imp.shape[0]
    blk = lax.broadcasted_iota(jnp.int32, (nsel, tq), 0)
    tb = (qi * tq + lax.broadcasted_iota(jnp.int32, (nsel, tq), 1)) // SEL_BLOCK
    forced = (blk == 0) | (blk == tb) | (blk == tb - 1)
    vals = jnp.where(forced, -jnp.inf, jnp.where(blk <= tb, imp, -1.0))
    blk_f = blk.astype(F32)
    sel = jnp.where(forced, 1.0, 0.0)
    for _ in range(SEL_TOPK - 3):
        top = jnp.max(vals, axis=0, keepdims=True)
        first = jnp.min(jnp.where(vals == top, blk_f, float(nsel)), axis=0, keepdims=True)
        pick = blk_f == first
        sel = jnp.where(pick, 1.0, sel)
        vals = jnp.where(pick, -jnp.inf, vals)
    selt_ref[...] = sel.astype(selt_ref.dtype)


def _overlap_matrix_t3(ncmp_pad, nsel):
    c = np.arange(ncmp_pad)[None, :]
    s = np.arange(nsel)[:, None]
    cmp_start = c * CMP_STRIDE
    cmp_end = cmp_start + CMP_BLOCK - 1
    blk_start = s * SEL_BLOCK
    ov = ((cmp_end >= blk_start) & (cmp_start <= blk_start + SEL_BLOCK - 1)).astype(np.float32)
    return jnp.asarray(np.concatenate([ov, ov, ov], axis=1), dtype=BF16)


def _nsa_cmp(qp, kc, vc, gates, batch, seq_len, tq=256):
    m, d = qp.shape
    nq = seq_len // tq
    ncmp = seq_len // CMP_STRIDE
    nsel = seq_len // SEL_BLOCK
    ovt3 = _overlap_matrix_t3(ncmp, nsel)
    qmap = lambda b, g, i: (b * nq + i, g)
    kmap = lambda b, g, i: (g * batch + b, 0)
    return pl.pallas_call(
        functools.partial(_nsa_cmp_kernel, tq=tq),
        out_shape=(jax.ShapeDtypeStruct((m, d), F32),
                   jax.ShapeDtypeStruct((NSA_GROUPS, nsel, m), BF16)),
        grid=(batch, NSA_GROUPS, nq),
        in_specs=[pl.BlockSpec((tq, NSA_KV), qmap),
                  pl.BlockSpec((ncmp, NSA_HEAD_DIM), kmap), pl.BlockSpec((ncmp, NSA_HEAD_DIM), kmap),
                  pl.BlockSpec((tq, LANES), qmap), _const_spec((nsel, 3 * ncmp))],
        out_specs=(pl.BlockSpec((tq, NSA_KV), qmap),
                   pl.BlockSpec((None, nsel, tq), lambda b, g, i: (g, 0, b * nq + i))),
        compiler_params=_cparams(("parallel", "parallel", "parallel")),
        name="nsa_cmp_topk",
    )(qp, kc, vc, gates, ovt3)


def _nsa_sel_kernel(q_ref, ks_ref, vs_ref, kw_ref, vw_ref, sel_ref, gate_ref, oc_ref, o_ref,
                    m_scr, l_scr, acc_scr, sta_scr, stb_scr, *, tq, tk):
    qi = pl.program_id(2)
    dh = NSA_HEAD_DIM
    cols = NSA_REP * tq
    q = q_ref[...]
    qs = jnp.concatenate([q[:, r * dh:(r + 1) * dh] for r in range(NSA_REP)], axis=0)
    selt = sel_ref[...].astype(F32)
    nsel = selt.shape[0]
    bmask_t = jnp.where(selt > 0.5, 0.0, MASKED)
    if nsel < LANES:
        bmask_t = jnp.concatenate([bmask_t, jnp.zeros((LANES - nsel, tq), F32)], axis=0)
    bmask = bmask_t.T.astype(BF16)
    zpad = jnp.zeros((tq, LANES - dh), BF16)
    qa = jnp.concatenate([jnp.concatenate([q[:, r * dh:(r + 1) * dh], zpad, bmask], axis=1)
                          for r in range(NSA_REP)], axis=0)
    q0 = qi * tq
    key_iota = lax.broadcasted_iota(jnp.int32, (tk, tq), 0)
    t_pos = q0 + lax.broadcasted_iota(jnp.int32, (tk, tq), 1)

    m_scr[...] = jnp.full((1, cols), -jnp.inf, F32)
    l_scr[...] = jnp.zeros((1, cols), F32)
    acc_scr[...] = jnp.zeros((dh, cols), F32)

    heads = range(NSA_REP)
    hsl = [slice(r * tq, (r + 1) * tq) for r in heads]

    def put_scores(scr, ki):
        k = ks_ref[pl.ds(pl.multiple_of(ki * tk, tk), tk), :]
        for r in heads:
            scr[r] = _dot_nt(k, qa[hsl[r]])

    def get_scores(scr):
        return [scr[r] for r in heads]

    def consume(ki, st, causal):
        k0 = pl.multiple_of(ki * tk, tk)
        v = vs_ref[pl.ds(k0, tk), :]
        if causal:
            neg = jnp.where(k0 + key_iota <= t_pos, 0.0, -jnp.inf)
            st = [s + neg for s in st]
        m_old = [m_scr[:, hsl[r]] for r in heads]
        m_new = [jnp.maximum(m_old[r], jnp.max(st[r], axis=0, keepdims=True)) for r in heads]
        alpha = [jnp.exp2(m_old[r] - m_new[r]) for r in heads]
        p = [jnp.exp2(st[r] - m_new[r]) for r in heads]
        psum = [jnp.sum(p[r], axis=0, keepdims=True) for r in heads]
        pv = [_dot_tn(v, p[r].astype(BF16)) for r in heads]
        for r in heads:
            l_scr[:, hsl[r]] = alpha[r] * l_scr[:, hsl[r]] + psum[r]
            acc_scr[:, hsl[r]] = alpha[r] * acc_scr[:, hsl[r]] + pv[r]
            m_scr[:, hsl[r]] = m_new[r]

    def tile_pair(j, carry):
        even = get_scores(sta_scr)
        put_scores(stb_scr, 2 * j + 1)
        consume(2 * j, even, False)
        odd = get_scores(stb_scr)
        put_scores(sta_scr, 2 * j + 2)
        consume(2 * j + 1, odd, False)
        return carry

    last = (q0 + tq - 1) // tk
    put_scores(sta_scr, 0)
    lax.fori_loop(0, last // 2, tile_pair, 0)

    @pl.when(last % 2 == 0)
    def _():
        consume(last, get_scores(sta_scr), True)

    @pl.when(last % 2 == 1)
    def _():
        even = get_scores(sta_scr)
        put_scores(stb_scr, last)
        consume(last - 1, even, False)
        consume(last, get_scores(stb_scr), True)
    o_sel = acc_scr[...] / jnp.maximum(l_scr[...], 1e-30)

    wlen = tq + WINDOW
    w0 = pl.multiple_of(jnp.maximum(q0 - WINDOW, 0), tq)
    kw = kw_ref[pl.ds(w0, wlen), :]
    vw = vw_ref[pl.ds(w0, wlen), :]
    pw = w0 + lax.broadcasted_iota(jnp.int32, (wlen, tq), 0)
    tw = q0 + lax.broadcasted_iota(jnp.int32, (wlen, tq), 1)
    neg_w = jnp.where((pw <= tw) & (pw > tw - WINDOW), 0.0, -jnp.inf)
    sw = [_dot_nt(kw, qs[hsl[r]]) + neg_w for r in heads]
    mw = [jnp.max(sw[r], axis=0, keepdims=True) for r in heads]
    mw = [jnp.where(mw[r] == -jnp.inf, 0.0, mw[r]) for r in heads]
    ew = [jnp.exp2(sw[r] - mw[r]) for r in heads]
    o_win = [_dot_tn(vw, ew[r].astype(BF16)) / jnp.maximum(jnp.sum(ew[r], axis=0, keepdims=True), 1e-30)
             for r in heads]

    gate = gate_ref[...]
    oc = oc_ref[...]
    for r in heads:
        o2r = jnp.concatenate([o_sel[:, hsl[r]], o_win[r]], axis=0).T
        tot = (oc[:, r * dh:(r + 1) * dh] + o2r[:, :dh] * gate[:, 3 * r + 1:3 * r + 2]
               + o2r[:, dh:] * gate[:, 3 * r + 2:3 * r + 3])
        o_ref[:, r * dh:(r + 1) * dh] = tot.astype(o_ref.dtype)


def _nsa_sel(qr, ks, vs, kw, vw, sel, gates, oc, batch, seq_len, tq=256, tk=512):
    m, d = qr.shape
    nq = seq_len // tq
    nsel = seq_len // SEL_BLOCK
    qmap = lambda b, g, i: (b * nq + i, g)
    kvmap = lambda b, g, i: (g, b, 0)
    kv_spec = pl.BlockSpec((None, seq_len, NSA_HEAD_DIM), kvmap)
    ks_spec = pl.BlockSpec((None, seq_len, 2 * LANES), kvmap)
    cols = NSA_REP * tq
    assert tk % tq == 0
    return pl.pallas_call(
        functools.partial(_nsa_sel_kernel, tq=tq, tk=tk),
        out_shape=jax.ShapeDtypeStruct((m, d), BF16),
        grid=(batch, NSA_GROUPS, nq),
        in_specs=[pl.BlockSpec((tq, NSA_KV), qmap), ks_spec, kv_spec, kv_spec, kv_spec,
                  pl.BlockSpec((None, nsel, tq), lambda b, g, i: (g, 0, b * nq + i)),
                  pl.BlockSpec((tq, LANES), qmap), pl.BlockSpec((tq, NSA_KV), qmap)],
        out_specs=pl.BlockSpec((tq, NSA_KV), qmap),
        scratch_shapes=[pltpu.VMEM((1, cols), F32),
                        pltpu.VMEM((1, cols), F32), pltpu.VMEM((NSA_HEAD_DIM, cols), F32),
                        pltpu.VMEM((NSA_REP, tk, tq), F32), pltpu.VMEM((NSA_REP, tk, tq), F32)],
        compiler_params=_cparams(("parallel", "parallel", "parallel")),
        name="nsa_sel_win",
    )(qr, ks, vs, kw, vw, sel, gates, oc)


def _nsa_layer(x, norm_g, w_in, pe_k, pe_v, w1k, w2k, w1v, w2v, b_gate, rope, batch, seq_len):
    qp, qr, kc_raw, vc_raw, ks, vs, kw, vw, gates = _nsa_proj(x, norm_g, w_in, b_gate, rope, seq_len)
    kc, vc = _compress(kc_raw, vc_raw, pe_k, pe_v, w1k, w2k, w1v, w2v, seq_len)
    oc, sel = _nsa_cmp(qp, kc, vc, gates, batch, seq_len)
    return _nsa_sel(qr, ks, vs, kw, vw, sel, gates, oc, batch, seq_len)


def _mlstm_proj_kernel(x_ref, g_ref, w_ref, wt_ref, bcol_ref, brow_ref,
                       qk_ref, v_ref, o_ref, gc_ref, gr_ref):
    hn = _rms(x_ref[...], g_ref[...]).astype(BF16)
    y = _dot(hn, w_ref[...])
    d = D_MODEL
    qk_ref[...] = y[:, :d]
    v_ref[...] = y[:, d:2 * d].astype(BF16)
    o_ref[...] = _sigmoid(y[:, 2 * d:3 * d])
    gc_ref[...] = y[:, 3 * d:] + bcol_ref[...]
    gr_ref[...] = _dot_nt(wt_ref[...], hn) + brow_ref[...]


def _mlstm_proj(x, g, w_in, b_gates, tm=512):
    m, d = x.shape
    h, dk = MLSTM_HEADS, MLSTM_QK_DIM
    wq = w_in[:, :h * dk].reshape(d, h, dk)
    wk = w_in[:, h * dk:2 * h * dk].reshape(d, h, dk)
    wqk = jnp.concatenate([wq, wk], axis=2).reshape(d, 2 * h * dk)
    wv = w_in[:, d:2 * d]
    wif = w_in[:, 2 * d:2 * d + 2 * h]
    wo = w_in[:, 2 * d + 2 * h:]
    w = jnp.concatenate([wqk, wv, wo, jnp.pad(wif, ((0, 0), (0, LANES - 2 * h)))], axis=1).astype(BF16)
    wt = wif.T.astype(BF16)
    bcol = jnp.pad(b_gates, (0, LANES - 2 * h)).reshape(1, LANES)
    brow = b_gates.reshape(2 * h, 1)
    n = w.shape[1]
    row = lambda i: (i, 0)
    return pl.pallas_call(
        _mlstm_proj_kernel,
        out_shape=(jax.ShapeDtypeStruct((m, d), F32), jax.ShapeDtypeStruct((m, d), BF16),
                   jax.ShapeDtypeStruct((m, d), F32), jax.ShapeDtypeStruct((m, LANES), F32),
                   jax.ShapeDtypeStruct((2 * h, m), F32)),
        grid=(m // tm,),
        in_specs=[pl.BlockSpec((tm, d), row), _const_spec((1, d)), _const_spec((d, n)),
                  _const_spec((2 * h, d)), _const_spec((1, LANES)), _const_spec((2 * h, 1))],
        out_specs=(pl.BlockSpec((tm, d), row), pl.BlockSpec((tm, d), row), pl.BlockSpec((tm, d), row),
                   pl.BlockSpec((tm, LANES), row), pl.BlockSpec((2 * h, tm), lambda i: (0, i))),
        compiler_params=_cparams(("parallel",)),
        name="mlstm_proj",
    )(x, g.reshape(1, d), w, wt, bcol, brow)


def _log_sigmoid(x):
    return jnp.minimum(x, 0.0) - jnp.log(1.0 + jnp.exp(-jnp.abs(x)))


def _mlstm_core_kernel(qk_ref, v_ref, og_ref, gc_ref, gr_ref, cw_ref, cb_ref, ng_ref, o_ref,
                       qkc_scr, ext_scr, c_scr, n_scr, m_scr, *, tt):
    L = MLSTM_CHUNK
    H, dk, dv = MLSTM_HEADS, MLSTM_QK_DIM, MLSTM_V_DIM
    seq_start = pl.program_id(1) == 0

    @pl.when(seq_start)
    def _():
        c_scr[...] = jnp.zeros_like(c_scr)
        n_scr[...] = jnp.zeros_like(n_scr)
        m_scr[...] = jnp.zeros_like(m_scr)
        ext_scr[0:SUBLANES, :] = jnp.zeros((SUBLANES, ext_scr.shape[1]), F32)

    ext_scr[SUBLANES:, :] = qk_ref[...]
    is_k = lax.broadcasted_iota(jnp.int32, (tt, 2 * dk), 1) >= dk
    for h in range(H):
        hsl = slice(h * 2 * dk, (h + 1) * 2 * dk)
        cw = cw_ref[:, hsl]
        acc = cb_ref[:, hsl] + cw[MLSTM_CONV - 1:MLSTM_CONV, :] * ext_scr[SUBLANES:, hsl]
        for s in range(1, MLSTM_CONV):
            acc = acc + cw[MLSTM_CONV - 1 - s:MLSTM_CONV - s, :] * ext_scr[SUBLANES - s:SUBLANES - s + tt, hsl]
        act = acc * _sigmoid(acc)
        qkc_scr[:, hsl] = jnp.where(is_k, act * dk ** -0.5, act)
    ext_scr[0:SUBLANES, :] = ext_scr[tt:tt + SUBLANES, :]

    tri_r = lax.broadcasted_iota(jnp.int32, (L, L), 0)
    tri_c = lax.broadcasted_iota(jnp.int32, (L, L), 1)
    tri_l = tri_r >= tri_c
    tri3 = jnp.concatenate([tri_l.astype(BF16)] * 3, axis=1)
    triu3 = jnp.concatenate([(tri_r <= tri_c).astype(BF16)] * 3, axis=0)

    def chunk(c, carry):
        r0 = pl.multiple_of(c * L, L)
        gcol = gc_ref[pl.ds(r0, L), :]
        grow = gr_ref[c]
        b_col = _cumsum_rows(tri3, _log_sigmoid(gcol))
        b_row = _cumsum_lanes(_log_sigmoid(grow), triu3)
        hs = range(H)
        vsl = [slice(h * dv, (h + 1) * dv) for h in hs]
        qk = [qkc_scr[pl.ds(r0, L), h * 2 * dk:(h + 1) * 2 * dk] for h in hs]
        qf = [qk[h][:, :dk] for h in hs]
        q = [qf[h].astype(BF16) for h in hs]
        k = [qk[h][:, dk:] for h in hs]
        v = [v_ref[pl.ds(r0, L), vsl[h]] for h in hs]
        li_c = [gcol[:, h:h + 1] for h in hs]
        b_c = [b_col[:, H + h:H + h + 1] for h in hs]
        li_r = [grow[h:h + 1, :] for h in hs]
        b_r = [b_row[H + h:H + h + 1, :] for h in hs]
        b_end = [b_r[h][:, L - 1:L] for h in hs]
        m_prev = [m_scr[h:h + 1, 0:1] for h in hs]
        c_prev = [c_scr[h] for h in hs]
        n_prev = [n_scr[h:h + 1, :] for h in hs]
        dmat = [jnp.where(tri_l, b_c[h] - b_r[h] + li_r[h], -jnp.inf) for h in hs]
        m_inter = [b_c[h] + m_prev[h] for h in hs]
        m_t = [jnp.maximum(m_inter[h], jnp.max(dmat[h], axis=-1, keepdims=True)) for h in hs]
        s_qk = [_dot_nt(q[h], k[h].astype(BF16)) for h in hs]
        q_c = [_dot(q[h], c_prev[h].astype(BF16)) for h in hs]
        att = [jnp.exp(dmat[h] - m_t[h]) * s_qk[h] for h in hs]
        inter = [jnp.exp(m_inter[h] - m_t[h]) for h in hs]
        num = [_dot(att[h].astype(BF16), v[h]) + inter[h] * q_c[h] for h in hs]
        den = [jnp.sum(att[h], axis=-1, keepdims=True)
               + inter[h] * jnp.sum(qf[h] * n_prev[h], axis=-1, keepdims=True) for h in hs]
        h_t = [num[h] / jnp.maximum(jnp.abs(den[h]), jnp.exp(-m_t[h])) for h in hs]
        h_t = [h_t[h] * lax.rsqrt(jnp.mean(h_t[h] * h_t[h], axis=-1, keepdims=True) + NORM_EPS) for h in hs]
        g_end_r = [b_end[h] - b_r[h] + li_r[h] for h in hs]
        g_max = [jnp.max(g_end_r[h], axis=-1, keepdims=True) for h in hs]
        kw = [k[h] * jnp.exp(b_end[h] - b_c[h] + li_c[h] - g_max[h]) for h in hs]
        c_loc = [_dot_tn(kw[h].astype(BF16), v[h]) for h in hs]
        n_loc = [jnp.sum(kw[h], axis=0, keepdims=True) for h in hs]
        m_new = [jnp.maximum(b_end[h] + m_prev[h], g_max[h]) for h in hs]
        a = [jnp.exp(b_end[h] + m_prev[h] - m_new[h]) for h in hs]
        sc = [jnp.exp(g_max[h] - m_new[h]) for h in hs]
        for h in hs:
            out = h_t[h] * ng_ref[:, vsl[h]] * og_ref[pl.ds(r0, L), vsl[h]]
            o_ref[pl.ds(r0, L), vsl[h]] = out.astype(o_ref.dtype)
            c_scr[h] = a[h] * c_prev[h] + sc[h] * c_loc[h]
            n_scr[h:h + 1, :] = a[h] * n_prev[h] + sc[h] * n_loc[h]
            m_scr[h:h + 1, :] = jnp.broadcast_to(m_new[h], (1, LANES))
        return carry

    lax.fori_loop(0, tt // L, chunk, 0)


def _mlstm_core(qk, v, og, gcol, grow3, conv_w, conv_b, norm_g, batch, seq_len, tt=256):
    m, d = qk.shape
    H, dk, dv = MLSTM_HEADS, MLSTM_QK_DIM, MLSTM_V_DIM
    nt = seq_len // tt
    ncs = tt // MLSTM_CHUNK
    row = lambda b, i: (b * nt + i, 0)
    return pl.pallas_call(
        functools.partial(_mlstm_core_kernel, tt=tt),
        out_shape=jax.ShapeDtypeStruct((m, d), BF16),
        grid=(batch, nt),
        in_specs=[pl.BlockSpec((tt, d), row), pl.BlockSpec((tt, d), row), pl.BlockSpec((tt, d), row),
                  pl.BlockSpec((tt, LANES), row),
                  pl.BlockSpec((ncs, 2 * H, MLSTM_CHUNK), lambda b, i: (b * nt + i, 0, 0)),
                  _const_spec((SUBLANES, d)), _const_spec((1, d)), _const_spec((1, d))],
        out_specs=pl.BlockSpec((tt, d), row),
        scratch_shapes=[pltpu.VMEM((tt, d), F32), pltpu.VMEM((tt + SUBLANES, d), F32),
                        pltpu.VMEM((H, dk, dv), F32), pltpu.VMEM((H, dk), F32),
                        pltpu.VMEM((H, LANES), F32)],
        compiler_params=_cparams(("arbitrary", "arbitrary")),
        name="mlstm_core",
    )(qk, v, og, gcol, grow3, conv_w, conv_b, norm_g)


def _mlstm_layer(x, norm_g, w_in, conv_w, conv_b, b_gates, hnorm, batch, seq_len):
    H, dk = MLSTM_HEADS, MLSTM_QK_DIM
    qk, v, og, gcol, grow = _mlstm_proj(x, norm_g, w_in, b_gates)
    m = x.shape[0]
    grow3 = grow.reshape(2 * H, m // MLSTM_CHUNK, MLSTM_CHUNK).transpose(1, 0, 2)

    def perm(z):
        lead = z.shape[:-1]
        zq = z[..., :H * dk].reshape(*lead, H, dk)
        zk = z[..., H * dk:].reshape(*lead, H, dk)
        return jnp.concatenate([zq, zk], axis=-1).reshape(*lead, 2 * H * dk)

    cw = jnp.pad(perm(conv_w), ((0, SUBLANES - MLSTM_CONV), (0, 0)))
    return _mlstm_core(qk, v, og, gcol, grow3, cw, perm(conv_b).reshape(1, -1), hnorm.reshape(1, -1),
                       batch, seq_len)


def _softplus(x):
    return jnp.maximum(x, 0.0) + jnp.log(1.0 + jnp.exp(-jnp.abs(x)))


def _rwkv_proj_kernel(x_ref, g_ref, mu_ref, wr_ref, wk_ref, wv_ref, ww1_ref, ww2_ref, w0_ref,
                      aw1_ref, aw2_ref, a0_ref, gw1_ref, gw2_ref,
                      r_ref, k_ref, v_ref, lw_ref, a_ref, go_ref, prev_scr, *, tiles_per_seq):
    tm = x_ref.shape[0]
    h = _rms(x_ref[...], g_ref[...])
    seq_start = (pl.program_id(0) % tiles_per_seq) == 0
    prev = jnp.where(seq_start, 0.0, prev_scr[SUBLANES - 1:SUBLANES, :])
    prev_scr[...] = h[tm - SUBLANES:, :]
    rows = lax.broadcasted_iota(jnp.int32, h.shape, 0)
    xx = jnp.where(rows == 0, prev, pltpu.roll(h, 1, 0)) - h
    mix = lambda j: (h + xx * mu_ref[j:j + 1, :]).astype(BF16)
    r_ref[...] = _dot(mix(0), wr_ref[...])
    k_ref[...] = _dot(mix(2), wk_ref[...])
    v_ref[...] = _dot(mix(3), wv_ref[...])
    wl = _dot(jnp.tanh(_dot(mix(1), ww1_ref[...])).astype(BF16), ww2_ref[...]) + w0_ref[...]
    w_log = -_softplus(-wl) - 0.5
    lw_ref[...] = -jnp.exp(w_log)
    al = _dot(_dot(mix(4), aw1_ref[...]).astype(BF16), aw2_ref[...]) + a0_ref[...]
    a_ref[...] = _sigmoid(al)
    go_ref[...] = _dot(_sigmoid(_dot(mix(5), gw1_ref[...])).astype(BF16), gw2_ref[...])


def _rwkv_proj(x, g, mu, w_r, w_k, w_v, w0, w_w1, w_w2, a0, a_w1, a_w2, g_w1, g_w2, seq_len, tm=256):
    m, d = x.shape
    row = lambda i: (i, 0)
    bf = lambda w: w.astype(BF16)
    mu8 = jnp.pad(mu, ((0, SUBLANES - mu.shape[0]), (0, 0)))
    consts = [g.reshape(1, d), mu8, bf(w_r), bf(w_k), bf(w_v), bf(w_w1), bf(w_w2), w0.reshape(1, d),
              bf(a_w1), bf(a_w2), a0.reshape(1, d), bf(g_w1), bf(g_w2)]
    out = jax.ShapeDtypeStruct((m, d), F32)
    return pl.pallas_call(
        functools.partial(_rwkv_proj_kernel, tiles_per_seq=seq_len // tm),
        out_shape=(out,) * 6,
        grid=(m // tm,),
        in_specs=[pl.BlockSpec((tm, d), row)] + [_const_spec(c.shape) for c in consts],
        out_specs=(pl.BlockSpec((tm, d), row),) * 6,
        scratch_shapes=[pltpu.VMEM((SUBLANES, d), F32)],
        compiler_params=_cparams(("arbitrary",)),
        name="rwkv_proj",
    )(x, *consts)


def _block_diag(x, lo):
    zero = jnp.zeros_like(x)
    return jnp.concatenate([jnp.where(lo, x, zero), jnp.where(lo, zero, x)], axis=0)


class _PairMat:
    def __init__(self, x, lo):
        self.x, self.lo = x, lo
        self._lhs = self._rhs = None

    def lhs(self):
        if self._lhs is None:
            hi, lo = _split2(self.x)
            self._lhs = jnp.concatenate([hi, lo, hi], axis=1)
        return self._lhs

    def rhs(self):
        if self._rhs is None:
            hi, lo = _split2(self.x)
            bh = _block_diag(hi, self.lo)
            self._rhs = jnp.concatenate([bh, bh, _block_diag(lo, self.lo)], axis=0)
        return self._rhs


def _pair_mm(p, q):
    return _dot(p.lhs(), q.rhs())


def _rwkv_core_kernel(r_ref, k_ref, v_ref, lw_ref, a_ref, go_ref, kk_ref, ka_ref, rk_ref,
                      lnw_ref, lnb_ref, o_ref, z_scr, *, tt):
    L, N = RWKV_CHUNK, RWKV_HEAD_DIM
    npair = z_scr.shape[0]
    pairs = range(npair)

    @pl.when(pl.program_id(1) == 0)
    def _():
        z_scr[...] = jnp.zeros_like(z_scr)

    ri = lax.broadcasted_iota(jnp.int32, (L, LANES), 0)
    ln = lax.broadcasted_iota(jnp.int32, (L, LANES), 1)
    si = ln % N
    lo = ln < N
    lower_incl = ri >= si
    lower_strict = ri > si
    blk_diag = (ri // RWKV_INV_BLOCK) == (si // RWKV_INV_BLOCK)
    eye = (ri == si).astype(F32)
    tri = (lax.broadcasted_iota(jnp.int32, (L, L), 0) >= lax.broadcasted_iota(jnp.int32, (L, L), 1))
    tri3 = jnp.concatenate([tri.astype(BF16)] * 3, axis=1)
    z_mask = ((lax.broadcasted_iota(jnp.int32, (2 * N, LANES), 0) // N)
              == (lax.broadcasted_iota(jnp.int32, (2 * N, LANES), 1) // N))

    def half_sum(x):
        s0 = jnp.sum(jnp.where(lo, x, 0.0), axis=-1, keepdims=True)
        s1 = jnp.sum(jnp.where(lo, 0.0, x), axis=-1, keepdims=True)
        return jnp.where(lo, s0, s1)

    bd = lambda x: _block_diag(x, lo)
    mk = lambda xs: [_PairMat(x, lo) for x in xs]
    mm = lambda ps, qs: [_pair_mm(p, q) for p, q in zip(ps, qs)]

    def chunk(c, carry):
        r0 = pl.multiple_of(c * L, L)
        sl = [slice(p * LANES, (p + 1) * LANES) for p in pairs]
        ld = lambda ref: [ref[pl.ds(r0, L), s] for s in sl]
        r, k, v, lw, a = ld(r_ref), ld(k_ref), ld(v_ref), ld(lw_ref), ld(a_ref)
        kk = [k[p] * kk_ref[:, sl[p]] for p in pairs]
        kk = [kk[p] / jnp.maximum(jnp.sqrt(half_sum(kk[p] * kk[p])), 1e-12) for p in pairs]
        km = [k[p] * (1.0 + (a[p] - 1.0) * ka_ref[:, sl[p]]) for p in pairs]
        bv = [kk[p] * a[p] for p in pairs]
        cum = [_cumsum_rows(tri3, lw[p]) for p in pairs]
        cum_end = [cum[p][L - 1:L, :] for p in pairs]
        w_inv = [jnp.exp(-cum[p]) for p in pairs]
        w_out = [jnp.exp(cum_end[p] - cum[p]) for p in pairs]
        kk_h = [(kk[p] * jnp.exp(cum[p] - lw[p])).astype(BF16) for p in pairs]
        r_h = [(r[p] * jnp.exp(cum[p])).astype(BF16) for p in pairs]
        b_t = [(bv[p] * w_inv[p]).astype(BF16) for p in pairs]
        k_t = [(km[p] * w_inv[p]).astype(BF16) for p in pairs]
        bbar = [(bv[p] * w_out[p]).astype(BF16) for p in pairs]
        kbar = [(km[p] * w_out[p]).astype(BF16) for p in pairs]
        vb = [v[p].astype(BF16) for p in pairs]
        lhs = [jnp.concatenate([kk_h[p], r_h[p]], axis=0) for p in pairs]
        ab = [_dot_nt(lhs[p], bd(b_t[p])) for p in pairs]
        ak = [_dot_nt(lhs[p], bd(k_t[p])) for p in pairs]
        a_ub = [jnp.where(lower_strict, ab[p][:L], 0.0) for p in pairs]
        a_rb = [jnp.where(lower_incl, ab[p][L:], 0.0).astype(BF16) for p in pairs]
        a_uk = [jnp.where(lower_strict, ak[p][:L], 0.0).astype(BF16) for p in pairs]
        a_rk = [jnp.where(lower_incl, ak[p][L:], 0.0).astype(BF16) for p in pairs]

        n1 = mk([jnp.where(blk_diag, -a_ub[p], 0.0) for p in pairs])
        n2 = mk(mm(n1, n1))
        n4 = mk(mm(n2, n2))
        n8 = mk(mm(n4, n4))
        acc = mk([eye + n1[p].x for p in pairs])
        for nk in (n2, n4):
            prod = mm(acc, nk)
            acc = mk([acc[p].x + prod[p] for p in pairs])
        prod = mm(acc, n8)
        d_inv = mk([acc[p].x + prod[p] for p in pairs])
        l_off = mk([jnp.where(blk_diag, 0.0, a_ub[p]) for p in pairs])
        e1 = mk([-x for x in mm(d_inv, l_off)])
        e2 = mk(mm(e1, e1))
        qm = mk([eye + e1[p].x for p in pairs])
        prod = mm(qm, e2)
        qm = mk([qm[p].x + prod[p] for p in pairs])
        t_inv = mk(mm(qm, d_inv))

        z = [z_scr[p] for p in pairs]
        zb = [z[p].astype(BF16) for p in pairs]
        bdv = [bd(vb[p]) for p in pairs]
        rhs_u = mk([_dot_nt(kk_h[p], zb[p]) + _dot(a_uk[p], bdv[p]) for p in pairs])
        u = [-x for x in mm(t_inv, rhs_u)]
        ub = [u[p].astype(BF16) for p in pairs]
        y = [_dot_nt(r_h[p], zb[p])
             + _dot(jnp.concatenate([a_rb[p], a_rk[p]], axis=1),
                    jnp.concatenate([bd(ub[p]), bdv[p]], axis=0)) for p in pairs]
        upd = [_dot_tn(jnp.concatenate([ub[p], vb[p]], axis=0),
                       jnp.concatenate([bbar[p], kbar[p]], axis=0)) for p in pairs]
        for p in pairs:
            z_scr[p] = z[p] * jnp.exp(cum_end[p]) + jnp.where(z_mask, upd[p], 0.0)

        inv_n = 1.0 / N
        mean = [half_sum(y[p]) * inv_n for p in pairs]
        yc = [y[p] - mean[p] for p in pairs]
        var = [half_sum(yc[p] * yc[p]) * inv_n for p in pairs]
        bonus = [half_sum(r[p] * km[p] * rk_ref[:, sl[p]]) * v[p] for p in pairs]
        for p in pairs:
            yn = yc[p] * lax.rsqrt(var[p] + RWKV_GN_EPS) * lnw_ref[:, sl[p]] + lnb_ref[:, sl[p]]
            o_ref[pl.ds(r0, L), sl[p]] = ((yn + bonus[p]) * go_ref[pl.ds(r0, L), sl[p]]).astype(o_ref.dtype)
        return carry

    lax.fori_loop(0, tt // L, chunk, 0)


def _rwkv_core(r, k, v, lw, a, go, k_k, k_a, r_k, ln_w, ln_b, batch, seq_len, tt=256):
    m, d = r.shape
    nt = seq_len // tt
    npair = d // LANES
    blk = pl.BlockSpec((tt, d), lambda b, i: (b * nt + i, 0))
    par = _const_spec((1, d))
    row1 = lambda z: z.reshape(1, d)
    return pl.pallas_call(
        functools.partial(_rwkv_core_kernel, tt=tt),
        out_shape=jax.ShapeDtypeStruct((m, d), BF16),
        grid=(batch, nt),
        in_specs=[blk] * 6 + [par] * 5,
        out_specs=blk,
        scratch_shapes=[pltpu.VMEM((npair, 2 * RWKV_HEAD_DIM, LANES), F32)],
        compiler_params=_cparams(("arbitrary", "arbitrary")),
        name="rwkv_core",
    )(r, k, v, lw, a, go, row1(k_k), row1(k_a), row1(r_k), row1(ln_w), row1(ln_b))


def _rwkv_layer(x, norm_g, mu, w_r, w_k, w_v, w0, w_w1, w_w2, a0, a_w1, a_w2, g_w1, g_w2,
                k_k, k_a, r_k, ln_w, ln_b, batch, seq_len):
    r, k, v, lw, a, go = _rwkv_proj(x, norm_g, mu, w_r, w_k, w_v, w0, w_w1, w_w2, a0, a_w1, a_w2,
                                    g_w1, g_w2, seq_len)
    return _rwkv_core(r, k, v, lw, a, go, k_k, k_a, r_k, ln_w, ln_b, batch, seq_len)


def kernel(x, norm_mixer, norm_ffn, ffn_w_up, ffn_conv_w, ffn_conv_b, ffn_w_down, nsa_w_in, nsa_pe_k, nsa_pe_v, nsa_cmp_k_w1, nsa_cmp_k_w2, nsa_cmp_v_w1, nsa_cmp_v_w2, nsa_b_gate, nsa_w_out, mlstm_w_in, mlstm_conv_w, mlstm_conv_b, mlstm_b_gates, mlstm_norm, mlstm_w_out, rwkv_mu, rwkv_w_r, rwkv_w_k, rwkv_w_v, rwkv_w_o, rwkv_w0, rwkv_w_w1, rwkv_w_w2, rwkv_a0, rwkv_a_w1, rwkv_a_w2, rwkv_g_w1, rwkv_g_w2, rwkv_k_k, rwkv_k_a, rwkv_r_k, rwkv_ln_w, rwkv_ln_b, final_norm):
    batch, seq_len, d = x.shape
    depth = norm_mixer.shape[0]
    rope = _rope_tables(seq_len)
    xf = x.reshape(batch * seq_len, d)
    for i in range(depth):
        kind, j = i % 3, i // 3
        if kind == 0:
            w_o = nsa_w_out[j]
            o = _nsa_layer(xf, norm_mixer[i], nsa_w_in[j], nsa_pe_k[j], nsa_pe_v[j], nsa_cmp_k_w1[j],
                           nsa_cmp_k_w2[j], nsa_cmp_v_w1[j], nsa_cmp_v_w2[j], nsa_b_gate[j], rope, batch, seq_len)
        elif kind == 1:
            w_o = mlstm_w_out[j]
            o = _mlstm_layer(xf, norm_mixer[i], mlstm_w_in[j], mlstm_conv_w[j], mlstm_conv_b[j],
                             mlstm_b_gates[j], mlstm_norm[j], batch, seq_len)
        else:
            w_o = rwkv_w_o[j]
            o = _rwkv_layer(xf, norm_mixer[i], rwkv_mu[j], rwkv_w_r[j], rwkv_w_k[j], rwkv_w_v[j],
                            rwkv_w0[j], rwkv_w_w1[j], rwkv_w_w2[j], rwkv_a0[j],
                            rwkv_a_w1[j], rwkv_a_w2[j], rwkv_g_w1[j], rwkv_g_w2[j], rwkv_k_k[j],
                            rwkv_k_a[j], rwkv_r_k[j], rwkv_ln_w[j], rwkv_ln_b[j], batch, seq_len)
        xf = _ffn(xf, o, w_o, norm_ffn[i], ffn_w_up[i], ffn_conv_w[i], ffn_conv_b[i], ffn_w_down[i],
                  seq_len, final_g=final_norm if i == depth - 1 else None)
    return xf.reshape(batch, seq_len, d)
```

```python
import functools
import math

import jax
import jax.numpy as jnp
import numpy as np
from jax import lax
from jax.experimental import pallas as pl
from jax.experimental.pallas import tpu as pltpu

F32 = jnp.float32
BF16 = jnp.bfloat16

D_MODEL = 1024
DEPTH = 4
NORM_EPS = 1e-6
ROPE_THETA = 500000.0

NSA_HEAD_DIM = 64
NSA_HEADS = 16
NSA_GROUPS = 4
NSA_REP = NSA_HEADS // NSA_GROUPS
NSA_ROT_DIM = 16
CMP_BLOCK = 32
CMP_STRIDE = 16
CMP_HIDDEN = 256
SEL_BLOCK = 64
SEL_TOPK = 16
WINDOW = 512
NSA_KV = NSA_GROUPS * NSA_HEAD_DIM
NSA_VT_ROWS = NSA_HEAD_DIM + 16

MLSTM_HEADS = 8
MLSTM_QK_DIM = 64
MLSTM_V_DIM = 128
MLSTM_TILE = 128
MLSTM_CONV = 4

RWKV_HEAD_DIM = 64
RWKV_HEADS = 16
RWKV_GN_EPS = 64e-5
RWKV_CHUNK = 64
RWKV_INV_BLOCK = 16

FFN_DIM = 2816
FFN_CONV = 3
FFN_CHUNK = 256

LOG2E = math.log2(math.e)
MASKED = -1e30

LANES = 128
SUBLANES = 8
VMEM_LIMIT = 56 * 1024 * 1024


def _dot(a, b):
    return jnp.dot(a, b, preferred_element_type=F32)


def _dot_nt(a, b):
    return lax.dot_general(a, b, (((1,), (1,)), ((), ())), preferred_element_type=F32)


def _dot_tn(a, b):
    return lax.dot_general(a, b, (((0,), (0,)), ((), ())), preferred_element_type=F32)


def _split2(x):
    hi = x.astype(BF16)
    return hi, (x - hi.astype(F32)).astype(BF16)


def _split3(x):
    hi = x.astype(BF16)
    r1 = x - hi.astype(F32)
    mid = r1.astype(BF16)
    return hi, mid, (r1 - mid.astype(F32)).astype(BF16)


def _cumsum_rows(tri3, x):
    return _dot(tri3, jnp.concatenate(_split3(x), axis=0))


def _cumsum_lanes(x, triu3):
    return _dot(jnp.concatenate(_split3(x), axis=1), triu3)


def _rms(x, g):
    ms = jnp.mean(x * x, axis=-1, keepdims=True)
    return x * lax.rsqrt(ms + NORM_EPS) * g


def _sigmoid(x):
    return 1.0 / (1.0 + jnp.exp(-x))


def _cparams(sem):
    return pltpu.CompilerParams(dimension_semantics=sem, vmem_limit_bytes=VMEM_LIMIT)


def _const_spec(shape):
    n = len(shape)
    return pl.BlockSpec(shape, lambda *_: (0,) * n, pipeline_mode=pl.Buffered(1))


def _ffn_kernel(res_ref, a_ref, wo_ref, g_ref, wu_ref, cw_ref, cb_ref, wd_ref, fg_ref, o_ref,
                h_scr, carry_scr, ga_scr, va_scr, gb_scr, vb_scr, *, tiles_per_seq, n_chunks, final_norm):
    tm = res_ref.shape[0]
    fc = FFN_CHUNK
    x = res_ref[...] + _dot(a_ref[...], wo_ref[...])
    h_scr[...] = _rms(x, g_ref[...]).astype(BF16)
    o_ref[...] = x
    seq_start = (pl.program_id(0) % tiles_per_seq) == 0
    rows = lax.broadcasted_iota(jnp.int32, (tm, fc), 0)
    cols = lambda c, base=0: pl.ds(pl.multiple_of(base + c * fc, LANES), fc)

    def up(c, g_scr, v_scr):
        h = h_scr[...]
        g_scr[...] = _dot(h, wu_ref[:, cols(c)])
        v_scr[...] = _dot(h, wu_ref[:, cols(c, FFN_DIM)])

    def down(c, g_scr, v_scr):
        gate, val = g_scr[...], v_scr[...]
        prev = carry_scr[:, cols(c)]
        prev = jnp.where(seq_start, 0.0, prev)
        p1 = prev[SUBLANES - 1:SUBLANES, :]
        p2 = prev[SUBLANES - 2:SUBLANES - 1, :]
        carry_scr[:, cols(c)] = gate[tm - SUBLANES:, :]
        g1 = jnp.where(rows == 0, p1, pltpu.roll(gate, 1, 0))
        g2 = jnp.where(rows == 0, p2, jnp.where(rows == 1, p1, pltpu.roll(gate, 2, 0)))
        cw = cw_ref[:, cols(c)]
        y = cw[2:3, :] * gate + cw[1:2, :] * g1 + cw[0:1, :] * g2 + cb_ref[:, cols(c)]
        act = (y * _sigmoid(y) * val).astype(BF16)
        o_ref[...] += _dot(act, wd_ref[pl.ds(pl.multiple_of(c * fc, fc), fc), :])

    assert n_chunks % 2 == 1

    def chunk_pair(j, carry):
        up(2 * j + 1, gb_scr, vb_scr)
        down(2 * j, ga_scr, va_scr)
        up(2 * j + 2, ga_scr, va_scr)
        down(2 * j + 1, gb_scr, vb_scr)
        return carry

    up(0, ga_scr, va_scr)
    lax.fori_loop(0, n_chunks // 2, chunk_pair, 0)
    down(n_chunks - 1, ga_scr, va_scr)
    if final_norm:
        o_ref[...] = _rms(o_ref[...], fg_ref[...])


def _ffn(res, a, w_o, g, w_up, conv_w, conv_b, w_down, seq_len, final_g=None, tm=1024):
    m, d = res.shape
    nc = FFN_DIM // FFN_CHUNK
    cw = jnp.pad(conv_w, ((0, SUBLANES - FFN_CONV), (0, 0)))
    row = lambda i: (i, 0)
    fg = jnp.ones((d,), F32) if final_g is None else final_g
    kern = functools.partial(_ffn_kernel, tiles_per_seq=seq_len // tm, n_chunks=nc,
                             final_norm=final_g is not None)
    return pl.pallas_call(
        kern,
        out_shape=jax.ShapeDtypeStruct((m, d), F32),
        grid=(m // tm,),
        in_specs=[pl.BlockSpec((tm, d), row), pl.BlockSpec((tm, d), row), _const_spec((d, d)),
                  _const_spec((1, d)), _const_spec((d, 2 * FFN_DIM)),
                  _const_spec((SUBLANES, FFN_DIM)), _const_spec((1, FFN_DIM)),
                  _const_spec((FFN_DIM, d)), _const_spec((1, d))],
        out_specs=pl.BlockSpec((tm, d), row),
        scratch_shapes=[pltpu.VMEM((tm, d), BF16),
                        pltpu.VMEM((SUBLANES, FFN_DIM), F32)]
                       + [pltpu.VMEM((tm, FFN_CHUNK), F32)] * 4,
        compiler_params=_cparams(("arbitrary",)),
        name="conv_ffn",
    )(res, a, w_o.astype(BF16), g.reshape(1, d), w_up.astype(BF16), cw, conv_b.reshape(1, FFN_DIM),
      w_down.astype(BF16), fg.reshape(1, d))


def _nsa_proj_kernel(x_ref, g_ref, w_ref, wvt_ref, bg_ref, rc_ref, rs1_ref, rs2_ref,
                     qp_ref, qr_ref, kc_ref, vc_ref, ks_ref, vs_ref, kw_ref, vw_ref, gate_ref,
                     *, tiles_per_seq):
    hn = _rms(x_ref[...], g_ref[...]).astype(BF16)
    y = _dot(hn, w_ref[...])
    vt = _dot_nt(wvt_ref[...], hn)
    rc, rs1, rs2 = rc_ref[...], rs1_ref[...], rs2_ref[...]
    dh = NSA_HEAD_DIM

    def rope(z):
        return z * rc + pltpu.roll(z, 8, 1) * rs1 + pltpu.roll(z, LANES - 8, 1) * rs2

    scale = dh ** -0.5 * LOG2E
    for j in range(D_MODEL // LANES):
        q = y[:, j * LANES:(j + 1) * LANES] * scale
        qp_ref[:, j * LANES:(j + 1) * LANES] = q.astype(BF16)
        qr_ref[:, j * LANES:(j + 1) * LANES] = rope(q).astype(BF16)

    def kv_chunk(idx):
        return y[:, D_MODEL + idx * NSA_KV:D_MODEL + (idx + 1) * NSA_KV]

    def split_groups(z, ref, dtype):
        for g in range(NSA_GROUPS):
            ref[g] = z[:, g * dh:(g + 1) * dh].astype(dtype)

    def rope256(z):
        return jnp.concatenate([rope(z[:, :LANES]), rope(z[:, LANES:])], axis=1)

    split_groups(kv_chunk(0), kc_ref, F32)
    split_groups(kv_chunk(1), vc_ref, F32)
    tm = y.shape[0]
    t_pos = (pl.program_id(0) % tiles_per_seq) * tm + lax.broadcasted_iota(jnp.int32, (tm, LANES), 0)
    onehot = (lax.broadcasted_iota(jnp.int32, (tm, LANES), 1) == t_pos // SEL_BLOCK).astype(F32)
    ksel = rope256(kv_chunk(2))
    zpad = jnp.zeros((tm, LANES - dh), F32)
    for g in range(NSA_GROUPS):
        ks_ref[g] = jnp.concatenate([ksel[:, g * dh:(g + 1) * dh], zpad, onehot], axis=1).astype(BF16)
    split_groups(rope256(kv_chunk(3)), kw_ref, BF16)
    ones_pad = (lax.broadcasted_iota(jnp.int32, (NSA_VT_ROWS - dh, tm), 0) == 0).astype(F32)
    for g in range(NSA_GROUPS):
        vs_ref[g] = jnp.concatenate([vt[g * dh:(g + 1) * dh], ones_pad], axis=0).astype(BF16)
        vw_ref[g] = jnp.concatenate([vt[NSA_KV + g * dh:NSA_KV + (g + 1) * dh], ones_pad], axis=0).astype(BF16)
    gate = y[:, D_MODEL + 4 * NSA_KV:] + bg_ref[...]
    gate_ref[...] = _sigmoid(gate)


def _rope_tables(seq_len):
    half = NSA_ROT_DIM // 2
    inv_freq = ROPE_THETA ** (-jnp.arange(half, dtype=F32) / half)
    ang = jnp.arange(seq_len, dtype=F32)[:, None] * inv_freq[None, :]
    cos, sin = jnp.cos(ang), jnp.sin(ang)
    zeros = jnp.zeros((seq_len, NSA_HEAD_DIM - NSA_ROT_DIM), F32)
    z8 = jnp.zeros((seq_len, half), F32)
    rc = jnp.concatenate([cos, cos, zeros + 1.0], axis=1)
    rs1 = jnp.concatenate([z8, sin, zeros], axis=1)
    rs2 = jnp.concatenate([-sin, z8, zeros], axis=1)
    two = lambda t: jnp.concatenate([t, t], axis=1)
    return two(rc), two(rs1), two(rs2)


def _nsa_proj(x, g, w_in, b_gate, rope, seq_len, tm=512):
    m, d = x.shape
    n_kv = 6 * NSA_KV
    kv = lambda idx: w_in[:, D_MODEL + idx * NSA_KV:D_MODEL + (idx + 1) * NSA_KV]
    wg = w_in[:, D_MODEL + n_kv:].reshape(d, NSA_GROUPS, NSA_REP * 3)
    wg = jnp.pad(wg, ((0, 0), (0, 0), (0, LANES - NSA_REP * 3))).reshape(d, NSA_GROUPS * LANES)
    w = jnp.concatenate([w_in[:, :D_MODEL], kv(0), kv(1), kv(2), kv(4), wg], axis=1).astype(BF16)
    wvt = jnp.concatenate([kv(3), kv(5)], axis=1).T.astype(BF16)
    bg = jnp.pad(b_gate.reshape(NSA_GROUPS, NSA_REP * 3), ((0, 0), (0, LANES - NSA_REP * 3)))
    bg = bg.reshape(1, NSA_GROUPS * LANES)
    n = w.shape[1]
    tps = seq_len // tm
    row = lambda i: (i, 0)
    rope_spec = pl.BlockSpec((tm, LANES), lambda i: (i % tps, 0))
    assert seq_len // SEL_BLOCK <= LANES
    g_out = lambda dt, w=NSA_HEAD_DIM: jax.ShapeDtypeStruct((NSA_GROUPS, m, w), dt)
    g_spec = pl.BlockSpec((NSA_GROUPS, tm, NSA_HEAD_DIM), lambda i: (0, i, 0))
    ks_spec = pl.BlockSpec((NSA_GROUPS, tm, 2 * LANES), lambda i: (0, i, 0))
    vt_out = jax.ShapeDtypeStruct((NSA_GROUPS, NSA_VT_ROWS, m), BF16)
    vt_spec = pl.BlockSpec((NSA_GROUPS, NSA_VT_ROWS, tm), lambda i: (0, 0, i))
    return pl.pallas_call(
        functools.partial(_nsa_proj_kernel, tiles_per_seq=tps),
        out_shape=(jax.ShapeDtypeStruct((m, d), BF16), jax.ShapeDtypeStruct((m, d), BF16),
                   g_out(F32), g_out(F32), g_out(BF16, 2 * LANES), vt_out, g_out(BF16), vt_out,
                   jax.ShapeDtypeStruct((m, NSA_GROUPS * LANES), F32)),
        grid=(m // tm,),
        in_specs=[pl.BlockSpec((tm, d), row), _const_spec((1, d)), _const_spec((d, n)),
                  _const_spec((2 * NSA_KV, d)),
                  _const_spec((1, NSA_GROUPS * LANES)), rope_spec, rope_spec, rope_spec],
        out_specs=(pl.BlockSpec((tm, d), row), pl.BlockSpec((tm, d), row),
                   g_spec, g_spec, ks_spec, vt_spec, g_spec, vt_spec,
                   pl.BlockSpec((tm, NSA_GROUPS * LANES), row)),
        compiler_params=_cparams(("parallel",)),
        name="nsa_proj",
    )(x, g.reshape(1, d), w, wvt, bg, *rope)


def _gelu_tanh(x):
    return 0.5 * x * (1.0 + jnp.tanh(math.sqrt(2.0 / math.pi) * (x + 0.044715 * (x * x * x))))


def _compress_kernel(zk_ref, zv_ref, pek_ref, pev_ref, w1k_ref, w2k_ref, w1v_ref, w2v_ref,
                     kc_ref, vc_ref):
    nrow = zk_ref.shape[0]
    rows = lax.broadcasted_iota(jnp.int32, (nrow, NSA_HEAD_DIM), 0)

    def one(z_ref, pe_ref, w1_ref, w2_ref, o_ref):
        z = z_ref[...]
        a = _dot((z + pe_ref[0:1, :]).astype(BF16), w1_ref[0])
        b = _dot((z + pe_ref[1:2, :]).astype(BF16), w1_ref[1])
        hid = a + pltpu.roll(b, nrow - 1, 0)
        out = _dot(_gelu_tanh(hid).astype(BF16), w2_ref[...])
        o_ref[...] = jnp.where(rows == nrow - 1, 0.0, out).astype(o_ref.dtype)

    one(zk_ref, pek_ref, w1k_ref, w2k_ref, kc_ref)
    one(zv_ref, pev_ref, w1v_ref, w2v_ref, vc_ref)


def _compress(kc_raw, vc_raw, pe_k, pe_v, w1k, w2k, w1v, w2v, seq_len):
    g, m, dh = kc_raw.shape
    half = CMP_STRIDE * dh
    nchunk = seq_len // CMP_STRIDE
    zk = kc_raw.reshape(g * m // CMP_STRIDE, half)
    zv = vc_raw.reshape(g * m // CMP_STRIDE, half)
    pe2 = lambda pe: pe.reshape(2, half)
    w1 = lambda w: w.astype(BF16).reshape(2, half, CMP_HIDDEN)
    nblk = zk.shape[0] // nchunk
    row = lambda i: (i, 0)
    return pl.pallas_call(
        _compress_kernel,
        out_shape=(jax.ShapeDtypeStruct((zk.shape[0], dh), BF16),
                   jax.ShapeDtypeStruct((zk.shape[0], dh), BF16)),
        grid=(nblk,),
        in_specs=[pl.BlockSpec((nchunk, half), row), pl.BlockSpec((nchunk, half), row),
                  _const_spec((2, half)), _const_spec((2, half)),
                  _const_spec((2, half, CMP_HIDDEN)), _const_spec((CMP_HIDDEN, dh)),
                  _const_spec((2, half, CMP_HIDDEN)), _const_spec((CMP_HIDDEN, dh))],
        out_specs=(pl.BlockSpec((nchunk, dh), row), pl.BlockSpec((nchunk, dh), row)),
        compiler_params=_cparams(("parallel",)),
        name="nsa_compress",
    )(zk, zv, pe2(pe_k), pe2(pe_v), w1(w1k), w2k.astype(BF16), w1(w1v), w2v.astype(BF16))


def _nsa_cmp_kernel(q_ref, kc_ref, vc_ref, gate_ref, ovt_ref, oc_ref, selt_ref, *, tq):
    qi = pl.program_id(2)
    dh = NSA_HEAD_DIM
    q = q_ref[...]
    qs = jnp.concatenate([q[:, r * dh:(r + 1) * dh] for r in range(NSA_REP)], axis=0)
    kc, vc = kc_ref[...], vc_ref[...]
    ncmp = kc.shape[0]
    t = qi * tq + lax.broadcasted_iota(jnp.int32, (ncmp, tq), 1)
    cmp_end = lax.broadcasted_iota(jnp.int32, (ncmp, tq), 0) * CMP_STRIDE + (CMP_BLOCK - 1)
    neg = jnp.where(cmp_end <= t, 0.0, -jnp.inf)
    heads = range(NSA_REP)
    st = [_dot_nt(kc, qs[r * tq:(r + 1) * tq]) + neg for r in heads]
    mx = [jnp.max(st[r], axis=0, keepdims=True) for r in heads]
    mx = [jnp.where(mx[r] == -jnp.inf, 0.0, mx[r]) for r in heads]
    e = [jnp.exp2(st[r] - mx[r]) for r in heads]
    p = [e[r] / jnp.maximum(jnp.sum(e[r], axis=0, keepdims=True), 1e-30) for r in heads]
    o_t = [_dot_tn(vc, p[r].astype(BF16)) for r in heads]
    gate = gate_ref[...]
    for r in heads:
        o_r = jnp.concatenate([o_t[r], o_t[r]], axis=0).T[:, :dh]
        oc_ref[:, r * dh:(r + 1) * dh] = o_r * gate[:, 3 * r:3 * r + 1]
    psum = sum(p)
    imp = _dot(ovt_ref[...], jnp.concatenate(_split3(psum), axis=0))
    nsel = imp.shape[0]
    blk = lax.broadcasted_iota(jnp.int32, (nsel, LANES), 0)
    blk_f = blk.astype(F32)
    for cb in range(tq // LANES):
        csl = slice(cb * LANES, (cb + 1) * LANES)
        tb = (qi * tq + cb * LANES + lax.broadcasted_iota(jnp.int32, (nsel, LANES), 1)) // SEL_BLOCK
        forced = (blk == 0) | (blk == tb) | (blk == tb - 1)
        vals = jnp.where(forced, -jnp.inf, jnp.where(blk <= tb, imp[:, csl], -1.0))
        sel = jnp.where(forced, 1.0, 0.0)
        for _ in range(SEL_TOPK - 3):
            top = jnp.max(vals, axis=0, keepdims=True)
            first = jnp.min(jnp.where(vals == top, blk_f, float(nsel)), axis=0, keepdims=True)
            pick = blk_f == first
            sel = jnp.where(pick, 1.0, sel)
            vals = jnp.where(pick, -jnp.inf, vals)
        selt_ref[:, csl] = sel.astype(selt_ref.dtype)


def _overlap_matrix_t3(ncmp_pad, nsel):
    c = np.arange(ncmp_pad)[None, :]
    s = np.arange(nsel)[:, None]
    cmp_start = c * CMP_STRIDE
    cmp_end = cmp_start + CMP_BLOCK - 1
    blk_start = s * SEL_BLOCK
    ov = ((cmp_end >= blk_start) & (cmp_start <= blk_start + SEL_BLOCK - 1)).astype(np.float32)
    return jnp.asarray(np.concatenate([ov, ov, ov], axis=1), dtype=BF16)


def _nsa_cmp(qp, kc, vc, gates, batch, seq_len, tq=512):
    m, d = qp.shape
    nq = seq_len // tq
    ncmp = seq_len // CMP_STRIDE
    nsel = seq_len // SEL_BLOCK
    ovt3 = _overlap_matrix_t3(ncmp, nsel)
    qmap = lambda b, g, i: (b * nq + i, g)
    kmap = lambda b, g, i: (g * batch + b, 0)
    return pl.pallas_call(
        functools.partial(_nsa_cmp_kernel, tq=tq),
        out_shape=(jax.ShapeDtypeStruct((m, d), F32),
                   jax.ShapeDtypeStruct((NSA_GROUPS, nsel, m), BF16)),
        grid=(batch, NSA_GROUPS, nq),
        in_specs=[pl.BlockSpec((tq, NSA_KV), qmap),
                  pl.BlockSpec((ncmp, NSA_HEAD_DIM), kmap), pl.BlockSpec((ncmp, NSA_HEAD_DIM), kmap),
                  pl.BlockSpec((tq, LANES), qmap), _const_spec((nsel, 3 * ncmp))],
        out_specs=(pl.BlockSpec((tq, NSA_KV), qmap),
                   pl.BlockSpec((None, nsel, tq), lambda b, g, i: (g, 0, b * nq + i))),
        compiler_params=_cparams(("parallel", "parallel", "parallel")),
        name="nsa_cmp_topk",
    )(qp, kc, vc, gates, ovt3)


def _nsa_sel_kernel(q_ref, ks_ref, vs_ref, kw_ref, vw_ref, sel_ref, gate_ref, oc_ref, o_ref,
                    m_scr, acc_scr, sta_scr, stb_scr, *, tq, tk):
    qi = pl.program_id(2)
    dh = NSA_HEAD_DIM
    cols = NSA_REP * tq
    q = q_ref[...]
    qs = jnp.concatenate([q[:, r * dh:(r + 1) * dh] for r in range(NSA_REP)], axis=0)
    selt = sel_ref[...].astype(F32)
    nsel = selt.shape[0]
    bmask_t = jnp.where(selt > 0.5, 0.0, MASKED)
    if nsel < LANES:
        bmask_t = jnp.concatenate([bmask_t, jnp.zeros((LANES - nsel, tq), F32)], axis=0)
    bmask = bmask_t.T.astype(BF16)
    zpad = jnp.zeros((tq, LANES - dh), BF16)
    qa = jnp.concatenate([jnp.concatenate([q[:, r * dh:(r + 1) * dh], zpad, bmask], axis=1)
                          for r in range(NSA_REP)], axis=0)
    q0 = qi * tq
    key_iota = lax.broadcasted_iota(jnp.int32, (tk, tq), 0)
    t_pos = q0 + lax.broadcasted_iota(jnp.int32, (tk, tq), 1)

    m_scr[...] = jnp.full((1, cols), -jnp.inf, F32)
    acc_scr[...] = jnp.zeros((NSA_VT_ROWS, cols), F32)

    heads = range(NSA_REP)
    hsl = [slice(r * tq, (r + 1) * tq) for r in heads]

    def put_scores(scr, ki):
        k = ks_ref[pl.ds(pl.multiple_of(ki * tk, tk), tk), :]
        for r in heads:
            scr[r] = _dot_nt(k, qa[hsl[r]])

    def get_scores(scr):
        return [scr[r] for r in heads]

    def consume(ki, st, causal):
        k0 = pl.multiple_of(ki * tk, tk)
        vt = vs_ref[:, pl.ds(k0, tk)]
        if causal:
            neg = jnp.where(k0 + key_iota <= t_pos, 0.0, -jnp.inf)
            st = [s + neg for s in st]
        m_old = [m_scr[:, hsl[r]] for r in heads]
        m_new = [jnp.maximum(m_old[r], jnp.max(st[r], axis=0, keepdims=True)) for r in heads]
        alpha = [jnp.exp2(m_old[r] - m_new[r]) for r in heads]
        p = [jnp.exp2(st[r] - m_new[r]).astype(BF16) for r in heads]
        pv = [_dot(vt, p[r]) for r in heads]
        for r in heads:
            acc_scr[:, hsl[r]] = alpha[r] * acc_scr[:, hsl[r]] + pv[r]
            m_scr[:, hsl[r]] = m_new[r]

    def tile_pair(j, carry):
        even = get_scores(sta_scr)
        put_scores(stb_scr, 2 * j + 1)
        consume(2 * j, even, False)
        odd = get_scores(stb_scr)
        put_scores(sta_scr, 2 * j + 2)
        consume(2 * j + 1, odd, False)
        return carry

    last = (q0 + tq - 1) // tk
    put_scores(sta_scr, 0)
    lax.fori_loop(0, last // 2, tile_pair, 0)

    @pl.when(last % 2 == 0)
    def _():
        consume(last, get_scores(sta_scr), True)

    @pl.when(last % 2 == 1)
    def _():
        even = get_scores(sta_scr)
        put_scores(stb_scr, last)
        consume(last - 1, even, False)
        consume(last, get_scores(stb_scr), True)
    o_sel = acc_scr[0:dh, :] / jnp.maximum(acc_scr[dh:dh + 1, :], 1e-30)

    wlen = tq + WINDOW
    w0 = pl.multiple_of(jnp.maximum(q0 - WINDOW, 0), tq)
    kw = kw_ref[pl.ds(w0, wlen), :]
    vwt = vw_ref[:, pl.ds(w0, wlen)]
    pw = w0 + lax.broadcasted_iota(jnp.int32, (wlen, tq), 0)
    tw = q0 + lax.broadcasted_iota(jnp.int32, (wlen, tq), 1)
    neg_w = jnp.where((pw <= tw) & (pw > tw - WINDOW), 0.0, -jnp.inf)
    sw = [_dot_nt(kw, qs[hsl[r]]) + neg_w for r in heads]
    mw = [jnp.max(sw[r], axis=0, keepdims=True) for r in heads]
    mw = [jnp.where(mw[r] == -jnp.inf, 0.0, mw[r]) for r in heads]
    ew = [jnp.exp2(sw[r] - mw[r]).astype(BF16) for r in heads]
    pvw = [_dot(vwt, ew[r]) for r in heads]
    o_win = [pvw[r][0:dh] / jnp.maximum(pvw[r][dh:dh + 1], 1e-30) for r in heads]

    gate = gate_ref[...]
    oc = oc_ref[...]
    for r in heads:
        o2r = jnp.concatenate([o_sel[:, hsl[r]], o_win[r]], axis=0).T
        tot = (oc[:, r * dh:(r + 1) * dh] + o2r[:, :dh] * gate[:, 3 * r + 1:3 * r + 2]
               + o2r[:, dh:] * gate[:, 3 * r + 2:3 * r + 3])
        o_ref[:, r * dh:(r + 1) * dh] = tot.astype(o_ref.dtype)


def _nsa_sel(qr, ks, vs, kw, vw, sel, gates, oc, batch, seq_len, tq=512, tk=512):
    m, d = qr.shape
    nq = seq_len // tq
    nsel = seq_len // SEL_BLOCK
    qmap = lambda b, g, i: (b * nq + i, g)
    kvmap = lambda b, g, i: (g, b, 0)
    kv_spec = pl.BlockSpec((None, seq_len, NSA_HEAD_DIM), kvmap)
    ks_spec = pl.BlockSpec((None, seq_len, 2 * LANES), kvmap)
    vt_spec = pl.BlockSpec((None, NSA_VT_ROWS, seq_len), lambda b, g, i: (g, 0, b))
    cols = NSA_REP * tq
    assert tk % tq == 0
    return pl.pallas_call(
        functools.partial(_nsa_sel_kernel, tq=tq, tk=tk),
        out_shape=jax.ShapeDtypeStruct((m, d), BF16),
        grid=(batch, NSA_GROUPS, nq),
        in_specs=[pl.BlockSpec((tq, NSA_KV), qmap), ks_spec, vt_spec, kv_spec, vt_spec,
                  pl.BlockSpec((None, nsel, tq), lambda b, g, i: (g, 0, b * nq + i)),
                  pl.BlockSpec((tq, LANES), qmap), pl.BlockSpec((tq, NSA_KV), qmap)],
        out_specs=pl.BlockSpec((tq, NSA_KV), qmap),
        scratch_shapes=[pltpu.VMEM((1, cols), F32), pltpu.VMEM((NSA_VT_ROWS, cols), F32),
                        pltpu.VMEM((NSA_REP, tk, tq), F32), pltpu.VMEM((NSA_REP, tk, tq), F32)],
        compiler_params=_cparams(("parallel", "parallel", "parallel")),
        name="nsa_sel_win",
    )(qr, ks, vs, kw, vw, sel, gates, oc)


def _nsa_layer(x, norm_g, w_in, pe_k, pe_v, w1k, w2k, w1v, w2v, b_gate, rope, batch, seq_len):
    qp, qr, kc_raw, vc_raw, ks, vs, kw, vw, gates = _nsa_proj(x, norm_g, w_in, b_gate, rope, seq_len)
    kc, vc = _compress(kc_raw, vc_raw, pe_k, pe_v, w1k, w2k, w1v, w2v, seq_len)
    oc, sel = _nsa_cmp(qp, kc, vc, gates, batch, seq_len)
    return _nsa_sel(qr, ks, vs, kw, vw, sel, gates, oc, batch, seq_len)


def _mlstm_proj_kernel(x_ref, g_ref, w_ref, wot_ref, wt_ref, bcol_ref, brow_ref,
                       qk_ref, v_ref, ot_ref, gc_ref, gr_ref):
    hn = _rms(x_ref[...], g_ref[...]).astype(BF16)
    y = _dot(hn, w_ref[...])
    d = D_MODEL
    qk_ref[...] = y[:, :d]
    v_ref[...] = y[:, d:2 * d].astype(BF16)
    ot_ref[...] = _sigmoid(_dot_nt(wot_ref[...], hn))
    gc_ref[...] = y[:, 2 * d:] + bcol_ref[...]
    gr_ref[...] = _dot_nt(wt_ref[...], hn) + brow_ref[...]


def _mlstm_proj(x, g, w_in, b_gates, tm=512):
    m, d = x.shape
    h, dk = MLSTM_HEADS, MLSTM_QK_DIM
    wq = w_in[:, :h * dk].reshape(d, h, dk)
    wk = w_in[:, h * dk:2 * h * dk].reshape(d, h, dk)
    wqk = jnp.concatenate([wq, wk], axis=2).reshape(d, 2 * h * dk)
    wv = w_in[:, d:2 * d]
    wif = w_in[:, 2 * d:2 * d + 2 * h]
    wo = w_in[:, 2 * d + 2 * h:]
    w = jnp.concatenate([wqk, wv, jnp.pad(wif, ((0, 0), (0, LANES - 2 * h)))], axis=1).astype(BF16)
    wot = wo.T.astype(BF16)
    wt = wif.T.astype(BF16)
    bcol = jnp.pad(b_gates, (0, LANES - 2 * h)).reshape(1, LANES)
    brow = b_gates.reshape(2 * h, 1)
    n = w.shape[1]
    row = lambda i: (i, 0)
    col = lambda i: (0, i)
    return pl.pallas_call(
        _mlstm_proj_kernel,
        out_shape=(jax.ShapeDtypeStruct((m, d), F32), jax.ShapeDtypeStruct((m, d), BF16),
                   jax.ShapeDtypeStruct((d, m), F32), jax.ShapeDtypeStruct((m, LANES), F32),
                   jax.ShapeDtypeStruct((2 * h, m), F32)),
        grid=(m // tm,),
        in_specs=[pl.BlockSpec((tm, d), row), _const_spec((1, d)), _const_spec((d, n)), _const_spec((d, d)),
                  _const_spec((2 * h, d)), _const_spec((1, LANES)), _const_spec((2 * h, 1))],
        out_specs=(pl.BlockSpec((tm, d), row), pl.BlockSpec((tm, d), row), pl.BlockSpec((d, tm), col),
                   pl.BlockSpec((tm, LANES), row), pl.BlockSpec((2 * h, tm), col)),
        compiler_params=_cparams(("parallel",)),
        name="mlstm_proj",
    )(x, g.reshape(1, d), w, wot, wt, bcol, brow)


def _log_sigmoid(x):
    return jnp.minimum(x, 0.0) - jnp.log(1.0 + jnp.exp(-jnp.abs(x)))


def _mlstm_core_kernel(qk_ref, v_ref, og_ref, gc_ref, gr_ref, cw_ref, cb_ref, ng_ref, o_ref,
                       qkc_scr, ext_scr, c_scr, n_scr, m_scr, *, tt):
    L = MLSTM_TILE
    H, dk, dv = MLSTM_HEADS, MLSTM_QK_DIM, MLSTM_V_DIM
    seq_start = pl.program_id(1) == 0

    @pl.when(seq_start)
    def _():
        c_scr[...] = jnp.zeros_like(c_scr)
        n_scr[...] = jnp.zeros_like(n_scr)
        m_scr[...] = jnp.zeros_like(m_scr)
        ext_scr[0:SUBLANES, :] = jnp.zeros((SUBLANES, ext_scr.shape[1]), F32)

    ext_scr[SUBLANES:, :] = qk_ref[...]
    is_k = lax.broadcasted_iota(jnp.int32, (tt, 2 * dk), 1) >= dk
    for h in range(H):
        hsl = slice(h * 2 * dk, (h + 1) * 2 * dk)
        cw = cw_ref[:, hsl]
        acc = cb_ref[:, hsl] + cw[MLSTM_CONV - 1:MLSTM_CONV, :] * ext_scr[SUBLANES:, hsl]
        for s in range(1, MLSTM_CONV):
            acc = acc + cw[MLSTM_CONV - 1 - s:MLSTM_CONV - s, :] * ext_scr[SUBLANES - s:SUBLANES - s + tt, hsl]
        act = acc * _sigmoid(acc)
        qkc_scr[:, hsl] = jnp.where(is_k, act * dk ** -0.5, act)
    ext_scr[0:SUBLANES, :] = ext_scr[tt:tt + SUBLANES, :]

    src = lax.broadcasted_iota(jnp.int32, (L, L), 0)
    tgt = lax.broadcasted_iota(jnp.int32, (L, L), 1)
    causal = src <= tgt
    tri3 = jnp.concatenate([(src >= tgt).astype(BF16)] * 3, axis=1)
    triu3 = jnp.concatenate([causal.astype(BF16)] * 3, axis=0)
    n_pad = jnp.zeros((SUBLANES - 3, dk), BF16)

    def chunk(c, carry):
        r0 = pl.multiple_of(c * L, L)
        gcol = gc_ref[pl.ds(r0, L), :]
        grow = gr_ref[c]
        b_col = _cumsum_rows(tri3, _log_sigmoid(gcol))
        b_row = _cumsum_lanes(_log_sigmoid(grow), triu3)
        hs = range(H)
        vsl = [slice(h * dv, (h + 1) * dv) for h in hs]
        qk = [qkc_scr[pl.ds(r0, L), h * 2 * dk:(h + 1) * 2 * dk] for h in hs]
        q = [qk[h][:, :dk].astype(BF16) for h in hs]
        k = [qk[h][:, dk:] for h in hs]
        v = [v_ref[pl.ds(r0, L), vsl[h]] for h in hs]
        col = [gcol[:, h:h + 1] - b_col[:, H + h:H + h + 1] for h in hs]
        li_r = [grow[h:h + 1, :] for h in hs]
        b_r = [b_row[H + h:H + h + 1, :] for h in hs]
        b_end = [b_r[h][:, L - 1:L] for h in hs]
        dmat = [jnp.where(causal, b_r[h] + col[h], -jnp.inf) for h in hs]
        d_max = [jnp.max(dmat[h], axis=0, keepdims=True) for h in hs]
        att = [jnp.exp(dmat[h] - d_max[h]) * _dot_nt(k[h].astype(BF16), q[h]) for h in hs]
        a_sum = [jnp.sum(att[h], axis=0, keepdims=True) for h in hs]
        intra = [_dot_tn(v[h], att[h].astype(BF16)) for h in hs]
        g_max = [jnp.max(b_end[h] - b_r[h] + li_r[h], axis=-1, keepdims=True) for h in hs]
        kw = [k[h] * jnp.exp(b_end[h] + col[h] - g_max[h]) for h in hs]
        c_loc = [_dot_tn(v[h], kw[h].astype(BF16)) for h in hs]
        n_loc = [jnp.sum(kw[h], axis=0, keepdims=True) for h in hs]
        m_prev = [m_scr[h:h + 1, 0:1] for h in hs]
        c_prev = [c_scr[h] for h in hs]
        n_prev = [n_scr[h:h + 1, :] for h in hs]
        m_inter = [b_r[h] + m_prev[h] for h in hs]
        m_t = [jnp.maximum(m_inter[h], d_max[h]) for h in hs]
        w_loc = [jnp.exp(d_max[h] - m_t[h]) for h in hs]
        w_int = [jnp.exp(m_inter[h] - m_t[h]) for h in hs]
        q_c = [_dot_nt(c_prev[h].astype(BF16), q[h]) for h in hs]
        q_n = [_dot_nt(jnp.concatenate(list(_split3(n_prev[h])) + [n_pad], axis=0), q[h]) for h in hs]
        q_n = [q_n[h][0:1] + q_n[h][1:2] + q_n[h][2:3] for h in hs]
        num = [w_loc[h] * intra[h] + w_int[h] * q_c[h] for h in hs]
        den = [w_loc[h] * a_sum[h] + w_int[h] * q_n[h] for h in hs]
        h_t = [num[h] / jnp.maximum(jnp.abs(den[h]), jnp.exp(-m_t[h])) for h in hs]
        h_t = [h_t[h] * lax.rsqrt(jnp.mean(h_t[h] * h_t[h], axis=0, keepdims=True) + NORM_EPS) for h in hs]
        m_new = [jnp.maximum(b_end[h] + m_prev[h], g_max[h]) for h in hs]
        a = [jnp.exp(b_end[h] + m_prev[h] - m_new[h]) for h in hs]
        sc = [jnp.exp(g_max[h] - m_new[h]) for h in hs]
        for h in hs:
            out = h_t[h] * ng_ref[vsl[h], :] * og_ref[vsl[h], pl.ds(r0, L)]
            o_ref[pl.ds(r0, L), vsl[h]] = out.T.astype(o_ref.dtype)
            c_scr[h] = a[h] * c_prev[h] + sc[h] * c_loc[h]
            n_scr[h:h + 1, :] = a[h] * n_prev[h] + sc[h] * n_loc[h]
            m_scr[h:h + 1, :] = jnp.broadcast_to(m_new[h], (1, LANES))
        return carry

    lax.fori_loop(0, tt // L, chunk, 0)


def _mlstm_core(qk, v, ogt, gcol, grow3, conv_w, conv_b, norm_gb, batch, seq_len, tt=256):
    m, d = qk.shape
    H, dk, dv = MLSTM_HEADS, MLSTM_QK_DIM, MLSTM_V_DIM
    nt = seq_len // tt
    ncs = tt // MLSTM_TILE
    row = lambda b, i: (b * nt + i, 0)
    return pl.pallas_call(
        functools.partial(_mlstm_core_kernel, tt=tt),
        out_shape=jax.ShapeDtypeStruct((m, d), BF16),
        grid=(batch, nt),
        in_specs=[pl.BlockSpec((tt, d), row), pl.BlockSpec((tt, d), row),
                  pl.BlockSpec((d, tt), lambda b, i: (0, b * nt + i)),
                  pl.BlockSpec((tt, LANES), row),
                  pl.BlockSpec((ncs, 2 * H, MLSTM_TILE), lambda b, i: (b * nt + i, 0, 0)),
                  _const_spec((SUBLANES, d)), _const_spec((1, d)), _const_spec((d, LANES))],
        out_specs=pl.BlockSpec((tt, d), row),
        scratch_shapes=[pltpu.VMEM((tt, d), F32), pltpu.VMEM((tt + SUBLANES, d), F32),
                        pltpu.VMEM((H, dv, dk), F32), pltpu.VMEM((H, dk), F32),
                        pltpu.VMEM((H, LANES), F32)],
        compiler_params=_cparams(("arbitrary", "arbitrary")),
        name="mlstm_core",
    )(qk, v, ogt, gcol, grow3, conv_w, conv_b, norm_gb)


def _mlstm_layer(x, norm_g, w_in, conv_w, conv_b, b_gates, hnorm, batch, seq_len):
    H, dk = MLSTM_HEADS, MLSTM_QK_DIM
    qk, v, ogt, gcol, grow = _mlstm_proj(x, norm_g, w_in, b_gates)
    m = x.shape[0]
    grow3 = grow.reshape(2 * H, m // MLSTM_TILE, MLSTM_TILE).transpose(1, 0, 2)
    norm_gb = jnp.broadcast_to(hnorm.reshape(-1, 1), (hnorm.shape[0], LANES))

    def perm(z):
        lead = z.shape[:-1]
        zq = z[..., :H * dk].reshape(*lead, H, dk)
        zk = z[..., H * dk:].reshape(*lead, H, dk)
        return jnp.concatenate([zq, zk], axis=-1).reshape(*lead, 2 * H * dk)

    cw = jnp.pad(perm(conv_w), ((0, SUBLANES - MLSTM_CONV), (0, 0)))
    return _mlstm_core(qk, v, ogt, gcol, grow3, cw, perm(conv_b).reshape(1, -1), norm_gb,
                       batch, seq_len)


def _softplus(x):
    return jnp.maximum(x, 0.0) + jnp.log(1.0 + jnp.exp(-jnp.abs(x)))


def _rwkv_proj_kernel(x_ref, g_ref, mu_ref, wr_ref, wk_ref, wv_ref, ww1_ref, ww2_ref, w0_ref,
                      aw1_ref, aw2_ref, a0_ref, gw1_ref, gw2_ref,
                      r_ref, k_ref, v_ref, lw_ref, a_ref, go_ref, prev_scr, *, tiles_per_seq):
    tm = x_ref.shape[0]
    h = _rms(x_ref[...], g_ref[...])
    seq_start = (pl.program_id(0) % tiles_per_seq) == 0
    prev = jnp.where(seq_start, 0.0, prev_scr[SUBLANES - 1:SUBLANES, :])
    prev_scr[...] = h[tm - SUBLANES:, :]
    rows = lax.broadcasted_iota(jnp.int32, h.shape, 0)
    xx = jnp.where(rows == 0, prev, pltpu.roll(h, 1, 0)) - h
    mix = lambda j: (h + xx * mu_ref[j:j + 1, :]).astype(BF16)
    r_ref[...] = _dot(mix(0), wr_ref[...])
    k_ref[...] = _dot(mix(2), wk_ref[...])
    v_ref[...] = _dot(mix(3), wv_ref[...])
    wl = _dot(jnp.tanh(_dot(mix(1), ww1_ref[...])).astype(BF16), ww2_ref[...]) + w0_ref[...]
    w_log = -_softplus(-wl) - 0.5
    lw_ref[...] = -jnp.exp(w_log)
    al = _dot(_dot(mix(4), aw1_ref[...]).astype(BF16), aw2_ref[...]) + a0_ref[...]
    a_ref[...] = _sigmoid(al)
    go_ref[...] = _dot(_sigmoid(_dot(mix(5), gw1_ref[...])).astype(BF16), gw2_ref[...])


def _rwkv_proj(x, g, mu, w_r, w_k, w_v, w0, w_w1, w_w2, a0, a_w1, a_w2, g_w1, g_w2, seq_len, tm=256):
    m, d = x.shape
    row = lambda i: (i, 0)
    bf = lambda w: w.astype(BF16)
    mu8 = jnp.pad(mu, ((0, SUBLANES - mu.shape[0]), (0, 0)))
    consts = [g.reshape(1, d), mu8, bf(w_r), bf(w_k), bf(w_v), bf(w_w1), bf(w_w2), w0.reshape(1, d),
              bf(a_w1), bf(a_w2), a0.reshape(1, d), bf(g_w1), bf(g_w2)]
    out = jax.ShapeDtypeStruct((m, d), F32)
    return pl.pallas_call(
        functools.partial(_rwkv_proj_kernel, tiles_per_seq=seq_len // tm),
        out_shape=(out,) * 6,
        grid=(m // tm,),
        in_specs=[pl.BlockSpec((tm, d), row)] + [_const_spec(c.shape) for c in consts],
        out_specs=(pl.BlockSpec((tm, d), row),) * 6,
        scratch_shapes=[pltpu.VMEM((SUBLANES, d), F32)],
        compiler_params=_cparams(("arbitrary",)),
        name="rwkv_proj",
    )(x, *consts)


def _block_diag(x, lo):
    zero = jnp.zeros_like(x)
    return jnp.concatenate([jnp.where(lo, x, zero), jnp.where(lo, zero, x)], axis=0)


class _PairMat:
    def __init__(self, x, lo):
        self.x, self.lo = x, lo
        self._lhs = self._rhs = None

    def lhs(self):
        if self._lhs is None:
            hi, lo = _split2(self.x)
            self._lhs = jnp.concatenate([hi, lo, hi], axis=1)
        return self._lhs

    def rhs(self):
        if self._rhs is None:
            hi, lo = _split2(self.x)
            bh = _block_diag(hi, self.lo)
            self._rhs = jnp.concatenate([bh, bh, _block_diag(lo, self.lo)], axis=0)
        return self._rhs


def _pair_mm(p, q):
    return _dot(p.lhs(), q.rhs())


def _rwkv_core_kernel(r_ref, k_ref, v_ref, lw_ref, a_ref, go_ref, kk_ref, ka_ref, rk_ref,
                      lnw_ref, lnb_ref, o_ref, z_scr, *, tt):
    L, N = RWKV_CHUNK, RWKV_HEAD_DIM
    npair = z_scr.shape[0]
    pairs = range(npair)

    @pl.when(pl.program_id(1) == 0)
    def _():
        z_scr[...] = jnp.zeros_like(z_scr)

    ri = lax.broadcasted_iota(jnp.int32, (L, LANES), 0)
    ln = lax.broadcasted_iota(jnp.int32, (L, LANES), 1)
    si = ln % N
    lo = ln < N
    lower_incl = ri >= si
    lower_strict = ri > si
    blk_diag = (ri // RWKV_INV_BLOCK) == (si // RWKV_INV_BLOCK)
    eye = (ri == si).astype(F32)
    tri = (lax.broadcasted_iota(jnp.int32, (L, L), 0) >= lax.broadcasted_iota(jnp.int32, (L, L), 1))
    tri3 = jnp.concatenate([tri.astype(BF16)] * 3, axis=1)
    z_mask = ((lax.broadcasted_iota(jnp.int32, (2 * N, LANES), 0) // N)
              == (lax.broadcasted_iota(jnp.int32, (2 * N, LANES), 1) // N))

    def half_sum(x):
        s0 = jnp.sum(jnp.where(lo, x, 0.0), axis=-1, keepdims=True)
        s1 = jnp.sum(jnp.where(lo, 0.0, x), axis=-1, keepdims=True)
        return jnp.where(lo, s0, s1)

    bd = lambda x: _block_diag(x, lo)
    mk = lambda xs: [_PairMat(x, lo) for x in xs]
    mm = lambda ps, qs: [_pair_mm(p, q) for p, q in zip(ps, qs)]

    def chunk(c, carry):
        r0 = pl.multiple_of(c * L, L)
        sl = [slice(p * LANES, (p + 1) * LANES) for p in pairs]
        ld = lambda ref: [ref[pl.ds(r0, L), s] for s in sl]
        r, k, v, lw, a = ld(r_ref), ld(k_ref), ld(v_ref), ld(lw_ref), ld(a_ref)
        kk = [k[p] * kk_ref[:, sl[p]] for p in pairs]
        kk = [kk[p] / jnp.maximum(jnp.sqrt(half_sum(kk[p] * kk[p])), 1e-12) for p in pairs]
        km = [k[p] * (1.0 + (a[p] - 1.0) * ka_ref[:, sl[p]]) for p in pairs]
        bv = [kk[p] * a[p] for p in pairs]
        cum = [_cumsum_rows(tri3, lw[p]) for p in pairs]
        cum_end = [cum[p][L - 1:L, :] for p in pairs]
        w_inv = [jnp.exp(-cum[p]) for p in pairs]
        w_out = [jnp.exp(cum_end[p] - cum[p]) for p in pairs]
        kk_h = [(kk[p] * jnp.exp(cum[p] - lw[p])).astype(BF16) for p in pairs]
        r_h = [(r[p] * jnp.exp(cum[p])).astype(BF16) for p in pairs]
        b_t = [(bv[p] * w_inv[p]).astype(BF16) for p in pairs]
        k_t = [(km[p] * w_inv[p]).astype(BF16) for p in pairs]
        bbar = [(bv[p] * w_out[p]).astype(BF16) for p in pairs]
        kbar = [(km[p] * w_out[p]).astype(BF16) for p in pairs]
        vb = [v[p].astype(BF16) for p in pairs]
        lhs = [jnp.concatenate([kk_h[p], r_h[p]], axis=0) for p in pairs]
        ab = [_dot_nt(lhs[p], bd(b_t[p])) for p in pairs]
        ak = [_dot_nt(lhs[p], bd(k_t[p])) for p in pairs]
        a_ub = [jnp.where(lower_strict, ab[p][:L], 0.0) for p in pairs]
        a_rb = [jnp.where(lower_incl, ab[p][L:], 0.0).astype(BF16) for p in pairs]
        a_uk = [jnp.where(lower_strict, ak[p][:L], 0.0).astype(BF16) for p in pairs]
        a_rk = [jnp.where(lower_incl, ak[p][L:], 0.0).astype(BF16) for p in pairs]

        n1 = mk([jnp.where(blk_diag, -a_ub[p], 0.0) for p in pairs])
        n2 = mk(mm(n1, n1))
        n4 = mk(mm(n2, n2))
        n8 = mk(mm(n4, n4))
        acc = mk([eye + n1[p].x for p in pairs])
        for nk in (n2, n4):
            prod = mm(acc, nk)
            acc = mk([acc[p].x + prod[p] for p in pairs])
        prod = mm(acc, n8)
        d_inv = mk([acc[p].x + prod[p] for p in pairs])
        l_off = mk([jnp.where(blk_diag, 0.0, a_ub[p]) for p in pairs])
        e1 = mk([-x for x in mm(d_inv, l_off)])
        e2 = mk(mm(e1, e1))
        qm = mk([eye + e1[p].x for p in pairs])
        prod = mm(qm, e2)
        qm = mk([qm[p].x + prod[p] for p in pairs])
        t_inv = mk(mm(qm, d_inv))

        z = [z_scr[p] for p in pairs]
        zb = [z[p].astype(BF16) for p in pairs]
        bdv = [bd(vb[p]) for p in pairs]
        rhs_u = mk([_dot_nt(kk_h[p], zb[p]) + _dot(a_uk[p], bdv[p]) for p in pairs])
        u = [-x for x in mm(t_inv, rhs_u)]
        ub = [u[p].astype(BF16) for p in pairs]
        y = [_dot_nt(r_h[p], zb[p])
             + _dot(jnp.concatenate([a_rb[p], a_rk[p]], axis=1),
                    jnp.concatenate([bd(ub[p]), bdv[p]], axis=0)) for p in pairs]
        upd = [_dot_tn(jnp.concatenate([ub[p], vb[p]], axis=0),
                       jnp.concatenate([bbar[p], kbar[p]], axis=0)) for p in pairs]
        for p in pairs:
            z_scr[p] = z[p] * jnp.exp(cum_end[p]) + jnp.where(z_mask, upd[p], 0.0)

        inv_n = 1.0 / N
        mean = [half_sum(y[p]) * inv_n for p in pairs]
        yc = [y[p] - mean[p] for p in pairs]
        var = [half_sum(yc[p] * yc[p]) * inv_n for p in pairs]
        bonus = [half_sum(r[p] * km[p] * rk_ref[:, sl[p]]) * v[p] for p in pairs]
        for p in pairs:
            yn = yc[p] * lax.rsqrt(var[p] + RWKV_GN_EPS) * lnw_ref[:, sl[p]] + lnb_ref[:, sl[p]]
            o_ref[pl.ds(r0, L), sl[p]] = ((yn + bonus[p]) * go_ref[pl.ds(r0, L), sl[p]]).astype(o_ref.dtype)
        return carry

    lax.fori_loop(0, tt // L, chunk, 0)


def _rwkv_core(r, k, v, lw, a, go, k_k, k_a, r_k, ln_w, ln_b, batch, seq_len, tt=256):
    m, d = r.shape
    nt = seq_len // tt
    npair = d // LANES
    blk = pl.BlockSpec((tt, d), lambda b, i: (b * nt + i, 0))
    par = _const_spec((1, d))
    row1 = lambda z: z.reshape(1, d)
    return pl.pallas_call(
        functools.partial(_rwkv_core_kernel, tt=tt),
        out_shape=jax.ShapeDtypeStruct((m, d), BF16),
        grid=(batch, nt),
        in_specs=[blk] * 6 + [par] * 5,
        out_specs=blk,
        scratch_shapes=[pltpu.VMEM((npair, 2 * RWKV_HEAD_DIM, LANES), F32)],
        compiler_params=_cparams(("arbitrary", "arbitrary")),
        name="rwkv_core",
    )(r, k, v, lw, a, go, row1(k_k), row1(k_a), row1(r_k), row1(ln_w), row1(ln_b))


def _rwkv_layer(x, norm_g, mu, w_r, w_k, w_v, w0, w_w1, w_w2, a0, a_w1, a_w2, g_w1, g_w2,
                k_k, k_a, r_k, ln_w, ln_b, batch, seq_len):
    r, k, v, lw, a, go = _rwkv_proj(x, norm_g, mu, w_r, w_k, w_v, w0, w_w1, w_w2, a0, a_w1, a_w2,
                                    g_w1, g_w2, seq_len)
    return _rwkv_core(r, k, v, lw, a, go, k_k, k_a, r_k, ln_w, ln_b, batch, seq_len)


def kernel(x, norm_mixer, norm_ffn, ffn_w_up, ffn_conv_w, ffn_conv_b, ffn_w_down, nsa_w_in, nsa_pe_k, nsa_pe_v, nsa_cmp_k_w1, nsa_cmp_k_w2, nsa_cmp_v_w1, nsa_cmp_v_w2, nsa_b_gate, nsa_w_out, mlstm_w_in, mlstm_conv_w, mlstm_conv_b, mlstm_b_gates, mlstm_norm, mlstm_w_out, rwkv_mu, rwkv_w_r, rwkv_w_k, rwkv_w_v, rwkv_w_o, rwkv_w0, rwkv_w_w1, rwkv_w_w2, rwkv_a0, rwkv_a_w1, rwkv_a_w2, rwkv_g_w1, rwkv_g_w2, rwkv_k_k, rwkv_k_a, rwkv_r_k, rwkv_ln_w, rwkv_ln_b, final_norm):
    batch, seq_len, d = x.shape
    depth = norm_mixer.shape[0]
    rope = _rope_tables(seq_len)
    xf = x.reshape(batch * seq_len, d)
    for i in range(depth):
        kind, j = i % 3, i // 3
        if kind == 0:
            w_o = nsa_w_out[j]
            o = _nsa_layer(xf, norm_mixer[i], nsa_w_in[j], nsa_pe_k[j], nsa_pe_v[j], nsa_cmp_k_w1[j],
                           nsa_cmp_k_w2[j], nsa_cmp_v_w1[j], nsa_cmp_v_w2[j], nsa_b_gate[j], rope, batch, seq_len)
        elif kind == 1:
            w_o = mlstm_w_out[j]
            o = _mlstm_layer(xf, norm_mixer[i], mlstm_w_in[j], mlstm_conv_w[j], mlstm_conv_b[j],
                             mlstm_b_gates[j], mlstm_norm[j], batch, seq_len)
        else:
            w_o = rwkv_w_o[j]
            o = _rwkv_layer(xf, norm_mixer[i], rwkv_mu[j], rwkv_w_r[j], rwkv_w_k[j], rwkv_w_v[j],
                            rwkv_w0[j], rwkv_w_w1[j], rwkv_w_w2[j], rwkv_a0[j],
                            rwkv_a_w1[j], rwkv_a_w2[j], rwkv_g_w1[j], rwkv_g_w2[j], rwkv_k_k[j],
                            rwkv_k_a[j], rwkv_r_k[j], rwkv_ln_w[j], rwkv_ln_b[j], batch, seq_len)
        xf = _ffn(xf, o, w_o, norm_ffn[i], ffn_w_up[i], ffn_conv_w[i], ffn_conv_b[i], ffn_w_down[i],
                  seq_len, final_g=final_norm if i == depth - 1 else None)
    return xf.reshape(batch, seq_len, d)
```

```python
import functools
import math

import jax
import jax.numpy as jnp
import numpy as np
from jax import lax
from jax.experimental import pallas as pl
from jax.experimental.pallas import tpu as pltpu

F32 = jnp.float32
BF16 = jnp.bfloat16

D_MODEL = 1024
DEPTH = 4
NORM_EPS = 1e-6
ROPE_THETA = 500000.0

NSA_HEAD_DIM = 64
NSA_HEADS = 16
NSA_GROUPS = 4
NSA_REP = NSA_HEADS // NSA_GROUPS
NSA_ROT_DIM = 16
CMP_BLOCK = 32
CMP_STRIDE = 16
CMP_HIDDEN = 256
SEL_BLOCK = 64
SEL_TOPK = 16
WINDOW = 512
NSA_KV = NSA_GROUPS * NSA_HEAD_DIM
NSA_VT_ROWS = NSA_HEAD_DIM + 16
NSA_GATE_ROWS = 16

MLSTM_HEADS = 8
MLSTM_QK_DIM = 64
MLSTM_V_DIM = 128
MLSTM_TILE = 128
MLSTM_CONV = 4

RWKV_HEAD_DIM = 64
RWKV_HEADS = 16
RWKV_GN_EPS = 64e-5
RWKV_CHUNK = 64
RWKV_INV_BLOCK = 16

FFN_DIM = 2816
FFN_CONV = 3
FFN_CHUNK = 256

LOG2E = math.log2(math.e)
MASKED = -1e30

LANES = 128
SUBLANES = 8
VMEM_LIMIT = 56 * 1024 * 1024


def _dot(a, b):
    return jnp.dot(a, b, preferred_element_type=F32)


def _dot_nt(a, b):
    return lax.dot_general(a, b, (((1,), (1,)), ((), ())), preferred_element_type=F32)


def _dot_tn(a, b):
    return lax.dot_general(a, b, (((0,), (0,)), ((), ())), preferred_element_type=F32)


def _split2(x):
    hi = x.astype(BF16)
    return hi, (x - hi.astype(F32)).astype(BF16)


def _split3(x):
    hi = x.astype(BF16)
    r1 = x - hi.astype(F32)
    mid = r1.astype(BF16)
    return hi, mid, (r1 - mid.astype(F32)).astype(BF16)


def _cumsum_rows(tri3, x):
    return _dot(tri3, jnp.concatenate(_split3(x), axis=0))


def _cumsum_lanes(x, triu3):
    return _dot(jnp.concatenate(_split3(x), axis=1), triu3)


def _rms(x, g):
    ms = jnp.mean(x * x, axis=-1, keepdims=True)
    return x * lax.rsqrt(ms + NORM_EPS) * g


def _sigmoid(x):
    return 1.0 / (1.0 + jnp.exp(-x))


def _cparams(sem):
    return pltpu.CompilerParams(dimension_semantics=sem, vmem_limit_bytes=VMEM_LIMIT)


def _const_spec(shape):
    n = len(shape)
    return pl.BlockSpec(shape, lambda *_: (0,) * n, pipeline_mode=pl.Buffered(1))


def _ffn_kernel(res_ref, a_ref, wo_ref, g_ref, wu_ref, cw_ref, cb_ref, wd_ref, fg_ref, o_ref,
                h_scr, carry_scr, ga_scr, va_scr, gb_scr, vb_scr, *, tiles_per_seq, n_chunks, final_norm):
    tm = res_ref.shape[0]
    fc = FFN_CHUNK
    x = res_ref[...] + _dot(a_ref[...], wo_ref[...])
    h_scr[...] = _rms(x, g_ref[...]).astype(BF16)
    o_ref[...] = x
    seq_start = (pl.program_id(0) % tiles_per_seq) == 0
    rows = lax.broadcasted_iota(jnp.int32, (tm, fc), 0)
    cols = lambda c, base=0: pl.ds(pl.multiple_of(base + c * fc, LANES), fc)

    def up(c, g_scr, v_scr):
        h = h_scr[...]
        g_scr[...] = _dot(h, wu_ref[:, cols(c)])
        v_scr[...] = _dot(h, wu_ref[:, cols(c, FFN_DIM)])

    def down(c, g_scr, v_scr):
        gate, val = g_scr[...], v_scr[...]
        prev = carry_scr[:, cols(c)]
        prev = jnp.where(seq_start, 0.0, prev)
        p1 = prev[SUBLANES - 1:SUBLANES, :]
        p2 = prev[SUBLANES - 2:SUBLANES - 1, :]
        carry_scr[:, cols(c)] = gate[tm - SUBLANES:, :]
        g1 = jnp.where(rows == 0, p1, pltpu.roll(gate, 1, 0))
        g2 = jnp.where(rows == 0, p2, jnp.where(rows == 1, p1, pltpu.roll(gate, 2, 0)))
        cw = cw_ref[:, cols(c)]
        y = cw[2:3, :] * gate + cw[1:2, :] * g1 + cw[0:1, :] * g2 + cb_ref[:, cols(c)]
        act = (y * _sigmoid(y) * val).astype(BF16)
        o_ref[...] += _dot(act, wd_ref[pl.ds(pl.multiple_of(c * fc, fc), fc), :])

    assert n_chunks % 2 == 1

    def chunk_pair(j, carry):
        up(2 * j + 1, gb_scr, vb_scr)
        down(2 * j, ga_scr, va_scr)
        up(2 * j + 2, ga_scr, va_scr)
        down(2 * j + 1, gb_scr, vb_scr)
        return carry

    up(0, ga_scr, va_scr)
    lax.fori_loop(0, n_chunks // 2, chunk_pair, 0)
    down(n_chunks - 1, ga_scr, va_scr)
    if final_norm:
        o_ref[...] = _rms(o_ref[...], fg_ref[...])


def _ffn(res, a, w_o, g, w_up, conv_w, conv_b, w_down, seq_len, final_g=None, tm=1024):
    m, d = res.shape
    nc = FFN_DIM // FFN_CHUNK
    cw = jnp.pad(conv_w, ((0, SUBLANES - FFN_CONV), (0, 0)))
    row = lambda i: (i, 0)
    fg = jnp.ones((d,), F32) if final_g is None else final_g
    kern = functools.partial(_ffn_kernel, tiles_per_seq=seq_len // tm, n_chunks=nc,
                             final_norm=final_g is not None)
    return pl.pallas_call(
        kern,
        out_shape=jax.ShapeDtypeStruct((m, d), F32),
        grid=(m // tm,),
        in_specs=[pl.BlockSpec((tm, d), row), pl.BlockSpec((tm, d), row), _const_spec((d, d)),
                  _const_spec((1, d)), _const_spec((d, 2 * FFN_DIM)),
                  _const_spec((SUBLANES, FFN_DIM)), _const_spec((1, FFN_DIM)),
                  _const_spec((FFN_DIM, d)), _const_spec((1, d))],
        out_specs=pl.BlockSpec((tm, d), row),
        scratch_shapes=[pltpu.VMEM((tm, d), BF16),
                        pltpu.VMEM((SUBLANES, FFN_DIM), F32)]
                       + [pltpu.VMEM((tm, FFN_CHUNK), F32)] * 4,
        compiler_params=_cparams(("arbitrary",)),
        name="conv_ffn",
    )(res, a, w_o.astype(BF16), g.reshape(1, d), w_up.astype(BF16), cw, conv_b.reshape(1, FFN_DIM),
      w_down.astype(BF16), fg.reshape(1, d))


def _nsa_proj_kernel(x_ref, g_ref, w_ref, wvt_ref, bg_ref, rc_ref, rs1_ref, rs2_ref,
                     qp_ref, qr_ref, kc_ref, vc_ref, ks_ref, vs_ref, kw_ref, vw_ref, gate_ref,
                     *, tiles_per_seq):
    hn = _rms(x_ref[...], g_ref[...]).astype(BF16)
    y = _dot(hn, w_ref[...])
    vt = _dot_nt(wvt_ref[...], hn)
    rc, rs1, rs2 = rc_ref[...], rs1_ref[...], rs2_ref[...]
    dh = NSA_HEAD_DIM

    def rope(z):
        return z * rc + pltpu.roll(z, 8, 1) * rs1 + pltpu.roll(z, LANES - 8, 1) * rs2

    scale = dh ** -0.5 * LOG2E
    for j in range(D_MODEL // LANES):
        q = y[:, j * LANES:(j + 1) * LANES] * scale
        qp_ref[:, j * LANES:(j + 1) * LANES] = q.astype(BF16)
        qr_ref[:, j * LANES:(j + 1) * LANES] = rope(q).astype(BF16)

    def kv_chunk(idx):
        return y[:, D_MODEL + idx * NSA_KV:D_MODEL + (idx + 1) * NSA_KV]

    def split_groups(z, ref, dtype):
        for g in range(NSA_GROUPS):
            ref[g] = z[:, g * dh:(g + 1) * dh].astype(dtype)

    def rope256(z):
        return jnp.concatenate([rope(z[:, :LANES]), rope(z[:, LANES:])], axis=1)

    split_groups(kv_chunk(0), kc_ref, F32)
    split_groups(kv_chunk(1), vc_ref, F32)
    tm = y.shape[0]
    t_pos = (pl.program_id(0) % tiles_per_seq) * tm + lax.broadcasted_iota(jnp.int32, (tm, LANES), 0)
    onehot = (lax.broadcasted_iota(jnp.int32, (tm, LANES), 1) == t_pos // SEL_BLOCK).astype(F32)
    ksel = rope256(kv_chunk(2))
    zpad = jnp.zeros((tm, LANES - dh), F32)
    for g in range(NSA_GROUPS):
        ks_ref[g] = jnp.concatenate([ksel[:, g * dh:(g + 1) * dh], zpad, onehot], axis=1).astype(BF16)
    split_groups(rope256(kv_chunk(3)), kw_ref, BF16)
    ones_pad = (lax.broadcasted_iota(jnp.int32, (NSA_VT_ROWS - dh, tm), 0) == 0).astype(F32)
    for g in range(NSA_GROUPS):
        vs_ref[g] = jnp.concatenate([vt[g * dh:(g + 1) * dh], ones_pad], axis=0).astype(BF16)
        vw_ref[g] = jnp.concatenate([vt[NSA_KV + g * dh:NSA_KV + (g + 1) * dh], ones_pad], axis=0).astype(BF16)
    gate = _sigmoid(vt[2 * NSA_KV:] + bg_ref[...])
    for g in range(NSA_GROUPS):
        gate_ref[g] = gate[g * NSA_GATE_ROWS:(g + 1) * NSA_GATE_ROWS]


def _rope_tables(seq_len):
    half = NSA_ROT_DIM // 2
    inv_freq = ROPE_THETA ** (-jnp.arange(half, dtype=F32) / half)
    ang = jnp.arange(seq_len, dtype=F32)[:, None] * inv_freq[None, :]
    cos, sin = jnp.cos(ang), jnp.sin(ang)
    zeros = jnp.zeros((seq_len, NSA_HEAD_DIM - NSA_ROT_DIM), F32)
    z8 = jnp.zeros((seq_len, half), F32)
    rc = jnp.concatenate([cos, cos, zeros + 1.0], axis=1)
    rs1 = jnp.concatenate([z8, sin, zeros], axis=1)
    rs2 = jnp.concatenate([-sin, z8, zeros], axis=1)
    two = lambda t: jnp.concatenate([t, t], axis=1)
    return two(rc), two(rs1), two(rs2)


def _nsa_proj(x, g, w_in, b_gate, rope, seq_len, tm=512):
    m, d = x.shape
    n_kv = 6 * NSA_KV
    kv = lambda idx: w_in[:, D_MODEL + idx * NSA_KV:D_MODEL + (idx + 1) * NSA_KV]
    pad_g = NSA_GATE_ROWS - NSA_REP * 3
    wg = w_in[:, D_MODEL + n_kv:].reshape(d, NSA_GROUPS, NSA_REP * 3)
    wg = jnp.pad(wg, ((0, 0), (0, 0), (0, pad_g))).reshape(d, NSA_GROUPS * NSA_GATE_ROWS)
    w = jnp.concatenate([w_in[:, :D_MODEL], kv(0), kv(1), kv(2), kv(4)], axis=1).astype(BF16)
    wvt = jnp.concatenate([kv(3), kv(5), wg], axis=1).T.astype(BF16)
    bg = jnp.pad(b_gate.reshape(NSA_GROUPS, NSA_REP * 3), ((0, 0), (0, pad_g)))
    bg = bg.reshape(NSA_GROUPS * NSA_GATE_ROWS, 1)
    n = w.shape[1]
    tps = seq_len // tm
    row = lambda i: (i, 0)
    rope_spec = pl.BlockSpec((tm, LANES), lambda i: (i % tps, 0))
    assert seq_len // SEL_BLOCK <= LANES
    g_out = lambda dt, w=NSA_HEAD_DIM: jax.ShapeDtypeStruct((NSA_GROUPS, m, w), dt)
    g_spec = pl.BlockSpec((NSA_GROUPS, tm, NSA_HEAD_DIM), lambda i: (0, i, 0))
    ks_spec = pl.BlockSpec((NSA_GROUPS, tm, 2 * LANES), lambda i: (0, i, 0))
    vt_out = jax.ShapeDtypeStruct((NSA_GROUPS, NSA_VT_ROWS, m), BF16)
    vt_spec = pl.BlockSpec((NSA_GROUPS, NSA_VT_ROWS, tm), lambda i: (0, 0, i))
    return pl.pallas_call(
        functools.partial(_nsa_proj_kernel, tiles_per_seq=tps),
        out_shape=(jax.ShapeDtypeStruct((m, d), BF16), jax.ShapeDtypeStruct((m, d), BF16),
                   g_out(F32), g_out(F32), g_out(BF16, 2 * LANES), vt_out, g_out(BF16), vt_out,
                   jax.ShapeDtypeStruct((NSA_GROUPS, NSA_GATE_ROWS, m), F32)),
        grid=(m // tm,),
        in_specs=[pl.BlockSpec((tm, d), row), _const_spec((1, d)), _const_spec((d, n)),
                  _const_spec((2 * NSA_KV + NSA_GROUPS * NSA_GATE_ROWS, d)),
                  _const_spec((NSA_GROUPS * NSA_GATE_ROWS, 1)), rope_spec, rope_spec, rope_spec],
        out_specs=(pl.BlockSpec((tm, d), row), pl.BlockSpec((tm, d), row),
                   g_spec, g_spec, ks_spec, vt_spec, g_spec, vt_spec,
                   pl.BlockSpec((NSA_GROUPS, NSA_GATE_ROWS, tm), lambda i: (0, 0, i))),
        compiler_params=_cparams(("parallel",)),
        name="nsa_proj",
    )(x, g.reshape(1, d), w, wvt, bg, *rope)


def _gelu_tanh(x):
    return 0.5 * x * (1.0 + jnp.tanh(math.sqrt(2.0 / math.pi) * (x + 0.044715 * (x * x * x))))


def _compress_kernel(zk_ref, zv_ref, pek_ref, pev_ref, w1k_ref, w2k_ref, w1v_ref, w2v_ref,
                     kc_ref, vc_ref):
    nrow = zk_ref.shape[0]
    rows = lax.broadcasted_iota(jnp.int32, (nrow, NSA_HEAD_DIM), 0)

    def one(z_ref, pe_ref, w1_ref, w2_ref, o_ref):
        z = z_ref[...]
        a = _dot((z + pe_ref[0:1, :]).astype(BF16), w1_ref[0])
        b = _dot((z + pe_ref[1:2, :]).astype(BF16), w1_ref[1])
        hid = a + pltpu.roll(b, nrow - 1, 0)
        out = _dot(_gelu_tanh(hid).astype(BF16), w2_ref[...])
        o_ref[...] = jnp.where(rows == nrow - 1, 0.0, out).astype(o_ref.dtype)

    one(zk_ref, pek_ref, w1k_ref, w2k_ref, kc_ref)
    one(zv_ref, pev_ref, w1v_ref, w2v_ref, vc_ref)


def _compress(kc_raw, vc_raw, pe_k, pe_v, w1k, w2k, w1v, w2v, seq_len):
    g, m, dh = kc_raw.shape
    half = CMP_STRIDE * dh
    nchunk = seq_len // CMP_STRIDE
    zk = kc_raw.reshape(g * m // CMP_STRIDE, half)
    zv = vc_raw.reshape(g * m // CMP_STRIDE, half)
    pe2 = lambda pe: pe.reshape(2, half)
    w1 = lambda w: w.astype(BF16).reshape(2, half, CMP_HIDDEN)
    nblk = zk.shape[0] // nchunk
    row = lambda i: (i, 0)
    return pl.pallas_call(
        _compress_kernel,
        out_shape=(jax.ShapeDtypeStruct((zk.shape[0], dh), BF16),
                   jax.ShapeDtypeStruct((zk.shape[0], dh), BF16)),
        grid=(nblk,),
        in_specs=[pl.BlockSpec((nchunk, half), row), pl.BlockSpec((nchunk, half), row),
                  _const_spec((2, half)), _const_spec((2, half)),
                  _const_spec((2, half, CMP_HIDDEN)), _const_spec((CMP_HIDDEN, dh)),
                  _const_spec((2, half, CMP_HIDDEN)), _const_spec((CMP_HIDDEN, dh))],
        out_specs=(pl.BlockSpec((nchunk, dh), row), pl.BlockSpec((nchunk, dh), row)),
        compiler_params=_cparams(("parallel",)),
        name="nsa_compress",
    )(zk, zv, pe2(pe_k), pe2(pe_v), w1(w1k), w2k.astype(BF16), w1(w1v), w2v.astype(BF16))


def _nsa_cmp_kernel(q_ref, kc_ref, vc_ref, gate_ref, ovt_ref, oc_ref, selt_ref, *, tq):
    qi = pl.program_id(2)
    dh = NSA_HEAD_DIM
    q = q_ref[...]
    qs = jnp.concatenate([q[:, r * dh:(r + 1) * dh] for r in range(NSA_REP)], axis=0)
    kc, vc = kc_ref[...], vc_ref[...]
    ncmp = kc.shape[0]
    t = qi * tq + lax.broadcasted_iota(jnp.int32, (ncmp, tq), 1)
    cmp_end = lax.broadcasted_iota(jnp.int32, (ncmp, tq), 0) * CMP_STRIDE + (CMP_BLOCK - 1)
    neg = jnp.where(cmp_end <= t, 0.0, -jnp.inf)
    heads = range(NSA_REP)
    st = [_dot_nt(kc, qs[r * tq:(r + 1) * tq]) + neg for r in heads]
    mx = [jnp.max(st[r], axis=0, keepdims=True) for r in heads]
    mx = [jnp.where(mx[r] == -jnp.inf, 0.0, mx[r]) for r in heads]
    e = [jnp.exp2(st[r] - mx[r]) for r in heads]
    p = [e[r] / jnp.maximum(jnp.sum(e[r], axis=0, keepdims=True), 1e-30) for r in heads]
    o_t = [_dot_tn(vc, p[r].astype(BF16)) for r in heads]
    gate = gate_ref[...]
    o_t = [o_t[r] * gate[3 * r:3 * r + 1, :] for r in heads]
    for j in range(NSA_REP // 2):
        pair = jnp.concatenate([o_t[2 * j], o_t[2 * j + 1]], axis=0).T
        oc_ref[:, 2 * j * dh:2 * (j + 1) * dh] = pair
    psum = sum(p)
    imp = _dot(ovt_ref[...], jnp.concatenate(_split3(psum), axis=0))
    nsel = imp.shape[0]
    blk = lax.broadcasted_iota(jnp.int32, (nsel, LANES), 0)
    blk_f = blk.astype(F32)
    for cb in range(tq // LANES):
        csl = slice(cb * LANES, (cb + 1) * LANES)
        tb = (qi * tq + cb * LANES + lax.broadcasted_iota(jnp.int32, (nsel, LANES), 1)) // SEL_BLOCK
        forced = (blk == 0) | (blk == tb) | (blk == tb - 1)
        vals = jnp.where(forced, -jnp.inf, jnp.where(blk <= tb, imp[:, csl], -1.0))
        sel = jnp.where(forced, 1.0, 0.0)
        for _ in range(SEL_TOPK - 3):
            top = jnp.max(vals, axis=0, keepdims=True)
            first = jnp.min(jnp.where(vals == top, blk_f, float(nsel)), axis=0, keepdims=True)
            pick = blk_f == first
            sel = jnp.where(pick, 1.0, sel)
            vals = jnp.where(pick, -jnp.inf, vals)
        selt_ref[:, csl] = sel.astype(selt_ref.dtype)


def _overlap_matrix_t3(ncmp_pad, nsel):
    c = np.arange(ncmp_pad)[None, :]
    s = np.arange(nsel)[:, None]
    cmp_start = c * CMP_STRIDE
    cmp_end = cmp_start + CMP_BLOCK - 1
    blk_start = s * SEL_BLOCK
    ov = ((cmp_end >= blk_start) & (cmp_start <= blk_start + SEL_BLOCK - 1)).astype(np.float32)
    return jnp.asarray(np.concatenate([ov, ov, ov], axis=1), dtype=BF16)


def _nsa_cmp(qp, kc, vc, gates, batch, seq_len, tq=512):
    m, d = qp.shape
    nq = seq_len // tq
    ncmp = seq_len // CMP_STRIDE
    nsel = seq_len // SEL_BLOCK
    ovt3 = _overlap_matrix_t3(ncmp, nsel)
    qmap = lambda b, g, i: (b * nq + i, g)
    kmap = lambda b, g, i: (g * batch + b, 0)
    return pl.pallas_call(
        functools.partial(_nsa_cmp_kernel, tq=tq),
        out_shape=(jax.ShapeDtypeStruct((m, d), F32),
                   jax.ShapeDtypeStruct((NSA_GROUPS, nsel, m), BF16)),
        grid=(batch, NSA_GROUPS, nq),
        in_specs=[pl.BlockSpec((tq, NSA_KV), qmap),
                  pl.BlockSpec((ncmp, NSA_HEAD_DIM), kmap), pl.BlockSpec((ncmp, NSA_HEAD_DIM), kmap),
                  pl.BlockSpec((None, NSA_GATE_ROWS, tq), lambda b, g, i: (g, 0, b * nq + i)),
                  _const_spec((nsel, 3 * ncmp))],
        out_specs=(pl.BlockSpec((tq, NSA_KV), qmap),
                   pl.BlockSpec((None, nsel, tq), lambda b, g, i: (g, 0, b * nq + i))),
        compiler_params=_cparams(("parallel", "parallel", "parallel")),
        name="nsa_cmp_topk",
    )(qp, kc, vc, gates, ovt3)


def _nsa_sel_kernel(q_ref, ks_ref, vs_ref, kw_ref, vw_ref, sel_ref, gate_ref, oc_ref, o_ref,
                    m_scr, acc_scr, sta_scr, stb_scr, *, tq, tk):
    qi = pl.program_id(2)
    dh = NSA_HEAD_DIM
    cols = NSA_REP * tq
    q = q_ref[...]
    qs = jnp.concatenate([q[:, r * dh:(r + 1) * dh] for r in range(NSA_REP)], axis=0)
    selt = sel_ref[...].astype(F32)
    nsel = selt.shape[0]
    bmask_t = jnp.where(selt > 0.5, 0.0, MASKED)
    if nsel < LANES:
        bmask_t = jnp.concatenate([bmask_t, jnp.zeros((LANES - nsel, tq), F32)], axis=0)
    bmask = bmask_t.T.astype(BF16)
    zpad = jnp.zeros((tq, LANES - dh), BF16)
    qa = jnp.concatenate([jnp.concatenate([q[:, r * dh:(r + 1) * dh], zpad, bmask], axis=1)
                          for r in range(NSA_REP)], axis=0)
    q0 = qi * tq
    key_iota = lax.broadcasted_iota(jnp.int32, (tk, tq), 0)
    t_pos = q0 + lax.broadcasted_iota(jnp.int32, (tk, tq), 1)

    m_scr[...] = jnp.full((1, cols), -jnp.inf, F32)
    acc_scr[...] = jnp.zeros((NSA_VT_ROWS, cols), F32)

    heads = range(NSA_REP)
    hsl = [slice(r * tq, (r + 1) * tq) for r in heads]

    def put_scores(scr, ki):
        k = ks_ref[pl.ds(pl.multiple_of(ki * tk, tk), tk), :]
        for r in heads:
            scr[r] = _dot_nt(k, qa[hsl[r]])

    def get_scores(scr):
        return [scr[r] for r in heads]

    def consume(ki, st, causal):
        k0 = pl.multiple_of(ki * tk, tk)
        vt = vs_ref[:, pl.ds(k0, tk)]
        if causal:
            neg = jnp.where(k0 + key_iota <= t_pos, 0.0, -jnp.inf)
            st = [s + neg for s in st]
        m_old = [m_scr[:, hsl[r]] for r in heads]
        m_new = [jnp.maximum(m_old[r], jnp.max(st[r], axis=0, keepdims=True)) for r in heads]
        alpha = [jnp.exp2(m_old[r] - m_new[r]) for r in heads]
        p = [jnp.exp2(st[r] - m_new[r]).astype(BF16) for r in heads]
        pv = [_dot(vt, p[r]) for r in heads]
        for r in heads:
            acc_scr[:, hsl[r]] = alpha[r] * acc_scr[:, hsl[r]] + pv[r]
            m_scr[:, hsl[r]] = m_new[r]

    def tile_pair(j, carry):
        even = get_scores(sta_scr)
        put_scores(stb_scr, 2 * j + 1)
        consume(2 * j, even, False)
        odd = get_scores(stb_scr)
        put_scores(sta_scr, 2 * j + 2)
        consume(2 * j + 1, odd, False)
        return carry

    last = (q0 + tq - 1) // tk
    put_scores(sta_scr, 0)
    lax.fori_loop(0, last // 2, tile_pair, 0)

    @pl.when(last % 2 == 0)
    def _():
        consume(last, get_scores(sta_scr), True)

    @pl.when(last % 2 == 1)
    def _():
        even = get_scores(sta_scr)
        put_scores(stb_scr, last)
        consume(last - 1, even, False)
        consume(last, get_scores(stb_scr), True)
    o_sel = acc_scr[0:dh, :] / jnp.maximum(acc_scr[dh:dh + 1, :], 1e-30)

    wq = min(tq, 256)
    wsub = wq + WINDOW
    nsub = tq // wq
    rel_iota = (lax.broadcasted_iota(jnp.int32, (wsub, wq), 0)
                - lax.broadcasted_iota(jnp.int32, (wsub, wq), 1))
    starts = [pl.multiple_of(jnp.maximum(q0 + wq * u - WINDOW, 0), wq) for u in range(nsub)]
    kw = [kw_ref[pl.ds(starts[u], wsub), :] for u in range(nsub)]
    vwt = [vw_ref[:, pl.ds(starts[u], wsub)] for u in range(nsub)]
    rel = [starts[u] - (q0 + wq * u) + rel_iota for u in range(nsub)]
    neg_w = [jnp.where((rel[u] <= 0) & (rel[u] > -WINDOW), 0.0, -jnp.inf) for u in range(nsub)]
    subs = [(r, u) for r in heads for u in range(nsub)]
    sw = [_dot_nt(kw[u], qs[r * tq + wq * u:r * tq + wq * (u + 1)]) + neg_w[u] for r, u in subs]
    mw = [jnp.max(s, axis=0, keepdims=True) for s in sw]
    mw = [jnp.where(m == -jnp.inf, 0.0, m) for m in mw]
    ew = [jnp.exp2(s - m).astype(BF16) for s, m in zip(sw, mw)]
    pvw = [_dot(vwt[u], e) for (r, u), e in zip(subs, ew)]
    pvw = [p[0:dh] / jnp.maximum(p[dh:dh + 1], 1e-30) for p in pvw]
    o_win = [jnp.concatenate(pvw[r * nsub:(r + 1) * nsub], axis=1) for r in heads]

    gate = gate_ref[...]
    mix = [o_sel[:, hsl[r]] * gate[3 * r + 1:3 * r + 2, :] + o_win[r] * gate[3 * r + 2:3 * r + 3, :]
           for r in heads]
    for j in range(NSA_REP // 2):
        psl = slice(2 * j * dh, 2 * (j + 1) * dh)
        pair = jnp.concatenate([mix[2 * j], mix[2 * j + 1]], axis=0).T
        o_ref[:, psl] = (oc_ref[:, psl] + pair).astype(o_ref.dtype)


def _nsa_sel(qr, ks, vs, kw, vw, sel, gates, oc, batch, seq_len, tq=512, tk=512):
    m, d = qr.shape
    nq = seq_len // tq
    nsel = seq_len // SEL_BLOCK
    qmap = lambda b, g, i: (b * nq + i, g)
    kvmap = lambda b, g, i: (g, b, 0)
    kv_spec = pl.BlockSpec((None, seq_len, NSA_HEAD_DIM), kvmap)
    ks_spec = pl.BlockSpec((None, seq_len, 2 * LANES), kvmap)
    vt_spec = pl.BlockSpec((None, NSA_VT_ROWS, seq_len), lambda b, g, i: (g, 0, b))
    cols = NSA_REP * tq
    assert tk % tq == 0
    return pl.pallas_call(
        functools.partial(_nsa_sel_kernel, tq=tq, tk=tk),
        out_shape=jax.ShapeDtypeStruct((m, d), BF16),
        grid=(batch, NSA_GROUPS, nq),
        in_specs=[pl.BlockSpec((tq, NSA_KV), qmap), ks_spec, vt_spec, kv_spec, vt_spec,
                  pl.BlockSpec((None, nsel, tq), lambda b, g, i: (g, 0, b * nq + i)),
                  pl.BlockSpec((None, NSA_GATE_ROWS, tq), lambda b, g, i: (g, 0, b * nq + i)),
                  pl.BlockSpec((tq, NSA_KV), qmap)],
        out_specs=pl.BlockSpec((tq, NSA_KV), qmap),
        scratch_shapes=[pltpu.VMEM((1, cols), F32), pltpu.VMEM((NSA_VT_ROWS, cols), F32),
                        pltpu.VMEM((NSA_REP, tk, tq), F32), pltpu.VMEM((NSA_REP, tk, tq), F32)],
        compiler_params=_cparams(("parallel", "parallel", "parallel")),
        name="nsa_sel_win",
    )(qr, ks, vs, kw, vw, sel, gates, oc)


def _nsa_layer(x, norm_g, w_in, pe_k, pe_v, w1k, w2k, w1v, w2v, b_gate, rope, batch, seq_len):
    qp, qr, kc_raw, vc_raw, ks, vs, kw, vw, gates = _nsa_proj(x, norm_g, w_in, b_gate, rope, seq_len)
    kc, vc = _compress(kc_raw, vc_raw, pe_k, pe_v, w1k, w2k, w1v, w2v, seq_len)
    oc, sel = _nsa_cmp(qp, kc, vc, gates, batch, seq_len)
    return _nsa_sel(qr, ks, vs, kw, vw, sel, gates, oc, batch, seq_len)


def _mlstm_proj_kernel(x_ref, g_ref, w_ref, wot_ref, wt_ref, bcol_ref, brow_ref,
                       qk_ref, v_ref, ot_ref, gc_ref, gr_ref):
    hn = _rms(x_ref[...], g_ref[...]).astype(BF16)
    y = _dot(hn, w_ref[...])
    d = D_MODEL
    qk_ref[...] = y[:, :d]
    v_ref[...] = y[:, d:2 * d].astype(BF16)
    ot_ref[...] = _sigmoid(_dot_nt(wot_ref[...], hn))
    gc_ref[...] = y[:, 2 * d:] + bcol_ref[...]
    gr_ref[...] = _dot_nt(wt_ref[...], hn) + brow_ref[...]


def _mlstm_proj(x, g, w_in, b_gates, tm=512):
    m, d = x.shape
    h, dk = MLSTM_HEADS, MLSTM_QK_DIM
    wq = w_in[:, :h * dk].reshape(d, h, dk)
    wk = w_in[:, h * dk:2 * h * dk].reshape(d, h, dk)
    wqk = jnp.concatenate([wq, wk], axis=2).reshape(d, 2 * h * dk)
    wv = w_in[:, d:2 * d]
    wif = w_in[:, 2 * d:2 * d + 2 * h]
    wo = w_in[:, 2 * d + 2 * h:]
    w = jnp.concatenate([wqk, wv, jnp.pad(wif, ((0, 0), (0, LANES - 2 * h)))], axis=1).astype(BF16)
    wot = wo.T.astype(BF16)
    wt = wif.T.astype(BF16)
    bcol = jnp.pad(b_gates, (0, LANES - 2 * h)).reshape(1, LANES)
    brow = b_gates.reshape(2 * h, 1)
    n = w.shape[1]
    row = lambda i: (i, 0)
    col = lambda i: (0, i)
    return pl.pallas_call(
        _mlstm_proj_kernel,
        out_shape=(jax.ShapeDtypeStruct((m, d), F32), jax.ShapeDtypeStruct((m, d), BF16),
                   jax.ShapeDtypeStruct((d, m), F32), jax.ShapeDtypeStruct((m, LANES), F32),
                   jax.ShapeDtypeStruct((2 * h, m), F32)),
        grid=(m // tm,),
        in_specs=[pl.BlockSpec((tm, d), row), _const_spec((1, d)), _const_spec((d, n)), _const_spec((d, d)),
                  _const_spec((2 * h, d)), _const_spec((1, LANES)), _const_spec((2 * h, 1))],
        out_specs=(pl.BlockSpec((tm, d), row), pl.BlockSpec((tm, d), row), pl.BlockSpec((d, tm), col),
                   pl.BlockSpec((tm, LANES), row), pl.BlockSpec((2 * h, tm), col)),
        compiler_params=_cparams(("parallel",)),
        name="mlstm_proj",
    )(x, g.reshape(1, d), w, wot, wt, bcol, brow)


def _log_sigmoid(x):
    return jnp.minimum(x, 0.0) - jnp.log(1.0 + jnp.exp(-jnp.abs(x)))


def _mlstm_core_kernel(qk_ref, v_ref, og_ref, gc_ref, gr_ref, cw_ref, cb_ref, ng_ref, o_ref,
                       qkc_scr, ext_scr, c_scr, n_scr, m_scr, *, tt):
    L = MLSTM_TILE
    H, dk, dv = MLSTM_HEADS, MLSTM_QK_DIM, MLSTM_V_DIM
    seq_start = pl.program_id(1) == 0

    @pl.when(seq_start)
    def _():
        c_scr[...] = jnp.zeros_like(c_scr)
        n_scr[...] = jnp.zeros_like(n_scr)
        m_scr[...] = jnp.zeros_like(m_scr)
        ext_scr[0:SUBLANES, :] = jnp.zeros((SUBLANES, ext_scr.shape[1]), F32)

    ext_scr[SUBLANES:, :] = qk_ref[...]
    is_k = lax.broadcasted_iota(jnp.int32, (tt, 2 * dk), 1) >= dk
    for h in range(H):
        hsl = slice(h * 2 * dk, (h + 1) * 2 * dk)
        cw = cw_ref[:, hsl]
        acc = cb_ref[:, hsl] + cw[MLSTM_CONV - 1:MLSTM_CONV, :] * ext_scr[SUBLANES:, hsl]
        for s in range(1, MLSTM_CONV):
            acc = acc + cw[MLSTM_CONV - 1 - s:MLSTM_CONV - s, :] * ext_scr[SUBLANES - s:SUBLANES - s + tt, hsl]
        act = acc * _sigmoid(acc)
        qkc_scr[:, hsl] = jnp.where(is_k, act * dk ** -0.5, act)
    ext_scr[0:SUBLANES, :] = ext_scr[tt:tt + SUBLANES, :]

    src = lax.broadcasted_iota(jnp.int32, (L, L), 0)
    tgt = lax.broadcasted_iota(jnp.int32, (L, L), 1)
    causal = src <= tgt
    tri3 = jnp.concatenate([(src >= tgt).astype(BF16)] * 3, axis=1)
    triu3 = jnp.concatenate([causal.astype(BF16)] * 3, axis=0)
    n_pad = jnp.zeros((SUBLANES - 3, dk), BF16)

    def chunk(c, carry):
        r0 = pl.multiple_of(c * L, L)
        gcol = gc_ref[pl.ds(r0, L), :]
        grow = gr_ref[c]
        b_col = _cumsum_rows(tri3, _log_sigmoid(gcol))
        b_row = _cumsum_lanes(_log_sigmoid(grow), triu3)
        hs = range(H)
        vsl = [slice(h * dv, (h + 1) * dv) for h in hs]
        qk = [qkc_scr[pl.ds(r0, L), h * 2 * dk:(h + 1) * 2 * dk] for h in hs]
        q = [qk[h][:, :dk].astype(BF16) for h in hs]
        k = [qk[h][:, dk:] for h in hs]
        v = [v_ref[pl.ds(r0, L), vsl[h]] for h in hs]
        col = [gcol[:, h:h + 1] - b_col[:, H + h:H + h + 1] for h in hs]
        li_r = [grow[h:h + 1, :] for h in hs]
        b_r = [b_row[H + h:H + h + 1, :] for h in hs]
        b_end = [b_r[h][:, L - 1:L] for h in hs]
        dmat = [jnp.where(causal, b_r[h] + col[h], -jnp.inf) for h in hs]
        d_max = [jnp.max(dmat[h], axis=0, keepdims=True) for h in hs]
        att = [jnp.exp(dmat[h] - d_max[h]) * _dot_nt(k[h].astype(BF16), q[h]) for h in hs]
        a_sum = [jnp.sum(att[h], axis=0, keepdims=True) for h in hs]
        intra = [_dot_tn(v[h], att[h].astype(BF16)) for h in hs]
        g_max = [jnp.max(b_end[h] - b_r[h] + li_r[h], axis=-1, keepdims=True) for h in hs]
        kw = [k[h] * jnp.exp(b_end[h] + col[h] - g_max[h]) for h in hs]
        c_loc = [_dot_tn(v[h], kw[h].astype(BF16)) for h in hs]
        n_loc = [jnp.sum(kw[h], axis=0, keepdims=True) for h in hs]
        m_prev = [m_scr[h:h + 1, 0:1] for h in hs]
        c_prev = [c_scr[h] for h in hs]
        n_prev = [n_scr[h:h + 1, :] for h in hs]
        m_inter = [b_r[h] + m_prev[h] for h in hs]
        m_t = [jnp.maximum(m_inter[h], d_max[h]) for h in hs]
        w_loc = [jnp.exp(d_max[h] - m_t[h]) for h in hs]
        w_int = [jnp.exp(m_inter[h] - m_t[h]) for h in hs]
        q_c = [_dot_nt(c_prev[h].astype(BF16), q[h]) for h in hs]
        q_n = [_dot_nt(jnp.concatenate(list(_split3(n_prev[h])) + [n_pad], axis=0), q[h]) for h in hs]
        q_n = [q_n[h][0:1] + q_n[h][1:2] + q_n[h][2:3] for h in hs]
        num = [w_loc[h] * intra[h] + w_int[h] * q_c[h] for h in hs]
        den = [w_loc[h] * a_sum[h] + w_int[h] * q_n[h] for h in hs]
        h_t = [num[h] / jnp.maximum(jnp.abs(den[h]), jnp.exp(-m_t[h])) for h in hs]
        h_t = [h_t[h] * lax.rsqrt(jnp.mean(h_t[h] * h_t[h], axis=0, keepdims=True) + NORM_EPS) for h in hs]
        m_new = [jnp.maximum(b_end[h] + m_prev[h], g_max[h]) for h in hs]
        a = [jnp.exp(b_end[h] + m_prev[h] - m_new[h]) for h in hs]
        sc = [jnp.exp(g_max[h] - m_new[h]) for h in hs]
        for h in hs:
            out = h_t[h] * ng_ref[vsl[h], :] * og_ref[vsl[h], pl.ds(r0, L)]
            o_ref[pl.ds(r0, L), vsl[h]] = out.T.astype(o_ref.dtype)
            c_scr[h] = a[h] * c_prev[h] + sc[h] * c_loc[h]
            n_scr[h:h + 1, :] = a[h] * n_prev[h] + sc[h] * n_loc[h]
            m_scr[h:h + 1, :] = jnp.broadcast_to(m_new[h], (1, LANES))
        return carry

    lax.fori_loop(0, tt // L, chunk, 0)


def _mlstm_core(qk, v, ogt, gcol, grow3, conv_w, conv_b, norm_gb, batch, seq_len, tt=256):
    m, d = qk.shape
    H, dk, dv = MLSTM_HEADS, MLSTM_QK_DIM, MLSTM_V_DIM
    nt = seq_len // tt
    ncs = tt // MLSTM_TILE
    row = lambda b, i: (b * nt + i, 0)
    return pl.pallas_call(
        functools.partial(_mlstm_core_kernel, tt=tt),
        out_shape=jax.ShapeDtypeStruct((m, d), BF16),
        grid=(batch, nt),
        in_specs=[pl.BlockSpec((tt, d), row), pl.BlockSpec((tt, d), row),
                  pl.BlockSpec((d, tt), lambda b, i: (0, b * nt + i)),
                  pl.BlockSpec((tt, LANES), row),
                  pl.BlockSpec((ncs, 2 * H, MLSTM_TILE), lambda b, i: (b * nt + i, 0, 0)),
                  _const_spec((SUBLANES, d)), _const_spec((1, d)), _const_spec((d, LANES))],
        out_specs=pl.BlockSpec((tt, d), row),
        scratch_shapes=[pltpu.VMEM((tt, d), F32), pltpu.VMEM((tt + SUBLANES, d), F32),
                        pltpu.VMEM((H, dv, dk), F32), pltpu.VMEM((H, dk), F32),
                        pltpu.VMEM((H, LANES), F32)],
        compiler_params=_cparams(("arbitrary", "arbitrary")),
        name="mlstm_core",
    )(qk, v, ogt, gcol, grow3, conv_w, conv_b, norm_gb)


def _mlstm_layer(x, norm_g, w_in, conv_w, conv_b, b_gates, hnorm, batch, seq_len):
    H, dk = MLSTM_HEADS, MLSTM_QK_DIM
    qk, v, ogt, gcol, grow = _mlstm_proj(x, norm_g, w_in, b_gates)
    m = x.shape[0]
    grow3 = grow.reshape(2 * H, m // MLSTM_TILE, MLSTM_TILE).transpose(1, 0, 2)
    norm_gb = jnp.broadcast_to(hnorm.reshape(-1, 1), (hnorm.shape[0], LANES))

    def perm(z):
        lead = z.shape[:-1]
        zq = z[..., :H * dk].reshape(*lead, H, dk)
        zk = z[..., H * dk:].reshape(*lead, H, dk)
        return jnp.concatenate([zq, zk], axis=-1).reshape(*lead, 2 * H * dk)

    cw = jnp.pad(perm(conv_w), ((0, SUBLANES - MLSTM_CONV), (0, 0)))
    return _mlstm_core(qk, v, ogt, gcol, grow3, cw, perm(conv_b).reshape(1, -1), norm_gb,
                       batch, seq_len)


def _softplus(x):
    return jnp.maximum(x, 0.0) + jnp.log(1.0 + jnp.exp(-jnp.abs(x)))


def _rwkv_proj_kernel(x_ref, g_ref, mu_ref, wr_ref, wk_ref, wv_ref, ww1_ref, ww2_ref, w0_ref,
                      aw1_ref, aw2_ref, a0_ref, gw1_ref, gw2_ref,
                      r_ref, k_ref, v_ref, lw_ref, a_ref, go_ref, prev_scr, *, tiles_per_seq):
    tm = x_ref.shape[0]
    h = _rms(x_ref[...], g_ref[...])
    seq_start = (pl.program_id(0) % tiles_per_seq) == 0
    prev = jnp.where(seq_start, 0.0, prev_scr[SUBLANES - 1:SUBLANES, :])
    prev_scr[...] = h[tm - SUBLANES:, :]
    rows = lax.broadcasted_iota(jnp.int32, h.shape, 0)
    xx = jnp.where(rows == 0, prev, pltpu.roll(h, 1, 0)) - h
    mix = lambda j: (h + xx * mu_ref[j:j + 1, :]).astype(BF16)
    r_ref[...] = _dot(mix(0), wr_ref[...])
    k_ref[...] = _dot(mix(2), wk_ref[...])
    v_ref[...] = _dot(mix(3), wv_ref[...])
    wl = _dot(jnp.tanh(_dot(mix(1), ww1_ref[...])).astype(BF16), ww2_ref[...]) + w0_ref[...]
    w_log = -_softplus(-wl) - 0.5
    lw_ref[...] = -jnp.exp(w_log)
    al = _dot(_dot(mix(4), aw1_ref[...]).astype(BF16), aw2_ref[...]) + a0_ref[...]
    a_ref[...] = _sigmoid(al)
    go_ref[...] = _dot(_sigmoid(_dot(mix(5), gw1_ref[...])).astype(BF16), gw2_ref[...])


def _rwkv_proj(x, g, mu, w_r, w_k, w_v, w0, w_w1, w_w2, a0, a_w1, a_w2, g_w1, g_w2, seq_len, tm=256):
    m, d = x.shape
    row = lambda i: (i, 0)
    bf = lambda w: w.astype(BF16)
    mu8 = jnp.pad(mu, ((0, SUBLANES - mu.shape[0]), (0, 0)))
    consts = [g.reshape(1, d), mu8, bf(w_r), bf(w_k), bf(w_v), bf(w_w1), bf(w_w2), w0.reshape(1, d),
              bf(a_w1), bf(a_w2), a0.reshape(1, d), bf(g_w1), bf(g_w2)]
    out = jax.ShapeDtypeStruct((m, d), F32)
    return pl.pallas_call(
        functools.partial(_rwkv_proj_kernel, tiles_per_seq=seq_len // tm),
        out_shape=(out,) * 6,
        grid=(m // tm,),
        in_specs=[pl.BlockSpec((tm, d), row)] + [_const_spec(c.shape) for c in consts],
        out_specs=(pl.BlockSpec((tm, d), row),) * 6,
        scratch_shapes=[pltpu.VMEM((SUBLANES, d), F32)],
        compiler_params=_cparams(("arbitrary",)),
        name="rwkv_proj",
    )(x, *consts)


def _block_diag(x, lo):
    zero = jnp.zeros_like(x)
    return jnp.concatenate([jnp.where(lo, x, zero), jnp.where(lo, zero, x)], axis=0)


class _PairMat:
    def __init__(self, x, lo):
        self.x, self.lo = x, lo
        self._lhs = self._rhs = None

    def lhs(self):
        if self._lhs is None:
            hi, lo = _split2(self.x)
            self._lhs = jnp.concatenate([hi, lo, hi], axis=1)
        return self._lhs

    def rhs(self):
        if self._rhs is None:
            hi, lo = _split2(self.x)
            bh = _block_diag(hi, self.lo)
            self._rhs = jnp.concatenate([bh, bh, _block_diag(lo, self.lo)], axis=0)
        return self._rhs


def _pair_mm(p, q):
    return _dot(p.lhs(), q.rhs())


def _rwkv_core_kernel(r_ref, k_ref, v_ref, lw_ref, a_ref, go_ref, kk_ref, ka_ref, rk_ref,
                      lnw_ref, lnb_ref, o_ref, z_scr, *, tt):
    L, N = RWKV_CHUNK, RWKV_HEAD_DIM
    npair = z_scr.shape[0]
    pairs = range(npair)

    @pl.when(pl.program_id(1) == 0)
    def _():
        z_scr[...] = jnp.zeros_like(z_scr)

    ri = lax.broadcasted_iota(jnp.int32, (L, LANES), 0)
    ln = lax.broadcasted_iota(jnp.int32, (L, LANES), 1)
    si = ln % N
    lo = ln < N
    lower_incl = ri >= si
    lower_strict = ri > si
    blk_diag = (ri // RWKV_INV_BLOCK) == (si // RWKV_INV_BLOCK)
    eye = (ri == si).astype(F32)
    tri = (lax.broadcasted_iota(jnp.int32, (L, L), 0) >= lax.broadcasted_iota(jnp.int32, (L, L), 1))
    tri3 = jnp.concatenate([tri.astype(BF16)] * 3, axis=1)
    z_mask = ((lax.broadcasted_iota(jnp.int32, (2 * N, LANES), 0) // N)
              == (lax.broadcasted_iota(jnp.int32, (2 * N, LANES), 1) // N))

    def half_sum(x):
        s0 = jnp.sum(jnp.where(lo, x, 0.0), axis=-1, keepdims=True)
        s1 = jnp.sum(jnp.where(lo, 0.0, x), axis=-1, keepdims=True)
        return jnp.where(lo, s0, s1)

    bd = lambda x: _block_diag(x, lo)
    mk = lambda xs: [_PairMat(x, lo) for x in xs]
    mm = lambda ps, qs: [_pair_mm(p, q) for p, q in zip(ps, qs)]

    def chunk(c, carry):
        r0 = pl.multiple_of(c * L, L)
        sl = [slice(p * LANES, (p + 1) * LANES) for p in pairs]
        ld = lambda ref: [ref[pl.ds(r0, L), s] for s in sl]
        r, k, v, lw, a = ld(r_ref), ld(k_ref), ld(v_ref), ld(lw_ref), ld(a_ref)
        kk = [k[p] * kk_ref[:, sl[p]] for p in pairs]
        kk = [kk[p] / jnp.maximum(jnp.sqrt(half_sum(kk[p] * kk[p])), 1e-12) for p in pairs]
        km = [k[p] * (1.0 + (a[p] - 1.0) * ka_ref[:, sl[p]]) for p in pairs]
        bv = [kk[p] * a[p] for p in pairs]
        cum = [_cumsum_rows(tri3, lw[p]) for p in pairs]
        cum_end = [cum[p][L - 1:L, :] for p in pairs]
        w_inv = [jnp.exp(-cum[p]) for p in pairs]
        w_out = [jnp.exp(cum_end[p] - cum[p]) for p in pairs]
        kk_h = [(kk[p] * jnp.exp(cum[p] - lw[p])).astype(BF16) for p in pairs]
        r_h = [(r[p] * jnp.exp(cum[p])).astype(BF16) for p in pairs]
        b_t = [(bv[p] * w_inv[p]).astype(BF16) for p in pairs]
        k_t = [(km[p] * w_inv[p]).astype(BF16) for p in pairs]
        bbar = [(bv[p] * w_out[p]).astype(BF16) for p in pairs]
        kbar = [(km[p] * w_out[p]).astype(BF16) for p in pairs]
        vb = [v[p].astype(BF16) for p in pairs]
        lhs = [jnp.concatenate([kk_h[p], r_h[p]], axis=0) for p in pairs]
        ab = [_dot_nt(lhs[p], bd(b_t[p])) for p in pairs]
        ak = [_dot_nt(lhs[p], bd(k_t[p])) for p in pairs]
        a_ub = [jnp.where(lower_strict, ab[p][:L], 0.0) for p in pairs]
        a_rb = [jnp.where(lower_incl, ab[p][L:], 0.0).astype(BF16) for p in pairs]
        a_uk = [jnp.where(lower_strict, ak[p][:L], 0.0).astype(BF16) for p in pairs]
        a_rk = [jnp.where(lower_incl, ak[p][L:], 0.0).astype(BF16) for p in pairs]

        n1 = mk([jnp.where(blk_diag, -a_ub[p], 0.0) for p in pairs])
        n2 = mk(mm(n1, n1))
        n4 = mk(mm(n2, n2))
        n8 = mk(mm(n4, n4))
        acc = mk([eye + n1[p].x for p in pairs])
        for nk in (n2, n4):
            prod = mm(acc, nk)
            acc = mk([acc[p].x + prod[p] for p in pairs])
        prod = mm(acc, n8)
        d_inv = mk([acc[p].x + prod[p] for p in pairs])
        l_off = mk([jnp.where(blk_diag, 0.0, a_ub[p]) for p in pairs])
        e1 = mk([-x for x in mm(d_inv, l_off)])
        e2 = mk(mm(e1, e1))
        qm = mk([eye + e1[p].x for p in pairs])
        prod = mm(qm, e2)
        qm = mk([qm[p].x + prod[p] for p in pairs])
        t_inv = mk(mm(qm, d_inv))

        z = [z_scr[p] for p in pairs]
        zb = [z[p].astype(BF16) for p in pairs]
        bdv = [bd(vb[p]) for p in pairs]
        rhs_u = mk([_dot_nt(kk_h[p], zb[p]) + _dot(a_uk[p], bdv[p]) for p in pairs])
        u = [-x for x in mm(t_inv, rhs_u)]
        ub = [u[p].astype(BF16) for p in pairs]
        y = [_dot_nt(r_h[p], zb[p])
             + _dot(jnp.concatenate([a_rb[p], a_rk[p]], axis=1),
                    jnp.concatenate([bd(ub[p]), bdv[p]], axis=0)) for p in pairs]
        upd = [_dot_tn(jnp.concatenate([ub[p], vb[p]], axis=0),
                       jnp.concatenate([bbar[p], kbar[p]], axis=0)) for p in pairs]
        for p in pairs:
            z_scr[p] = z[p] * jnp.exp(cum_end[p]) + jnp.where(z_mask, upd[p], 0.0)

        inv_n = 1.0 / N
        mean = [half_sum(y[p]) * inv_n for p in pairs]
        yc = [y[p] - mean[p] for p in pairs]
        var = [half_sum(yc[p] * yc[p]) * inv_n for p in pairs]
        bonus = [half_sum(r[p] * km[p] * rk_ref[:, sl[p]]) * v[p] for p in pairs]
        for p in pairs:
            yn = yc[p] * lax.rsqrt(var[p] + RWKV_GN_EPS) * lnw_ref[:, sl[p]] + lnb_ref[:, sl[p]]
            o_ref[pl.ds(r0, L), sl[p]] = ((yn + bonus[p]) * go_ref[pl.ds(r0, L), sl[p]]).astype(o_ref.dtype)
        return carry

    lax.fori_loop(0, tt // L, chunk, 0)


def _rwkv_core(r, k, v, lw, a, go, k_k, k_a, r_k, ln_w, ln_b, batch, seq_len, tt=256):
    m, d = r.shape
    nt = seq_len // tt
    npair = d // LANES
    blk = pl.BlockSpec((tt, d), lambda b, i: (b * nt + i, 0))
    par = _const_spec((1, d))
    row1 = lambda z: z.reshape(1, d)
    return pl.pallas_call(
        functools.partial(_rwkv_core_kernel, tt=tt),
        out_shape=jax.ShapeDtypeStruct((m, d), BF16),
        grid=(batch, nt),
        in_specs=[blk] * 6 + [par] * 5,
        out_specs=blk,
        scratch_shapes=[pltpu.VMEM((npair, 2 * RWKV_HEAD_DIM, LANES), F32)],
        compiler_params=_cparams(("arbitrary", "arbitrary")),
        name="rwkv_core",
    )(r, k, v, lw, a, go, row1(k_k), row1(k_a), row1(r_k), row1(ln_w), row1(ln_b))


def _rwkv_layer(x, norm_g, mu, w_r, w_k, w_v, w0, w_w1, w_w2, a0, a_w1, a_w2, g_w1, g_w2,
                k_k, k_a, r_k, ln_w, ln_b, batch, seq_len):
    r, k, v, lw, a, go = _rwkv_proj(x, norm_g, mu, w_r, w_k, w_v, w0, w_w1, w_w2, a0, a_w1, a_w2,
                                    g_w1, g_w2, seq_len)
    return _rwkv_core(r, k, v, lw, a, go, k_k, k_a, r_k, ln_w, ln_b, batch, seq_len)


def kernel(x, norm_mixer, norm_ffn, ffn_w_up, ffn_conv_w, ffn_conv_b, ffn_w_down, nsa_w_in, nsa_pe_k, nsa_pe_v, nsa_cmp_k_w1, nsa_cmp_k_w2, nsa_cmp_v_w1, nsa_cmp_v_w2, nsa_b_gate, nsa_w_out, mlstm_w_in, mlstm_conv_w, mlstm_conv_b, mlstm_b_gates, mlstm_norm, mlstm_w_out, rwkv_mu, rwkv_w_r, rwkv_w_k, rwkv_w_v, rwkv_w_o, rwkv_w0, rwkv_w_w1, rwkv_w_w2, rwkv_a0, rwkv_a_w1, rwkv_a_w2, rwkv_g_w1, rwkv_g_w2, rwkv_k_k, rwkv_k_a, rwkv_r_k, rwkv_ln_w, rwkv_ln_b, final_norm):
    batch, seq_len, d = x.shape
    depth = norm_mixer.shape[0]
    rope = _rope_tables(seq_len)
    xf = x.reshape(batch * seq_len, d)
    for i in range(depth):
        kind, j = i % 3, i // 3
        if kind == 0:
            w_o = nsa_w_out[j]
            o = _nsa_layer(xf, norm_mixer[i], nsa_w_in[j], nsa_pe_k[j], nsa_pe_v[j], nsa_cmp_k_w1[j],
                           nsa_cmp_k_w2[j], nsa_cmp_v_w1[j], nsa_cmp_v_w2[j], nsa_b_gate[j], rope, batch, seq_len)
        elif kind == 1:
            w_o = mlstm_w_out[j]
            o = _mlstm_layer(xf, norm_mixer[i], mlstm_w_in[j], mlstm_conv_w[j], mlstm_conv_b[j],
                             mlstm_b_gates[j], mlstm_norm[j], batch, seq_len)
        else:
            w_o = rwkv_w_o[j]
            o = _rwkv_layer(xf, norm_mixer[i], rwkv_mu[j], rwkv_w_r[j], rwkv_w_k[j], rwkv_w_v[j],
                            rwkv_w0[j], rwkv_w_w1[j], rwkv_w_w2[j], rwkv_a0[j],
                            rwkv_a_w1[j], rwkv_a_w2[j], rwkv_g_w1[j], rwkv_g_w2[j], rwkv_k_k[j],
                            rwkv_k_a[j], rwkv_r_k[j], rwkv_ln_w[j], rwkv_ln_b[j], batch, seq_len)
        xf = _ffn(xf, o, w_o, norm_ffn[i], ffn_w_up[i], ffn_conv_w[i], ffn_conv_b[i], ffn_w_down[i],
                  seq_len, final_g=final_norm if i == depth - 1 else None)
    return xf.reshape(batch, seq_len, d)
```

```python
import functools
import math

import jax
import jax.numpy as jnp
import numpy as np
from jax import lax
from jax.experimental import pallas as pl
from jax.experimental.pallas import tpu as pltpu

F32 = jnp.float32
BF16 = jnp.bfloat16

D_MODEL = 1024
DEPTH = 4
NORM_EPS = 1e-6
ROPE_THETA = 500000.0

NSA_HEAD_DIM = 64
NSA_HEADS = 16
NSA_GROUPS = 4
NSA_REP = NSA_HEADS // NSA_GROUPS
NSA_ROT_DIM = 16
CMP_BLOCK = 32
CMP_STRIDE = 16
CMP_HIDDEN = 256
SEL_BLOCK = 64
SEL_TOPK = 16
WINDOW = 512
NSA_KV = NSA_GROUPS * NSA_HEAD_DIM
NSA_VT_ROWS = NSA_HEAD_DIM + 16
NSA_GATE_ROWS = 16

MLSTM_HEADS = 8
MLSTM_QK_DIM = 64
MLSTM_V_DIM = 128
MLSTM_TILE = 128
MLSTM_CONV = 4

RWKV_HEAD_DIM = 64
RWKV_HEADS = 16
RWKV_GN_EPS = 64e-5
RWKV_CHUNK = 64
RWKV_INV_BLOCK = 16

FFN_DIM = 2816
FFN_CONV = 3
FFN_CHUNK = 256

LOG2E = math.log2(math.e)
MASKED = -1e30

LANES = 128
SUBLANES = 8
VMEM_LIMIT = 56 * 1024 * 1024


def _dot(a, b):
    return jnp.dot(a, b, preferred_element_type=F32)


def _dot_nt(a, b):
    return lax.dot_general(a, b, (((1,), (1,)), ((), ())), preferred_element_type=F32)


def _dot_tn(a, b):
    return lax.dot_general(a, b, (((0,), (0,)), ((), ())), preferred_element_type=F32)


def _split2(x):
    hi = x.astype(BF16)
    return hi, (x - hi.astype(F32)).astype(BF16)


def _split3(x):
    hi = x.astype(BF16)
    r1 = x - hi.astype(F32)
    mid = r1.astype(BF16)
    return hi, mid, (r1 - mid.astype(F32)).astype(BF16)


def _cumsum_rows(tri3, x):
    return _dot(tri3, jnp.concatenate(_split3(x), axis=0))


def _cumsum_lanes(x, triu3):
    return _dot(jnp.concatenate(_split3(x), axis=1), triu3)


def _rms(x, g):
    ms = jnp.mean(x * x, axis=-1, keepdims=True)
    return x * lax.rsqrt(ms + NORM_EPS) * g


def _sigmoid(x):
    return 1.0 / (1.0 + jnp.exp(-x))


def _cparams(sem):
    return pltpu.CompilerParams(dimension_semantics=sem, vmem_limit_bytes=VMEM_LIMIT)


def _const_spec(shape):
    n = len(shape)
    return pl.BlockSpec(shape, lambda *_: (0,) * n, pipeline_mode=pl.Buffered(1))


def _ffn_kernel(res_ref, a_ref, wo_ref, g_ref, wu_ref, cw_ref, cb_ref, wd_ref, fg_ref, o_ref,
                h_scr, carry_scr, ga_scr, va_scr, gb_scr, vb_scr, *, tiles_per_seq, n_chunks, final_norm):
    tm = res_ref.shape[0]
    fc = FFN_CHUNK
    x = res_ref[...] + _dot(a_ref[...], wo_ref[...])
    h_scr[...] = _rms(x, g_ref[...]).astype(BF16)
    o_ref[...] = x
    seq_start = (pl.program_id(0) % tiles_per_seq) == 0
    rows = lax.broadcasted_iota(jnp.int32, (tm, fc), 0)
    cols = lambda c, base=0: pl.ds(pl.multiple_of(base + c * fc, LANES), fc)

    def up(c, g_scr, v_scr):
        h = h_scr[...]
        g_scr[...] = _dot(h, wu_ref[:, cols(c)])
        v_scr[...] = _dot(h, wu_ref[:, cols(c, FFN_DIM)])

    def down(c, g_scr, v_scr):
        gate, val = g_scr[...], v_scr[...]
        prev = carry_scr[:, cols(c)]
        prev = jnp.where(seq_start, 0.0, prev)
        p1 = prev[SUBLANES - 1:SUBLANES, :]
        p2 = prev[SUBLANES - 2:SUBLANES - 1, :]
        carry_scr[:, cols(c)] = gate[tm - SUBLANES:, :]
        g1 = jnp.where(rows == 0, p1, pltpu.roll(gate, 1, 0))
        g2 = jnp.where(rows == 0, p2, jnp.where(rows == 1, p1, pltpu.roll(gate, 2, 0)))
        cw = cw_ref[:, cols(c)]
        y = cw[2:3, :] * gate + cw[1:2, :] * g1 + cw[0:1, :] * g2 + cb_ref[:, cols(c)]
        act = (y * _sigmoid(y) * val).astype(BF16)
        o_ref[...] += _dot(act, wd_ref[pl.ds(pl.multiple_of(c * fc, fc), fc), :])

    assert n_chunks % 2 == 1

    def chunk_pair(j, carry):
        up(2 * j + 1, gb_scr, vb_scr)
        down(2 * j, ga_scr, va_scr)
        up(2 * j + 2, ga_scr, va_scr)
        down(2 * j + 1, gb_scr, vb_scr)
        return carry

    up(0, ga_scr, va_scr)
    lax.fori_loop(0, n_chunks // 2, chunk_pair, 0)
    down(n_chunks - 1, ga_scr, va_scr)
    if final_norm:
        o_ref[...] = _rms(o_ref[...], fg_ref[...])


def _ffn(res, a, w_o, g, w_up, conv_w, conv_b, w_down, seq_len, final_g=None, tm=1024):
    m, d = res.shape
    nc = FFN_DIM // FFN_CHUNK
    cw = jnp.pad(conv_w, ((0, SUBLANES - FFN_CONV), (0, 0)))
    row = lambda i: (i, 0)
    fg = jnp.ones((d,), F32) if final_g is None else final_g
    kern = functools.partial(_ffn_kernel, tiles_per_seq=seq_len // tm, n_chunks=nc,
                             final_norm=final_g is not None)
    return pl.pallas_call(
        kern,
        out_shape=jax.ShapeDtypeStruct((m, d), F32),
        grid=(m // tm,),
        in_specs=[pl.BlockSpec((tm, d), row), pl.BlockSpec((tm, d), row), _const_spec((d, d)),
                  _const_spec((1, d)), _const_spec((d, 2 * FFN_DIM)),
                  _const_spec((SUBLANES, FFN_DIM)), _const_spec((1, FFN_DIM)),
                  _const_spec((FFN_DIM, d)), _const_spec((1, d))],
        out_specs=pl.BlockSpec((tm, d), row),
        scratch_shapes=[pltpu.VMEM((tm, d), BF16),
                        pltpu.VMEM((SUBLANES, FFN_DIM), F32)]
                       + [pltpu.VMEM((tm, FFN_CHUNK), F32)] * 4,
        compiler_params=_cparams(("arbitrary",)),
        name="conv_ffn",
    )(res, a, w_o.astype(BF16), g.reshape(1, d), w_up.astype(BF16), cw, conv_b.reshape(1, FFN_DIM),
      w_down.astype(BF16), fg.reshape(1, d))


def _nsa_proj_kernel(x_ref, g_ref, w_ref, wvt_ref, bg_ref, rc_ref, rs1_ref, rs2_ref,
                     qp_ref, qr_ref, kc_ref, vc_ref, ks_ref, vs_ref, kw_ref, vw_ref, gate_ref,
                     cmp_scr, *, tiles_per_seq):
    hn = _rms(x_ref[...], g_ref[...]).astype(BF16)
    y = _dot(hn, w_ref[...])
    vt = _dot_nt(wvt_ref[...], hn)
    rc, rs1, rs2 = rc_ref[...], rs1_ref[...], rs2_ref[...]
    dh = NSA_HEAD_DIM

    def rope(z):
        return z * rc + pltpu.roll(z, 8, 1) * rs1 + pltpu.roll(z, LANES - 8, 1) * rs2

    scale = dh ** -0.5 * LOG2E
    for j in range(D_MODEL // LANES):
        q = y[:, j * LANES:(j + 1) * LANES] * scale
        qp_ref[:, j * LANES:(j + 1) * LANES] = q.astype(BF16)
        qr_ref[:, j * LANES:(j + 1) * LANES] = rope(q).astype(BF16)

    def kv_chunk(idx):
        return y[:, D_MODEL + idx * NSA_KV:D_MODEL + (idx + 1) * NSA_KV]

    def split_groups(z, ref, dtype):
        for g in range(NSA_GROUPS):
            ref[g] = z[:, g * dh:(g + 1) * dh].astype(dtype)

    def rope256(z):
        return jnp.concatenate([rope(z[:, :LANES]), rope(z[:, LANES:])], axis=1)

    tm = y.shape[0]
    nrow = tm // CMP_STRIDE
    for j in range(2 * NSA_KV // LANES):
        cmp_scr[j] = y[:, D_MODEL + j * LANES:D_MODEL + (j + 1) * LANES]
    for j in range(2 * NSA_KV // LANES):
        ref = kc_ref if j < NSA_KV // LANES else vc_ref
        toks = [cmp_scr[j, pl.ds(tok, nrow, stride=CMP_STRIDE), :] for tok in range(CMP_STRIDE)]
        for half in range(LANES // dh):
            g = (j % (NSA_KV // LANES)) * (LANES // dh) + half
            ref[g] = jnp.concatenate([t[:, half * dh:(half + 1) * dh] for t in toks], axis=1)
    t_pos = (pl.program_id(0) % tiles_per_seq) * tm + lax.broadcasted_iota(jnp.int32, (tm, LANES), 0)
    onehot = (lax.broadcasted_iota(jnp.int32, (tm, LANES), 1) == t_pos // SEL_BLOCK).astype(F32)
    ksel = rope256(kv_chunk(2))
    zpad = jnp.zeros((tm, LANES - dh), F32)
    for g in range(NSA_GROUPS):
        ks_ref[g] = jnp.concatenate([ksel[:, g * dh:(g + 1) * dh], zpad, onehot], axis=1).astype(BF16)
    split_groups(rope256(kv_chunk(3)), kw_ref, BF16)
    ones_pad = (lax.broadcasted_iota(jnp.int32, (NSA_VT_ROWS - dh, tm), 0) == 0).astype(F32)
    for g in range(NSA_GROUPS):
        vs_ref[g] = jnp.concatenate([vt[g * dh:(g + 1) * dh], ones_pad], axis=0).astype(BF16)
        vw_ref[g] = jnp.concatenate([vt[NSA_KV + g * dh:NSA_KV + (g + 1) * dh], ones_pad], axis=0).astype(BF16)
    gate = _sigmoid(vt[2 * NSA_KV:] + bg_ref[...])
    for g in range(NSA_GROUPS):
        gate_ref[g] = gate[g * NSA_GATE_ROWS:(g + 1) * NSA_GATE_ROWS]


def _rope_tables(seq_len):
    half = NSA_ROT_DIM // 2
    inv_freq = ROPE_THETA ** (-jnp.arange(half, dtype=F32) / half)
    ang = jnp.arange(seq_len, dtype=F32)[:, None] * inv_freq[None, :]
    cos, sin = jnp.cos(ang), jnp.sin(ang)
    zeros = jnp.zeros((seq_len, NSA_HEAD_DIM - NSA_ROT_DIM), F32)
    z8 = jnp.zeros((seq_len, half), F32)
    rc = jnp.concatenate([cos, cos, zeros + 1.0], axis=1)
    rs1 = jnp.concatenate([z8, sin, zeros], axis=1)
    rs2 = jnp.concatenate([-sin, z8, zeros], axis=1)
    two = lambda t: jnp.concatenate([t, t], axis=1)
    return two(rc), two(rs1), two(rs2)


def _nsa_proj(x, g, w_in, b_gate, rope, seq_len, tm=512):
    m, d = x.shape
    n_kv = 6 * NSA_KV
    kv = lambda idx: w_in[:, D_MODEL + idx * NSA_KV:D_MODEL + (idx + 1) * NSA_KV]
    pad_g = NSA_GATE_ROWS - NSA_REP * 3
    wg = w_in[:, D_MODEL + n_kv:].reshape(d, NSA_GROUPS, NSA_REP * 3)
    wg = jnp.pad(wg, ((0, 0), (0, 0), (0, pad_g))).reshape(d, NSA_GROUPS * NSA_GATE_ROWS)
    w = jnp.concatenate([w_in[:, :D_MODEL], kv(0), kv(1), kv(2), kv(4)], axis=1).astype(BF16)
    wvt = jnp.concatenate([kv(3), kv(5), wg], axis=1).T.astype(BF16)
    bg = jnp.pad(b_gate.reshape(NSA_GROUPS, NSA_REP * 3), ((0, 0), (0, pad_g)))
    bg = bg.reshape(NSA_GROUPS * NSA_GATE_ROWS, 1)
    n = w.shape[1]
    tps = seq_len // tm
    row = lambda i: (i, 0)
    rope_spec = pl.BlockSpec((tm, LANES), lambda i: (i % tps, 0))
    assert seq_len // SEL_BLOCK <= LANES
    g_out = lambda dt, w=NSA_HEAD_DIM: jax.ShapeDtypeStruct((NSA_GROUPS, m, w), dt)
    g_spec = pl.BlockSpec((NSA_GROUPS, tm, NSA_HEAD_DIM), lambda i: (0, i, 0))
    ks_spec = pl.BlockSpec((NSA_GROUPS, tm, 2 * LANES), lambda i: (0, i, 0))
    cmp_out = jax.ShapeDtypeStruct((NSA_GROUPS, m // CMP_STRIDE, CMP_STRIDE * NSA_HEAD_DIM), F32)
    cmp_spec = pl.BlockSpec((NSA_GROUPS, tm // CMP_STRIDE, CMP_STRIDE * NSA_HEAD_DIM), lambda i: (0, i, 0))
    vt_out = jax.ShapeDtypeStruct((NSA_GROUPS, NSA_VT_ROWS, m), BF16)
    vt_spec = pl.BlockSpec((NSA_GROUPS, NSA_VT_ROWS, tm), lambda i: (0, 0, i))
    return pl.pallas_call(
        functools.partial(_nsa_proj_kernel, tiles_per_seq=tps),
        out_shape=(jax.ShapeDtypeStruct((m, d), BF16), jax.ShapeDtypeStruct((m, d), BF16),
                   cmp_out, cmp_out, g_out(BF16, 2 * LANES), vt_out, g_out(BF16), vt_out,
                   jax.ShapeDtypeStruct((NSA_GROUPS, NSA_GATE_ROWS, m), F32)),
        grid=(m // tm,),
        in_specs=[pl.BlockSpec((tm, d), row), _const_spec((1, d)), _const_spec((d, n)),
                  _const_spec((2 * NSA_KV + NSA_GROUPS * NSA_GATE_ROWS, d)),
                  _const_spec((NSA_GROUPS * NSA_GATE_ROWS, 1)), rope_spec, rope_spec, rope_spec],
        out_specs=(pl.BlockSpec((tm, d), row), pl.BlockSpec((tm, d), row),
                   cmp_spec, cmp_spec, ks_spec, vt_spec, g_spec, vt_spec,
                   pl.BlockSpec((NSA_GROUPS, NSA_GATE_ROWS, tm), lambda i: (0, 0, i))),
        scratch_shapes=[pltpu.VMEM((2 * NSA_KV // LANES, tm, LANES), F32)],
        compiler_params=_cparams(("parallel",)),
        name="nsa_proj",
    )(x, g.reshape(1, d), w, wvt, bg, *rope)


def _gelu_tanh(x):
    return 0.5 * x * (1.0 + jnp.tanh(math.sqrt(2.0 / math.pi) * (x + 0.044715 * (x * x * x))))


def _compress_kernel(zk_ref, zv_ref, pek_ref, pev_ref, w1k_ref, w2k_ref, w1v_ref, w2v_ref,
                     kc_ref, vc_ref):
    nrow = zk_ref.shape[0]
    rows = lax.broadcasted_iota(jnp.int32, (nrow, NSA_HEAD_DIM), 0)

    def one(z_ref, pe_ref, w1_ref, w2_ref, o_ref):
        z = z_ref[...]
        a = _dot((z + pe_ref[0:1, :]).astype(BF16), w1_ref[0])
        b = _dot((z + pe_ref[1:2, :]).astype(BF16), w1_ref[1])
        hid = a + pltpu.roll(b, nrow - 1, 0)
        out = _dot(_gelu_tanh(hid).astype(BF16), w2_ref[...])
        o_ref[...] = jnp.where(rows == nrow - 1, 0.0, out).astype(o_ref.dtype)

    one(zk_ref, pek_ref, w1k_ref, w2k_ref, kc_ref)
    one(zv_ref, pev_ref, w1v_ref, w2v_ref, vc_ref)


def _compress(kc_raw, vc_raw, pe_k, pe_v, w1k, w2k, w1v, w2v, seq_len):
    g, nrows, half = kc_raw.shape
    dh = half // CMP_STRIDE
    nchunk = seq_len // CMP_STRIDE
    zk = kc_raw.reshape(g * nrows, half)
    zv = vc_raw.reshape(g * nrows, half)
    pe2 = lambda pe: pe.reshape(2, half)
    w1 = lambda w: w.astype(BF16).reshape(2, half, CMP_HIDDEN)
    nblk = zk.shape[0] // nchunk
    row = lambda i: (i, 0)
    return pl.pallas_call(
        _compress_kernel,
        out_shape=(jax.ShapeDtypeStruct((zk.shape[0], dh), BF16),
                   jax.ShapeDtypeStruct((zk.shape[0], dh), BF16)),
        grid=(nblk,),
        in_specs=[pl.BlockSpec((nchunk, half), row), pl.BlockSpec((nchunk, half), row),
                  _const_spec((2, half)), _const_spec((2, half)),
                  _const_spec((2, half, CMP_HIDDEN)), _const_spec((CMP_HIDDEN, dh)),
                  _const_spec((2, half, CMP_HIDDEN)), _const_spec((CMP_HIDDEN, dh))],
        out_specs=(pl.BlockSpec((nchunk, dh), row), pl.BlockSpec((nchunk, dh), row)),
        compiler_params=_cparams(("parallel",)),
        name="nsa_compress",
    )(zk, zv, pe2(pe_k), pe2(pe_v), w1(w1k), w2k.astype(BF16), w1(w1v), w2v.astype(BF16))


def _nsa_cmp_kernel(q_ref, kc_ref, vc_ref, gate_ref, ovt_ref, oc_ref, selt_ref, *, tq):
    qi = pl.program_id(2)
    dh = NSA_HEAD_DIM
    q = q_ref[...]
    qs = jnp.concatenate([q[:, r * dh:(r + 1) * dh] for r in range(NSA_REP)], axis=0)
    kc, vc = kc_ref[...], vc_ref[...]
    ncmp = kc.shape[0]
    t = qi * tq + lax.broadcasted_iota(jnp.int32, (ncmp, tq), 1)
    cmp_end = lax.broadcasted_iota(jnp.int32, (ncmp, tq), 0) * CMP_STRIDE + (CMP_BLOCK - 1)
    neg = jnp.where(cmp_end <= t, 0.0, -jnp.inf)
    heads = range(NSA_REP)
    st = [_dot_nt(kc, qs[r * tq:(r + 1) * tq]) + neg for r in heads]
    mx = [jnp.max(st[r], axis=0, keepdims=True) for r in heads]
    mx = [jnp.where(mx[r] == -jnp.inf, 0.0, mx[r]) for r in heads]
    e = [jnp.exp2(st[r] - mx[r]) for r in heads]
    p = [e[r] / jnp.maximum(jnp.sum(e[r], axis=0, keepdims=True), 1e-30) for r in heads]
    o_t = [_dot_tn(vc, p[r].astype(BF16)) for r in heads]
    gate = gate_ref[...]
    o_t = [o_t[r] * gate[3 * r:3 * r + 1, :] for r in heads]
    for j in range(NSA_REP // 2):
        pair = jnp.concatenate([o_t[2 * j], o_t[2 * j + 1]], axis=0).T
        oc_ref[:, 2 * j * dh:2 * (j + 1) * dh] = pair
    psum = sum(p)
    imp = _dot(ovt_ref[...], jnp.concatenate(_split3(psum), axis=0))
    nsel = imp.shape[0]
    blk = lax.broadcasted_iota(jnp.int32, (nsel, LANES), 0)
    blk_f = blk.astype(F32)
    for cb in range(tq // LANES):
        csl = slice(cb * LANES, (cb + 1) * LANES)
        tb = (qi * tq + cb * LANES + lax.broadcasted_iota(jnp.int32, (nsel, LANES), 1)) // SEL_BLOCK
        forced = (blk == 0) | (blk == tb) | (blk == tb - 1)
        vals = jnp.where(forced, -jnp.inf, jnp.where(blk <= tb, imp[:, csl], -1.0))
        sel = jnp.where(forced, 1.0, 0.0)
        for _ in range(SEL_TOPK - 3):
            top = jnp.max(vals, axis=0, keepdims=True)
            first = jnp.min(jnp.where(vals == top, blk_f, float(nsel)), axis=0, keepdims=True)
            pick = blk_f == first
            sel = jnp.where(pick, 1.0, sel)
            vals = jnp.where(pick, -jnp.inf, vals)
        selt_ref[:, csl] = sel.astype(selt_ref.dtype)


def _overlap_matrix_t3(ncmp_pad, nsel):
    c = np.arange(ncmp_pad)[None, :]
    s = np.arange(nsel)[:, None]
    cmp_start = c * CMP_STRIDE
    cmp_end = cmp_start + CMP_BLOCK - 1
    blk_start = s * SEL_BLOCK
    ov = ((cmp_end >= blk_start) & (cmp_start <= blk_start + SEL_BLOCK - 1)).astype(np.float32)
    return jnp.asarray(np.concatenate([ov, ov, ov], axis=1), dtype=BF16)


def _nsa_cmp(qp, kc, vc, gates, batch, seq_len, tq=512):
    m, d = qp.shape
    nq = seq_len // tq
    ncmp = seq_len // CMP_STRIDE
    nsel = seq_len // SEL_BLOCK
    ovt3 = _overlap_matrix_t3(ncmp, nsel)
    qmap = lambda b, g, i: (b * nq + i, g)
    kmap = lambda b, g, i: (g * batch + b, 0)
    return pl.pallas_call(
        functools.partial(_nsa_cmp_kernel, tq=tq),
        out_shape=(jax.ShapeDtypeStruct((m, d), F32),
                   jax.ShapeDtypeStruct((NSA_GROUPS, nsel, m), BF16)),
        grid=(batch, NSA_GROUPS, nq),
        in_specs=[pl.BlockSpec((tq, NSA_KV), qmap),
                  pl.BlockSpec((ncmp, NSA_HEAD_DIM), kmap), pl.BlockSpec((ncmp, NSA_HEAD_DIM), kmap),
                  pl.BlockSpec((None, NSA_GATE_ROWS, tq), lambda b, g, i: (g, 0, b * nq + i)),
                  _const_spec((nsel, 3 * ncmp))],
        out_specs=(pl.BlockSpec((tq, NSA_KV), qmap),
                   pl.BlockSpec((None, nsel, tq), lambda b, g, i: (g, 0, b * nq + i))),
        compiler_params=_cparams(("parallel", "parallel", "parallel")),
        name="nsa_cmp_topk",
    )(qp, kc, vc, gates, ovt3)


def _nsa_sel_kernel(q_ref, ks_ref, vs_ref, kw_ref, vw_ref, sel_ref, gate_ref, oc_ref, o_ref,
                    m_scr, acc_scr, ow_scr, sta_scr, stb_scr, *, tq, tk):
    qi = pl.program_id(2)
    dh = NSA_HEAD_DIM
    cols = NSA_REP * tq
    q = q_ref[...]
    qs = jnp.concatenate([q[:, r * dh:(r + 1) * dh] for r in range(NSA_REP)], axis=0)
    selt = sel_ref[...].astype(F32)
    nsel = selt.shape[0]
    bmask_t = jnp.where(selt > 0.5, 0.0, MASKED)
    if nsel < LANES:
        bmask_t = jnp.concatenate([bmask_t, jnp.zeros((LANES - nsel, tq), F32)], axis=0)
    bmask = bmask_t.T.astype(BF16)
    zpad = jnp.zeros((tq, LANES - dh), BF16)
    qa = jnp.concatenate([jnp.concatenate([q[:, r * dh:(r + 1) * dh], zpad, bmask], axis=1)
                          for r in range(NSA_REP)], axis=0)
    q0 = qi * tq
    key_iota = lax.broadcasted_iota(jnp.int32, (tk, tq), 0)
    t_pos = q0 + lax.broadcasted_iota(jnp.int32, (tk, tq), 1)

    m_scr[...] = jnp.full((1, cols), -jnp.inf, F32)
    acc_scr[...] = jnp.zeros((NSA_VT_ROWS, cols), F32)

    heads = range(NSA_REP)
    hsl = [slice(r * tq, (r + 1) * tq) for r in heads]

    def put_scores(scr, ki):
        k = ks_ref[pl.ds(pl.multiple_of(ki * tk, tk), tk), :]
        for r in heads:
            scr[r] = _dot_nt(k, qa[hsl[r]])

    def get_scores(scr):
        return [scr[r] for r in heads]

    def consume(ki, st, causal):
        k0 = pl.multiple_of(ki * tk, tk)
        vt = vs_ref[:, pl.ds(k0, tk)]
        if causal:
            neg = jnp.where(k0 + key_iota <= t_pos, 0.0, -jnp.inf)
            st = [s + neg for s in st]
        m_old = [m_scr[:, hsl[r]] for r in heads]
        m_new = [jnp.maximum(m_old[r], jnp.max(st[r], axis=0, keepdims=True)) for r in heads]
        alpha = [jnp.exp2(m_old[r] - m_new[r]) for r in heads]
        p = [jnp.exp2(st[r] - m_new[r]).astype(BF16) for r in heads]
        pv = [_dot(vt, p[r]) for r in heads]
        for r in heads:
            acc_scr[:, hsl[r]] = alpha[r] * acc_scr[:, hsl[r]] + pv[r]
            m_scr[:, hsl[r]] = m_new[r]

    def tile_pair(j, carry):
        even = get_scores(sta_scr)
        put_scores(stb_scr, 2 * j + 1)
        consume(2 * j, even, False)
        odd = get_scores(stb_scr)
        put_scores(sta_scr, 2 * j + 2)
        consume(2 * j + 1, odd, False)
        return carry

    last = (q0 + tq - 1) // tk
    put_scores(sta_scr, 0)

    gate = gate_ref[...]
    wq = min(tq, 256)
    wsub = wq + WINDOW
    nsub = tq // wq
    rel_iota = (lax.broadcasted_iota(jnp.int32, (wsub, wq), 0)
                - lax.broadcasted_iota(jnp.int32, (wsub, wq), 1))
    starts = [pl.multiple_of(jnp.maximum(q0 + wq * u - WINDOW, 0), wq) for u in range(nsub)]
    kw = [kw_ref[pl.ds(starts[u], wsub), :] for u in range(nsub)]
    vwt = [vw_ref[:, pl.ds(starts[u], wsub)] for u in range(nsub)]
    rel = [starts[u] - (q0 + wq * u) + rel_iota for u in range(nsub)]
    neg_w = [jnp.where((rel[u] <= 0) & (rel[u] > -WINDOW), 0.0, -jnp.inf) for u in range(nsub)]
    subs = [(r, u) for r in heads for u in range(nsub)]
    sw = [_dot_nt(kw[u], qs[r * tq + wq * u:r * tq + wq * (u + 1)]) + neg_w[u] for r, u in subs]
    mw = [jnp.max(s, axis=0, keepdims=True) for s in sw]
    mw = [jnp.where(m == -jnp.inf, 0.0, m) for m in mw]
    ew = [jnp.exp2(s - m).astype(BF16) for s, m in zip(sw, mw)]
    pvw = [_dot(vwt[u], e) for (r, u), e in zip(subs, ew)]
    pvw = [p[0:dh] / jnp.maximum(p[dh:dh + 1], 1e-30) for p in pvw]
    for r in heads:
        o_win = jnp.concatenate(pvw[r * nsub:(r + 1) * nsub], axis=1)
        ow_scr[:, hsl[r]] = o_win * gate[3 * r + 2:3 * r + 3, :]

    lax.fori_loop(0, last // 2, tile_pair, 0)

    @pl.when(last % 2 == 0)
    def _():
        consume(last, get_scores(sta_scr), True)

    @pl.when(last % 2 == 1)
    def _():
        even = get_scores(sta_scr)
        put_scores(stb_scr, last)
        consume(last - 1, even, False)
        consume(last, get_scores(stb_scr), True)
    o_sel = acc_scr[0:dh, :] / jnp.maximum(acc_scr[dh:dh + 1, :], 1e-30)

    mix = [o_sel[:, hsl[r]] * gate[3 * r + 1:3 * r + 2, :] + ow_scr[:, hsl[r]] for r in heads]
    for j in range(NSA_REP // 2):
        psl = slice(2 * j * dh, 2 * (j + 1) * dh)
        pair = jnp.concatenate([mix[2 * j], mix[2 * j + 1]], axis=0).T
        o_ref[:, psl] = (oc_ref[:, psl] + pair).astype(o_ref.dtype)


def _nsa_sel(qr, ks, vs, kw, vw, sel, gates, oc, batch, seq_len, tq=512, tk=512):
    m, d = qr.shape
    nq = seq_len // tq
    nsel = seq_len // SEL_BLOCK
    qmap = lambda b, g, i: (b * nq + i, g)
    kvmap = lambda b, g, i: (g, b, 0)
    kv_spec = pl.BlockSpec((None, seq_len, NSA_HEAD_DIM), kvmap)
    ks_spec = pl.BlockSpec((None, seq_len, 2 * LANES), kvmap)
    vt_spec = pl.BlockSpec((None, NSA_VT_ROWS, seq_len), lambda b, g, i: (g, 0, b))
    cols = NSA_REP * tq
    assert tk % tq == 0
    return pl.pallas_call(
        functools.partial(_nsa_sel_kernel, tq=tq, tk=tk),
        out_shape=jax.ShapeDtypeStruct((m, d), BF16),
        grid=(batch, NSA_GROUPS, nq),
        in_specs=[pl.BlockSpec((tq, NSA_KV), qmap), ks_spec, vt_spec, kv_spec, vt_spec,
                  pl.BlockSpec((None, nsel, tq), lambda b, g, i: (g, 0, b * nq + i)),
                  pl.BlockSpec((None, NSA_GATE_ROWS, tq), lambda b, g, i: (g, 0, b * nq + i)),
                  pl.BlockSpec((tq, NSA_KV), qmap)],
        out_specs=pl.BlockSpec((tq, NSA_KV), qmap),
        scratch_shapes=[pltpu.VMEM((1, cols), F32), pltpu.VMEM((NSA_VT_ROWS, cols), F32),
                        pltpu.VMEM((NSA_HEAD_DIM, cols), F32),
                        pltpu.VMEM((NSA_REP, tk, tq), F32), pltpu.VMEM((NSA_REP, tk, tq), F32)],
        compiler_params=_cparams(("parallel", "parallel", "parallel")),
        name="nsa_sel_win",
    )(qr, ks, vs, kw, vw, sel, gates, oc)


def _nsa_layer(x, norm_g, w_in, pe_k, pe_v, w1k, w2k, w1v, w2v, b_gate, rope, batch, seq_len):
    qp, qr, kc_raw, vc_raw, ks, vs, kw, vw, gates = _nsa_proj(x, norm_g, w_in, b_gate, rope, seq_len)
    kc, vc = _compress(kc_raw, vc_raw, pe_k, pe_v, w1k, w2k, w1v, w2v, seq_len)
    oc, sel = _nsa_cmp(qp, kc, vc, gates, batch, seq_len)
    return _nsa_sel(qr, ks, vs, kw, vw, sel, gates, oc, batch, seq_len)


def _mlstm_proj_kernel(x_ref, g_ref, w_ref, wot_ref, wt_ref, bcol_ref, brow_ref,
                       qk_ref, v_ref, ot_ref, gc_ref, gr_ref):
    hn = _rms(x_ref[...], g_ref[...]).astype(BF16)
    y = _dot(hn, w_ref[...])
    d = D_MODEL
    qk_ref[...] = y[:, :d]
    v_ref[...] = y[:, d:2 * d].astype(BF16)
    ot_ref[...] = _sigmoid(_dot_nt(wot_ref[...], hn))
    gc_ref[...] = y[:, 2 * d:] + bcol_ref[...]
    gr_ref[...] = _dot_nt(wt_ref[...], hn) + brow_ref[...]


def _mlstm_proj(x, g, w_in, b_gates, tm=512):
    m, d = x.shape
    h, dk = MLSTM_HEADS, MLSTM_QK_DIM
    wq = w_in[:, :h * dk].reshape(d, h, dk)
    wk = w_in[:, h * dk:2 * h * dk].reshape(d, h, dk)
    wqk = jnp.concatenate([wq, wk], axis=2).reshape(d, 2 * h * dk)
    wv = w_in[:, d:2 * d]
    wif = w_in[:, 2 * d:2 * d + 2 * h]
    wo = w_in[:, 2 * d + 2 * h:]
    w = jnp.concatenate([wqk, wv, jnp.pad(wif, ((0, 0), (0, LANES - 2 * h)))], axis=1).astype(BF16)
    wot = wo.T.astype(BF16)
    wt = wif.T.astype(BF16)
    bcol = jnp.pad(b_gates, (0, LANES - 2 * h)).reshape(1, LANES)
    brow = b_gates.reshape(2 * h, 1)
    n = w.shape[1]
    row = lambda i: (i, 0)
    col = lambda i: (0, i)
    return pl.pallas_call(
        _mlstm_proj_kernel,
        out_shape=(jax.ShapeDtypeStruct((m, d), F32), jax.ShapeDtypeStruct((m, d), BF16),
                   jax.ShapeDtypeStruct((d, m), F32), jax.ShapeDtypeStruct((m, LANES), F32),
                   jax.ShapeDtypeStruct((2 * h, m), F32)),
        grid=(m // tm,),
        in_specs=[pl.BlockSpec((tm, d), row), _const_spec((1, d)), _const_spec((d, n)), _const_spec((d, d)),
                  _const_spec((2 * h, d)), _const_spec((1, LANES)), _const_spec((2 * h, 1))],
        out_specs=(pl.BlockSpec((tm, d), row), pl.BlockSpec((tm, d), row), pl.BlockSpec((d, tm), col),
                   pl.BlockSpec((tm, LANES), row), pl.BlockSpec((2 * h, tm), col)),
        compiler_params=_cparams(("parallel",)),
        name="mlstm_proj",
    )(x, g.reshape(1, d), w, wot, wt, bcol, brow)


def _log_sigmoid(x):
    return jnp.minimum(x, 0.0) - jnp.log(1.0 + jnp.exp(-jnp.abs(x)))


def _mlstm_core_kernel(qk_ref, v_ref, og_ref, gc_ref, gr_ref, cw_ref, cb_ref, ng_ref, o_ref,
                       qkc_scr, ext_scr, c_scr, n_scr, m_scr, *, tt):
    L = MLSTM_TILE
    H, dk, dv = MLSTM_HEADS, MLSTM_QK_DIM, MLSTM_V_DIM
    seq_start = pl.program_id(1) == 0

    @pl.when(seq_start)
    def _():
        c_scr[...] = jnp.zeros_like(c_scr)
        n_scr[...] = jnp.zeros_like(n_scr)
        m_scr[...] = jnp.zeros_like(m_scr)
        ext_scr[0:SUBLANES, :] = jnp.zeros((SUBLANES, ext_scr.shape[1]), F32)

    ext_scr[SUBLANES:, :] = qk_ref[...]
    is_k = lax.broadcasted_iota(jnp.int32, (tt, 2 * dk), 1) >= dk
    for h in range(H):
        hsl = slice(h * 2 * dk, (h + 1) * 2 * dk)
        cw = cw_ref[:, hsl]
        acc = cb_ref[:, hsl] + cw[MLSTM_CONV - 1:MLSTM_CONV, :] * ext_scr[SUBLANES:, hsl]
        for s in range(1, MLSTM_CONV):
            acc = acc + cw[MLSTM_CONV - 1 - s:MLSTM_CONV - s, :] * ext_scr[SUBLANES - s:SUBLANES - s + tt, hsl]
        act = acc * _sigmoid(acc)
        qkc_scr[:, hsl] = jnp.where(is_k, act * dk ** -0.5, act)
    ext_scr[0:SUBLANES, :] = ext_scr[tt:tt + SUBLANES, :]

    src = lax.broadcasted_iota(jnp.int32, (L, L), 0)
    tgt = lax.broadcasted_iota(jnp.int32, (L, L), 1)
    causal = src <= tgt
    tri3 = jnp.concatenate([(src >= tgt).astype(BF16)] * 3, axis=1)
    triu3 = jnp.concatenate([causal.astype(BF16)] * 3, axis=0)
    n_pad = jnp.zeros((SUBLANES - 3, dk), BF16)

    def chunk(c, carry):
        r0 = pl.multiple_of(c * L, L)
        gcol = gc_ref[pl.ds(r0, L), :]
        grow = gr_ref[c]
        b_col = _cumsum_rows(tri3, _log_sigmoid(gcol))
        b_row = _cumsum_lanes(_log_sigmoid(grow), triu3)
        hs = range(H)
        vsl = [slice(h * dv, (h + 1) * dv) for h in hs]
        qk = [qkc_scr[pl.ds(r0, L), h * 2 * dk:(h + 1) * 2 * dk] for h in hs]
        q = [qk[h][:, :dk].astype(BF16) for h in hs]
        k = [qk[h][:, dk:] for h in hs]
        v = [v_ref[pl.ds(r0, L), vsl[h]] for h in hs]
        col = [gcol[:, h:h + 1] - b_col[:, H + h:H + h + 1] for h in hs]
        li_r = [grow[h:h + 1, :] for h in hs]
        b_r = [b_row[H + h:H + h + 1, :] for h in hs]
        b_end = [b_r[h][:, L - 1:L] for h in hs]
        dmat = [jnp.where(causal, b_r[h] + col[h], -jnp.inf) for h in hs]
        d_max = [jnp.max(dmat[h], axis=0, keepdims=True) for h in hs]
        att = [jnp.exp(dmat[h] - d_max[h]) * _dot_nt(k[h].astype(BF16), q[h]) for h in hs]
        a_sum = [jnp.sum(att[h], axis=0, keepdims=True) for h in hs]
        intra = [_dot_tn(v[h], att[h].astype(BF16)) for h in hs]
        g_max = [jnp.max(b_end[h] - b_r[h] + li_r[h], axis=-1, keepdims=True) for h in hs]
        kw = [k[h] * jnp.exp(b_end[h] + col[h] - g_max[h]) for h in hs]
        c_loc = [_dot_tn(v[h], kw[h].astype(BF16)) for h in hs]
        n_loc = [jnp.sum(kw[h], axis=0, keepdims=True) for h in hs]
        m_prev = [m_scr[h:h + 1, 0:1] for h in hs]
        c_prev = [c_scr[h] for h in hs]
        n_prev = [n_scr[h:h + 1, :] for h in hs]
        m_inter = [b_r[h] + m_prev[h] for h in hs]
        m_t = [jnp.maximum(m_inter[h], d_max[h]) for h in hs]
        w_loc = [jnp.exp(d_max[h] - m_t[h]) for h in hs]
        w_int = [jnp.exp(m_inter[h] - m_t[h]) for h in hs]
        q_c = [_dot_nt(c_prev[h].astype(BF16), q[h]) for h in hs]
        q_n = [_dot_nt(jnp.concatenate(list(_split3(n_prev[h])) + [n_pad], axis=0), q[h]) for h in hs]
        q_n = [q_n[h][0:1] + q_n[h][1:2] + q_n[h][2:3] for h in hs]
        num = [w_loc[h] * intra[h] + w_int[h] * q_c[h] for h in hs]
        den = [w_loc[h] * a_sum[h] + w_int[h] * q_n[h] for h in hs]
        h_t = [num[h] / jnp.maximum(jnp.abs(den[h]), jnp.exp(-m_t[h])) for h in hs]
        h_t = [h_t[h] * lax.rsqrt(jnp.mean(h_t[h] * h_t[h], axis=0, keepdims=True) + NORM_EPS) for h in hs]
        m_new = [jnp.maximum(b_end[h] + m_prev[h], g_max[h]) for h in hs]
        a = [jnp.exp(b_end[h] + m_prev[h] - m_new[h]) for h in hs]
        sc = [jnp.exp(g_max[h] - m_new[h]) for h in hs]
        for h in hs:
            out = h_t[h] * ng_ref[vsl[h], :] * og_ref[vsl[h], pl.ds(r0, L)]
            o_ref[pl.ds(r0, L), vsl[h]] = out.T.astype(o_ref.dtype)
            c_scr[h] = a[h] * c_prev[h] + sc[h] * c_loc[h]
            n_scr[h:h + 1, :] = a[h] * n_prev[h] + sc[h] * n_loc[h]
            m_scr[h:h + 1, :] = jnp.broadcast_to(m_new[h], (1, LANES))
        return carry

    lax.fori_loop(0, tt // L, chunk, 0)


def _mlstm_core(qk, v, ogt, gcol, grow3, conv_w, conv_b, norm_gb, batch, seq_len, tt=256):
    m, d = qk.shape
    H, dk, dv = MLSTM_HEADS, MLSTM_QK_DIM, MLSTM_V_DIM
    nt = seq_len // tt
    ncs = tt // MLSTM_TILE
    row = lambda b, i: (b * nt + i, 0)
    return pl.pallas_call(
        functools.partial(_mlstm_core_kernel, tt=tt),
        out_shape=jax.ShapeDtypeStruct((m, d), BF16),
        grid=(batch, nt),
        in_specs=[pl.BlockSpec((tt, d), row), pl.BlockSpec((tt, d), row),
                  pl.BlockSpec((d, tt), lambda b, i: (0, b * nt + i)),
                  pl.BlockSpec((tt, LANES), row),
                  pl.BlockSpec((ncs, 2 * H, MLSTM_TILE), lambda b, i: (b * nt + i, 0, 0)),
                  _const_spec((SUBLANES, d)), _const_spec((1, d)), _const_spec((d, LANES))],
        out_specs=pl.BlockSpec((tt, d), row),
        scratch_shapes=[pltpu.VMEM((tt, d), F32), pltpu.VMEM((tt + SUBLANES, d), F32),
                        pltpu.VMEM((H, dv, dk), F32), pltpu.VMEM((H, dk), F32),
                        pltpu.VMEM((H, LANES), F32)],
        compiler_params=_cparams(("arbitrary", "arbitrary")),
        name="mlstm_core",
    )(qk, v, ogt, gcol, grow3, conv_w, conv_b, norm_gb)


def _mlstm_layer(x, norm_g, w_in, conv_w, conv_b, b_gates, hnorm, batch, seq_len):
    H, dk = MLSTM_HEADS, MLSTM_QK_DIM
    qk, v, ogt, gcol, grow = _mlstm_proj(x, norm_g, w_in, b_gates)
    m = x.shape[0]
    grow3 = grow.reshape(2 * H, m // MLSTM_TILE, MLSTM_TILE).transpose(1, 0, 2)
    norm_gb = jnp.broadcast_to(hnorm.reshape(-1, 1), (hnorm.shape[0], LANES))

    def perm(z):
        lead = z.shape[:-1]
        zq = z[..., :H * dk].reshape(*lead, H, dk)
        zk = z[..., H * dk:].reshape(*lead, H, dk)
        return jnp.concatenate([zq, zk], axis=-1).reshape(*lead, 2 * H * dk)

    cw = jnp.pad(perm(conv_w), ((0, SUBLANES - MLSTM_CONV), (0, 0)))
    return _mlstm_core(qk, v, ogt, gcol, grow3, cw, perm(conv_b).reshape(1, -1), norm_gb,
                       batch, seq_len)


def _softplus(x):
    return jnp.maximum(x, 0.0) + jnp.log(1.0 + jnp.exp(-jnp.abs(x)))


def _rwkv_proj_kernel(x_ref, g_ref, mu_ref, wr_ref, wk_ref, wv_ref, ww1_ref, ww2_ref, w0_ref,
                      aw1_ref, aw2_ref, a0_ref, gw1_ref, gw2_ref,
                      r_ref, k_ref, v_ref, lw_ref, a_ref, go_ref, prev_scr, *, tiles_per_seq):
    tm = x_ref.shape[0]
    h = _rms(x_ref[...], g_ref[...])
    seq_start = (pl.program_id(0) % tiles_per_seq) == 0
    prev = jnp.where(seq_start, 0.0, prev_scr[SUBLANES - 1:SUBLANES, :])
    prev_scr[...] = h[tm - SUBLANES:, :]
    rows = lax.broadcasted_iota(jnp.int32, h.shape, 0)
    xx = jnp.where(rows == 0, prev, pltpu.roll(h, 1, 0)) - h
    mix = lambda j: (h + xx * mu_ref[j:j + 1, :]).astype(BF16)
    r_ref[...] = _dot(mix(0), wr_ref[...])
    k_ref[...] = _dot(mix(2), wk_ref[...])
    v_ref[...] = _dot(mix(3), wv_ref[...])
    wl = _dot(jnp.tanh(_dot(mix(1), ww1_ref[...])).astype(BF16), ww2_ref[...]) + w0_ref[...]
    w_log = -_softplus(-wl) - 0.5
    lw_ref[...] = -jnp.exp(w_log)
    al = _dot(_dot(mix(4), aw1_ref[...]).astype(BF16), aw2_ref[...]) + a0_ref[...]
    a_ref[...] = _sigmoid(al)
    go_ref[...] = _dot(_sigmoid(_dot(mix(5), gw1_ref[...])).astype(BF16), gw2_ref[...])


def _rwkv_proj(x, g, mu, w_r, w_k, w_v, w0, w_w1, w_w2, a0, a_w1, a_w2, g_w1, g_w2, seq_len, tm=512):
    m, d = x.shape
    row = lambda i: (i, 0)
    bf = lambda w: w.astype(BF16)
    mu8 = jnp.pad(mu, ((0, SUBLANES - mu.shape[0]), (0, 0)))
    consts = [g.reshape(1, d), mu8, bf(w_r), bf(w_k), bf(w_v), bf(w_w1), bf(w_w2), w0.reshape(1, d),
              bf(a_w1), bf(a_w2), a0.reshape(1, d), bf(g_w1), bf(g_w2)]
    out = jax.ShapeDtypeStruct((m, d), F32)
    return pl.pallas_call(
        functools.partial(_rwkv_proj_kernel, tiles_per_seq=seq_len // tm),
        out_shape=(out,) * 6,
        grid=(m // tm,),
        in_specs=[pl.BlockSpec((tm, d), row)] + [_const_spec(c.shape) for c in consts],
        out_specs=(pl.BlockSpec((tm, d), row),) * 6,
        scratch_shapes=[pltpu.VMEM((SUBLANES, d), F32)],
        compiler_params=_cparams(("arbitrary",)),
        name="rwkv_proj",
    )(x, *consts)


def _block_diag(x, lo):
    zero = jnp.zeros_like(x)
    return jnp.concatenate([jnp.where(lo, x, zero), jnp.where(lo, zero, x)], axis=0)


class _PairMat:
    def __init__(self, x, lo):
        self.x, self.lo = x, lo
        self._lhs = self._rhs = None

    def lhs(self):
        if self._lhs is None:
            hi, lo = _split2(self.x)
            self._lhs = jnp.concatenate([hi, lo, hi], axis=1)
        return self._lhs

    def rhs(self):
        if self._rhs is None:
            hi, lo = _split2(self.x)
            bh = _block_diag(hi, self.lo)
            self._rhs = jnp.concatenate([bh, bh, _block_diag(lo, self.lo)], axis=0)
        return self._rhs


def _pair_mm(p, q):
    return _dot(p.lhs(), q.rhs())


def _rwkv_core_kernel(r_ref, k_ref, v_ref, lw_ref, a_ref, go_ref, kk_ref, ka_ref, rk_ref,
                      lnw_ref, lnb_ref, o_ref, z_scr, *, tt):
    L, N = RWKV_CHUNK, RWKV_HEAD_DIM
    npair = z_scr.shape[0]
    pairs = range(npair)

    @pl.when(pl.program_id(1) == 0)
    def _():
        z_scr[...] = jnp.zeros_like(z_scr)

    ri = lax.broadcasted_iota(jnp.int32, (L, LANES), 0)
    ln = lax.broadcasted_iota(jnp.int32, (L, LANES), 1)
    si = ln % N
    lo = ln < N
    lower_incl = ri >= si
    lower_strict = ri > si
    blk_diag = (ri // RWKV_INV_BLOCK) == (si // RWKV_INV_BLOCK)
    eye = (ri == si).astype(F32)
    tri = (lax.broadcasted_iota(jnp.int32, (L, L), 0) >= lax.broadcasted_iota(jnp.int32, (L, L), 1))
    tri3 = jnp.concatenate([tri.astype(BF16)] * 3, axis=1)
    z_mask = ((lax.broadcasted_iota(jnp.int32, (2 * N, LANES), 0) // N)
              == (lax.broadcasted_iota(jnp.int32, (2 * N, LANES), 1) // N))

    def half_sum(x):
        s0 = jnp.sum(jnp.where(lo, x, 0.0), axis=-1, keepdims=True)
        s1 = jnp.sum(jnp.where(lo, 0.0, x), axis=-1, keepdims=True)
        return jnp.where(lo, s0, s1)

    bd = lambda x: _block_diag(x, lo)
    mk = lambda xs: [_PairMat(x, lo) for x in xs]
    mm = lambda ps, qs: [_pair_mm(p, q) for p, q in zip(ps, qs)]

    def chunk(c, carry):
        r0 = pl.multiple_of(c * L, L)
        sl = [slice(p * LANES, (p + 1) * LANES) for p in pairs]
        ld = lambda ref: [ref[pl.ds(r0, L), s] for s in sl]
        r, k, v, lw, a = ld(r_ref), ld(k_ref), ld(v_ref), ld(lw_ref), ld(a_ref)
        kk = [k[p] * kk_ref[:, sl[p]] for p in pairs]
        kk = [kk[p] / jnp.maximum(jnp.sqrt(half_sum(kk[p] * kk[p])), 1e-12) for p in pairs]
        km = [k[p] * (1.0 + (a[p] - 1.0) * ka_ref[:, sl[p]]) for p in pairs]
        bv = [kk[p] * a[p] for p in pairs]
        cum = [_cumsum_rows(tri3, lw[p]) for p in pairs]
        cum_end = [cum[p][L - 1:L, :] for p in pairs]
        w_inv = [jnp.exp(-cum[p]) for p in pairs]
        w_out = [jnp.exp(cum_end[p] - cum[p]) for p in pairs]
        kk_h = [(kk[p] * jnp.exp(cum[p] - lw[p])).astype(BF16) for p in pairs]
        r_h = [(r[p] * jnp.exp(cum[p])).astype(BF16) for p in pairs]
        b_t = [(bv[p] * w_inv[p]).astype(BF16) for p in pairs]
        k_t = [(km[p] * w_inv[p]).astype(BF16) for p in pairs]
        bbar = [(bv[p] * w_out[p]).astype(BF16) for p in pairs]
        kbar = [(km[p] * w_out[p]).astype(BF16) for p in pairs]
        vb = [v[p].astype(BF16) for p in pairs]
        lhs = [jnp.concatenate([kk_h[p], r_h[p]], axis=0) for p in pairs]
        ab = [_dot_nt(lhs[p], bd(b_t[p])) for p in pairs]
        ak = [_dot_nt(lhs[p], bd(k_t[p])) for p in pairs]
        a_ub = [jnp.where(lower_strict, ab[p][:L], 0.0) for p in pairs]
        a_rb = [jnp.where(lower_incl, ab[p][L:], 0.0).astype(BF16) for p in pairs]
        a_uk = [jnp.where(lower_strict, ak[p][:L], 0.0).astype(BF16) for p in pairs]
        a_rk = [jnp.where(lower_incl, ak[p][L:], 0.0).astype(BF16) for p in pairs]

        n1 = mk([jnp.where(blk_diag, -a_ub[p], 0.0) for p in pairs])
        n2 = mk(mm(n1, n1))
        n4 = mk(mm(n2, n2))
        n8 = mk(mm(n4, n4))
        acc = mk([eye + n1[p].x for p in pairs])
        for nk in (n2, n4):
            prod = mm(acc, nk)
            acc = mk([acc[p].x + prod[p] for p in pairs])
        prod = mm(acc, n8)
        d_inv = mk([acc[p].x + prod[p] for p in pairs])
        l_off = mk([jnp.where(blk_diag, 0.0, a_ub[p]) for p in pairs])
        e1 = mk([-x for x in mm(d_inv, l_off)])
        e2 = mk(mm(e1, e1))
        qm = mk([eye + e1[p].x for p in pairs])
        prod = mm(qm, e2)
        qm = mk([qm[p].x + prod[p] for p in pairs])
        t_inv = mk(mm(qm, d_inv))

        z = [z_scr[p] for p in pairs]
        zb = [z[p].astype(BF16) for p in pairs]
        bdv = [bd(vb[p]) for p in pairs]
        rhs_u = mk([_dot_nt(kk_h[p], zb[p]) + _dot(a_uk[p], bdv[p]) for p in pairs])
        u = [-x for x in mm(t_inv, rhs_u)]
        ub = [u[p].astype(BF16) for p in pairs]
        y = [_dot_nt(r_h[p], zb[p])
             + _dot(jnp.concatenate([a_rb[p], a_rk[p]], axis=1),
                    jnp.concatenate([bd(ub[p]), bdv[p]], axis=0)) for p in pairs]
        upd = [_dot_tn(jnp.concatenate([ub[p], vb[p]], axis=0),
                       jnp.concatenate([bbar[p], kbar[p]], axis=0)) for p in pairs]
        for p in pairs:
            z_scr[p] = z[p] * jnp.exp(cum_end[p]) + jnp.where(z_mask, upd[p], 0.0)

        inv_n = 1.0 / N
        mean = [half_sum(y[p]) * inv_n for p in pairs]
        yc = [y[p] - mean[p] for p in pairs]
        var = [half_sum(yc[p] * yc[p]) * inv_n for p in pairs]
        bonus = [half_sum(r[p] * km[p] * rk_ref[:, sl[p]]) * v[p] for p in pairs]
        for p in pairs:
            yn = yc[p] * lax.rsqrt(var[p] + RWKV_GN_EPS) * lnw_ref[:, sl[p]] + lnb_ref[:, sl[p]]
            o_ref[pl.ds(r0, L), sl[p]] = ((yn + bonus[p]) * go_ref[pl.ds(r0, L), sl[p]]).astype(o_ref.dtype)
        return carry

    lax.fori_loop(0, tt // L, chunk, 0)


def _rwkv_core(r, k, v, lw, a, go, k_k, k_a, r_k, ln_w, ln_b, batch, seq_len, tt=256):
    m, d = r.shape
    nt = seq_len // tt
    npair = d // LANES
    blk = pl.BlockSpec((tt, d), lambda b, i: (b * nt + i, 0))
    par = _const_spec((1, d))
    row1 = lambda z: z.reshape(1, d)
    return pl.pallas_call(
        functools.partial(_rwkv_core_kernel, tt=tt),
        out_shape=jax.ShapeDtypeStruct((m, d), BF16),
        grid=(batch, nt),
        in_specs=[blk] * 6 + [par] * 5,
        out_specs=blk,
        scratch_shapes=[pltpu.VMEM((npair, 2 * RWKV_HEAD_DIM, LANES), F32)],
        compiler_params=_cparams(("arbitrary", "arbitrary")),
        name="rwkv_core",
    )(r, k, v, lw, a, go, row1(k_k), row1(k_a), row1(r_k), row1(ln_w), row1(ln_b))


def _rwkv_layer(x, norm_g, mu, w_r, w_k, w_v, w0, w_w1, w_w2, a0, a_w1, a_w2, g_w1, g_w2,
                k_k, k_a, r_k, ln_w, ln_b, batch, seq_len):
    r, k, v, lw, a, go = _rwkv_proj(x, norm_g, mu, w_r, w_k, w_v, w0, w_w1, w_w2, a0, a_w1, a_w2,
                                    g_w1, g_w2, seq_len)
    return _rwkv_core(r, k, v, lw, a, go, k_k, k_a, r_k, ln_w, ln_b, batch, seq_len)


def kernel(x, norm_mixer, norm_ffn, ffn_w_up, ffn_conv_w, ffn_conv_b, ffn_w_down, nsa_w_in, nsa_pe_k, nsa_pe_v, nsa_cmp_k_w1, nsa_cmp_k_w2, nsa_cmp_v_w1, nsa_cmp_v_w2, nsa_b_gate, nsa_w_out, mlstm_w_in, mlstm_conv_w, mlstm_conv_b, mlstm_b_gates, mlstm_norm, mlstm_w_out, rwkv_mu, rwkv_w_r, rwkv_w_k, rwkv_w_v, rwkv_w_o, rwkv_w0, rwkv_w_w1, rwkv_w_w2, rwkv_a0, rwkv_a_w1, rwkv_a_w2, rwkv_g_w1, rwkv_g_w2, rwkv_k_k, rwkv_k_a, rwkv_r_k, rwkv_ln_w, rwkv_ln_b, final_norm):
    batch, seq_len, d = x.shape
    depth = norm_mixer.shape[0]
    rope = _rope_tables(seq_len)
    xf = x.reshape(batch * seq_len, d)
    for i in range(depth):
        kind, j = i % 3, i // 3
        if kind == 0:
            w_o = nsa_w_out[j]
            o = _nsa_layer(xf, norm_mixer[i], nsa_w_in[j], nsa_pe_k[j], nsa_pe_v[j], nsa_cmp_k_w1[j],
                           nsa_cmp_k_w2[j], nsa_cmp_v_w1[j], nsa_cmp_v_w2[j], nsa_b_gate[j], rope, batch, seq_len)
        elif kind == 1:
            w_o = mlstm_w_out[j]
            o = _mlstm_layer(xf, norm_mixer[i], mlstm_w_in[j], mlstm_conv_w[j], mlstm_conv_b[j],
                             mlstm_b_gates[j], mlstm_norm[j], batch, seq_len)
        else:
            w_o = rwkv_w_o[j]
            o = _rwkv_layer(xf, norm_mixer[i], rwkv_mu[j], rwkv_w_r[j], rwkv_w_k[j], rwkv_w_v[j],
                            rwkv_w0[j], rwkv_w_w1[j], rwkv_w_w2[j], rwkv_a0[j],
                            rwkv_a_w1[j], rwkv_a_w2[j], rwkv_g_w1[j], rwkv_g_w2[j], rwkv_k_k[j],
                            rwkv_k_a[j], rwkv_r_k[j], rwkv_ln_w[j], rwkv_ln_b[j], batch, seq_len)
        xf = _ffn(xf, o, w_o, norm_ffn[i], ffn_w_up[i], ffn_conv_w[i], ffn_conv_b[i], ffn_w_down[i],
                  seq_len, final_g=final_norm if i == depth - 1 else None)
    return xf.reshape(batch, seq_len, d)
```

```python
import functools
import math

import jax
import jax.numpy as jnp
import numpy as np
from jax import lax
from jax.experimental import pallas as pl
from jax.experimental.pallas import tpu as pltpu

F32 = jnp.float32
BF16 = jnp.bfloat16

D_MODEL = 1024
NORM_EPS = 1e-6
ROPE_THETA = 500000.0

NSA_HEAD_DIM = 64
NSA_HEADS = 16
NSA_GROUPS = 4
NSA_REP = NSA_HEADS // NSA_GROUPS
NSA_ROT_DIM = 16
CMP_BLOCK = 32
CMP_STRIDE = 16
CMP_HIDDEN = 256
SEL_BLOCK = 64
SEL_TOPK = 16
WINDOW = 512
NSA_KV = NSA_GROUPS * NSA_HEAD_DIM
BF16_SUBLANES = 16
NSA_VT_ROWS = NSA_HEAD_DIM + BF16_SUBLANES
NSA_GATE_ROWS = 16
NSA_WINDOW_QUERIES = 256

MLSTM_HEADS = 8
MLSTM_QK_DIM = 64
MLSTM_V_DIM = 128
MLSTM_TILE = 128
MLSTM_CONV = 4

RWKV_HEAD_DIM = 64
RWKV_GN_EPS = 64e-5
RWKV_CHUNK = 64
RWKV_INV_BLOCK = 16

FFN_DIM = 2816
FFN_CONV = 3
FFN_CHUNK = 256

LOG2E = math.log2(math.e)
MASKED = -1e30

LANES = 128
SUBLANES = 8
VMEM_LIMIT = 56 * 1024 * 1024


def _dot(a, b):
    return jnp.dot(a, b, preferred_element_type=F32)


def _dot_nt(a, b):
    return lax.dot_general(a, b, (((1,), (1,)), ((), ())), preferred_element_type=F32)


def _dot_tn(a, b):
    return lax.dot_general(a, b, (((0,), (0,)), ((), ())), preferred_element_type=F32)


def _split2(x):
    hi = x.astype(BF16)
    return hi, (x - hi.astype(F32)).astype(BF16)


def _split3(x):
    hi = x.astype(BF16)
    r1 = x - hi.astype(F32)
    mid = r1.astype(BF16)
    return hi, mid, (r1 - mid.astype(F32)).astype(BF16)


def _cumsum_rows(tri3, x):
    return _dot(tri3, jnp.concatenate(_split3(x), axis=0))


def _cumsum_lanes(x, triu3):
    return _dot(jnp.concatenate(_split3(x), axis=1), triu3)


def _rms(x, g):
    ms = jnp.mean(x * x, axis=-1, keepdims=True)
    return x * lax.rsqrt(ms + NORM_EPS) * g


def _sigmoid(x):
    return 1.0 / (1.0 + jnp.exp(-x))


def _cparams(sem):
    return pltpu.CompilerParams(dimension_semantics=sem, vmem_limit_bytes=VMEM_LIMIT)


def _const_spec(shape):
    n = len(shape)
    return pl.BlockSpec(shape, lambda *_: (0,) * n, pipeline_mode=pl.Buffered(1))


def _ffn_kernel(res_ref, a_ref, wo_ref, g_ref, wu_ref, cw_ref, cb_ref, wd_ref, fg_ref, o_ref,
                h_scr, carry_scr, ga_scr, va_scr, gb_scr, vb_scr, *, tiles_per_seq, n_chunks, final_norm):
    tm = res_ref.shape[0]
    fc = FFN_CHUNK
    x = res_ref[...] + _dot(a_ref[...], wo_ref[...])
    h_scr[...] = _rms(x, g_ref[...]).astype(BF16)
    o_ref[...] = x
    seq_start = (pl.program_id(0) % tiles_per_seq) == 0
    rows = lax.broadcasted_iota(jnp.int32, (tm, fc), 0)
    cols = lambda c, base=0: pl.ds(pl.multiple_of(base + c * fc, LANES), fc)

    def up(c, g_scr, v_scr):
        h = h_scr[...]
        g_scr[...] = _dot(h, wu_ref[:, cols(c)])
        v_scr[...] = _dot(h, wu_ref[:, cols(c, FFN_DIM)])

    def down(c, g_scr, v_scr):
        gate, val = g_scr[...], v_scr[...]
        prev = carry_scr[:, cols(c)]
        prev = jnp.where(seq_start, 0.0, prev)
        p1 = prev[SUBLANES - 1:SUBLANES, :]
        p2 = prev[SUBLANES - 2:SUBLANES - 1, :]
        carry_scr[:, cols(c)] = gate[tm - SUBLANES:, :]
        g1 = jnp.where(rows == 0, p1, pltpu.roll(gate, 1, 0))
        g2 = jnp.where(rows == 0, p2, jnp.where(rows == 1, p1, pltpu.roll(gate, 2, 0)))
        cw = cw_ref[:, cols(c)]
        y = cw[2:3, :] * gate + cw[1:2, :] * g1 + cw[0:1, :] * g2 + cb_ref[:, cols(c)]
        act = (y * _sigmoid(y) * val).astype(BF16)
        o_ref[...] += _dot(act, wd_ref[pl.ds(pl.multiple_of(c * fc, fc), fc), :])

    assert n_chunks % 2 == 1

    def chunk_pair(j, carry):
        up(2 * j + 1, gb_scr, vb_scr)
        down(2 * j, ga_scr, va_scr)
        up(2 * j + 2, ga_scr, va_scr)
        down(2 * j + 1, gb_scr, vb_scr)
        return carry

    up(0, ga_scr, va_scr)
    lax.fori_loop(0, n_chunks // 2, chunk_pair, 0)
    down(n_chunks - 1, ga_scr, va_scr)
    if final_norm:
        o_ref[...] = _rms(o_ref[...], fg_ref[...])


def _ffn(res, a, w_o, g, w_up, conv_w, conv_b, w_down, seq_len, final_g=None, tm=1024):
    m, d = res.shape
    nc = FFN_DIM // FFN_CHUNK
    cw = jnp.pad(conv_w, ((0, SUBLANES - FFN_CONV), (0, 0)))
    row = lambda i: (i, 0)
    fg = jnp.ones((d,), F32) if final_g is None else final_g
    kern = functools.partial(_ffn_kernel, tiles_per_seq=seq_len // tm, n_chunks=nc,
                             final_norm=final_g is not None)
    return pl.pallas_call(
        kern,
        out_shape=jax.ShapeDtypeStruct((m, d), F32),
        grid=(m // tm,),
        in_specs=[pl.BlockSpec((tm, d), row), pl.BlockSpec((tm, d), row), _const_spec((d, d)),
                  _const_spec((1, d)), _const_spec((d, 2 * FFN_DIM)),
                  _const_spec((SUBLANES, FFN_DIM)), _const_spec((1, FFN_DIM)),
                  _const_spec((FFN_DIM, d)), _const_spec((1, d))],
        out_specs=pl.BlockSpec((tm, d), row),
        scratch_shapes=[pltpu.VMEM((tm, d), BF16),
                        pltpu.VMEM((SUBLANES, FFN_DIM), F32)]
                       + [pltpu.VMEM((tm, FFN_CHUNK), F32)] * 4,
        compiler_params=_cparams(("arbitrary",)),
        name="conv_ffn",
    )(res, a, w_o.astype(BF16), g.reshape(1, d), w_up.astype(BF16), cw, conv_b.reshape(1, FFN_DIM),
      w_down.astype(BF16), fg.reshape(1, d))


def _nsa_proj_kernel(x_ref, g_ref, w_ref, wvt_ref, bg_ref, rc_ref, rs1_ref, rs2_ref,
                     qp_ref, qr_ref, kc_ref, vc_ref, ks_ref, vs_ref, kw_ref, vw_ref, gate_ref,
                     cmp_scr, *, tiles_per_seq):
    hn = _rms(x_ref[...], g_ref[...]).astype(BF16)
    y = _dot(hn, w_ref[...])
    vt = _dot_nt(wvt_ref[...], hn)
    rc, rs1, rs2 = rc_ref[...], rs1_ref[...], rs2_ref[...]
    dh = NSA_HEAD_DIM

    def rope(z):
        half = NSA_ROT_DIM // 2
        return z * rc + pltpu.roll(z, half, 1) * rs1 + pltpu.roll(z, LANES - half, 1) * rs2

    scale = dh ** -0.5 * LOG2E
    for j in range(D_MODEL // LANES):
        q = y[:, j * LANES:(j + 1) * LANES] * scale
        qp_ref[:, j * LANES:(j + 1) * LANES] = q.astype(BF16)
        qr_ref[:, j * LANES:(j + 1) * LANES] = rope(q).astype(BF16)

    def kv_chunk(idx):
        return y[:, D_MODEL + idx * NSA_KV:D_MODEL + (idx + 1) * NSA_KV]

    def split_groups(z, ref, dtype):
        for g in range(NSA_GROUPS):
            ref[g] = z[:, g * dh:(g + 1) * dh].astype(dtype)

    def rope256(z):
        return jnp.concatenate([rope(z[:, :LANES]), rope(z[:, LANES:])], axis=1)

    tm = y.shape[0]
    nrow = tm // CMP_STRIDE
    for j in range(2 * NSA_KV // LANES):
        cmp_scr[j] = y[:, D_MODEL + j * LANES:D_MODEL + (j + 1) * LANES]
    for j in range(2 * NSA_KV // LANES):
        ref = kc_ref if j < NSA_KV // LANES else vc_ref
        toks = [cmp_scr[j, pl.ds(tok, nrow, stride=CMP_STRIDE), :] for tok in range(CMP_STRIDE)]
        for half in range(LANES // dh):
            g = (j % (NSA_KV // LANES)) * (LANES // dh) + half
            ref[g] = jnp.concatenate([t[:, half * dh:(half + 1) * dh] for t in toks], axis=1)
    t_pos = (pl.program_id(0) % tiles_per_seq) * tm + lax.broadcasted_iota(jnp.int32, (tm, LANES), 0)
    onehot = (lax.broadcasted_iota(jnp.int32, (tm, LANES), 1) == t_pos // SEL_BLOCK).astype(F32)
    ksel = rope256(kv_chunk(2))
    zpad = jnp.zeros((tm, LANES - dh), F32)
    for g in range(NSA_GROUPS):
        ks_ref[g] = jnp.concatenate([ksel[:, g * dh:(g + 1) * dh], zpad, onehot], axis=1).astype(BF16)
    split_groups(rope256(kv_chunk(3)), kw_ref, BF16)
    ones_pad = (lax.broadcasted_iota(jnp.int32, (NSA_VT_ROWS - dh, tm), 0) == 0).astype(F32)
    for g in range(NSA_GROUPS):
        vs_ref[g] = jnp.concatenate([vt[g * dh:(g + 1) * dh], ones_pad], axis=0).astype(BF16)
        vw_ref[g] = jnp.concatenate([vt[NSA_KV + g * dh:NSA_KV + (g + 1) * dh], ones_pad], axis=0).astype(BF16)
    gate = _sigmoid(vt[2 * NSA_KV:] + bg_ref[...])
    for g in range(NSA_GROUPS):
        gate_ref[g] = gate[g * NSA_GATE_ROWS:(g + 1) * NSA_GATE_ROWS]


def _rope_tables(seq_len):
    half = NSA_ROT_DIM // 2
    inv_freq = ROPE_THETA ** (-jnp.arange(half, dtype=F32) / half)
    ang = jnp.arange(seq_len, dtype=F32)[:, None] * inv_freq[None, :]
    cos, sin = jnp.cos(ang), jnp.sin(ang)
    zeros = jnp.zeros((seq_len, NSA_HEAD_DIM - NSA_ROT_DIM), F32)
    z8 = jnp.zeros((seq_len, half), F32)
    rc = jnp.concatenate([cos, cos, zeros + 1.0], axis=1)
    rs1 = jnp.concatenate([z8, sin, zeros], axis=1)
    rs2 = jnp.concatenate([-sin, z8, zeros], axis=1)
    two = lambda t: jnp.concatenate([t, t], axis=1)
    return two(rc), two(rs1), two(rs2)


def _nsa_proj(x, g, w_in, b_gate, rope, seq_len, tm=512):
    m, d = x.shape
    n_kv = 6 * NSA_KV
    kv = lambda idx: w_in[:, D_MODEL + idx * NSA_KV:D_MODEL + (idx + 1) * NSA_KV]
    pad_g = NSA_GATE_ROWS - NSA_REP * 3
    wg = w_in[:, D_MODEL + n_kv:].reshape(d, NSA_GROUPS, NSA_REP * 3)
    wg = jnp.pad(wg, ((0, 0), (0, 0), (0, pad_g))).reshape(d, NSA_GROUPS * NSA_GATE_ROWS)
    w = jnp.concatenate([w_in[:, :D_MODEL], kv(0), kv(1), kv(2), kv(4)], axis=1).astype(BF16)
    wvt = jnp.concatenate([kv(3), kv(5), wg], axis=1).T.astype(BF16)
    bg = jnp.pad(b_gate.reshape(NSA_GROUPS, NSA_REP * 3), ((0, 0), (0, pad_g)))
    bg = bg.reshape(NSA_GROUPS * NSA_GATE_ROWS, 1)
    n = w.shape[1]
    tps = seq_len // tm
    row = lambda i: (i, 0)
    rope_spec = pl.BlockSpec((tm, LANES), lambda i: (i % tps, 0))
    assert seq_len // SEL_BLOCK <= LANES
    g_out = lambda dt, w=NSA_HEAD_DIM: jax.ShapeDtypeStruct((NSA_GROUPS, m, w), dt)
    g_spec = pl.BlockSpec((NSA_GROUPS, tm, NSA_HEAD_DIM), lambda i: (0, i, 0))
    ks_spec = pl.BlockSpec((NSA_GROUPS, tm, 2 * LANES), lambda i: (0, i, 0))
    cmp_out = jax.ShapeDtypeStruct((NSA_GROUPS, m // CMP_STRIDE, CMP_STRIDE * NSA_HEAD_DIM), F32)
    cmp_spec = pl.BlockSpec((NSA_GROUPS, tm // CMP_STRIDE, CMP_STRIDE * NSA_HEAD_DIM), lambda i: (0, i, 0))
    vt_out = jax.ShapeDtypeStruct((NSA_GROUPS, NSA_VT_ROWS, m), BF16)
    vt_spec = pl.BlockSpec((NSA_GROUPS, NSA_VT_ROWS, tm), lambda i: (0, 0, i))
    return pl.pallas_call(
        functools.partial(_nsa_proj_kernel, tiles_per_seq=tps),
        out_shape=(jax.ShapeDtypeStruct((m, d), BF16), jax.ShapeDtypeStruct((m, d), BF16),
                   cmp_out, cmp_out, g_out(BF16, 2 * LANES), vt_out, g_out(BF16), vt_out,
                   jax.ShapeDtypeStruct((NSA_GROUPS, NSA_GATE_ROWS, m), F32)),
        grid=(m // tm,),
        in_specs=[pl.BlockSpec((tm, d), row), _const_spec((1, d)), _const_spec((d, n)),
                  _const_spec((2 * NSA_KV + NSA_GROUPS * NSA_GATE_ROWS, d)),
                  _const_spec((NSA_GROUPS * NSA_GATE_ROWS, 1)), rope_spec, rope_spec, rope_spec],
        out_specs=(pl.BlockSpec((tm, d), row), pl.BlockSpec((tm, d), row),
                   cmp_spec, cmp_spec, ks_spec, vt_spec, g_spec, vt_spec,
                   pl.BlockSpec((NSA_GROUPS, NSA_GATE_ROWS, tm), lambda i: (0, 0, i))),
        scratch_shapes=[pltpu.VMEM((2 * NSA_KV // LANES, tm, LANES), F32)],
        compiler_params=_cparams(("parallel",)),
        name="nsa_proj",
    )(x, g.reshape(1, d), w, wvt, bg, *rope)


def _gelu_tanh(x):
    return 0.5 * x * (1.0 + jnp.tanh(math.sqrt(2.0 / math.pi) * (x + 0.044715 * (x * x * x))))


def _compress_kernel(zk_ref, zv_ref, pek_ref, pev_ref, w1k_ref, w2k_ref, w1v_ref, w2v_ref,
                     kc_ref, vc_ref):
    nrow = zk_ref.shape[0]
    rows = lax.broadcasted_iota(jnp.int32, (nrow, NSA_HEAD_DIM), 0)

    def one(z_ref, pe_ref, w1_ref, w2_ref, o_ref):
        z = z_ref[...]
        a = _dot((z + pe_ref[0:1, :]).astype(BF16), w1_ref[0])
        b = _dot((z + pe_ref[1:2, :]).astype(BF16), w1_ref[1])
        hid = a + pltpu.roll(b, nrow - 1, 0)
        out = _dot(_gelu_tanh(hid).astype(BF16), w2_ref[...])
        o_ref[...] = jnp.where(rows == nrow - 1, 0.0, out).astype(o_ref.dtype)

    one(zk_ref, pek_ref, w1k_ref, w2k_ref, kc_ref)
    one(zv_ref, pev_ref, w1v_ref, w2v_ref, vc_ref)


def _compress(kc_raw, vc_raw, pe_k, pe_v, w1k, w2k, w1v, w2v, seq_len):
    g, nrows, half = kc_raw.shape
    dh = half // CMP_STRIDE
    nchunk = seq_len // CMP_STRIDE
    zk = kc_raw.reshape(g * nrows, half)
    zv = vc_raw.reshape(g * nrows, half)
    pe2 = lambda pe: pe.reshape(2, half)
    w1 = lambda w: w.astype(BF16).reshape(2, half, CMP_HIDDEN)
    nblk = zk.shape[0] // nchunk
    row = lambda i: (i, 0)
    return pl.pallas_call(
        _compress_kernel,
        out_shape=(jax.ShapeDtypeStruct((zk.shape[0], dh), BF16),
                   jax.ShapeDtypeStruct((zk.shape[0], dh), BF16)),
        grid=(nblk,),
        in_specs=[pl.BlockSpec((nchunk, half), row), pl.BlockSpec((nchunk, half), row),
                  _const_spec((2, half)), _const_spec((2, half)),
                  _const_spec((2, half, CMP_HIDDEN)), _const_spec((CMP_HIDDEN, dh)),
                  _const_spec((2, half, CMP_HIDDEN)), _const_spec((CMP_HIDDEN, dh))],
        out_specs=(pl.BlockSpec((nchunk, dh), row), pl.BlockSpec((nchunk, dh), row)),
        compiler_params=_cparams(("parallel",)),
        name="nsa_compress",
    )(zk, zv, pe2(pe_k), pe2(pe_v), w1(w1k), w2k.astype(BF16), w1(w1v), w2v.astype(BF16))


def _nsa_cmp_kernel(q_ref, kc_ref, vc_ref, gate_ref, ovt_ref, oc_ref, selt_ref, imp_scr, *, tq, n_classes):
    qi = pl.program_id(2)
    nq = pl.num_programs(2)
    dh = NSA_HEAD_DIM
    ncmp = kc_ref.shape[0]
    nsel = imp_scr.shape[0]
    heads = range(NSA_REP)

    def attend(nrows):
        q = q_ref[...]
        qs = jnp.concatenate([q[:, r * dh:(r + 1) * dh] for r in heads], axis=0)
        kc, vc = kc_ref[0:nrows, :], vc_ref[0:nrows, :]
        t = qi * tq + lax.broadcasted_iota(jnp.int32, (nrows, tq), 1)
        cmp_end = lax.broadcasted_iota(jnp.int32, (nrows, tq), 0) * CMP_STRIDE + (CMP_BLOCK - 1)
        neg = jnp.where(cmp_end <= t, 0.0, -jnp.inf)
        st = [_dot_nt(kc, qs[r * tq:(r + 1) * tq]) + neg for r in heads]
        mx = [jnp.max(st[r], axis=0, keepdims=True) for r in heads]
        mx = [jnp.where(mx[r] == -jnp.inf, 0.0, mx[r]) for r in heads]
        e = [jnp.exp2(st[r] - mx[r]) for r in heads]
        p = [e[r] / jnp.maximum(jnp.sum(e[r], axis=0, keepdims=True), 1e-30) for r in heads]
        o_t = [_dot_tn(vc, p[r].astype(BF16)) for r in heads]
        gate = gate_ref[...]
        o_t = [o_t[r] * gate[3 * r:3 * r + 1, :] for r in heads]
        for j in range(NSA_REP // 2):
            pair = jnp.concatenate([o_t[2 * j], o_t[2 * j + 1]], axis=0).T
            oc_ref[:, 2 * j * dh:2 * (j + 1) * dh] = pair
        psum = sum(p)
        ovt = jnp.concatenate([ovt_ref[:, j * ncmp:j * ncmp + nrows] for j in range(3)], axis=1)
        imp_scr[...] = _dot(ovt, jnp.concatenate(_split3(psum), axis=0))

    for cls in range(n_classes):
        @pl.when((qi * n_classes) // nq == cls)
        def _(cls=cls):
            attend((cls + 1) * ncmp // n_classes)

    imp = imp_scr[...]
    blk = lax.broadcasted_iota(jnp.int32, (nsel, LANES), 0)
    blk_f = blk.astype(F32)
    for cb in range(tq // LANES):
        csl = slice(cb * LANES, (cb + 1) * LANES)
        tb = (qi * tq + cb * LANES + lax.broadcasted_iota(jnp.int32, (nsel, LANES), 1)) // SEL_BLOCK
        forced = (blk == 0) | (blk == tb) | (blk == tb - 1)
        vals = jnp.where(forced, -jnp.inf, jnp.where(blk <= tb, imp[:, csl], -1.0))
        for _ in range(SEL_TOPK - 3):
            top = jnp.max(vals, axis=0, keepdims=True)
            first = jnp.min(jnp.where(vals == top, blk_f, float(nsel)), axis=0, keepdims=True)
            vals = jnp.where(blk_f == first, -jnp.inf, vals)
        selt_ref[:, csl] = jnp.where(vals == -jnp.inf, 1.0, 0.0).astype(selt_ref.dtype)


def _overlap_matrix_t3(ncmp_pad, nsel):
    c = np.arange(ncmp_pad)[None, :]
    s = np.arange(nsel)[:, None]
    cmp_start = c * CMP_STRIDE
    cmp_end = cmp_start + CMP_BLOCK - 1
    blk_start = s * SEL_BLOCK
    ov = ((cmp_end >= blk_start) & (cmp_start <= blk_start + SEL_BLOCK - 1)).astype(np.float32)
    return jnp.asarray(np.concatenate([ov, ov, ov], axis=1), dtype=BF16)


def _nsa_cmp(qp, kc, vc, gates, batch, seq_len, tq=512):
    m, d = qp.shape
    nq = seq_len // tq
    ncmp = seq_len // CMP_STRIDE
    nsel = seq_len // SEL_BLOCK
    ovt3 = _overlap_matrix_t3(ncmp, nsel)
    qmap = lambda b, g, i: (b * nq + i, g)
    kmap = lambda b, g, i: (g * batch + b, 0)
    n_classes = math.gcd(nq, 4)
    return pl.pallas_call(
        functools.partial(_nsa_cmp_kernel, tq=tq, n_classes=n_classes),
        out_shape=(jax.ShapeDtypeStruct((m, d), F32),
                   jax.ShapeDtypeStruct((NSA_GROUPS, nsel, m), BF16)),
        grid=(batch, NSA_GROUPS, nq),
        in_specs=[pl.BlockSpec((tq, NSA_KV), qmap),
                  pl.BlockSpec((ncmp, NSA_HEAD_DIM), kmap), pl.BlockSpec((ncmp, NSA_HEAD_DIM), kmap),
                  pl.BlockSpec((None, NSA_GATE_ROWS, tq), lambda b, g, i: (g, 0, b * nq + i)),
                  _const_spec((nsel, 3 * ncmp))],
        out_specs=(pl.BlockSpec((tq, NSA_KV), qmap),
                   pl.BlockSpec((None, nsel, tq), lambda b, g, i: (g, 0, b * nq + i))),
        scratch_shapes=[pltpu.VMEM((nsel, tq), F32)],
        compiler_params=_cparams(("parallel", "parallel", "parallel")),
        name="nsa_cmp_topk",
    )(qp, kc, vc, gates, ovt3)


def _nsa_sel_kernel(q_ref, ks_ref, vs_ref, kw_ref, vw_ref, sel_ref, gate_ref, oc_ref, o_ref,
                    m_scr, acc_scr, ow_scr, sta_scr, stb_scr, *, tq, tk):
    qi = pl.program_id(2)
    dh = NSA_HEAD_DIM
    cols = NSA_REP * tq
    q = q_ref[...]
    qs = jnp.concatenate([q[:, r * dh:(r + 1) * dh] for r in range(NSA_REP)], axis=0)
    selt = sel_ref[...].astype(F32)
    nsel = selt.shape[0]
    bmask_t = jnp.where(selt > 0.5, 0.0, MASKED)
    if nsel < LANES:
        bmask_t = jnp.concatenate([bmask_t, jnp.zeros((LANES - nsel, tq), F32)], axis=0)
    bmask = bmask_t.T.astype(BF16)
    zpad = jnp.zeros((tq, LANES - dh), BF16)
    qa = jnp.concatenate([jnp.concatenate([q[:, r * dh:(r + 1) * dh], zpad, bmask], axis=1)
                          for r in range(NSA_REP)], axis=0)
    q0 = qi * tq
    key_iota = lax.broadcasted_iota(jnp.int32, (tk, tq), 0)
    t_pos = q0 + lax.broadcasted_iota(jnp.int32, (tk, tq), 1)

    m_scr[...] = jnp.full((1, cols), -jnp.inf, F32)
    acc_scr[...] = jnp.zeros((NSA_VT_ROWS, cols), F32)

    heads = range(NSA_REP)
    hsl = [slice(r * tq, (r + 1) * tq) for r in heads]

    def put_scores(scr, ki):
        k = ks_ref[pl.ds(pl.multiple_of(ki * tk, tk), tk), :]
        for r in heads:
            scr[r] = _dot_nt(k, qa[hsl[r]])

    def get_scores(scr):
        return [scr[r] for r in heads]

    def consume(ki, st, causal):
        k0 = pl.multiple_of(ki * tk, tk)
        vt = vs_ref[:, pl.ds(k0, tk)]
        if causal:
            neg = jnp.where(k0 + key_iota <= t_pos, 0.0, -jnp.inf)
            st = [s + neg for s in st]
        m_old = [m_scr[:, hsl[r]] for r in heads]
        m_new = [jnp.maximum(m_old[r], jnp.max(st[r], axis=0, keepdims=True)) for r in heads]
        alpha = [jnp.exp2(m_old[r] - m_new[r]) for r in heads]
        p = [jnp.exp2(st[r] - m_new[r]).astype(BF16) for r in heads]
        pv = [_dot(vt, p[r]) for r in heads]
        for r in heads:
            acc_scr[:, hsl[r]] = alpha[r] * acc_scr[:, hsl[r]] + pv[r]
            m_scr[:, hsl[r]] = m_new[r]

    def tile_pair(j, carry):
        even = get_scores(sta_scr)
        put_scores(stb_scr, 2 * j + 1)
        consume(2 * j, even, False)
        odd = get_scores(stb_scr)
        put_scores(sta_scr, 2 * j + 2)
        consume(2 * j + 1, odd, False)
        return carry

    last = (q0 + tq - 1) // tk
    put_scores(sta_scr, 0)

    gate = gate_ref[...]
    wq = min(tq, NSA_WINDOW_QUERIES)
    wsub = wq + WINDOW
    nsub = tq // wq
    rel_iota = (lax.broadcasted_iota(jnp.int32, (wsub, wq), 0)
                - lax.broadcasted_iota(jnp.int32, (wsub, wq), 1))
    starts = [pl.multiple_of(jnp.maximum(q0 + wq * u - WINDOW, 0), wq) for u in range(nsub)]
    kw = [kw_ref[pl.ds(starts[u], wsub), :] for u in range(nsub)]
    vwt = [vw_ref[:, pl.ds(starts[u], wsub)] for u in range(nsub)]
    rel = [starts[u] - (q0 + wq * u) + rel_iota for u in range(nsub)]
    neg_w = [jnp.where((rel[u] <= 0) & (rel[u] > -WINDOW), 0.0, -jnp.inf) for u in range(nsub)]
    subs = [(r, u) for r in heads for u in range(nsub)]
    sw = [_dot_nt(kw[u], qs[r * tq + wq * u:r * tq + wq * (u + 1)]) + neg_w[u] for r, u in subs]
    mw = [jnp.max(s, axis=0, keepdims=True) for s in sw]
    mw = [jnp.where(m == -jnp.inf, 0.0, m) for m in mw]
    ew = [jnp.exp2(s - m).astype(BF16) for s, m in zip(sw, mw)]
    pvw = [_dot(vwt[u], e) for (r, u), e in zip(subs, ew)]
    pvw = [p[0:dh] / jnp.maximum(p[dh:dh + 1], 1e-30) for p in pvw]
    for r in heads:
        o_win = jnp.concatenate(pvw[r * nsub:(r + 1) * nsub], axis=1)
        ow_scr[:, hsl[r]] = o_win * gate[3 * r + 2:3 * r + 3, :]

    lax.fori_loop(0, last // 2, tile_pair, 0)

    @pl.when(last % 2 == 0)
    def _():
        consume(last, get_scores(sta_scr), True)

    @pl.when(last % 2 == 1)
    def _():
        even = get_scores(sta_scr)
        put_scores(stb_scr, last)
        consume(last - 1, even, False)
        consume(last, get_scores(stb_scr), True)
    o_sel = acc_scr[0:dh, :] / jnp.maximum(acc_scr[dh:dh + 1, :], 1e-30)

    mix = [o_sel[:, hsl[r]] * gate[3 * r + 1:3 * r + 2, :] + ow_scr[:, hsl[r]] for r in heads]
    for j in range(NSA_REP // 2):
        psl = slice(2 * j * dh, 2 * (j + 1) * dh)
        pair = jnp.concatenate([mix[2 * j], mix[2 * j + 1]], axis=0).T
        o_ref[:, psl] = (oc_ref[:, psl] + pair).astype(o_ref.dtype)


def _nsa_sel(qr, ks, vs, kw, vw, sel, gates, oc, batch, seq_len, tq=512, tk=512):
    m, d = qr.shape
    nq = seq_len // tq
    nsel = seq_len // SEL_BLOCK
    qmap = lambda b, g, i: (b * nq + i, g)
    kvmap = lambda b, g, i: (g, b, 0)
    kv_spec = pl.BlockSpec((None, seq_len, NSA_HEAD_DIM), kvmap)
    ks_spec = pl.BlockSpec((None, seq_len, 2 * LANES), kvmap)
    vt_spec = pl.BlockSpec((None, NSA_VT_ROWS, seq_len), lambda b, g, i: (g, 0, b))
    cols = NSA_REP * tq
    assert tk % tq == 0
    return pl.pallas_call(
        functools.partial(_nsa_sel_kernel, tq=tq, tk=tk),
        out_shape=jax.ShapeDtypeStruct((m, d), BF16),
        grid=(batch, NSA_GROUPS, nq),
        in_specs=[pl.BlockSpec((tq, NSA_KV), qmap), ks_spec, vt_spec, kv_spec, vt_spec,
                  pl.BlockSpec((None, nsel, tq), lambda b, g, i: (g, 0, b * nq + i)),
                  pl.BlockSpec((None, NSA_GATE_ROWS, tq), lambda b, g, i: (g, 0, b * nq + i)),
                  pl.BlockSpec((tq, NSA_KV), qmap)],
        out_specs=pl.BlockSpec((tq, NSA_KV), qmap),
        scratch_shapes=[pltpu.VMEM((1, cols), F32), pltpu.VMEM((NSA_VT_ROWS, cols), F32),
                        pltpu.VMEM((NSA_HEAD_DIM, cols), F32),
                        pltpu.VMEM((NSA_REP, tk, tq), F32), pltpu.VMEM((NSA_REP, tk, tq), F32)],
        compiler_params=_cparams(("parallel", "parallel", "parallel")),
        name="nsa_sel_win",
    )(qr, ks, vs, kw, vw, sel, gates, oc)


def _nsa_layer(x, norm_g, w_in, pe_k, pe_v, w1k, w2k, w1v, w2v, b_gate, rope, batch, seq_len):
    qp, qr, kc_raw, vc_raw, ks, vs, kw, vw, gates = _nsa_proj(x, norm_g, w_in, b_gate, rope, seq_len)
    kc, vc = _compress(kc_raw, vc_raw, pe_k, pe_v, w1k, w2k, w1v, w2v, seq_len)
    oc, sel = _nsa_cmp(qp, kc, vc, gates, batch, seq_len)
    return _nsa_sel(qr, ks, vs, kw, vw, sel, gates, oc, batch, seq_len)


def _mlstm_proj_kernel(x_ref, g_ref, w_ref, wot_ref, wt_ref, bcol_ref, brow_ref,
                       qk_ref, v_ref, ot_ref, gc_ref, gr_ref):
    hn = _rms(x_ref[...], g_ref[...]).astype(BF16)
    y = _dot(hn, w_ref[...])
    d = D_MODEL
    qk_ref[...] = y[:, :d]
    v_ref[...] = y[:, d:2 * d].astype(BF16)
    ot_ref[...] = _sigmoid(_dot_nt(wot_ref[...], hn))
    gc_ref[...] = y[:, 2 * d:] + bcol_ref[...]
    gr_ref[...] = _dot_nt(wt_ref[...], hn) + brow_ref[...]


def _mlstm_proj(x, g, w_in, b_gates, tm=512):
    m, d = x.shape
    h, dk = MLSTM_HEADS, MLSTM_QK_DIM
    wq = w_in[:, :h * dk].reshape(d, h, dk)
    wk = w_in[:, h * dk:2 * h * dk].reshape(d, h, dk)
    wqk = jnp.concatenate([wq, wk], axis=2).reshape(d, 2 * h * dk)
    wv = w_in[:, d:2 * d]
    wif = w_in[:, 2 * d:2 * d + 2 * h]
    wo = w_in[:, 2 * d + 2 * h:]
    w = jnp.concatenate([wqk, wv, jnp.pad(wif, ((0, 0), (0, LANES - 2 * h)))], axis=1).astype(BF16)
    wot = wo.T.astype(BF16)
    wt = wif.T.astype(BF16)
    bcol = jnp.pad(b_gates, (0, LANES - 2 * h)).reshape(1, LANES)
    brow = b_gates.reshape(2 * h, 1)
    n = w.shape[1]
    row = lambda i: (i, 0)
    col = lambda i: (0, i)
    return pl.pallas_call(
        _mlstm_proj_kernel,
        out_shape=(jax.ShapeDtypeStruct((m, d), F32), jax.ShapeDtypeStruct((m, d), BF16),
                   jax.ShapeDtypeStruct((d, m), F32), jax.ShapeDtypeStruct((m, LANES), F32),
                   jax.ShapeDtypeStruct((2 * h, m), F32)),
        grid=(m // tm,),
        in_specs=[pl.BlockSpec((tm, d), row), _const_spec((1, d)), _const_spec((d, n)), _const_spec((d, d)),
                  _const_spec((2 * h, d)), _const_spec((1, LANES)), _const_spec((2 * h, 1))],
        out_specs=(pl.BlockSpec((tm, d), row), pl.BlockSpec((tm, d), row), pl.BlockSpec((d, tm), col),
                   pl.BlockSpec((tm, LANES), row), pl.BlockSpec((2 * h, tm), col)),
        compiler_params=_cparams(("parallel",)),
        name="mlstm_proj",
    )(x, g.reshape(1, d), w, wot, wt, bcol, brow)


def _log_sigmoid(x):
    return jnp.minimum(x, 0.0) - jnp.log(1.0 + jnp.exp(-jnp.abs(x)))


def _mlstm_core_kernel(qk_ref, v_ref, og_ref, gc_ref, gr_ref, cw_ref, cb_ref, ng_ref, o_ref,
                       qkc_scr, ext_scr, c_scr, n_scr, m_scr, *, tt):
    L = MLSTM_TILE
    H, dk, dv = MLSTM_HEADS, MLSTM_QK_DIM, MLSTM_V_DIM
    seq_start = pl.program_id(1) == 0

    @pl.when(seq_start)
    def _():
        c_scr[...] = jnp.zeros_like(c_scr)
        n_scr[...] = jnp.zeros_like(n_scr)
        m_scr[...] = jnp.zeros_like(m_scr)
        ext_scr[0:SUBLANES, :] = jnp.zeros((SUBLANES, ext_scr.shape[1]), F32)

    ext_scr[SUBLANES:, :] = qk_ref[...]
    is_k = lax.broadcasted_iota(jnp.int32, (tt, 2 * dk), 1) >= dk
    for h in range(H):
        hsl = slice(h * 2 * dk, (h + 1) * 2 * dk)
        cw = cw_ref[:, hsl]
        acc = cb_ref[:, hsl] + cw[MLSTM_CONV - 1:MLSTM_CONV, :] * ext_scr[SUBLANES:, hsl]
        for s in range(1, MLSTM_CONV):
            acc = acc + cw[MLSTM_CONV - 1 - s:MLSTM_CONV - s, :] * ext_scr[SUBLANES - s:SUBLANES - s + tt, hsl]
        act = acc * _sigmoid(acc)
        qkc_scr[:, hsl] = jnp.where(is_k, act * dk ** -0.5, act)
    ext_scr[0:SUBLANES, :] = ext_scr[tt:tt + SUBLANES, :]

    src = lax.broadcasted_iota(jnp.int32, (L, L), 0)
    tgt = lax.broadcasted_iota(jnp.int32, (L, L), 1)
    causal = src <= tgt
    tri3 = jnp.concatenate([(src >= tgt).astype(BF16)] * 3, axis=1)
    triu3 = jnp.concatenate([causal.astype(BF16)] * 3, axis=0)
    n_pad = jnp.zeros((SUBLANES - 3, dk), BF16)

    def chunk(c, carry):
        r0 = pl.multiple_of(c * L, L)
        gcol = gc_ref[pl.ds(r0, L), :]
        grow = gr_ref[c]
        b_col = _cumsum_rows(tri3, _log_sigmoid(gcol))
        b_row = _cumsum_lanes(_log_sigmoid(grow), triu3)
        hs = range(H)
        vsl = [slice(h * dv, (h + 1) * dv) for h in hs]
        qk = [qkc_scr[pl.ds(r0, L), h * 2 * dk:(h + 1) * 2 * dk] for h in hs]
        q = [qk[h][:, :dk].astype(BF16) for h in hs]
        k = [qk[h][:, dk:] for h in hs]
        v = [v_ref[pl.ds(r0, L), vsl[h]] for h in hs]
        col = [gcol[:, h:h + 1] - b_col[:, H + h:H + h + 1] for h in hs]
        li_r = [grow[h:h + 1, :] for h in hs]
        b_r = [b_row[H + h:H + h + 1, :] for h in hs]
        b_end = [b_r[h][:, L - 1:L] for h in hs]
        dmat = [jnp.where(causal, b_r[h] + col[h], -jnp.inf) for h in hs]
        d_max = [jnp.max(dmat[h], axis=0, keepdims=True) for h in hs]
        att = [jnp.exp(dmat[h] - d_max[h]) * _dot_nt(k[h].astype(BF16), q[h]) for h in hs]
        a_sum = [jnp.sum(att[h], axis=0, keepdims=True) for h in hs]
        intra = [_dot_tn(v[h], att[h].astype(BF16)) for h in hs]
        g_max = [jnp.max(b_end[h] - b_r[h] + li_r[h], axis=-1, keepdims=True) for h in hs]
        kw = [k[h] * jnp.exp(b_end[h] + col[h] - g_max[h]) for h in hs]
        c_loc = [_dot_tn(v[h], kw[h].astype(BF16)) for h in hs]
        n_loc = [jnp.sum(kw[h], axis=0, keepdims=True) for h in hs]
        m_prev = [m_scr[h:h + 1, 0:1] for h in hs]
        c_prev = [c_scr[h] for h in hs]
        n_prev = [n_scr[h:h + 1, :] for h in hs]
        m_inter = [b_r[h] + m_prev[h] for h in hs]
        m_t = [jnp.maximum(m_inter[h], d_max[h]) for h in hs]
        w_loc = [jnp.exp(d_max[h] - m_t[h]) for h in hs]
        w_int = [jnp.exp(m_inter[h] - m_t[h]) for h in hs]
        q_c = [_dot_nt(c_prev[h].astype(BF16), q[h]) for h in hs]
        q_n = [_dot_nt(jnp.concatenate(list(_split3(n_prev[h])) + [n_pad], axis=0), q[h]) for h in hs]
        q_n = [q_n[h][0:1] + q_n[h][1:2] + q_n[h][2:3] for h in hs]
        num = [w_loc[h] * intra[h] + w_int[h] * q_c[h] for h in hs]
        den = [w_loc[h] * a_sum[h] + w_int[h] * q_n[h] for h in hs]
        h_t = [num[h] / jnp.maximum(jnp.abs(den[h]), jnp.exp(-m_t[h])) for h in hs]
        h_t = [h_t[h] * lax.rsqrt(jnp.mean(h_t[h] * h_t[h], axis=0, keepdims=True) + NORM_EPS) for h in hs]
        m_new = [jnp.maximum(b_end[h] + m_prev[h], g_max[h]) for h in hs]
        a = [jnp.exp(b_end[h] + m_prev[h] - m_new[h]) for h in hs]
        sc = [jnp.exp(g_max[h] - m_new[h]) for h in hs]
        for h in hs:
            out = h_t[h] * ng_ref[vsl[h], :] * og_ref[vsl[h], pl.ds(r0, L)]
            o_ref[pl.ds(r0, L), vsl[h]] = out.T.astype(o_ref.dtype)
            c_scr[h] = a[h] * c_prev[h] + sc[h] * c_loc[h]
            n_scr[h:h + 1, :] = a[h] * n_prev[h] + sc[h] * n_loc[h]
            m_scr[h:h + 1, :] = jnp.broadcast_to(m_new[h], (1, LANES))
        return carry

    lax.fori_loop(0, tt // L, chunk, 0)


def _mlstm_core(qk, v, ogt, gcol, grow3, conv_w, conv_b, norm_gb, batch, seq_len, tt=256):
    m, d = qk.shape
    H, dk, dv = MLSTM_HEADS, MLSTM_QK_DIM, MLSTM_V_DIM
    nt = seq_len // tt
    ncs = tt // MLSTM_TILE
    row = lambda b, i: (b * nt + i, 0)
    return pl.pallas_call(
        functools.partial(_mlstm_core_kernel, tt=tt),
        out_shape=jax.ShapeDtypeStruct((m, d), BF16),
        grid=(batch, nt),
        in_specs=[pl.BlockSpec((tt, d), row), pl.BlockSpec((tt, d), row),
                  pl.BlockSpec((d, tt), lambda b, i: (0, b * nt + i)),
                  pl.BlockSpec((tt, LANES), row),
                  pl.BlockSpec((ncs, 2 * H, MLSTM_TILE), lambda b, i: (b * nt + i, 0, 0)),
                  _const_spec((SUBLANES, d)), _const_spec((1, d)), _const_spec((d, LANES))],
        out_specs=pl.BlockSpec((tt, d), row),
        scratch_shapes=[pltpu.VMEM((tt, d), F32), pltpu.VMEM((tt + SUBLANES, d), F32),
                        pltpu.VMEM((H, dv, dk), F32), pltpu.VMEM((H, dk), F32),
                        pltpu.VMEM((H, LANES), F32)],
        compiler_params=_cparams(("arbitrary", "arbitrary")),
        name="mlstm_core",
    )(qk, v, ogt, gcol, grow3, conv_w, conv_b, norm_gb)


def _mlstm_layer(x, norm_g, w_in, conv_w, conv_b, b_gates, hnorm, batch, seq_len):
    H, dk = MLSTM_HEADS, MLSTM_QK_DIM
    qk, v, ogt, gcol, grow = _mlstm_proj(x, norm_g, w_in, b_gates)
    m = x.shape[0]
    grow3 = grow.reshape(2 * H, m // MLSTM_TILE, MLSTM_TILE).transpose(1, 0, 2)
    norm_gb = jnp.broadcast_to(hnorm.reshape(-1, 1), (hnorm.shape[0], LANES))

    def perm(z):
        lead = z.shape[:-1]
        zq = z[..., :H * dk].reshape(*lead, H, dk)
        zk = z[..., H * dk:].reshape(*lead, H, dk)
        return jnp.concatenate([zq, zk], axis=-1).reshape(*lead, 2 * H * dk)

    cw = jnp.pad(perm(conv_w), ((0, SUBLANES - MLSTM_CONV), (0, 0)))
    return _mlstm_core(qk, v, ogt, gcol, grow3, cw, perm(conv_b).reshape(1, -1), norm_gb,
                       batch, seq_len)


def _softplus(x):
    return jnp.maximum(x, 0.0) + jnp.log(1.0 + jnp.exp(-jnp.abs(x)))


def _rwkv_proj_kernel(x_ref, g_ref, mu_ref, wr_ref, wk_ref, wv_ref, ww1_ref, ww2_ref, w0_ref,
                      aw1_ref, aw2_ref, a0_ref, gw1_ref, gw2_ref,
                      r_ref, k_ref, v_ref, lw_ref, a_ref, go_ref, prev_scr, *, tiles_per_seq):
    tm = x_ref.shape[0]
    h = _rms(x_ref[...], g_ref[...])
    seq_start = (pl.program_id(0) % tiles_per_seq) == 0
    prev = jnp.where(seq_start, 0.0, prev_scr[SUBLANES - 1:SUBLANES, :])
    prev_scr[...] = h[tm - SUBLANES:, :]
    rows = lax.broadcasted_iota(jnp.int32, h.shape, 0)
    xx = jnp.where(rows == 0, prev, pltpu.roll(h, 1, 0)) - h
    mix = lambda j: (h + xx * mu_ref[j:j + 1, :]).astype(BF16)
    r_ref[...] = _dot(mix(0), wr_ref[...])
    k_ref[...] = _dot(mix(2), wk_ref[...])
    v_ref[...] = _dot(mix(3), wv_ref[...])
    wl = _dot(jnp.tanh(_dot(mix(1), ww1_ref[...])).astype(BF16), ww2_ref[...]) + w0_ref[...]
    w_log = -_softplus(-wl) - 0.5
    lw_ref[...] = -jnp.exp(w_log)
    al = _dot(_dot(mix(4), aw1_ref[...]).astype(BF16), aw2_ref[...]) + a0_ref[...]
    a_ref[...] = _sigmoid(al)
    go_ref[...] = _dot(_sigmoid(_dot(mix(5), gw1_ref[...])).astype(BF16), gw2_ref[...])


def _rwkv_proj(x, g, mu, w_r, w_k, w_v, w0, w_w1, w_w2, a0, a_w1, a_w2, g_w1, g_w2, seq_len, tm=512):
    m, d = x.shape
    row = lambda i: (i, 0)
    bf = lambda w: w.astype(BF16)
    mu8 = jnp.pad(mu, ((0, SUBLANES - mu.shape[0]), (0, 0)))
    consts = [g.reshape(1, d), mu8, bf(w_r), bf(w_k), bf(w_v), bf(w_w1), bf(w_w2), w0.reshape(1, d),
              bf(a_w1), bf(a_w2), a0.reshape(1, d), bf(g_w1), bf(g_w2)]
    out = jax.ShapeDtypeStruct((m, d), F32)
    return pl.pallas_call(
        functools.partial(_rwkv_proj_kernel, tiles_per_seq=seq_len // tm),
        out_shape=(out,) * 6,
        grid=(m // tm,),
        in_specs=[pl.BlockSpec((tm, d), row)] + [_const_spec(c.shape) for c in consts],
        out_specs=(pl.BlockSpec((tm, d), row),) * 6,
        scratch_shapes=[pltpu.VMEM((SUBLANES, d), F32)],
        compiler_params=_cparams(("arbitrary",)),
        name="rwkv_proj",
    )(x, *consts)


def _block_diag(x, lo):
    zero = jnp.zeros_like(x)
    return jnp.concatenate([jnp.where(lo, x, zero), jnp.where(lo, zero, x)], axis=0)


class _PairMat:
    def __init__(self, x, lo):
        self.x, self.lo = x, lo
        self._lhs = self._rhs = None

    def lhs(self):
        if self._lhs is None:
            hi, lo = _split2(self.x)
            self._lhs = jnp.concatenate([hi, lo, hi], axis=1)
        return self._lhs

    def rhs(self):
        if self._rhs is None:
            hi, lo = _split2(self.x)
            bh = _block_diag(hi, self.lo)
            self._rhs = jnp.concatenate([bh, bh, _block_diag(lo, self.lo)], axis=0)
        return self._rhs


def _pair_mm(p, q):
    return _dot(p.lhs(), q.rhs())


def _rwkv_core_kernel(r_ref, k_ref, v_ref, lw_ref, a_ref, go_ref, kk_ref, ka_ref, rk_ref,
                      lnw_ref, lnb_ref, o_ref, z_scr, *, tt):
    L, N = RWKV_CHUNK, RWKV_HEAD_DIM
    npair = z_scr.shape[0]
    pairs = range(npair)

    @pl.when(pl.program_id(1) == 0)
    def _():
        z_scr[...] = jnp.zeros_like(z_scr)

    ri = lax.broadcasted_iota(jnp.int32, (L, LANES), 0)
    ln = lax.broadcasted_iota(jnp.int32, (L, LANES), 1)
    si = ln % N
    lo = ln < N
    lower_incl = ri >= si
    lower_strict = ri > si
    blk_diag = (ri // RWKV_INV_BLOCK) == (si // RWKV_INV_BLOCK)
    eye = (ri == si).astype(F32)
    tri = (lax.broadcasted_iota(jnp.int32, (L, L), 0) >= lax.broadcasted_iota(jnp.int32, (L, L), 1))
    tri3 = jnp.concatenate([tri.astype(BF16)] * 3, axis=1)
    z_mask = ((lax.broadcasted_iota(jnp.int32, (2 * N, LANES), 0) // N)
              == (lax.broadcasted_iota(jnp.int32, (2 * N, LANES), 1) // N))

    def half_sum(x):
        s0 = jnp.sum(jnp.where(lo, x, 0.0), axis=-1, keepdims=True)
        s1 = jnp.sum(jnp.where(lo, 0.0, x), axis=-1, keepdims=True)
        return jnp.where(lo, s0, s1)

    bd = lambda x: _block_diag(x, lo)
    mk = lambda xs: [_PairMat(x, lo) for x in xs]
    mm = lambda ps, qs: [_pair_mm(p, q) for p, q in zip(ps, qs)]

    def chunk(c, carry):
        r0 = pl.multiple_of(c * L, L)
        sl = [slice(p * LANES, (p + 1) * LANES) for p in pairs]
        ld = lambda ref: [ref[pl.ds(r0, L), s] for s in sl]
        r, k, v, lw, a = ld(r_ref), ld(k_ref), ld(v_ref), ld(lw_ref), ld(a_ref)
        kk = [k[p] * kk_ref[:, sl[p]] for p in pairs]
        kk = [kk[p] / jnp.maximum(jnp.sqrt(half_sum(kk[p] * kk[p])), 1e-12) for p in pairs]
        km = [k[p] * (1.0 + (a[p] - 1.0) * ka_ref[:, sl[p]]) for p in pairs]
        bv = [kk[p] * a[p] for p in pairs]
        cum = [_cumsum_rows(tri3, lw[p]) for p in pairs]
        cum_end = [cum[p][L - 1:L, :] for p in pairs]
        w_inv = [jnp.exp(-cum[p]) for p in pairs]
        w_out = [jnp.exp(cum_end[p] - cum[p]) for p in pairs]
        kk_h = [(kk[p] * jnp.exp(cum[p] - lw[p])).astype(BF16) for p in pairs]
        r_h = [(r[p] * jnp.exp(cum[p])).astype(BF16) for p in pairs]
        b_t = [(bv[p] * w_inv[p]).astype(BF16) for p in pairs]
        k_t = [(km[p] * w_inv[p]).astype(BF16) for p in pairs]
        bbar = [(bv[p] * w_out[p]).astype(BF16) for p in pairs]
        kbar = [(km[p] * w_out[p]).astype(BF16) for p in pairs]
        vb = [v[p].astype(BF16) for p in pairs]
        lhs = [jnp.concatenate([kk_h[p], r_h[p]], axis=0) for p in pairs]
        ab = [_dot_nt(lhs[p], bd(b_t[p])) for p in pairs]
        ak = [_dot_nt(lhs[p], bd(k_t[p])) for p in pairs]
        a_ub = [jnp.where(lower_strict, ab[p][:L], 0.0) for p in pairs]
        a_rb = [jnp.where(lower_incl, ab[p][L:], 0.0).astype(BF16) for p in pairs]
        a_uk = [jnp.where(lower_strict, ak[p][:L], 0.0).astype(BF16) for p in pairs]
        a_rk = [jnp.where(lower_incl, ak[p][L:], 0.0).astype(BF16) for p in pairs]

        n1 = mk([jnp.where(blk_diag, -a_ub[p], 0.0) for p in pairs])
        n2 = mk(mm(n1, n1))
        n4 = mk(mm(n2, n2))
        n8 = mk(mm(n4, n4))
        acc = mk([eye + n1[p].x for p in pairs])
        for nk in (n2, n4):
            prod = mm(acc, nk)
            acc = mk([acc[p].x + prod[p] for p in pairs])
        prod = mm(acc, n8)
        d_inv = mk([acc[p].x + prod[p] for p in pairs])
        l_off = mk([jnp.where(blk_diag, 0.0, a_ub[p]) for p in pairs])
        e1 = mk([-x for x in mm(d_inv, l_off)])
        e2 = mk(mm(e1, e1))
        qm = mk([eye + e1[p].x for p in pairs])
        prod = mm(qm, e2)
        qm = mk([qm[p].x + prod[p] for p in pairs])
        t_inv = mk(mm(qm, d_inv))

        z = [z_scr[p] for p in pairs]
        zb = [z[p].astype(BF16) for p in pairs]
        bdv = [bd(vb[p]) for p in pairs]
        rhs_u = mk([_dot_nt(kk_h[p], zb[p]) + _dot(a_uk[p], bdv[p]) for p in pairs])
        u = [-x for x in mm(t_inv, rhs_u)]
        ub = [u[p].astype(BF16) for p in pairs]
        y = [_dot_nt(r_h[p], zb[p])
             + _dot(jnp.concatenate([a_rb[p], a_rk[p]], axis=1),
                    jnp.concatenate([bd(ub[p]), bdv[p]], axis=0)) for p in pairs]
        upd = [_dot_tn(jnp.concatenate([ub[p], vb[p]], axis=0),
                       jnp.concatenate([bbar[p], kbar[p]], axis=0)) for p in pairs]
        for p in pairs:
            z_scr[p] = z[p] * jnp.exp(cum_end[p]) + jnp.where(z_mask, upd[p], 0.0)

        inv_n = 1.0 / N
        mean = [half_sum(y[p]) * inv_n for p in pairs]
        yc = [y[p] - mean[p] for p in pairs]
        var = [half_sum(yc[p] * yc[p]) * inv_n for p in pairs]
        bonus = [half_sum(r[p] * km[p] * rk_ref[:, sl[p]]) * v[p] for p in pairs]
        for p in pairs:
            yn = yc[p] * lax.rsqrt(var[p] + RWKV_GN_EPS) * lnw_ref[:, sl[p]] + lnb_ref[:, sl[p]]
            o_ref[pl.ds(r0, L), sl[p]] = ((yn + bonus[p]) * go_ref[pl.ds(r0, L), sl[p]]).astype(o_ref.dtype)
        return carry

    lax.fori_loop(0, tt // L, chunk, 0)


def _rwkv_core(r, k, v, lw, a, go, k_k, k_a, r_k, ln_w, ln_b, batch, seq_len, tt=256):
    m, d = r.shape
    nt = seq_len // tt
    npair = d // LANES
    blk = pl.BlockSpec((tt, d), lambda b, i: (b * nt + i, 0))
    par = _const_spec((1, d))
    row1 = lambda z: z.reshape(1, d)
    return pl.pallas_call(
        functools.partial(_rwkv_core_kernel, tt=tt),
        out_shape=jax.ShapeDtypeStruct((m, d), BF16),
        grid=(batch, nt),
        in_specs=[blk] * 6 + [par] * 5,
        out_specs=blk,
        scratch_shapes=[pltpu.VMEM((npair, 2 * RWKV_HEAD_DIM, LANES), F32)],
        compiler_params=_cparams(("arbitrary", "arbitrary")),
        name="rwkv_core",
    )(r, k, v, lw, a, go, row1(k_k), row1(k_a), row1(r_k), row1(ln_w), row1(ln_b))


def _rwkv_layer(x, norm_g, mu, w_r, w_k, w_v, w0, w_w1, w_w2, a0, a_w1, a_w2, g_w1, g_w2,
                k_k, k_a, r_k, ln_w, ln_b, batch, seq_len):
    r, k, v, lw, a, go = _rwkv_proj(x, norm_g, mu, w_r, w_k, w_v, w0, w_w1, w_w2, a0, a_w1, a_w2,
                                    g_w1, g_w2, seq_len)
    return _rwkv_core(r, k, v, lw, a, go, k_k, k_a, r_k, ln_w, ln_b, batch, seq_len)


def kernel(x, norm_mixer, norm_ffn, ffn_w_up, ffn_conv_w, ffn_conv_b, ffn_w_down, nsa_w_in, nsa_pe_k, nsa_pe_v, nsa_cmp_k_w1, nsa_cmp_k_w2, nsa_cmp_v_w1, nsa_cmp_v_w2, nsa_b_gate, nsa_w_out, mlstm_w_in, mlstm_conv_w, mlstm_conv_b, mlstm_b_gates, mlstm_norm, mlstm_w_out, rwkv_mu, rwkv_w_r, rwkv_w_k, rwkv_w_v, rwkv_w_o, rwkv_w0, rwkv_w_w1, rwkv_w_w2, rwkv_a0, rwkv_a_w1, rwkv_a_w2, rwkv_g_w1, rwkv_g_w2, rwkv_k_k, rwkv_k_a, rwkv_r_k, rwkv_ln_w, rwkv_ln_b, final_norm):
    batch, seq_len, d = x.shape
    depth = norm_mixer.shape[0]
    rope = _rope_tables(seq_len)
    xf = x.reshape(batch * seq_len, d)
    for i in range(depth):
        kind, j = i % 3, i // 3
        if kind == 0:
            w_o = nsa_w_out[j]
            o = _nsa_layer(xf, norm_mixer[i], nsa_w_in[j], nsa_pe_k[j], nsa_pe_v[j], nsa_cmp_k_w1[j],
                           nsa_cmp_k_w2[j], nsa_cmp_v_w1[j], nsa_cmp_v_w2[j], nsa_b_gate[j], rope, batch, seq_len)
        elif kind == 1:
            w_o = mlstm_w_out[j]
            o = _mlstm_layer(xf, norm_mixer[i], mlstm_w_in[j], mlstm_conv_w[j], mlstm_conv_b[j],
                             mlstm_b_gates[j], mlstm_norm[j], batch, seq_len)
        else:
            w_o = rwkv_w_o[j]
            o = _rwkv_layer(xf, norm_mixer[i], rwkv_mu[j], rwkv_w_r[j], rwkv_w_k[j], rwkv_w_v[j],
                            rwkv_w0[j], rwkv_w_w1[j], rwkv_w_w2[j], rwkv_a0[j],
                            rwkv_a_w1[j], rwkv_a_w2[j], rwkv_g_w1[j], rwkv_g_w2[j], rwkv_k_k[j],
                            rwkv_k_a[j], rwkv_r_k[j], rwkv_ln_w[j], rwkv_ln_b[j], batch, seq_len)
        xf = _ffn(xf, o, w_o, norm_ffn[i], ffn_w_up[i], ffn_conv_w[i], ffn_conv_b[i], ffn_w_down[i],
                  seq_len, final_g=final_norm if i == depth - 1 else None)
    return xf.reshape(batch, seq_len, d)
```

```python
import functools
import math

import jax
import jax.numpy as jnp
import numpy as np
from jax import lax
from jax.experimental import pallas as pl
from jax.experimental.pallas import tpu as pltpu

F32 = jnp.float32
BF16 = jnp.bfloat16

D_MODEL = 1024
NORM_EPS = 1e-6
ROPE_THETA = 500000.0

NSA_HEAD_DIM = 64
NSA_HEADS = 16
NSA_GROUPS = 4
NSA_REP = NSA_HEADS // NSA_GROUPS
NSA_ROT_DIM = 16
CMP_BLOCK = 32
CMP_STRIDE = 16
CMP_HIDDEN = 256
SEL_BLOCK = 64
SEL_TOPK = 16
WINDOW = 512
NSA_KV = NSA_GROUPS * NSA_HEAD_DIM
BF16_SUBLANES = 16
NSA_VT_ROWS = NSA_HEAD_DIM + BF16_SUBLANES
NSA_GATE_ROWS = 16
NSA_WINDOW_QUERIES = 256

MLSTM_HEADS = 8
MLSTM_QK_DIM = 64
MLSTM_V_DIM = 128
MLSTM_TILE = 128
MLSTM_CONV = 4

RWKV_HEAD_DIM = 64
RWKV_GN_EPS = 64e-5
RWKV_CHUNK = 64
RWKV_INV_BLOCK = 16

FFN_DIM = 2816
FFN_CONV = 3
FFN_CHUNK = 256
FFN_ROW_BLOCK = 64

LOG2E = math.log2(math.e)
MASKED = -1e30

LANES = 128
SUBLANES = 8
VMEM_LIMIT = 56 * 1024 * 1024


def _dot(a, b):
    return jnp.dot(a, b, preferred_element_type=F32)


def _dot_nt(a, b):
    return lax.dot_general(a, b, (((1,), (1,)), ((), ())), preferred_element_type=F32)


def _dot_tn(a, b):
    return lax.dot_general(a, b, (((0,), (0,)), ((), ())), preferred_element_type=F32)


def _split2(x):
    hi = x.astype(BF16)
    return hi, (x - hi.astype(F32)).astype(BF16)


def _split3(x):
    hi = x.astype(BF16)
    r1 = x - hi.astype(F32)
    mid = r1.astype(BF16)
    return hi, mid, (r1 - mid.astype(F32)).astype(BF16)


def _cumsum_rows(tri3, x):
    return _dot(tri3, jnp.concatenate(_split3(x), axis=0))


def _cumsum_lanes(x, triu3):
    return _dot(jnp.concatenate(_split3(x), axis=1), triu3)


def _rms(x, g):
    ms = jnp.mean(x * x, axis=-1, keepdims=True)
    return x * lax.rsqrt(ms + NORM_EPS) * g


def _sigmoid(x):
    return 1.0 / (1.0 + jnp.exp(-x))


def _cparams(sem):
    return pltpu.CompilerParams(dimension_semantics=sem, vmem_limit_bytes=VMEM_LIMIT)


def _const_spec(shape):
    n = len(shape)
    return pl.BlockSpec(shape, lambda *_: (0,) * n, pipeline_mode=pl.Buffered(1))


def _ffn_kernel(res_ref, a_ref, wo_ref, g_ref, wu_ref, cw_ref, cb_ref, wd_ref, fg_ref, o_ref,
                h_scr, carry_scr, act_scr, ga_scr, va_scr, gb_scr, vb_scr,
                *, tiles_per_seq, n_chunks, final_norm):
    tm = res_ref.shape[0]
    fc = FFN_CHUNK
    halo = SUBLANES
    x = res_ref[...] + _dot(a_ref[...], wo_ref[...])
    h_scr[...] = _rms(x, g_ref[...]).astype(BF16)
    o_ref[...] = x
    seq_start = (pl.program_id(0) % tiles_per_seq) == 0
    cols = lambda c, base=0: pl.ds(pl.multiple_of(base + c * fc, LANES), fc)

    def up(c, g_scr, v_scr):
        h = h_scr[...]
        g_scr[halo:, :] = _dot(h, wu_ref[:, cols(c)])
        v_scr[...] = _dot(h, wu_ref[:, cols(c, FFN_DIM)])

    def down(c, g_scr, v_scr):
        g_scr[0:halo, :] = jnp.where(seq_start, 0.0, carry_scr[:, cols(c)])
        carry_scr[:, cols(c)] = g_scr[tm:tm + halo, :]
        cw = cw_ref[:, cols(c)]
        cb = cb_ref[:, cols(c)]
        for r0 in range(0, tm, FFN_ROW_BLOCK):
            y = cb + sum(cw[FFN_CONV - 1 - s:FFN_CONV - s, :] * g_scr[halo - s + r0:halo - s + r0 + FFN_ROW_BLOCK, :]
                         for s in range(FFN_CONV))
            act = y * _sigmoid(y) * v_scr[r0:r0 + FFN_ROW_BLOCK, :]
            act_scr[r0:r0 + FFN_ROW_BLOCK, :] = act.astype(BF16)
        o_ref[...] += _dot(act_scr[...], wd_ref[pl.ds(pl.multiple_of(c * fc, fc), fc), :])

    assert n_chunks % 2 == 1

    def chunk_pair(j, carry):
        up(2 * j + 1, gb_scr, vb_scr)
        down(2 * j, ga_scr, va_scr)
        up(2 * j + 2, ga_scr, va_scr)
        down(2 * j + 1, gb_scr, vb_scr)
        return carry

    up(0, ga_scr, va_scr)
    for j in range(n_chunks // 2):
        chunk_pair(j, 0)
    down(n_chunks - 1, ga_scr, va_scr)
    if final_norm:
        o_ref[...] = _rms(o_ref[...], fg_ref[...])


def _ffn(res, a, w_o, g, w_up, conv_w, conv_b, w_down, seq_len, final_g=None, tm=1024):
    m, d = res.shape
    nc = FFN_DIM // FFN_CHUNK
    cw = jnp.pad(conv_w, ((0, SUBLANES - FFN_CONV), (0, 0)))
    row = lambda i: (i, 0)
    fg = jnp.ones((d,), F32) if final_g is None else final_g
    kern = functools.partial(_ffn_kernel, tiles_per_seq=seq_len // tm, n_chunks=nc,
                             final_norm=final_g is not None)
    return pl.pallas_call(
        kern,
        out_shape=jax.ShapeDtypeStruct((m, d), F32),
        grid=(m // tm,),
        in_specs=[pl.BlockSpec((tm, d), row), pl.BlockSpec((tm, d), row), _const_spec((d, d)),
                  _const_spec((1, d)), _const_spec((d, 2 * FFN_DIM)),
                  _const_spec((SUBLANES, FFN_DIM)), _const_spec((1, FFN_DIM)),
                  _const_spec((FFN_DIM, d)), _const_spec((1, d))],
        out_specs=pl.BlockSpec((tm, d), row),
        scratch_shapes=[pltpu.VMEM((tm, d), BF16),
                        pltpu.VMEM((SUBLANES, FFN_DIM), F32),
                        pltpu.VMEM((tm, FFN_CHUNK), BF16)]
                       + [pltpu.VMEM((tm + SUBLANES, FFN_CHUNK), F32), pltpu.VMEM((tm, FFN_CHUNK), F32)] * 2,
        compiler_params=_cparams(("arbitrary",)),
        name="conv_ffn",
    )(res, a, w_o.astype(BF16), g.reshape(1, d), w_up.astype(BF16), cw, conv_b.reshape(1, FFN_DIM),
      w_down.astype(BF16), fg.reshape(1, d))


def _nsa_proj_kernel(x_ref, g_ref, w_ref, wvt_ref, bg_ref, rc_ref, rs1_ref, rs2_ref,
                     qp_ref, qr_ref, kc_ref, vc_ref, ks_ref, vs_ref, kw_ref, vw_ref, gate_ref,
                     cmp_scr, *, tiles_per_seq):
    hn = _rms(x_ref[...], g_ref[...]).astype(BF16)
    y = _dot(hn, w_ref[...])
    vt = _dot_nt(wvt_ref[...], hn)
    rc, rs1, rs2 = rc_ref[...], rs1_ref[...], rs2_ref[...]
    dh = NSA_HEAD_DIM

    def rope(z):
        half = NSA_ROT_DIM // 2
        return z * rc + pltpu.roll(z, half, 1) * rs1 + pltpu.roll(z, LANES - half, 1) * rs2

    scale = dh ** -0.5 * LOG2E
    for j in range(D_MODEL // LANES):
        q = y[:, j * LANES:(j + 1) * LANES] * scale
        qp_ref[:, j * LANES:(j + 1) * LANES] = q.astype(BF16)
        qr_ref[:, j * LANES:(j + 1) * LANES] = rope(q).astype(BF16)

    def kv_chunk(idx):
        return y[:, D_MODEL + idx * NSA_KV:D_MODEL + (idx + 1) * NSA_KV]

    def split_groups(z, ref, dtype):
        for g in range(NSA_GROUPS):
            ref[g] = z[:, g * dh:(g + 1) * dh].astype(dtype)

    def rope256(z):
        return jnp.concatenate([rope(z[:, :LANES]), rope(z[:, LANES:])], axis=1)

    tm = y.shape[0]
    nrow = tm // CMP_STRIDE
    for j in range(2 * NSA_KV // LANES):
        cmp_scr[j] = y[:, D_MODEL + j * LANES:D_MODEL + (j + 1) * LANES]
    for j in range(2 * NSA_KV // LANES):
        ref = kc_ref if j < NSA_KV // LANES else vc_ref
        toks = [cmp_scr[j, pl.ds(tok, nrow, stride=CMP_STRIDE), :] for tok in range(CMP_STRIDE)]
        for half in range(LANES // dh):
            g = (j % (NSA_KV // LANES)) * (LANES // dh) + half
            ref[g] = jnp.concatenate([t[:, half * dh:(half + 1) * dh] for t in toks], axis=1)
    t_pos = (pl.program_id(0) % tiles_per_seq) * tm + lax.broadcasted_iota(jnp.int32, (tm, LANES), 0)
    onehot = (lax.broadcasted_iota(jnp.int32, (tm, LANES), 1) == t_pos // SEL_BLOCK).astype(F32)
    ksel = rope256(kv_chunk(2))
    zpad = jnp.zeros((tm, LANES - dh), F32)
    for g in range(NSA_GROUPS):
        ks_ref[g] = jnp.concatenate([ksel[:, g * dh:(g + 1) * dh], zpad, onehot], axis=1).astype(BF16)
    split_groups(rope256(kv_chunk(3)), kw_ref, BF16)
    ones_pad = (lax.broadcasted_iota(jnp.int32, (NSA_VT_ROWS - dh, tm), 0) == 0).astype(F32)
    for g in range(NSA_GROUPS):
        vs_ref[g] = jnp.concatenate([vt[g * dh:(g + 1) * dh], ones_pad], axis=0).astype(BF16)
        vw_ref[g] = jnp.concatenate([vt[NSA_KV + g * dh:NSA_KV + (g + 1) * dh], ones_pad], axis=0).astype(BF16)
    gate = _sigmoid(vt[2 * NSA_KV:] + bg_ref[...])
    for g in range(NSA_GROUPS):
        gate_ref[g] = gate[g * NSA_GATE_ROWS:(g + 1) * NSA_GATE_ROWS]


def _rope_tables(seq_len):
    half = NSA_ROT_DIM // 2
    inv_freq = ROPE_THETA ** (-jnp.arange(half, dtype=F32) / half)
    ang = jnp.arange(seq_len, dtype=F32)[:, None] * inv_freq[None, :]
    cos, sin = jnp.cos(ang), jnp.sin(ang)
    zeros = jnp.zeros((seq_len, NSA_HEAD_DIM - NSA_ROT_DIM), F32)
    z8 = jnp.zeros((seq_len, half), F32)
    rc = jnp.concatenate([cos, cos, zeros + 1.0], axis=1)
    rs1 = jnp.concatenate([z8, sin, zeros], axis=1)
    rs2 = jnp.concatenate([-sin, z8, zeros], axis=1)
    two = lambda t: jnp.concatenate([t, t], axis=1)
    return two(rc), two(rs1), two(rs2)


def _nsa_proj(x, g, w_in, b_gate, rope, seq_len, tm=512):
    m, d = x.shape
    n_kv = 6 * NSA_KV
    kv = lambda idx: w_in[:, D_MODEL + idx * NSA_KV:D_MODEL + (idx + 1) * NSA_KV]
    pad_g = NSA_GATE_ROWS - NSA_REP * 3
    wg = w_in[:, D_MODEL + n_kv:].reshape(d, NSA_GROUPS, NSA_REP * 3)
    wg = jnp.pad(wg, ((0, 0), (0, 0), (0, pad_g))).reshape(d, NSA_GROUPS * NSA_GATE_ROWS)
    w = jnp.concatenate([w_in[:, :D_MODEL], kv(0), kv(1), kv(2), kv(4)], axis=1).astype(BF16)
    wvt = jnp.concatenate([kv(3), kv(5), wg], axis=1).T.astype(BF16)
    bg = jnp.pad(b_gate.reshape(NSA_GROUPS, NSA_REP * 3), ((0, 0), (0, pad_g)))
    bg = bg.reshape(NSA_GROUPS * NSA_GATE_ROWS, 1)
    n = w.shape[1]
    tps = seq_len // tm
    row = lambda i: (i, 0)
    rope_spec = pl.BlockSpec((tm, LANES), lambda i: (i % tps, 0))
    assert seq_len // SEL_BLOCK <= LANES
    g_out = lambda dt, w=NSA_HEAD_DIM: jax.ShapeDtypeStruct((NSA_GROUPS, m, w), dt)
    g_spec = pl.BlockSpec((NSA_GROUPS, tm, NSA_HEAD_DIM), lambda i: (0, i, 0))
    ks_spec = pl.BlockSpec((NSA_GROUPS, tm, 2 * LANES), lambda i: (0, i, 0))
    cmp_out = jax.ShapeDtypeStruct((NSA_GROUPS, m // CMP_STRIDE, CMP_STRIDE * NSA_HEAD_DIM), F32)
    cmp_spec = pl.BlockSpec((NSA_GROUPS, tm // CMP_STRIDE, CMP_STRIDE * NSA_HEAD_DIM), lambda i: (0, i, 0))
    vt_out = jax.ShapeDtypeStruct((NSA_GROUPS, NSA_VT_ROWS, m), BF16)
    vt_spec = pl.BlockSpec((NSA_GROUPS, NSA_VT_ROWS, tm), lambda i: (0, 0, i))
    return pl.pallas_call(
        functools.partial(_nsa_proj_kernel, tiles_per_seq=tps),
        out_shape=(jax.ShapeDtypeStruct((m, d), BF16), jax.ShapeDtypeStruct((m, d), BF16),
                   cmp_out, cmp_out, g_out(BF16, 2 * LANES), vt_out, g_out(BF16), vt_out,
                   jax.ShapeDtypeStruct((NSA_GROUPS, NSA_GATE_ROWS, m), F32)),
        grid=(m // tm,),
        in_specs=[pl.BlockSpec((tm, d), row), _const_spec((1, d)), _const_spec((d, n)),
                  _const_spec((2 * NSA_KV + NSA_GROUPS * NSA_GATE_ROWS, d)),
                  _const_spec((NSA_GROUPS * NSA_GATE_ROWS, 1)), rope_spec, rope_spec, rope_spec],
        out_specs=(pl.BlockSpec((tm, d), row), pl.BlockSpec((tm, d), row),
                   cmp_spec, cmp_spec, ks_spec, vt_spec, g_spec, vt_spec,
                   pl.BlockSpec((NSA_GROUPS, NSA_GATE_ROWS, tm), lambda i: (0, 0, i))),
        scratch_shapes=[pltpu.VMEM((2 * NSA_KV // LANES, tm, LANES), F32)],
        compiler_params=_cparams(("parallel",)),
        name="nsa_proj",
    )(x, g.reshape(1, d), w, wvt, bg, *rope)


def _gelu_tanh(x):
    return 0.5 * x * (1.0 + jnp.tanh(math.sqrt(2.0 / math.pi) * (x + 0.044715 * (x * x * x))))


def _compress_kernel(zk_ref, zv_ref, pek_ref, pev_ref, w1k_ref, w2k_ref, w1v_ref, w2v_ref,
                     kc_ref, vc_ref):
    nrow = zk_ref.shape[0]
    rows = lax.broadcasted_iota(jnp.int32, (nrow, NSA_HEAD_DIM), 0)

    def one(z_ref, pe_ref, w1_ref, w2_ref, o_ref):
        z = z_ref[...]
        a = _dot((z + pe_ref[0:1, :]).astype(BF16), w1_ref[0])
        b = _dot((z + pe_ref[1:2, :]).astype(BF16), w1_ref[1])
        hid = a + pltpu.roll(b, nrow - 1, 0)
        out = _dot(_gelu_tanh(hid).astype(BF16), w2_ref[...])
        o_ref[...] = jnp.where(rows == nrow - 1, 0.0, out).astype(o_ref.dtype)

    one(zk_ref, pek_ref, w1k_ref, w2k_ref, kc_ref)
    one(zv_ref, pev_ref, w1v_ref, w2v_ref, vc_ref)


def _compress(kc_raw, vc_raw, pe_k, pe_v, w1k, w2k, w1v, w2v, seq_len):
    g, nrows, half = kc_raw.shape
    dh = half // CMP_STRIDE
    nchunk = seq_len // CMP_STRIDE
    zk = kc_raw.reshape(g * nrows, half)
    zv = vc_raw.reshape(g * nrows, half)
    pe2 = lambda pe: pe.reshape(2, half)
    w1 = lambda w: w.astype(BF16).reshape(2, half, CMP_HIDDEN)
    nblk = zk.shape[0] // nchunk
    row = lambda i: (i, 0)
    return pl.pallas_call(
        _compress_kernel,
        out_shape=(jax.ShapeDtypeStruct((zk.shape[0], dh), BF16),
                   jax.ShapeDtypeStruct((zk.shape[0], dh), BF16)),
        grid=(nblk,),
        in_specs=[pl.BlockSpec((nchunk, half), row), pl.BlockSpec((nchunk, half), row),
                  _const_spec((2, half)), _const_spec((2, half)),
                  _const_spec((2, half, CMP_HIDDEN)), _const_spec((CMP_HIDDEN, dh)),
                  _const_spec((2, half, CMP_HIDDEN)), _const_spec((CMP_HIDDEN, dh))],
        out_specs=(pl.BlockSpec((nchunk, dh), row), pl.BlockSpec((nchunk, dh), row)),
        compiler_params=_cparams(("parallel",)),
        name="nsa_compress",
    )(zk, zv, pe2(pe_k), pe2(pe_v), w1(w1k), w2k.astype(BF16), w1(w1v), w2v.astype(BF16))


def _nsa_cmp_kernel(q_ref, kc_ref, vc_ref, gate_ref, ovt_ref, oc_ref, selt_ref, imp_scr, *, tq, n_classes):
    qi = pl.program_id(2)
    nq = pl.num_programs(2)
    dh = NSA_HEAD_DIM
    ncmp = kc_ref.shape[0]
    nsel = imp_scr.shape[0]
    heads = range(NSA_REP)

    def attend(nrows):
        q = q_ref[...]
        qs = jnp.concatenate([q[:, r * dh:(r + 1) * dh] for r in heads], axis=0)
        kc, vc = kc_ref[0:nrows, :], vc_ref[0:nrows, :]
        t = qi * tq + lax.broadcasted_iota(jnp.int32, (nrows, tq), 1)
        cmp_end = lax.broadcasted_iota(jnp.int32, (nrows, tq), 0) * CMP_STRIDE + (CMP_BLOCK - 1)
        neg = jnp.where(cmp_end <= t, 0.0, -jnp.inf)
        st = [_dot_nt(kc, qs[r * tq:(r + 1) * tq]) + neg for r in heads]
        mx = [jnp.max(st[r], axis=0, keepdims=True) for r in heads]
        mx = [jnp.where(mx[r] == -jnp.inf, 0.0, mx[r]) for r in heads]
        e = [jnp.exp2(st[r] - mx[r]) for r in heads]
        p = [e[r] / jnp.maximum(jnp.sum(e[r], axis=0, keepdims=True), 1e-30) for r in heads]
        o_t = [_dot_tn(vc, p[r].astype(BF16)) for r in heads]
        gate = gate_ref[...]
        o_t = [o_t[r] * gate[3 * r:3 * r + 1, :] for r in heads]
        for j in range(NSA_REP // 2):
            pair = jnp.concatenate([o_t[2 * j], o_t[2 * j + 1]], axis=0).T
            oc_ref[:, 2 * j * dh:2 * (j + 1) * dh] = pair
        psum = sum(p)
        ovt = jnp.concatenate([ovt_ref[:, j * ncmp:j * ncmp + nrows] for j in range(3)], axis=1)
        imp_scr[...] = _dot(ovt, jnp.concatenate(_split3(psum), axis=0))

    for cls in range(n_classes):
        @pl.when((qi * n_classes) // nq == cls)
        def _(cls=cls):
            attend((cls + 1) * ncmp // n_classes)

    imp = imp_scr[...]
    blk = lax.broadcasted_iota(jnp.int32, (nsel, LANES), 0)
    blk_f = blk.astype(F32)
    for cb in range(tq // LANES):
        csl = slice(cb * LANES, (cb + 1) * LANES)
        tb = (qi * tq + cb * LANES + lax.broadcasted_iota(jnp.int32, (nsel, LANES), 1)) // SEL_BLOCK
        forced = (blk == 0) | (blk == tb) | (blk == tb - 1)
        vals = jnp.where(forced, -jnp.inf, jnp.where(blk <= tb, imp[:, csl], -1.0))
        for _ in range(SEL_TOPK - 3):
            top = jnp.max(vals, axis=0, keepdims=True)
            first = jnp.min(jnp.where(vals == top, blk_f, float(nsel)), axis=0, keepdims=True)
            vals = jnp.where(blk_f == first, -jnp.inf, vals)
        selt_ref[:, csl] = jnp.where(vals == -jnp.inf, 1.0, 0.0).astype(selt_ref.dtype)


def _overlap_matrix_t3(ncmp_pad, nsel):
    c = np.arange(ncmp_pad)[None, :]
    s = np.arange(nsel)[:, None]
    cmp_start = c * CMP_STRIDE
    cmp_end = cmp_start + CMP_BLOCK - 1
    blk_start = s * SEL_BLOCK
    ov = ((cmp_end >= blk_start) & (cmp_start <= blk_start + SEL_BLOCK - 1)).astype(np.float32)
    return jnp.asarray(np.concatenate([ov, ov, ov], axis=1), dtype=BF16)


def _nsa_cmp(qp, kc, vc, gates, batch, seq_len, tq=512):
    m, d = qp.shape
    nq = seq_len // tq
    ncmp = seq_len // CMP_STRIDE
    nsel = seq_len // SEL_BLOCK
    ovt3 = _overlap_matrix_t3(ncmp, nsel)
    qmap = lambda b, g, i: (b * nq + i, g)
    kmap = lambda b, g, i: (g * batch + b, 0)
    n_classes = math.gcd(nq, 4)
    return pl.pallas_call(
        functools.partial(_nsa_cmp_kernel, tq=tq, n_classes=n_classes),
        out_shape=(jax.ShapeDtypeStruct((m, d), F32),
                   jax.ShapeDtypeStruct((NSA_GROUPS, nsel, m), BF16)),
        grid=(batch, NSA_GROUPS, nq),
        in_specs=[pl.BlockSpec((tq, NSA_KV), qmap),
                  pl.BlockSpec((ncmp, NSA_HEAD_DIM), kmap), pl.BlockSpec((ncmp, NSA_HEAD_DIM), kmap),
                  pl.BlockSpec((None, NSA_GATE_ROWS, tq), lambda b, g, i: (g, 0, b * nq + i)),
                  _const_spec((nsel, 3 * ncmp))],
        out_specs=(pl.BlockSpec((tq, NSA_KV), qmap),
                   pl.BlockSpec((None, nsel, tq), lambda b, g, i: (g, 0, b * nq + i))),
        scratch_shapes=[pltpu.VMEM((nsel, tq), F32)],
        compiler_params=_cparams(("parallel", "parallel", "parallel")),
        name="nsa_cmp_topk",
    )(qp, kc, vc, gates, ovt3)


def _nsa_sel_kernel(q_ref, ks_ref, vs_ref, kw_ref, vw_ref, sel_ref, gate_ref, oc_ref, o_ref,
                    m_scr, acc_scr, ow_scr, sta_scr, stb_scr, *, tq, tk):
    qi = pl.program_id(2)
    dh = NSA_HEAD_DIM
    cols = NSA_REP * tq
    q = q_ref[...]
    qs = jnp.concatenate([q[:, r * dh:(r + 1) * dh] for r in range(NSA_REP)], axis=0)
    selt = sel_ref[...].astype(F32)
    nsel = selt.shape[0]
    bmask_t = jnp.where(selt > 0.5, 0.0, MASKED)
    if nsel < LANES:
        bmask_t = jnp.concatenate([bmask_t, jnp.zeros((LANES - nsel, tq), F32)], axis=0)
    bmask = bmask_t.T.astype(BF16)
    zpad = jnp.zeros((tq, LANES - dh), BF16)
    qa = jnp.concatenate([jnp.concatenate([q[:, r * dh:(r + 1) * dh], zpad, bmask], axis=1)
                          for r in range(NSA_REP)], axis=0)
    q0 = qi * tq
    key_iota = lax.broadcasted_iota(jnp.int32, (tk, tq), 0)
    t_pos = q0 + lax.broadcasted_iota(jnp.int32, (tk, tq), 1)

    m_scr[...] = jnp.full((1, cols), -jnp.inf, F32)
    acc_scr[...] = jnp.zeros((NSA_VT_ROWS, cols), F32)

    heads = range(NSA_REP)
    hsl = [slice(r * tq, (r + 1) * tq) for r in heads]

    def put_scores(scr, ki):
        k = ks_ref[pl.ds(pl.multiple_of(ki * tk, tk), tk), :]
        for r in heads:
            scr[r] = _dot_nt(k, qa[hsl[r]])

    def get_scores(scr):
        return [scr[r] for r in heads]

    def consume(ki, st, causal):
        k0 = pl.multiple_of(ki * tk, tk)
        vt = vs_ref[:, pl.ds(k0, tk)]
        if causal:
            neg = jnp.where(k0 + key_iota <= t_pos, 0.0, -jnp.inf)
            st = [s + neg for s in st]
        m_old = [m_scr[:, hsl[r]] for r in heads]
        m_new = [jnp.maximum(m_old[r], jnp.max(st[r], axis=0, keepdims=True)) for r in heads]
        alpha = [jnp.exp2(m_old[r] - m_new[r]) for r in heads]
        p = [jnp.exp2(st[r] - m_new[r]).astype(BF16) for r in heads]
        pv = [_dot(vt, p[r]) for r in heads]
        for r in heads:
            acc_scr[:, hsl[r]] = alpha[r] * acc_scr[:, hsl[r]] + pv[r]
            m_scr[:, hsl[r]] = m_new[r]

    def tile_pair(j, carry):
        even = get_scores(sta_scr)
        put_scores(stb_scr, 2 * j + 1)
        consume(2 * j, even, False)
        odd = get_scores(stb_scr)
        put_scores(sta_scr, 2 * j + 2)
        consume(2 * j + 1, odd, False)
        return carry

    last = (q0 + tq - 1) // tk
    put_scores(sta_scr, 0)

    gate = gate_ref[...]
    wq = min(tq, NSA_WINDOW_QUERIES)
    wsub = wq + WINDOW
    nsub = tq // wq
    rel_iota = (lax.broadcasted_iota(jnp.int32, (wsub, wq), 0)
                - lax.broadcasted_iota(jnp.int32, (wsub, wq), 1))
    starts = [pl.multiple_of(jnp.maximum(q0 + wq * u - WINDOW, 0), wq) for u in range(nsub)]
    kw = [kw_ref[pl.ds(starts[u], wsub), :] for u in range(nsub)]
    vwt = [vw_ref[:, pl.ds(starts[u], wsub)] for u in range(nsub)]
    rel = [starts[u] - (q0 + wq * u) + rel_iota for u in range(nsub)]
    neg_w = [jnp.where((rel[u] <= 0) & (rel[u] > -WINDOW), 0.0, -jnp.inf) for u in range(nsub)]
    subs = [(r, u) for r in heads for u in range(nsub)]
    sw = [_dot_nt(kw[u], qs[r * tq + wq * u:r * tq + wq * (u + 1)]) + neg_w[u] for r, u in subs]
    mw = [jnp.max(s, axis=0, keepdims=True) for s in sw]
    mw = [jnp.where(m == -jnp.inf, 0.0, m) for m in mw]
    ew = [jnp.exp2(s - m).astype(BF16) for s, m in zip(sw, mw)]
    pvw = [_dot(vwt[u], e) for (r, u), e in zip(subs, ew)]
    pvw = [p[0:dh] / jnp.maximum(p[dh:dh + 1], 1e-30) for p in pvw]
    for r in heads:
        o_win = jnp.concatenate(pvw[r * nsub:(r + 1) * nsub], axis=1)
        ow_scr[:, hsl[r]] = o_win * gate[3 * r + 2:3 * r + 3, :]

    lax.fori_loop(0, last // 2, tile_pair, 0)

    @pl.when(last % 2 == 0)
    def _():
        consume(last, get_scores(sta_scr), True)

    @pl.when(last % 2 == 1)
    def _():
        even = get_scores(sta_scr)
        put_scores(stb_scr, last)
        consume(last - 1, even, False)
        consume(last, get_scores(stb_scr), True)
    o_sel = acc_scr[0:dh, :] / jnp.maximum(acc_scr[dh:dh + 1, :], 1e-30)

    mix = [o_sel[:, hsl[r]] * gate[3 * r + 1:3 * r + 2, :] + ow_scr[:, hsl[r]] for r in heads]
    for j in range(NSA_REP // 2):
        psl = slice(2 * j * dh, 2 * (j + 1) * dh)
        pair = jnp.concatenate([mix[2 * j], mix[2 * j + 1]], axis=0).T
        o_ref[:, psl] = (oc_ref[:, psl] + pair).astype(o_ref.dtype)


def _nsa_sel(qr, ks, vs, kw, vw, sel, gates, oc, batch, seq_len, tq=512, tk=512):
    m, d = qr.shape
    nq = seq_len // tq
    nsel = seq_len // SEL_BLOCK
    qmap = lambda b, g, i: (b * nq + i, g)
    kvmap = lambda b, g, i: (g, b, 0)
    kv_spec = pl.BlockSpec((None, seq_len, NSA_HEAD_DIM), kvmap)
    ks_spec = pl.BlockSpec((None, seq_len, 2 * LANES), kvmap)
    vt_spec = pl.BlockSpec((None, NSA_VT_ROWS, seq_len), lambda b, g, i: (g, 0, b))
    cols = NSA_REP * tq
    assert tk % tq == 0
    return pl.pallas_call(
        functools.partial(_nsa_sel_kernel, tq=tq, tk=tk),
        out_shape=jax.ShapeDtypeStruct((m, d), BF16),
        grid=(batch, NSA_GROUPS, nq),
        in_specs=[pl.BlockSpec((tq, NSA_KV), qmap), ks_spec, vt_spec, kv_spec, vt_spec,
                  pl.BlockSpec((None, nsel, tq), lambda b, g, i: (g, 0, b * nq + i)),
                  pl.BlockSpec((None, NSA_GATE_ROWS, tq), lambda b, g, i: (g, 0, b * nq + i)),
                  pl.BlockSpec((tq, NSA_KV), qmap)],
        out_specs=pl.BlockSpec((tq, NSA_KV), qmap),
        scratch_shapes=[pltpu.VMEM((1, cols), F32), pltpu.VMEM((NSA_VT_ROWS, cols), F32),
                        pltpu.VMEM((NSA_HEAD_DIM, cols), F32),
                        pltpu.VMEM((NSA_REP, tk, tq), F32), pltpu.VMEM((NSA_REP, tk, tq), F32)],
        compiler_params=_cparams(("parallel", "parallel", "parallel")),
        name="nsa_sel_win",
    )(qr, ks, vs, kw, vw, sel, gates, oc)


def _nsa_layer(x, norm_g, w_in, pe_k, pe_v, w1k, w2k, w1v, w2v, b_gate, rope, batch, seq_len):
    qp, qr, kc_raw, vc_raw, ks, vs, kw, vw, gates = _nsa_proj(x, norm_g, w_in, b_gate, rope, seq_len)
    kc, vc = _compress(kc_raw, vc_raw, pe_k, pe_v, w1k, w2k, w1v, w2v, seq_len)
    oc, sel = _nsa_cmp(qp, kc, vc, gates, batch, seq_len)
    return _nsa_sel(qr, ks, vs, kw, vw, sel, gates, oc, batch, seq_len)


def _mlstm_proj_kernel(x_ref, g_ref, w_ref, wot_ref, wt_ref, bcol_ref, brow_ref,
                       qk_ref, v_ref, ot_ref, gc_ref, gr_ref):
    hn = _rms(x_ref[...], g_ref[...]).astype(BF16)
    y = _dot(hn, w_ref[...])
    d = D_MODEL
    qk_ref[...] = y[:, :d]
    v_ref[...] = y[:, d:2 * d].astype(BF16)
    ot_ref[...] = _sigmoid(_dot_nt(wot_ref[...], hn))
    gc_ref[...] = y[:, 2 * d:] + bcol_ref[...]
    gr_ref[...] = _dot_nt(wt_ref[...], hn) + brow_ref[...]


def _mlstm_proj(x, g, w_in, b_gates, tm=512):
    m, d = x.shape
    h, dk = MLSTM_HEADS, MLSTM_QK_DIM
    wq = w_in[:, :h * dk].reshape(d, h, dk)
    wk = w_in[:, h * dk:2 * h * dk].reshape(d, h, dk)
    wqk = jnp.concatenate([wq, wk], axis=2).reshape(d, 2 * h * dk)
    wv = w_in[:, d:2 * d]
    wif = w_in[:, 2 * d:2 * d + 2 * h]
    wo = w_in[:, 2 * d + 2 * h:]
    w = jnp.concatenate([wqk, wv, jnp.pad(wif, ((0, 0), (0, LANES - 2 * h)))], axis=1).astype(BF16)
    wot = wo.T.astype(BF16)
    wt = wif.T.astype(BF16)
    bcol = jnp.pad(b_gates, (0, LANES - 2 * h)).reshape(1, LANES)
    brow = b_gates.reshape(2 * h, 1)
    n = w.shape[1]
    row = lambda i: (i, 0)
    col = lambda i: (0, i)
    return pl.pallas_call(
        _mlstm_proj_kernel,
        out_shape=(jax.ShapeDtypeStruct((m, d), F32), jax.ShapeDtypeStruct((m, d), BF16),
                   jax.ShapeDtypeStruct((d, m), F32), jax.ShapeDtypeStruct((m, LANES), F32),
                   jax.ShapeDtypeStruct((2 * h, m), F32)),
        grid=(m // tm,),
        in_specs=[pl.BlockSpec((tm, d), row), _const_spec((1, d)), _const_spec((d, n)), _const_spec((d, d)),
                  _const_spec((2 * h, d)), _const_spec((1, LANES)), _const_spec((2 * h, 1))],
        out_specs=(pl.BlockSpec((tm, d), row), pl.BlockSpec((tm, d), row), pl.BlockSpec((d, tm), col),
                   pl.BlockSpec((tm, LANES), row), pl.BlockSpec((2 * h, tm), col)),
        compiler_params=_cparams(("parallel",)),
        name="mlstm_proj",
    )(x, g.reshape(1, d), w, wot, wt, bcol, brow)


def _log_sigmoid(x):
    return jnp.minimum(x, 0.0) - jnp.log(1.0 + jnp.exp(-jnp.abs(x)))


def _mlstm_core_kernel(qk_ref, v_ref, og_ref, gc_ref, gr_ref, cw_ref, cb_ref, ng_ref, o_ref,
                       qkc_scr, ext_scr, c_scr, n_scr, m_scr, *, tt):
    L = MLSTM_TILE
    H, dk, dv = MLSTM_HEADS, MLSTM_QK_DIM, MLSTM_V_DIM
    seq_start = pl.program_id(1) == 0

    @pl.when(seq_start)
    def _():
        c_scr[...] = jnp.zeros_like(c_scr)
        n_scr[...] = jnp.zeros_like(n_scr)
        m_scr[...] = jnp.zeros_like(m_scr)
        ext_scr[0:SUBLANES, :] = jnp.zeros((SUBLANES, ext_scr.shape[1]), F32)

    ext_scr[SUBLANES:, :] = qk_ref[...]
    is_k = lax.broadcasted_iota(jnp.int32, (tt, 2 * dk), 1) >= dk
    for h in range(H):
        hsl = slice(h * 2 * dk, (h + 1) * 2 * dk)
        cw = cw_ref[:, hsl]
        acc = cb_ref[:, hsl] + cw[MLSTM_CONV - 1:MLSTM_CONV, :] * ext_scr[SUBLANES:, hsl]
        for s in range(1, MLSTM_CONV):
            acc = acc + cw[MLSTM_CONV - 1 - s:MLSTM_CONV - s, :] * ext_scr[SUBLANES - s:SUBLANES - s + tt, hsl]
        act = acc * _sigmoid(acc)
        qkc_scr[:, hsl] = jnp.where(is_k, act * dk ** -0.5, act)
    ext_scr[0:SUBLANES, :] = ext_scr[tt:tt + SUBLANES, :]

    src = lax.broadcasted_iota(jnp.int32, (L, L), 0)
    tgt = lax.broadcasted_iota(jnp.int32, (L, L), 1)
    causal = src <= tgt
    tri3 = jnp.concatenate([(src >= tgt).astype(BF16)] * 3, axis=1)
    triu3 = jnp.concatenate([causal.astype(BF16)] * 3, axis=0)
    n_pad = jnp.zeros((SUBLANES - 3, dk), BF16)

    def chunk(c, carry):
        r0 = pl.multiple_of(c * L, L)
        gcol = gc_ref[pl.ds(r0, L), :]
        grow = gr_ref[c]
        b_col = _cumsum_rows(tri3, _log_sigmoid(gcol))
        b_row = _cumsum_lanes(_log_sigmoid(grow), triu3)
        hs = range(H)
        vsl = [slice(h * dv, (h + 1) * dv) for h in hs]
        qk = [qkc_scr[pl.ds(r0, L), h * 2 * dk:(h + 1) * 2 * dk] for h in hs]
        q = [qk[h][:, :dk].astype(BF16) for h in hs]
        k = [qk[h][:, dk:] for h in hs]
        v = [v_ref[pl.ds(r0, L), vsl[h]] for h in hs]
        col = [gcol[:, h:h + 1] - b_col[:, H + h:H + h + 1] for h in hs]
        li_r = [grow[h:h + 1, :] for h in hs]
        b_r = [b_row[H + h:H + h + 1, :] for h in hs]
        b_end = [b_r[h][:, L - 1:L] for h in hs]
        dmat = [jnp.where(causal, b_r[h] + col[h], -jnp.inf) for h in hs]
        d_max = [jnp.max(dmat[h], axis=0, keepdims=True) for h in hs]
        att = [jnp.exp(dmat[h] - d_max[h]) * _dot_nt(k[h].astype(BF16), q[h]) for h in hs]
        a_sum = [jnp.sum(att[h], axis=0, keepdims=True) for h in hs]
        intra = [_dot_tn(v[h], att[h].astype(BF16)) for h in hs]
        g_max = [jnp.max(b_end[h] - b_r[h] + li_r[h], axis=-1, keepdims=True) for h in hs]
        kw = [k[h] * jnp.exp(b_end[h] + col[h] - g_max[h]) for h in hs]
        c_loc = [_dot_tn(v[h], kw[h].astype(BF16)) for h in hs]
        n_loc = [jnp.sum(kw[h], axis=0, keepdims=True) for h in hs]
        m_prev = [m_scr[h:h + 1, 0:1] for h in hs]
        c_prev = [c_scr[h] for h in hs]
        n_prev = [n_scr[h:h + 1, :] for h in hs]
        m_inter = [b_r[h] + m_prev[h] for h in hs]
        m_t = [jnp.maximum(m_inter[h], d_max[h]) for h in hs]
        w_loc = [jnp.exp(d_max[h] - m_t[h]) for h in hs]
        w_int = [jnp.exp(m_inter[h] - m_t[h]) for h in hs]
        q_c = [_dot_nt(c_prev[h].astype(BF16), q[h]) for h in hs]
        q_n = [_dot_nt(jnp.concatenate(list(_split3(n_prev[h])) + [n_pad], axis=0), q[h]) for h in hs]
        q_n = [q_n[h][0:1] + q_n[h][1:2] + q_n[h][2:3] for h in hs]
        num = [w_loc[h] * intra[h] + w_int[h] * q_c[h] for h in hs]
        den = [w_loc[h] * a_sum[h] + w_int[h] * q_n[h] for h in hs]
        h_t = [num[h] / jnp.maximum(jnp.abs(den[h]), jnp.exp(-m_t[h])) for h in hs]
        h_t = [h_t[h] * lax.rsqrt(jnp.mean(h_t[h] * h_t[h], axis=0, keepdims=True) + NORM_EPS) for h in hs]
        m_new = [jnp.maximum(b_end[h] + m_prev[h], g_max[h]) for h in hs]
        a = [jnp.exp(b_end[h] + m_prev[h] - m_new[h]) for h in hs]
        sc = [jnp.exp(g_max[h] - m_new[h]) for h in hs]
        for h in hs:
            out = h_t[h] * ng_ref[vsl[h], :] * og_ref[vsl[h], pl.ds(r0, L)]
            o_ref[pl.ds(r0, L), vsl[h]] = out.T.astype(o_ref.dtype)
            c_scr[h] = a[h] * c_prev[h] + sc[h] * c_loc[h]
            n_scr[h:h + 1, :] = a[h] * n_prev[h] + sc[h] * n_loc[h]
            m_scr[h:h + 1, :] = jnp.broadcast_to(m_new[h], (1, LANES))
        return carry

    lax.fori_loop(0, tt // L, chunk, 0)


def _mlstm_core(qk, v, ogt, gcol, grow3, conv_w, conv_b, norm_gb, batch, seq_len, tt=256):
    m, d = qk.shape
    H, dk, dv = MLSTM_HEADS, MLSTM_QK_DIM, MLSTM_V_DIM
    nt = seq_len // tt
    ncs = tt // MLSTM_TILE
    row = lambda b, i: (b * nt + i, 0)
    return pl.pallas_call(
        functools.partial(_mlstm_core_kernel, tt=tt),
        out_shape=jax.ShapeDtypeStruct((m, d), BF16),
        grid=(batch, nt),
        in_specs=[pl.BlockSpec((tt, d), row), pl.BlockSpec((tt, d), row),
                  pl.BlockSpec((d, tt), lambda b, i: (0, b * nt + i)),
                  pl.BlockSpec((tt, LANES), row),
                  pl.BlockSpec((ncs, 2 * H, MLSTM_TILE), lambda b, i: (b * nt + i, 0, 0)),
                  _const_spec((SUBLANES, d)), _const_spec((1, d)), _const_spec((d, LANES))],
        out_specs=pl.BlockSpec((tt, d), row),
        scratch_shapes=[pltpu.VMEM((tt, d), F32), pltpu.VMEM((tt + SUBLANES, d), F32),
                        pltpu.VMEM((H, dv, dk), F32), pltpu.VMEM((H, dk), F32),
                        pltpu.VMEM((H, LANES), F32)],
        compiler_params=_cparams(("arbitrary", "arbitrary")),
        name="mlstm_core",
    )(qk, v, ogt, gcol, grow3, conv_w, conv_b, norm_gb)


def _mlstm_layer(x, norm_g, w_in, conv_w, conv_b, b_gates, hnorm, batch, seq_len):
    H, dk = MLSTM_HEADS, MLSTM_QK_DIM
    qk, v, ogt, gcol, grow = _mlstm_proj(x, norm_g, w_in, b_gates)
    m = x.shape[0]
    grow3 = grow.reshape(2 * H, m // MLSTM_TILE, MLSTM_TILE).transpose(1, 0, 2)
    norm_gb = jnp.broadcast_to(hnorm.reshape(-1, 1), (hnorm.shape[0], LANES))

    def perm(z):
        lead = z.shape[:-1]
        zq = z[..., :H * dk].reshape(*lead, H, dk)
        zk = z[..., H * dk:].reshape(*lead, H, dk)
        return jnp.concatenate([zq, zk], axis=-1).reshape(*lead, 2 * H * dk)

    cw = jnp.pad(perm(conv_w), ((0, SUBLANES - MLSTM_CONV), (0, 0)))
    return _mlstm_core(qk, v, ogt, gcol, grow3, cw, perm(conv_b).reshape(1, -1), norm_gb,
                       batch, seq_len)


def _softplus(x):
    return jnp.maximum(x, 0.0) + jnp.log(1.0 + jnp.exp(-jnp.abs(x)))


def _rwkv_proj_kernel(x_ref, g_ref, mu_ref, wr_ref, wk_ref, wv_ref, ww1_ref, ww2_ref, w0_ref,
                      aw1_ref, aw2_ref, a0_ref, gw1_ref, gw2_ref,
                      r_ref, k_ref, v_ref, lw_ref, a_ref, go_ref, prev_scr, *, tiles_per_seq):
    tm = x_ref.shape[0]
    h = _rms(x_ref[...], g_ref[...])
    seq_start = (pl.program_id(0) % tiles_per_seq) == 0
    prev = jnp.where(seq_start, 0.0, prev_scr[SUBLANES - 1:SUBLANES, :])
    prev_scr[...] = h[tm - SUBLANES:, :]
    rows = lax.broadcasted_iota(jnp.int32, h.shape, 0)
    xx = jnp.where(rows == 0, prev, pltpu.roll(h, 1, 0)) - h
    mix = lambda j: (h + xx * mu_ref[j:j + 1, :]).astype(BF16)
    r_ref[...] = _dot(mix(0), wr_ref[...])
    k_ref[...] = _dot(mix(2), wk_ref[...])
    v_ref[...] = _dot(mix(3), wv_ref[...])
    wl = _dot(jnp.tanh(_dot(mix(1), ww1_ref[...])).astype(BF16), ww2_ref[...]) + w0_ref[...]
    w_log = -_softplus(-wl) - 0.5
    lw_ref[...] = -jnp.exp(w_log)
    al = _dot(_dot(mix(4), aw1_ref[...]).astype(BF16), aw2_ref[...]) + a0_ref[...]
    a_ref[...] = _sigmoid(al)
    go_ref[...] = _dot(_sigmoid(_dot(mix(5), gw1_ref[...])).astype(BF16), gw2_ref[...])


def _rwkv_proj(x, g, mu, w_r, w_k, w_v, w0, w_w1, w_w2, a0, a_w1, a_w2, g_w1, g_w2, seq_len, tm=512):
    m, d = x.shape
    row = lambda i: (i, 0)
    bf = lambda w: w.astype(BF16)
    mu8 = jnp.pad(mu, ((0, SUBLANES - mu.shape[0]), (0, 0)))
    consts = [g.reshape(1, d), mu8, bf(w_r), bf(w_k), bf(w_v), bf(w_w1), bf(w_w2), w0.reshape(1, d),
              bf(a_w1), bf(a_w2), a0.reshape(1, d), bf(g_w1), bf(g_w2)]
    out = jax.ShapeDtypeStruct((m, d), F32)
    return pl.pallas_call(
        functools.partial(_rwkv_proj_kernel, tiles_per_seq=seq_len // tm),
        out_shape=(out,) * 6,
        grid=(m // tm,),
        in_specs=[pl.BlockSpec((tm, d), row)] + [_const_spec(c.shape) for c in consts],
        out_specs=(pl.BlockSpec((tm, d), row),) * 6,
        scratch_shapes=[pltpu.VMEM((SUBLANES, d), F32)],
        compiler_params=_cparams(("arbitrary",)),
        name="rwkv_proj",
    )(x, *consts)


def _block_diag(x, lo):
    zero = jnp.zeros_like(x)
    return jnp.concatenate([jnp.where(lo, x, zero), jnp.where(lo, zero, x)], axis=0)


class _PairMat:
    def __init__(self, x, lo):
        self.x, self.lo = x, lo
        self._lhs = self._rhs = None

    def lhs(self):
        if self._lhs is None:
            hi, lo = _split2(self.x)
            self._lhs = jnp.concatenate([hi, lo, hi], axis=1)
        return self._lhs

    def rhs(self):
        if self._rhs is None:
            hi, lo = _split2(self.x)
            bh = _block_diag(hi, self.lo)
            self._rhs = jnp.concatenate([bh, bh, _block_diag(lo, self.lo)], axis=0)
        return self._rhs


def _pair_mm(p, q):
    return _dot(p.lhs(), q.rhs())


def _rwkv_core_kernel(r_ref, k_ref, v_ref, lw_ref, a_ref, go_ref, kk_ref, ka_ref, rk_ref,
                      lnw_ref, lnb_ref, o_ref, z_scr, *, tt):
    L, N = RWKV_CHUNK, RWKV_HEAD_DIM
    npair = z_scr.shape[0]
    pairs = range(npair)

    @pl.when(pl.program_id(1) == 0)
    def _():
        z_scr[...] = jnp.zeros_like(z_scr)

    ri = lax.broadcasted_iota(jnp.int32, (L, LANES), 0)
    ln = lax.broadcasted_iota(jnp.int32, (L, LANES), 1)
    si = ln % N
    lo = ln < N
    lower_incl = ri >= si
    lower_strict = ri > si
    blk_diag = (ri // RWKV_INV_BLOCK) == (si // RWKV_INV_BLOCK)
    eye = (ri == si).astype(F32)
    tri = (lax.broadcasted_iota(jnp.int32, (L, L), 0) >= lax.broadcasted_iota(jnp.int32, (L, L), 1))
    tri3 = jnp.concatenate([tri.astype(BF16)] * 3, axis=1)
    z_mask = ((lax.broadcasted_iota(jnp.int32, (2 * N, LANES), 0) // N)
              == (lax.broadcasted_iota(jnp.int32, (2 * N, LANES), 1) // N))

    def half_sum(x):
        s0 = jnp.sum(jnp.where(lo, x, 0.0), axis=-1, keepdims=True)
        s1 = jnp.sum(jnp.where(lo, 0.0, x), axis=-1, keepdims=True)
        return jnp.where(lo, s0, s1)

    bd = lambda x: _block_diag(x, lo)
    mk = lambda xs: [_PairMat(x, lo) for x in xs]
    mm = lambda ps, qs: [_pair_mm(p, q) for p, q in zip(ps, qs)]

    def prep(grp, r0):
        idx = range(len(grp))
        sl = [slice(p * LANES, (p + 1) * LANES) for p in grp]
        ld = lambda ref: [ref[pl.ds(r0, L), s] for s in sl]
        r, k, v, lw, a = ld(r_ref), ld(k_ref), ld(v_ref), ld(lw_ref), ld(a_ref)
        kk = [k[i] * kk_ref[:, sl[i]] for i in idx]
        kk = [kk[i] / jnp.maximum(jnp.sqrt(half_sum(kk[i] * kk[i])), 1e-12) for i in idx]
        km = [k[i] * (1.0 + (a[i] - 1.0) * ka_ref[:, sl[i]]) for i in idx]
        bv = [kk[i] * a[i] for i in idx]
        cum = [_cumsum_rows(tri3, lw[i]) for i in idx]
        cum_end = [cum[i][L - 1:L, :] for i in idx]
        w_inv = [jnp.exp(-cum[i]) for i in idx]
        w_out = [jnp.exp(cum_end[i] - cum[i]) for i in idx]
        kk_h = [(kk[i] * jnp.exp(cum[i] - lw[i])).astype(BF16) for i in idx]
        r_h = [(r[i] * jnp.exp(cum[i])).astype(BF16) for i in idx]
        b_t = [(bv[i] * w_inv[i]).astype(BF16) for i in idx]
        k_t = [(km[i] * w_inv[i]).astype(BF16) for i in idx]
        bbar = [(bv[i] * w_out[i]).astype(BF16) for i in idx]
        kbar = [(km[i] * w_out[i]).astype(BF16) for i in idx]
        vb = [v[i].astype(BF16) for i in idx]
        lhs = [jnp.concatenate([kk_h[i], r_h[i]], axis=0) for i in idx]
        ab = [_dot_nt(lhs[i], bd(b_t[i])) for i in idx]
        ak = [_dot_nt(lhs[i], bd(k_t[i])) for i in idx]
        return dict(
            sl=sl, r=r, v=v, km=km, kk_h=kk_h, r_h=r_h, bbar=bbar, kbar=kbar, vb=vb, cum_end=cum_end,
            a_ub=[jnp.where(lower_strict, ab[i][:L], 0.0) for i in idx],
            a_rb=[jnp.where(lower_incl, ab[i][L:], 0.0).astype(BF16) for i in idx],
            a_uk=[jnp.where(lower_strict, ak[i][:L], 0.0).astype(BF16) for i in idx],
            a_rk=[jnp.where(lower_incl, ak[i][L:], 0.0).astype(BF16) for i in idx])

    def inverse(s):
        a_ub = s["a_ub"]
        idx = range(len(a_ub))
        n1 = mk([jnp.where(blk_diag, -a_ub[i], 0.0) for i in idx])
        n2 = mk(mm(n1, n1))
        n4 = mk(mm(n2, n2))
        n8 = mk(mm(n4, n4))
        acc = mk([eye + n1[i].x for i in idx])
        for nk in (n2, n4):
            prod = mm(acc, nk)
            acc = mk([acc[i].x + prod[i] for i in idx])
        prod = mm(acc, n8)
        d_inv = mk([acc[i].x + prod[i] for i in idx])
        l_off = mk([jnp.where(blk_diag, 0.0, a_ub[i]) for i in idx])
        e1 = mk([-x for x in mm(d_inv, l_off)])
        e2 = mk(mm(e1, e1))
        qm = mk([eye + e1[i].x for i in idx])
        prod = mm(qm, e2)
        qm = mk([qm[i].x + prod[i] for i in idx])
        s["t_inv"] = mk(mm(qm, d_inv))

    def state(grp, s):
        idx = range(len(grp))
        z = [z_scr[p] for p in grp]
        zb = [z[i].astype(BF16) for i in idx]
        vb = s["vb"]
        bdv = [bd(vb[i]) for i in idx]
        rhs_u = mk([_dot_nt(s["kk_h"][i], zb[i]) + _dot(s["a_uk"][i], bdv[i]) for i in idx])
        u = [-x for x in mm(s["t_inv"], rhs_u)]
        ub = [u[i].astype(BF16) for i in idx]
        s["y"] = [_dot_nt(s["r_h"][i], zb[i])
                  + _dot(jnp.concatenate([s["a_rb"][i], s["a_rk"][i]], axis=1),
                         jnp.concatenate([bd(ub[i]), bdv[i]], axis=0)) for i in idx]
        upd = [_dot_tn(jnp.concatenate([ub[i], vb[i]], axis=0),
                       jnp.concatenate([s["bbar"][i], s["kbar"][i]], axis=0)) for i in idx]
        for i, p in enumerate(grp):
            z_scr[p] = z[i] * jnp.exp(s["cum_end"][i]) + jnp.where(z_mask, upd[i], 0.0)

    def post(s, r0):
        y, sl, r, v, km = s["y"], s["sl"], s["r"], s["v"], s["km"]
        idx = range(len(y))
        inv_n = 1.0 / N
        mean = [half_sum(y[i]) * inv_n for i in idx]
        yc = [y[i] - mean[i] for i in idx]
        var = [half_sum(yc[i] * yc[i]) * inv_n for i in idx]
        bonus = [half_sum(r[i] * km[i] * rk_ref[:, sl[i]]) * v[i] for i in idx]
        for i in idx:
            yn = yc[i] * lax.rsqrt(var[i] + RWKV_GN_EPS) * lnw_ref[:, sl[i]] + lnb_ref[:, sl[i]]
            o_ref[pl.ds(r0, L), sl[i]] = ((yn + bonus[i]) * go_ref[pl.ds(r0, L), sl[i]]).astype(o_ref.dtype)

    group = list(pairs)

    n_chunks = tt // L
    cur = prep(group, 0)
    inverse(cur)
    for c in range(1, n_chunks):
        nxt = prep(group, c * L)
        state(group, cur)
        inverse(nxt)
        post(cur, (c - 1) * L)
        cur = nxt
    state(group, cur)
    post(cur, (n_chunks - 1) * L)


def _rwkv_core(r, k, v, lw, a, go, k_k, k_a, r_k, ln_w, ln_b, batch, seq_len, tt=256):
    m, d = r.shape
    nt = seq_len // tt
    npair = d // LANES
    blk = pl.BlockSpec((tt, d), lambda b, i: (b * nt + i, 0))
    par = _const_spec((1, d))
    row1 = lambda z: z.reshape(1, d)
    return pl.pallas_call(
        functools.partial(_rwkv_core_kernel, tt=tt),
        out_shape=jax.ShapeDtypeStruct((m, d), BF16),
        grid=(batch, nt),
        in_specs=[blk] * 6 + [par] * 5,
        out_specs=blk,
        scratch_shapes=[pltpu.VMEM((npair, 2 * RWKV_HEAD_DIM, LANES), F32)],
        compiler_params=_cparams(("arbitrary", "arbitrary")),
        name="rwkv_core",
    )(r, k, v, lw, a, go, row1(k_k), row1(k_a), row1(r_k), row1(ln_w), row1(ln_b))


def _rwkv_layer(x, norm_g, mu, w_r, w_k, w_v, w0, w_w1, w_w2, a0, a_w1, a_w2, g_w1, g_w2,
                k_k, k_a, r_k, ln_w, ln_b, batch, seq_len):
    r, k, v, lw, a, go = _rwkv_proj(x, norm_g, mu, w_r, w_k, w_v, w0, w_w1, w_w2, a0, a_w1, a_w2,
                                    g_w1, g_w2, seq_len)
    return _rwkv_core(r, k, v, lw, a, go, k_k, k_a, r_k, ln_w, ln_b, batch, seq_len)


def kernel(x, norm_mixer, norm_ffn, ffn_w_up, ffn_conv_w, ffn_conv_b, ffn_w_down, nsa_w_in, nsa_pe_k, nsa_pe_v, nsa_cmp_k_w1, nsa_cmp_k_w2, nsa_cmp_v_w1, nsa_cmp_v_w2, nsa_b_gate, nsa_w_out, mlstm_w_in, mlstm_conv_w, mlstm_conv_b, mlstm_b_gates, mlstm_norm, mlstm_w_out, rwkv_mu, rwkv_w_r, rwkv_w_k, rwkv_w_v, rwkv_w_o, rwkv_w0, rwkv_w_w1, rwkv_w_w2, rwkv_a0, rwkv_a_w1, rwkv_a_w2, rwkv_g_w1, rwkv_g_w2, rwkv_k_k, rwkv_k_a, rwkv_r_k, rwkv_ln_w, rwkv_ln_b, final_norm):
    batch, seq_len, d = x.shape
    depth = norm_mixer.shape[0]
    rope = _rope_tables(seq_len)
    xf = x.reshape(batch * seq_len, d)
    for i in range(depth):
        kind, j = i % 3, i // 3
        if kind == 0:
            w_o = nsa_w_out[j]
            o = _nsa_layer(xf, norm_mixer[i], nsa_w_in[j], nsa_pe_k[j], nsa_pe_v[j], nsa_cmp_k_w1[j],
                           nsa_cmp_k_w2[j], nsa_cmp_v_w1[j], nsa_cmp_v_w2[j], nsa_b_gate[j], rope, batch, seq_len)
        elif kind == 1:
            w_o = mlstm_w_out[j]
            o = _mlstm_layer(xf, norm_mixer[i], mlstm_w_in[j], mlstm_conv_w[j], mlstm_conv_b[j],
                             mlstm_b_gates[j], mlstm_norm[j], batch, seq_len)
        else:
            w_o = rwkv_w_o[j]
            o = _rwkv_layer(xf, norm_mixer[i], rwkv_mu[j], rwkv_w_r[j], rwkv_w_k[j], rwkv_w_v[j],
                            rwkv_w0[j], rwkv_w_w1[j], rwkv_w_w2[j], rwkv_a0[j],
                            rwkv_a_w1[j], rwkv_a_w2[j], rwkv_g_w1[j], rwkv_g_w2[j], rwkv_k_k[j],
                            rwkv_k_a[j], rwkv_r_k[j], rwkv_ln_w[j], rwkv_ln_b[j], batch, seq_len)
        xf = _ffn(xf, o, w_o, norm_ffn[i], ffn_w_up[i], ffn_conv_w[i], ffn_conv_b[i], ffn_w_down[i],
                  seq_len, final_g=final_norm if i == depth - 1 else None)
    return xf.reshape(batch, seq_len, d)
```

```python
import functools
import math

import jax
import jax.numpy as jnp
import numpy as np
from jax import lax
from jax.experimental import pallas as pl
from jax.experimental.pallas import tpu as pltpu

F32 = jnp.float32
BF16 = jnp.bfloat16

D_MODEL = 1024
NORM_EPS = 1e-6
ROPE_THETA = 500000.0

NSA_HEAD_DIM = 64
NSA_HEADS = 16
NSA_GROUPS = 4
NSA_REP = NSA_HEADS // NSA_GROUPS
NSA_ROT_DIM = 16
CMP_BLOCK = 32
CMP_STRIDE = 16
CMP_HIDDEN = 256
SEL_BLOCK = 64
SEL_TOPK = 16
WINDOW = 512
NSA_KV = NSA_GROUPS * NSA_HEAD_DIM
BF16_SUBLANES = 16
NSA_VT_ROWS = NSA_HEAD_DIM + BF16_SUBLANES
NSA_GATE_ROWS = 16
NSA_WINDOW_QUERIES = 256

MLSTM_HEADS = 8
MLSTM_QK_DIM = 64
MLSTM_V_DIM = 128
MLSTM_TILE = 128
MLSTM_CONV = 4

RWKV_HEAD_DIM = 64
RWKV_GN_EPS = 64e-5
RWKV_CHUNK = 64
RWKV_INV_BLOCK = 16

FFN_DIM = 2816
FFN_CONV = 3
FFN_CHUNK = 256
FFN_ROW_BLOCK = 64

LOG2E = math.log2(math.e)
MASKED = -1e30

LANES = 128
SUBLANES = 8
VMEM_LIMIT = 56 * 1024 * 1024


def _dot(a, b):
    return jnp.dot(a, b, preferred_element_type=F32)


def _dot_nt(a, b):
    return lax.dot_general(a, b, (((1,), (1,)), ((), ())), preferred_element_type=F32)


def _dot_tn(a, b):
    return lax.dot_general(a, b, (((0,), (0,)), ((), ())), preferred_element_type=F32)


def _split2(x):
    hi = x.astype(BF16)
    return hi, (x - hi.astype(F32)).astype(BF16)


def _split3(x):
    hi = x.astype(BF16)
    r1 = x - hi.astype(F32)
    mid = r1.astype(BF16)
    return hi, mid, (r1 - mid.astype(F32)).astype(BF16)


def _cumsum_rows(tri3, x):
    return _dot(tri3, jnp.concatenate(_split3(x), axis=0))


def _cumsum_lanes(x, triu3):
    return _dot(jnp.concatenate(_split3(x), axis=1), triu3)


def _rms(x, g):
    ms = jnp.mean(x * x, axis=-1, keepdims=True)
    return x * lax.rsqrt(ms + NORM_EPS) * g


def _sigmoid(x):
    return 1.0 / (1.0 + jnp.exp(-x))


def _cparams(sem):
    return pltpu.CompilerParams(dimension_semantics=sem, vmem_limit_bytes=VMEM_LIMIT)


def _const_spec(shape):
    n = len(shape)
    return pl.BlockSpec(shape, lambda *_: (0,) * n, pipeline_mode=pl.Buffered(1))


def _ffn_kernel(res_ref, a_ref, wo_ref, g_ref, wu_ref, cw_ref, cb_ref, wd_ref, fg_ref, o_ref,
                h_scr, carry_scr, act_scr, ga_scr, va_scr, gb_scr, vb_scr,
                *, tiles_per_seq, n_chunks, final_norm):
    tm = res_ref.shape[0]
    fc = FFN_CHUNK
    halo = SUBLANES
    x = res_ref[...] + _dot(a_ref[...], wo_ref[...])
    h_scr[...] = _rms(x, g_ref[...]).astype(BF16)
    o_ref[...] = x
    seq_start = (pl.program_id(0) % tiles_per_seq) == 0
    cols = lambda c, base=0: pl.ds(pl.multiple_of(base + c * fc, LANES), fc)

    def up(c, g_scr, v_scr):
        h = h_scr[...]
        g_scr[halo:, :] = _dot(h, wu_ref[:, cols(c)])
        v_scr[...] = _dot(h, wu_ref[:, cols(c, FFN_DIM)])

    def down(c, g_scr, v_scr):
        g_scr[0:halo, :] = jnp.where(seq_start, 0.0, carry_scr[:, cols(c)])
        carry_scr[:, cols(c)] = g_scr[tm:tm + halo, :]
        cw = cw_ref[:, cols(c)]
        cb = cb_ref[:, cols(c)]
        for r0 in range(0, tm, FFN_ROW_BLOCK):
            y = cb + sum(cw[FFN_CONV - 1 - s:FFN_CONV - s, :] * g_scr[halo - s + r0:halo - s + r0 + FFN_ROW_BLOCK, :]
                         for s in range(FFN_CONV))
            act = y * _sigmoid(y) * v_scr[r0:r0 + FFN_ROW_BLOCK, :]
            act_scr[r0:r0 + FFN_ROW_BLOCK, :] = act.astype(BF16)
        o_ref[...] += _dot(act_scr[...], wd_ref[pl.ds(pl.multiple_of(c * fc, fc), fc), :])

    assert n_chunks % 2 == 1

    def chunk_pair(j, carry):
        up(2 * j + 1, gb_scr, vb_scr)
        down(2 * j, ga_scr, va_scr)
        up(2 * j + 2, ga_scr, va_scr)
        down(2 * j + 1, gb_scr, vb_scr)
        return carry

    up(0, ga_scr, va_scr)
    for j in range(n_chunks // 2):
        chunk_pair(j, 0)
    down(n_chunks - 1, ga_scr, va_scr)
    if final_norm:
        o_ref[...] = _rms(o_ref[...], fg_ref[...])


def _ffn(res, a, w_o, g, w_up, conv_w, conv_b, w_down, seq_len, final_g=None, tm=1024):
    m, d = res.shape
    nc = FFN_DIM // FFN_CHUNK
    cw = jnp.pad(conv_w, ((0, SUBLANES - FFN_CONV), (0, 0)))
    row = lambda i: (i, 0)
    fg = jnp.ones((d,), F32) if final_g is None else final_g
    kern = functools.partial(_ffn_kernel, tiles_per_seq=seq_len // tm, n_chunks=nc,
                             final_norm=final_g is not None)
    return pl.pallas_call(
        kern,
        out_shape=jax.ShapeDtypeStruct((m, d), F32),
        grid=(m // tm,),
        in_specs=[pl.BlockSpec((tm, d), row), pl.BlockSpec((tm, d), row), _const_spec((d, d)),
                  _const_spec((1, d)), _const_spec((d, 2 * FFN_DIM)),
                  _const_spec((SUBLANES, FFN_DIM)), _const_spec((1, FFN_DIM)),
                  _const_spec((FFN_DIM, d)), _const_spec((1, d))],
        out_specs=pl.BlockSpec((tm, d), row),
        scratch_shapes=[pltpu.VMEM((tm, d), BF16),
                        pltpu.VMEM((SUBLANES, FFN_DIM), F32),
                        pltpu.VMEM((tm, FFN_CHUNK), BF16)]
                       + [pltpu.VMEM((tm + SUBLANES, FFN_CHUNK), F32), pltpu.VMEM((tm, FFN_CHUNK), F32)] * 2,
        compiler_params=_cparams(("arbitrary",)),
        name="conv_ffn",
    )(res, a, w_o.astype(BF16), g.reshape(1, d), w_up.astype(BF16), cw, conv_b.reshape(1, FFN_DIM),
      w_down.astype(BF16), fg.reshape(1, d))


def _nsa_proj_kernel(x_ref, g_ref, w_ref, wvt_ref, bg_ref, rc_ref, rs1_ref, rs2_ref,
                     qp_ref, qr_ref, kc_ref, vc_ref, ks_ref, vs_ref, kw_ref, vw_ref, gate_ref,
                     cmp_scr, *, tiles_per_seq):
    hn = _rms(x_ref[...], g_ref[...]).astype(BF16)
    y = _dot(hn, w_ref[...])
    vt = _dot_nt(wvt_ref[...], hn)
    rc, rs1, rs2 = rc_ref[...], rs1_ref[...], rs2_ref[...]
    dh = NSA_HEAD_DIM

    def rope(z):
        half = NSA_ROT_DIM // 2
        return z * rc + pltpu.roll(z, half, 1) * rs1 + pltpu.roll(z, LANES - half, 1) * rs2

    scale = dh ** -0.5 * LOG2E
    for j in range(D_MODEL // LANES):
        q = y[:, j * LANES:(j + 1) * LANES] * scale
        qp_ref[:, j * LANES:(j + 1) * LANES] = q.astype(BF16)
        qr_ref[:, j * LANES:(j + 1) * LANES] = rope(q).astype(BF16)

    def kv_chunk(idx):
        return y[:, D_MODEL + idx * NSA_KV:D_MODEL + (idx + 1) * NSA_KV]

    def split_groups(z, ref, dtype):
        for g in range(NSA_GROUPS):
            ref[g] = z[:, g * dh:(g + 1) * dh].astype(dtype)

    def rope256(z):
        return jnp.concatenate([rope(z[:, :LANES]), rope(z[:, LANES:])], axis=1)

    tm = y.shape[0]
    nrow = tm // CMP_STRIDE
    for j in range(2 * NSA_KV // LANES):
        cmp_scr[j] = y[:, D_MODEL + j * LANES:D_MODEL + (j + 1) * LANES]
    for j in range(2 * NSA_KV // LANES):
        ref = kc_ref if j < NSA_KV // LANES else vc_ref
        toks = [cmp_scr[j, pl.ds(tok, nrow, stride=CMP_STRIDE), :] for tok in range(CMP_STRIDE)]
        for half in range(LANES // dh):
            g = (j % (NSA_KV // LANES)) * (LANES // dh) + half
            ref[g] = jnp.concatenate([t[:, half * dh:(half + 1) * dh] for t in toks], axis=1)
    t_pos = (pl.program_id(0) % tiles_per_seq) * tm + lax.broadcasted_iota(jnp.int32, (tm, LANES), 0)
    onehot = (lax.broadcasted_iota(jnp.int32, (tm, LANES), 1) == t_pos // SEL_BLOCK).astype(F32)
    ksel = rope256(kv_chunk(2))
    zpad = jnp.zeros((tm, LANES - dh), F32)
    for g in range(NSA_GROUPS):
        ks_ref[g] = jnp.concatenate([ksel[:, g * dh:(g + 1) * dh], zpad, onehot], axis=1).astype(BF16)
    split_groups(rope256(kv_chunk(3)), kw_ref, BF16)
    ones_pad = (lax.broadcasted_iota(jnp.int32, (NSA_VT_ROWS - dh, tm), 0) == 0).astype(F32)
    for g in range(NSA_GROUPS):
        vs_ref[g] = jnp.concatenate([vt[g * dh:(g + 1) * dh], ones_pad], axis=0).astype(BF16)
        vw_ref[g] = jnp.concatenate([vt[NSA_KV + g * dh:NSA_KV + (g + 1) * dh], ones_pad], axis=0).astype(BF16)
    gate = _sigmoid(vt[2 * NSA_KV:] + bg_ref[...])
    for g in range(NSA_GROUPS):
        gate_ref[g] = gate[g * NSA_GATE_ROWS:(g + 1) * NSA_GATE_ROWS]


def _rope_tables(seq_len):
    half = NSA_ROT_DIM // 2
    inv_freq = ROPE_THETA ** (-jnp.arange(half, dtype=F32) / half)
    ang = jnp.arange(seq_len, dtype=F32)[:, None] * inv_freq[None, :]
    cos, sin = jnp.cos(ang), jnp.sin(ang)
    zeros = jnp.zeros((seq_len, NSA_HEAD_DIM - NSA_ROT_DIM), F32)
    z8 = jnp.zeros((seq_len, half), F32)
    rc = jnp.concatenate([cos, cos, zeros + 1.0], axis=1)
    rs1 = jnp.concatenate([z8, sin, zeros], axis=1)
    rs2 = jnp.concatenate([-sin, z8, zeros], axis=1)
    two = lambda t: jnp.concatenate([t, t], axis=1)
    return two(rc), two(rs1), two(rs2)


def _nsa_proj(x, g, w_in, b_gate, rope, seq_len, tm=512):
    m, d = x.shape
    n_kv = 6 * NSA_KV
    kv = lambda idx: w_in[:, D_MODEL + idx * NSA_KV:D_MODEL + (idx + 1) * NSA_KV]
    pad_g = NSA_GATE_ROWS - NSA_REP * 3
    wg = w_in[:, D_MODEL + n_kv:].reshape(d, NSA_GROUPS, NSA_REP * 3)
    wg = jnp.pad(wg, ((0, 0), (0, 0), (0, pad_g))).reshape(d, NSA_GROUPS * NSA_GATE_ROWS)
    w = jnp.concatenate([w_in[:, :D_MODEL], kv(0), kv(1), kv(2), kv(4)], axis=1).astype(BF16)
    wvt = jnp.concatenate([kv(3), kv(5), wg], axis=1).T.astype(BF16)
    bg = jnp.pad(b_gate.reshape(NSA_GROUPS, NSA_REP * 3), ((0, 0), (0, pad_g)))
    bg = bg.reshape(NSA_GROUPS * NSA_GATE_ROWS, 1)
    n = w.shape[1]
    tps = seq_len // tm
    row = lambda i: (i, 0)
    rope_spec = pl.BlockSpec((tm, LANES), lambda i: (i % tps, 0))
    assert seq_len // SEL_BLOCK <= LANES
    g_out = lambda dt, w=NSA_HEAD_DIM: jax.ShapeDtypeStruct((NSA_GROUPS, m, w), dt)
    g_spec = pl.BlockSpec((NSA_GROUPS, tm, NSA_HEAD_DIM), lambda i: (0, i, 0))
    ks_spec = pl.BlockSpec((NSA_GROUPS, tm, 2 * LANES), lambda i: (0, i, 0))
    cmp_out = jax.ShapeDtypeStruct((NSA_GROUPS, m // CMP_STRIDE, CMP_STRIDE * NSA_HEAD_DIM), F32)
    cmp_spec = pl.BlockSpec((NSA_GROUPS, tm // CMP_STRIDE, CMP_STRIDE * NSA_HEAD_DIM), lambda i: (0, i, 0))
    vt_out = jax.ShapeDtypeStruct((NSA_GROUPS, NSA_VT_ROWS, m), BF16)
    vt_spec = pl.BlockSpec((NSA_GROUPS, NSA_VT_ROWS, tm), lambda i: (0, 0, i))
    return pl.pallas_call(
        functools.partial(_nsa_proj_kernel, tiles_per_seq=tps),
        out_shape=(jax.ShapeDtypeStruct((m, d), BF16), jax.ShapeDtypeStruct((m, d), BF16),
                   cmp_out, cmp_out, g_out(BF16, 2 * LANES), vt_out, g_out(BF16), vt_out,
                   jax.ShapeDtypeStruct((NSA_GROUPS, NSA_GATE_ROWS, m), F32)),
        grid=(m // tm,),
        in_specs=[pl.BlockSpec((tm, d), row), _const_spec((1, d)), _const_spec((d, n)),
                  _const_spec((2 * NSA_KV + NSA_GROUPS * NSA_GATE_ROWS, d)),
                  _const_spec((NSA_GROUPS * NSA_GATE_ROWS, 1)), rope_spec, rope_spec, rope_spec],
        out_specs=(pl.BlockSpec((tm, d), row), pl.BlockSpec((tm, d), row),
                   cmp_spec, cmp_spec, ks_spec, vt_spec, g_spec, vt_spec,
                   pl.BlockSpec((NSA_GROUPS, NSA_GATE_ROWS, tm), lambda i: (0, 0, i))),
        scratch_shapes=[pltpu.VMEM((2 * NSA_KV // LANES, tm, LANES), F32)],
        compiler_params=_cparams(("parallel",)),
        name="nsa_proj",
    )(x, g.reshape(1, d), w, wvt, bg, *rope)


def _gelu_tanh(x):
    return 0.5 * x * (1.0 + jnp.tanh(math.sqrt(2.0 / math.pi) * (x + 0.044715 * (x * x * x))))


def _compress_kernel(zk_ref, zv_ref, pek_ref, pev_ref, w1k_ref, w2k_ref, w1v_ref, w2v_ref,
                     kc_ref, vc_ref):
    nrow = zk_ref.shape[0]
    rows = lax.broadcasted_iota(jnp.int32, (nrow, NSA_HEAD_DIM), 0)

    def one(z_ref, pe_ref, w1_ref, w2_ref, o_ref):
        z = z_ref[...]
        a = _dot((z + pe_ref[0:1, :]).astype(BF16), w1_ref[0])
        b = _dot((z + pe_ref[1:2, :]).astype(BF16), w1_ref[1])
        hid = a + pltpu.roll(b, nrow - 1, 0)
        out = _dot(_gelu_tanh(hid).astype(BF16), w2_ref[...])
        o_ref[...] = jnp.where(rows == nrow - 1, 0.0, out).astype(o_ref.dtype)

    one(zk_ref, pek_ref, w1k_ref, w2k_ref, kc_ref)
    one(zv_ref, pev_ref, w1v_ref, w2v_ref, vc_ref)


def _compress(kc_raw, vc_raw, pe_k, pe_v, w1k, w2k, w1v, w2v, seq_len):
    g, nrows, half = kc_raw.shape
    dh = half // CMP_STRIDE
    nchunk = seq_len // CMP_STRIDE
    zk = kc_raw.reshape(g * nrows, half)
    zv = vc_raw.reshape(g * nrows, half)
    pe2 = lambda pe: pe.reshape(2, half)
    w1 = lambda w: w.astype(BF16).reshape(2, half, CMP_HIDDEN)
    nblk = zk.shape[0] // nchunk
    row = lambda i: (i, 0)
    return pl.pallas_call(
        _compress_kernel,
        out_shape=(jax.ShapeDtypeStruct((zk.shape[0], dh), BF16),
                   jax.ShapeDtypeStruct((zk.shape[0], dh), BF16)),
        grid=(nblk,),
        in_specs=[pl.BlockSpec((nchunk, half), row), pl.BlockSpec((nchunk, half), row),
                  _const_spec((2, half)), _const_spec((2, half)),
                  _const_spec((2, half, CMP_HIDDEN)), _const_spec((CMP_HIDDEN, dh)),
                  _const_spec((2, half, CMP_HIDDEN)), _const_spec((CMP_HIDDEN, dh))],
        out_specs=(pl.BlockSpec((nchunk, dh), row), pl.BlockSpec((nchunk, dh), row)),
        compiler_params=_cparams(("parallel",)),
        name="nsa_compress",
    )(zk, zv, pe2(pe_k), pe2(pe_v), w1(w1k), w2k.astype(BF16), w1(w1v), w2v.astype(BF16))


def _nsa_cmp_kernel(q_ref, kc_ref, vc_ref, gate_ref, ovt_ref, oc_ref, selt_ref, imp_scr, *, tq, n_classes):
    qi = pl.program_id(2)
    nq = pl.num_programs(2)
    dh = NSA_HEAD_DIM
    ncmp = kc_ref.shape[0]
    nsel = imp_scr.shape[0]
    heads = range(NSA_REP)

    def attend(nrows):
        q = q_ref[...]
        qs = jnp.concatenate([q[:, r * dh:(r + 1) * dh] for r in heads], axis=0)
        kc, vc = kc_ref[0:nrows, :], vc_ref[0:nrows, :]
        t = qi * tq + lax.broadcasted_iota(jnp.int32, (nrows, tq), 1)
        cmp_end = lax.broadcasted_iota(jnp.int32, (nrows, tq), 0) * CMP_STRIDE + (CMP_BLOCK - 1)
        neg = jnp.where(cmp_end <= t, 0.0, -jnp.inf)
        st = [_dot_nt(kc, qs[r * tq:(r + 1) * tq]) + neg for r in heads]
        mx = [jnp.max(st[r], axis=0, keepdims=True) for r in heads]
        mx = [jnp.where(mx[r] == -jnp.inf, 0.0, mx[r]) for r in heads]
        e = [jnp.exp2(st[r] - mx[r]) for r in heads]
        p = [e[r] / jnp.maximum(jnp.sum(e[r], axis=0, keepdims=True), 1e-30) for r in heads]
        o_t = [_dot_tn(vc, p[r].astype(BF16)) for r in heads]
        gate = gate_ref[...]
        o_t = [o_t[r] * gate[3 * r:3 * r + 1, :] for r in heads]
        for j in range(NSA_REP // 2):
            pair = jnp.concatenate([o_t[2 * j], o_t[2 * j + 1]], axis=0).T
            oc_ref[:, 2 * j * dh:2 * (j + 1) * dh] = pair
        psum = sum(p)
        ovt = jnp.concatenate([ovt_ref[:, j * ncmp:j * ncmp + nrows] for j in range(3)], axis=1)
        imp_scr[...] = _dot(ovt, jnp.concatenate(_split3(psum), axis=0))

    def select(nblk):
        blk = lax.broadcasted_iota(jnp.int32, (nblk, LANES), 0)
        blk_f = blk.astype(F32)
        if nblk < nsel:
            selt_ref[nblk:, :] = jnp.zeros((nsel - nblk, tq), selt_ref.dtype)
        for cb in range(tq // LANES):
            csl = slice(cb * LANES, (cb + 1) * LANES)
            tb = (qi * tq + cb * LANES + lax.broadcasted_iota(jnp.int32, (nblk, LANES), 1)) // SEL_BLOCK
            forced = (blk == 0) | (blk == tb) | (blk == tb - 1)
            vals = jnp.where(forced, -jnp.inf, jnp.where(blk <= tb, imp_scr[0:nblk, csl], -1.0))
            for _ in range(SEL_TOPK - 3):
                top = jnp.max(vals, axis=0, keepdims=True)
                first = jnp.min(jnp.where(vals == top, blk_f, float(nsel)), axis=0, keepdims=True)
                vals = jnp.where(blk_f == first, -jnp.inf, vals)
            selt_ref[0:nblk, csl] = jnp.where(vals == -jnp.inf, 1.0, 0.0).astype(selt_ref.dtype)

    for cls in range(n_classes):
        @pl.when((qi * n_classes) // nq == cls)
        def _(cls=cls):
            attend((cls + 1) * ncmp // n_classes)
            select((cls + 1) * nsel // n_classes)


def _overlap_matrix_t3(ncmp_pad, nsel):
    c = np.arange(ncmp_pad)[None, :]
    s = np.arange(nsel)[:, None]
    cmp_start = c * CMP_STRIDE
    cmp_end = cmp_start + CMP_BLOCK - 1
    blk_start = s * SEL_BLOCK
    ov = ((cmp_end >= blk_start) & (cmp_start <= blk_start + SEL_BLOCK - 1)).astype(np.float32)
    return jnp.asarray(np.concatenate([ov, ov, ov], axis=1), dtype=BF16)


def _nsa_cmp(qp, kc, vc, gates, batch, seq_len, tq=512):
    m, d = qp.shape
    nq = seq_len // tq
    ncmp = seq_len // CMP_STRIDE
    nsel = seq_len // SEL_BLOCK
    ovt3 = _overlap_matrix_t3(ncmp, nsel)
    qmap = lambda b, g, i: (b * nq + i, g)
    kmap = lambda b, g, i: (g * batch + b, 0)
    n_classes = math.gcd(nq, 4)
    return pl.pallas_call(
        functools.partial(_nsa_cmp_kernel, tq=tq, n_classes=n_classes),
        out_shape=(jax.ShapeDtypeStruct((m, d), F32),
                   jax.ShapeDtypeStruct((NSA_GROUPS, nsel, m), BF16)),
        grid=(batch, NSA_GROUPS, nq),
        in_specs=[pl.BlockSpec((tq, NSA_KV), qmap),
                  pl.BlockSpec((ncmp, NSA_HEAD_DIM), kmap), pl.BlockSpec((ncmp, NSA_HEAD_DIM), kmap),
                  pl.BlockSpec((None, NSA_GATE_ROWS, tq), lambda b, g, i: (g, 0, b * nq + i)),
                  _const_spec((nsel, 3 * ncmp))],
        out_specs=(pl.BlockSpec((tq, NSA_KV), qmap),
                   pl.BlockSpec((None, nsel, tq), lambda b, g, i: (g, 0, b * nq + i))),
        scratch_shapes=[pltpu.VMEM((nsel, tq), F32)],
        compiler_params=_cparams(("parallel", "parallel", "parallel")),
        name="nsa_cmp_topk",
    )(qp, kc, vc, gates, ovt3)


def _nsa_sel_kernel(q_ref, ks_ref, vs_ref, kw_ref, vw_ref, sel_ref, gate_ref, oc_ref, o_ref,
                    m_scr, acc_scr, ow_scr, sta_scr, stb_scr, *, tq, tk):
    qi = pl.program_id(2)
    dh = NSA_HEAD_DIM
    cols = NSA_REP * tq
    q = q_ref[...]
    qs = jnp.concatenate([q[:, r * dh:(r + 1) * dh] for r in range(NSA_REP)], axis=0)
    selt = sel_ref[...].astype(F32)
    nsel = selt.shape[0]
    bmask_t = jnp.where(selt > 0.5, 0.0, MASKED)
    if nsel < LANES:
        bmask_t = jnp.concatenate([bmask_t, jnp.zeros((LANES - nsel, tq), F32)], axis=0)
    bmask = bmask_t.T.astype(BF16)
    zpad = jnp.zeros((tq, LANES - dh), BF16)
    qa = jnp.concatenate([jnp.concatenate([q[:, r * dh:(r + 1) * dh], zpad, bmask], axis=1)
                          for r in range(NSA_REP)], axis=0)
    q0 = qi * tq
    key_iota = lax.broadcasted_iota(jnp.int32, (tk, tq), 0)
    t_pos = q0 + lax.broadcasted_iota(jnp.int32, (tk, tq), 1)

    m_scr[...] = jnp.full((1, cols), -jnp.inf, F32)
    acc_scr[...] = jnp.zeros((NSA_VT_ROWS, cols), F32)

    heads = range(NSA_REP)
    hsl = [slice(r * tq, (r + 1) * tq) for r in heads]

    def put_scores(scr, ki):
        k = ks_ref[pl.ds(pl.multiple_of(ki * tk, tk), tk), :]
        for r in heads:
            scr[r] = _dot_nt(k, qa[hsl[r]])

    def get_scores(scr):
        return [scr[r] for r in heads]

    def consume(ki, st, causal):
        k0 = pl.multiple_of(ki * tk, tk)
        vt = vs_ref[:, pl.ds(k0, tk)]
        if causal:
            neg = jnp.where(k0 + key_iota <= t_pos, 0.0, -jnp.inf)
            st = [s + neg for s in st]
        m_old = [m_scr[:, hsl[r]] for r in heads]
        m_new = [jnp.maximum(m_old[r], jnp.max(st[r], axis=0, keepdims=True)) for r in heads]
        alpha = [jnp.exp2(m_old[r] - m_new[r]) for r in heads]
        p = [jnp.exp2(st[r] - m_new[r]).astype(BF16) for r in heads]
        pv = [_dot(vt, p[r]) for r in heads]
        for r in heads:
            acc_scr[:, hsl[r]] = alpha[r] * acc_scr[:, hsl[r]] + pv[r]
            m_scr[:, hsl[r]] = m_new[r]

    def tile_pair(j, carry):
        even = get_scores(sta_scr)
        put_scores(stb_scr, 2 * j + 1)
        consume(2 * j, even, False)
        odd = get_scores(stb_scr)
        put_scores(sta_scr, 2 * j + 2)
        consume(2 * j + 1, odd, False)
        return carry

    last = (q0 + tq - 1) // tk
    put_scores(sta_scr, 0)

    gate = gate_ref[...]
    wq = min(tq, NSA_WINDOW_QUERIES)
    wsub = wq + WINDOW
    nsub = tq // wq
    rel_iota = (lax.broadcasted_iota(jnp.int32, (wsub, wq), 0)
                - lax.broadcasted_iota(jnp.int32, (wsub, wq), 1))
    starts = [pl.multiple_of(jnp.maximum(q0 + wq * u - WINDOW, 0), wq) for u in range(nsub)]
    kw = [kw_ref[pl.ds(starts[u], wsub), :] for u in range(nsub)]
    vwt = [vw_ref[:, pl.ds(starts[u], wsub)] for u in range(nsub)]
    rel = [starts[u] - (q0 + wq * u) + rel_iota for u in range(nsub)]
    neg_w = [jnp.where((rel[u] <= 0) & (rel[u] > -WINDOW), 0.0, -jnp.inf) for u in range(nsub)]
    subs = [(r, u) for r in heads for u in range(nsub)]
    sw = [_dot_nt(kw[u], qs[r * tq + wq * u:r * tq + wq * (u + 1)]) + neg_w[u] for r, u in subs]
    mw = [jnp.max(s, axis=0, keepdims=True) for s in sw]
    mw = [jnp.where(m == -jnp.inf, 0.0, m) for m in mw]
    ew = [jnp.exp2(s - m).astype(BF16) for s, m in zip(sw, mw)]
    pvw = [_dot(vwt[u], e) for (r, u), e in zip(subs, ew)]
    pvw = [p[0:dh] / jnp.maximum(p[dh:dh + 1], 1e-30) for p in pvw]
    for r in heads:
        o_win = jnp.concatenate(pvw[r * nsub:(r + 1) * nsub], axis=1)
        ow_scr[:, hsl[r]] = o_win * gate[3 * r + 2:3 * r + 3, :]

    lax.fori_loop(0, last // 2, tile_pair, 0)

    @pl.when(last % 2 == 0)
    def _():
        consume(last, get_scores(sta_scr), True)

    @pl.when(last % 2 == 1)
    def _():
        even = get_scores(sta_scr)
        put_scores(stb_scr, last)
        consume(last - 1, even, False)
        consume(last, get_scores(stb_scr), True)
    o_sel = acc_scr[0:dh, :] / jnp.maximum(acc_scr[dh:dh + 1, :], 1e-30)

    mix = [o_sel[:, hsl[r]] * gate[3 * r + 1:3 * r + 2, :] + ow_scr[:, hsl[r]] for r in heads]
    for j in range(NSA_REP // 2):
        psl = slice(2 * j * dh, 2 * (j + 1) * dh)
        pair = jnp.concatenate([mix[2 * j], mix[2 * j + 1]], axis=0).T
        o_ref[:, psl] = (oc_ref[:, psl] + pair).astype(o_ref.dtype)


def _nsa_sel(qr, ks, vs, kw, vw, sel, gates, oc, batch, seq_len, tq=512, tk=512):
    m, d = qr.shape
    nq = seq_len // tq
    nsel = seq_len // SEL_BLOCK
    qmap = lambda b, g, i: (b * nq + i, g)
    kvmap = lambda b, g, i: (g, b, 0)
    kv_spec = pl.BlockSpec((None, seq_len, NSA_HEAD_DIM), kvmap)
    ks_spec = pl.BlockSpec((None, seq_len, 2 * LANES), kvmap)
    vt_spec = pl.BlockSpec((None, NSA_VT_ROWS, seq_len), lambda b, g, i: (g, 0, b))
    cols = NSA_REP * tq
    assert tk % tq == 0
    return pl.pallas_call(
        functools.partial(_nsa_sel_kernel, tq=tq, tk=tk),
        out_shape=jax.ShapeDtypeStruct((m, d), BF16),
        grid=(batch, NSA_GROUPS, nq),
        in_specs=[pl.BlockSpec((tq, NSA_KV), qmap), ks_spec, vt_spec, kv_spec, vt_spec,
                  pl.BlockSpec((None, nsel, tq), lambda b, g, i: (g, 0, b * nq + i)),
                  pl.BlockSpec((None, NSA_GATE_ROWS, tq), lambda b, g, i: (g, 0, b * nq + i)),
                  pl.BlockSpec((tq, NSA_KV), qmap)],
        out_specs=pl.BlockSpec((tq, NSA_KV), qmap),
        scratch_shapes=[pltpu.VMEM((1, cols), F32), pltpu.VMEM((NSA_VT_ROWS, cols), F32),
                        pltpu.VMEM((NSA_HEAD_DIM, cols), F32),
                        pltpu.VMEM((NSA_REP, tk, tq), F32), pltpu.VMEM((NSA_REP, tk, tq), F32)],
        compiler_params=_cparams(("parallel", "parallel", "parallel")),
        name="nsa_sel_win",
    )(qr, ks, vs, kw, vw, sel, gates, oc)


def _nsa_layer(x, norm_g, w_in, pe_k, pe_v, w1k, w2k, w1v, w2v, b_gate, rope, batch, seq_len):
    qp, qr, kc_raw, vc_raw, ks, vs, kw, vw, gates = _nsa_proj(x, norm_g, w_in, b_gate, rope, seq_len)
    kc, vc = _compress(kc_raw, vc_raw, pe_k, pe_v, w1k, w2k, w1v, w2v, seq_len)
    oc, sel = _nsa_cmp(qp, kc, vc, gates, batch, seq_len)
    return _nsa_sel(qr, ks, vs, kw, vw, sel, gates, oc, batch, seq_len)


def _mlstm_proj_kernel(x_ref, g_ref, w_ref, wot_ref, wt_ref, bcol_ref, brow_ref,
                       qk_ref, v_ref, ot_ref, gc_ref, gr_ref):
    hn = _rms(x_ref[...], g_ref[...]).astype(BF16)
    y = _dot(hn, w_ref[...])
    d = D_MODEL
    qk_ref[...] = y[:, :d]
    v_ref[...] = y[:, d:2 * d].astype(BF16)
    ot_ref[...] = _sigmoid(_dot_nt(wot_ref[...], hn))
    gc_ref[...] = y[:, 2 * d:] + bcol_ref[...]
    gr_ref[...] = _dot_nt(wt_ref[...], hn) + brow_ref[...]


def _mlstm_proj(x, g, w_in, b_gates, tm=512):
    m, d = x.shape
    h, dk = MLSTM_HEADS, MLSTM_QK_DIM
    wq = w_in[:, :h * dk].reshape(d, h, dk)
    wk = w_in[:, h * dk:2 * h * dk].reshape(d, h, dk)
    wqk = jnp.concatenate([wq, wk], axis=2).reshape(d, 2 * h * dk)
    wv = w_in[:, d:2 * d]
    wif = w_in[:, 2 * d:2 * d + 2 * h]
    wo = w_in[:, 2 * d + 2 * h:]
    w = jnp.concatenate([wqk, wv, jnp.pad(wif, ((0, 0), (0, LANES - 2 * h)))], axis=1).astype(BF16)
    wot = wo.T.astype(BF16)
    wt = wif.T.astype(BF16)
    bcol = jnp.pad(b_gates, (0, LANES - 2 * h)).reshape(1, LANES)
    brow = b_gates.reshape(2 * h, 1)
    n = w.shape[1]
    row = lambda i: (i, 0)
    col = lambda i: (0, i)
    return pl.pallas_call(
        _mlstm_proj_kernel,
        out_shape=(jax.ShapeDtypeStruct((m, d), F32), jax.ShapeDtypeStruct((m, d), BF16),
                   jax.ShapeDtypeStruct((d, m), F32), jax.ShapeDtypeStruct((m, LANES), F32),
                   jax.ShapeDtypeStruct((2 * h, m), F32)),
        grid=(m // tm,),
        in_specs=[pl.BlockSpec((tm, d), row), _const_spec((1, d)), _const_spec((d, n)), _const_spec((d, d)),
                  _const_spec((2 * h, d)), _const_spec((1, LANES)), _const_spec((2 * h, 1))],
        out_specs=(pl.BlockSpec((tm, d), row), pl.BlockSpec((tm, d), row), pl.BlockSpec((d, tm), col),
                   pl.BlockSpec((tm, LANES), row), pl.BlockSpec((2 * h, tm), col)),
        compiler_params=_cparams(("parallel",)),
        name="mlstm_proj",
    )(x, g.reshape(1, d), w, wot, wt, bcol, brow)


def _log_sigmoid(x):
    return jnp.minimum(x, 0.0) - jnp.log(1.0 + jnp.exp(-jnp.abs(x)))


def _mlstm_core_kernel(qk_ref, v_ref, og_ref, gc_ref, gr_ref, cw_ref, cb_ref, ng_ref, o_ref,
                       qkc_scr, ext_scr, c_scr, n_scr, m_scr, *, tt):
    L = MLSTM_TILE
    H, dk, dv = MLSTM_HEADS, MLSTM_QK_DIM, MLSTM_V_DIM
    seq_start = pl.program_id(1) == 0

    @pl.when(seq_start)
    def _():
        c_scr[...] = jnp.zeros_like(c_scr)
        n_scr[...] = jnp.zeros_like(n_scr)
        m_scr[...] = jnp.zeros_like(m_scr)
        ext_scr[0:SUBLANES, :] = jnp.zeros((SUBLANES, ext_scr.shape[1]), F32)

    ext_scr[SUBLANES:, :] = qk_ref[...]
    is_k = lax.broadcasted_iota(jnp.int32, (tt, 2 * dk), 1) >= dk
    for h in range(H):
        hsl = slice(h * 2 * dk, (h + 1) * 2 * dk)
        cw = cw_ref[:, hsl]
        acc = cb_ref[:, hsl] + cw[MLSTM_CONV - 1:MLSTM_CONV, :] * ext_scr[SUBLANES:, hsl]
        for s in range(1, MLSTM_CONV):
            acc = acc + cw[MLSTM_CONV - 1 - s:MLSTM_CONV - s, :] * ext_scr[SUBLANES - s:SUBLANES - s + tt, hsl]
        act = acc * _sigmoid(acc)
        qkc_scr[:, hsl] = jnp.where(is_k, act * dk ** -0.5, act)
    ext_scr[0:SUBLANES, :] = ext_scr[tt:tt + SUBLANES, :]

    src = lax.broadcasted_iota(jnp.int32, (L, L), 0)
    tgt = lax.broadcasted_iota(jnp.int32, (L, L), 1)
    causal = src <= tgt
    tri3 = jnp.concatenate([(src >= tgt).astype(BF16)] * 3, axis=1)
    triu3 = jnp.concatenate([causal.astype(BF16)] * 3, axis=0)
    n_pad = jnp.zeros((SUBLANES - 3, dk), BF16)

    def chunk(c, carry):
        r0 = pl.multiple_of(c * L, L)
        gcol = gc_ref[pl.ds(r0, L), :]
        grow = gr_ref[c]
        b_col = _cumsum_rows(tri3, _log_sigmoid(gcol))
        b_row = _cumsum_lanes(_log_sigmoid(grow), triu3)
        hs = range(H)
        vsl = [slice(h * dv, (h + 1) * dv) for h in hs]
        qk = [qkc_scr[pl.ds(r0, L), h * 2 * dk:(h + 1) * 2 * dk] for h in hs]
        q = [qk[h][:, :dk].astype(BF16) for h in hs]
        k = [qk[h][:, dk:] for h in hs]
        v = [v_ref[pl.ds(r0, L), vsl[h]] for h in hs]
        col = [gcol[:, h:h + 1] - b_col[:, H + h:H + h + 1] for h in hs]
        li_r = [grow[h:h + 1, :] for h in hs]
        b_r = [b_row[H + h:H + h + 1, :] for h in hs]
        b_end = [b_r[h][:, L - 1:L] for h in hs]
        dmat = [jnp.where(causal, b_r[h] + col[h], -jnp.inf) for h in hs]
        d_max = [jnp.max(dmat[h], axis=0, keepdims=True) for h in hs]
        att = [jnp.exp(dmat[h] - d_max[h]) * _dot_nt(k[h].astype(BF16), q[h]) for h in hs]
        a_sum = [jnp.sum(att[h], axis=0, keepdims=True) for h in hs]
        intra = [_dot_tn(v[h], att[h].astype(BF16)) for h in hs]
        g_max = [jnp.max(b_end[h] - b_r[h] + li_r[h], axis=-1, keepdims=True) for h in hs]
        kw = [k[h] * jnp.exp(b_end[h] + col[h] - g_max[h]) for h in hs]
        c_loc = [_dot_tn(v[h], kw[h].astype(BF16)) for h in hs]
        n_loc = [jnp.sum(kw[h], axis=0, keepdims=True) for h in hs]
        m_prev = [m_scr[h:h + 1, 0:1] for h in hs]
        c_prev = [c_scr[h] for h in hs]
        n_prev = [n_scr[h:h + 1, :] for h in hs]
        m_inter = [b_r[h] + m_prev[h] for h in hs]
        m_t = [jnp.maximum(m_inter[h], d_max[h]) for h in hs]
        w_loc = [jnp.exp(d_max[h] - m_t[h]) for h in hs]
        w_int = [jnp.exp(m_inter[h] - m_t[h]) for h in hs]
        q_c = [_dot_nt(c_prev[h].astype(BF16), q[h]) for h in hs]
        q_n = [_dot_nt(jnp.concatenate(list(_split3(n_prev[h])) + [n_pad], axis=0), q[h]) for h in hs]
        q_n = [q_n[h][0:1] + q_n[h][1:2] + q_n[h][2:3] for h in hs]
        num = [w_loc[h] * intra[h] + w_int[h] * q_c[h] for h in hs]
        den = [w_loc[h] * a_sum[h] + w_int[h] * q_n[h] for h in hs]
        h_t = [num[h] / jnp.maximum(jnp.abs(den[h]), jnp.exp(-m_t[h])) for h in hs]
        h_t = [h_t[h] * lax.rsqrt(jnp.mean(h_t[h] * h_t[h], axis=0, keepdims=True) + NORM_EPS) for h in hs]
        m_new = [jnp.maximum(b_end[h] + m_prev[h], g_max[h]) for h in hs]
        a = [jnp.exp(b_end[h] + m_prev[h] - m_new[h]) for h in hs]
        sc = [jnp.exp(g_max[h] - m_new[h]) for h in hs]
        for h in hs:
            out = h_t[h] * ng_ref[vsl[h], :] * og_ref[vsl[h], pl.ds(r0, L)]
            o_ref[pl.ds(r0, L), vsl[h]] = out.T.astype(o_ref.dtype)
            c_scr[h] = a[h] * c_prev[h] + sc[h] * c_loc[h]
            n_scr[h:h + 1, :] = a[h] * n_prev[h] + sc[h] * n_loc[h]
            m_scr[h:h + 1, :] = jnp.broadcast_to(m_new[h], (1, LANES))
        return carry

    lax.fori_loop(0, tt // L, chunk, 0)


def _mlstm_core(qk, v, ogt, gcol, grow3, conv_w, conv_b, norm_gb, batch, seq_len, tt=256):
    m, d = qk.shape
    H, dk, dv = MLSTM_HEADS, MLSTM_QK_DIM, MLSTM_V_DIM
    nt = seq_len // tt
    ncs = tt // MLSTM_TILE
    row = lambda b, i: (b * nt + i, 0)
    return pl.pallas_call(
        functools.partial(_mlstm_core_kernel, tt=tt),
        out_shape=jax.ShapeDtypeStruct((m, d), BF16),
        grid=(batch, nt),
        in_specs=[pl.BlockSpec((tt, d), row), pl.BlockSpec((tt, d), row),
                  pl.BlockSpec((d, tt), lambda b, i: (0, b * nt + i)),
                  pl.BlockSpec((tt, LANES), row),
                  pl.BlockSpec((ncs, 2 * H, MLSTM_TILE), lambda b, i: (b * nt + i, 0, 0)),
                  _const_spec((SUBLANES, d)), _const_spec((1, d)), _const_spec((d, LANES))],
        out_specs=pl.BlockSpec((tt, d), row),
        scratch_shapes=[pltpu.VMEM((tt, d), F32), pltpu.VMEM((tt + SUBLANES, d), F32),
                        pltpu.VMEM((H, dv, dk), F32), pltpu.VMEM((H, dk), F32),
                        pltpu.VMEM((H, LANES), F32)],
        compiler_params=_cparams(("arbitrary", "arbitrary")),
        name="mlstm_core",
    )(qk, v, ogt, gcol, grow3, conv_w, conv_b, norm_gb)


def _mlstm_layer(x, norm_g, w_in, conv_w, conv_b, b_gates, hnorm, batch, seq_len):
    H, dk = MLSTM_HEADS, MLSTM_QK_DIM
    qk, v, ogt, gcol, grow = _mlstm_proj(x, norm_g, w_in, b_gates)
    m = x.shape[0]
    grow3 = grow.reshape(2 * H, m // MLSTM_TILE, MLSTM_TILE).transpose(1, 0, 2)
    norm_gb = jnp.broadcast_to(hnorm.reshape(-1, 1), (hnorm.shape[0], LANES))

    def perm(z):
        lead = z.shape[:-1]
        zq = z[..., :H * dk].reshape(*lead, H, dk)
        zk = z[..., H * dk:].reshape(*lead, H, dk)
        return jnp.concatenate([zq, zk], axis=-1).reshape(*lead, 2 * H * dk)

    cw = jnp.pad(perm(conv_w), ((0, SUBLANES - MLSTM_CONV), (0, 0)))
    return _mlstm_core(qk, v, ogt, gcol, grow3, cw, perm(conv_b).reshape(1, -1), norm_gb,
                       batch, seq_len)


def _softplus(x):
    return jnp.maximum(x, 0.0) + jnp.log(1.0 + jnp.exp(-jnp.abs(x)))


def _rwkv_proj_kernel(x_ref, g_ref, mu_ref, wr_ref, wk_ref, wv_ref, ww1_ref, ww2_ref, w0_ref,
                      aw1_ref, aw2_ref, a0_ref, gw1_ref, gw2_ref,
                      r_ref, k_ref, v_ref, lw_ref, a_ref, go_ref, prev_scr, *, tiles_per_seq):
    tm = x_ref.shape[0]
    h = _rms(x_ref[...], g_ref[...])
    seq_start = (pl.program_id(0) % tiles_per_seq) == 0
    prev = jnp.where(seq_start, 0.0, prev_scr[SUBLANES - 1:SUBLANES, :])
    prev_scr[...] = h[tm - SUBLANES:, :]
    rows = lax.broadcasted_iota(jnp.int32, h.shape, 0)
    xx = jnp.where(rows == 0, prev, pltpu.roll(h, 1, 0)) - h
    mix = lambda j: (h + xx * mu_ref[j:j + 1, :]).astype(BF16)
    r_ref[...] = _dot(mix(0), wr_ref[...])
    k_ref[...] = _dot(mix(2), wk_ref[...])
    v_ref[...] = _dot(mix(3), wv_ref[...])
    wl = _dot(jnp.tanh(_dot(mix(1), ww1_ref[...])).astype(BF16), ww2_ref[...]) + w0_ref[...]
    w_log = -_softplus(-wl) - 0.5
    lw_ref[...] = -jnp.exp(w_log)
    al = _dot(_dot(mix(4), aw1_ref[...]).astype(BF16), aw2_ref[...]) + a0_ref[...]
    a_ref[...] = _sigmoid(al)
    go_ref[...] = _dot(_sigmoid(_dot(mix(5), gw1_ref[...])).astype(BF16), gw2_ref[...])


def _rwkv_proj(x, g, mu, w_r, w_k, w_v, w0, w_w1, w_w2, a0, a_w1, a_w2, g_w1, g_w2, seq_len, tm=512):
    m, d = x.shape
    row = lambda i: (i, 0)
    bf = lambda w: w.astype(BF16)
    mu8 = jnp.pad(mu, ((0, SUBLANES - mu.shape[0]), (0, 0)))
    consts = [g.reshape(1, d), mu8, bf(w_r), bf(w_k), bf(w_v), bf(w_w1), bf(w_w2), w0.reshape(1, d),
              bf(a_w1), bf(a_w2), a0.reshape(1, d), bf(g_w1), bf(g_w2)]
    out = jax.ShapeDtypeStruct((m, d), F32)
    return pl.pallas_call(
        functools.partial(_rwkv_proj_kernel, tiles_per_seq=seq_len // tm),
        out_shape=(out,) * 6,
        grid=(m // tm,),
        in_specs=[pl.BlockSpec((tm, d), row)] + [_const_spec(c.shape) for c in consts],
        out_specs=(pl.BlockSpec((tm, d), row),) * 6,
        scratch_shapes=[pltpu.VMEM((SUBLANES, d), F32)],
        compiler_params=_cparams(("arbitrary",)),
        name="rwkv_proj",
    )(x, *consts)


def _block_diag(x, lo):
    zero = jnp.zeros_like(x)
    return jnp.concatenate([jnp.where(lo, x, zero), jnp.where(lo, zero, x)], axis=0)


class _PairMat:
    def __init__(self, x, lo):
        self.x, self.lo = x, lo
        self._lhs = self._rhs = None

    def lhs(self):
        if self._lhs is None:
            hi, lo = _split2(self.x)
            self._lhs = jnp.concatenate([hi, lo, hi], axis=1)
        return self._lhs

    def rhs(self):
        if self._rhs is None:
            hi, lo = _split2(self.x)
            bh = _block_diag(hi, self.lo)
            self._rhs = jnp.concatenate([bh, bh, _block_diag(lo, self.lo)], axis=0)
        return self._rhs


def _pair_mm(p, q):
    return _dot(p.lhs(), q.rhs())


def _rwkv_core_kernel(r_ref, k_ref, v_ref, lw_ref, a_ref, go_ref, kk_ref, ka_ref, rk_ref,
                      lnw_ref, lnb_ref, o_ref, z_scr, *, tt):
    L, N = RWKV_CHUNK, RWKV_HEAD_DIM
    npair = z_scr.shape[0]
    pairs = range(npair)

    @pl.when(pl.program_id(1) == 0)
    def _():
        z_scr[...] = jnp.zeros_like(z_scr)

    ri = lax.broadcasted_iota(jnp.int32, (L, LANES), 0)
    ln = lax.broadcasted_iota(jnp.int32, (L, LANES), 1)
    si = ln % N
    lo = ln < N
    lower_incl = ri >= si
    lower_strict = ri > si
    blk_diag = (ri // RWKV_INV_BLOCK) == (si // RWKV_INV_BLOCK)
    eye = (ri == si).astype(F32)
    tri = (lax.broadcasted_iota(jnp.int32, (L, L), 0) >= lax.broadcasted_iota(jnp.int32, (L, L), 1))
    tri3 = jnp.concatenate([tri.astype(BF16)] * 3, axis=1)
    z_mask = ((lax.broadcasted_iota(jnp.int32, (2 * N, LANES), 0) // N)
              == (lax.broadcasted_iota(jnp.int32, (2 * N, LANES), 1) // N))

    def half_sum(x):
        s0 = jnp.sum(jnp.where(lo, x, 0.0), axis=-1, keepdims=True)
        s1 = jnp.sum(jnp.where(lo, 0.0, x), axis=-1, keepdims=True)
        return jnp.where(lo, s0, s1)

    bd = lambda x: _block_diag(x, lo)
    mk = lambda xs: [_PairMat(x, lo) for x in xs]
    mm = lambda ps, qs: [_pair_mm(p, q) for p, q in zip(ps, qs)]

    def prep(grp, r0):
        idx = range(len(grp))
        sl = [slice(p * LANES, (p + 1) * LANES) for p in grp]
        ld = lambda ref: [ref[pl.ds(r0, L), s] for s in sl]
        r, k, v, lw, a = ld(r_ref), ld(k_ref), ld(v_ref), ld(lw_ref), ld(a_ref)
        kk = [k[i] * kk_ref[:, sl[i]] for i in idx]
        kk = [kk[i] / jnp.maximum(jnp.sqrt(half_sum(kk[i] * kk[i])), 1e-12) for i in idx]
        km = [k[i] * (1.0 + (a[i] - 1.0) * ka_ref[:, sl[i]]) for i in idx]
        bv = [kk[i] * a[i] for i in idx]
        cum = [_cumsum_rows(tri3, lw[i]) for i in idx]
        cum_end = [cum[i][L - 1:L, :] for i in idx]
        w_inv = [jnp.exp(-cum[i]) for i in idx]
        w_out = [jnp.exp(cum_end[i] - cum[i]) for i in idx]
        kk_h = [(kk[i] * jnp.exp(cum[i] - lw[i])).astype(BF16) for i in idx]
        r_h = [(r[i] * jnp.exp(cum[i])).astype(BF16) for i in idx]
        b_t = [(bv[i] * w_inv[i]).astype(BF16) for i in idx]
        k_t = [(km[i] * w_inv[i]).astype(BF16) for i in idx]
        bbar = [(bv[i] * w_out[i]).astype(BF16) for i in idx]
        kbar = [(km[i] * w_out[i]).astype(BF16) for i in idx]
        vb = [v[i].astype(BF16) for i in idx]
        lhs = [jnp.concatenate([kk_h[i], r_h[i]], axis=0) for i in idx]
        ab = [_dot_nt(lhs[i], bd(b_t[i])) for i in idx]
        ak = [_dot_nt(lhs[i], bd(k_t[i])) for i in idx]
        return dict(
            sl=sl, r=r, v=v, km=km, kk_h=kk_h, r_h=r_h, bbar=bbar, kbar=kbar, vb=vb, cum_end=cum_end,
            a_ub=[jnp.where(lower_strict, ab[i][:L], 0.0) for i in idx],
            a_rb=[jnp.where(lower_incl, ab[i][L:], 0.0).astype(BF16) for i in idx],
            a_uk=[jnp.where(lower_strict, ak[i][:L], 0.0).astype(BF16) for i in idx],
            a_rk=[jnp.where(lower_incl, ak[i][L:], 0.0).astype(BF16) for i in idx])

    def inverse(s):
        a_ub = s["a_ub"]
        idx = range(len(a_ub))
        n1 = mk([jnp.where(blk_diag, -a_ub[i], 0.0) for i in idx])
        n2 = mk(mm(n1, n1))
        n4 = mk(mm(n2, n2))
        n8 = mk(mm(n4, n4))
        acc = mk([eye + n1[i].x for i in idx])
        for nk in (n2, n4):
            prod = mm(acc, nk)
            acc = mk([acc[i].x + prod[i] for i in idx])
        prod = mm(acc, n8)
        d_inv = mk([acc[i].x + prod[i] for i in idx])
        l_off = mk([jnp.where(blk_diag, 0.0, a_ub[i]) for i in idx])
        e1 = mk([-x for x in mm(d_inv, l_off)])
        e2 = mk(mm(e1, e1))
        qm = mk([eye + e1[i].x for i in idx])
        prod = mm(qm, e2)
        qm = mk([qm[i].x + prod[i] for i in idx])
        s["t_inv"] = mk(mm(qm, d_inv))

    def state(grp, s):
        idx = range(len(grp))
        z = [z_scr[p] for p in grp]
        zb = [z[i].astype(BF16) for i in idx]
        vb = s["vb"]
        bdv = [bd(vb[i]) for i in idx]
        rhs_u = mk([_dot_nt(s["kk_h"][i], zb[i]) + _dot(s["a_uk"][i], bdv[i]) for i in idx])
        u = [-x for x in mm(s["t_inv"], rhs_u)]
        ub = [u[i].astype(BF16) for i in idx]
        s["y"] = [_dot_nt(s["r_h"][i], zb[i])
                  + _dot(jnp.concatenate([s["a_rb"][i], s["a_rk"][i]], axis=1),
                         jnp.concatenate([bd(ub[i]), bdv[i]], axis=0)) for i in idx]
        upd = [_dot_tn(jnp.concatenate([ub[i], vb[i]], axis=0),
                       jnp.concatenate([s["bbar"][i], s["kbar"][i]], axis=0)) for i in idx]
        for i, p in enumerate(grp):
            z_scr[p] = z[i] * jnp.exp(s["cum_end"][i]) + jnp.where(z_mask, upd[i], 0.0)

    def post(s, r0):
        y, sl, r, v, km = s["y"], s["sl"], s["r"], s["v"], s["km"]
        idx = range(len(y))
        inv_n = 1.0 / N
        mean = [half_sum(y[i]) * inv_n for i in idx]
        yc = [y[i] - mean[i] for i in idx]
        var = [half_sum(yc[i] * yc[i]) * inv_n for i in idx]
        bonus = [half_sum(r[i] * km[i] * rk_ref[:, sl[i]]) * v[i] for i in idx]
        for i in idx:
            yn = yc[i] * lax.rsqrt(var[i] + RWKV_GN_EPS) * lnw_ref[:, sl[i]] + lnb_ref[:, sl[i]]
            o_ref[pl.ds(r0, L), sl[i]] = ((yn + bonus[i]) * go_ref[pl.ds(r0, L), sl[i]]).astype(o_ref.dtype)

    group = list(pairs)

    n_chunks = tt // L
    cur = prep(group, 0)
    inverse(cur)
    for c in range(1, n_chunks):
        nxt = prep(group, c * L)
        state(group, cur)
        inverse(nxt)
        post(cur, (c - 1) * L)
        cur = nxt
    state(group, cur)
    post(cur, (n_chunks - 1) * L)


def _rwkv_core(r, k, v, lw, a, go, k_k, k_a, r_k, ln_w, ln_b, batch, seq_len, tt=256):
    m, d = r.shape
    nt = seq_len // tt
    npair = d // LANES
    blk = pl.BlockSpec((tt, d), lambda b, i: (b * nt + i, 0))
    par = _const_spec((1, d))
    row1 = lambda z: z.reshape(1, d)
    return pl.pallas_call(
        functools.partial(_rwkv_core_kernel, tt=tt),
        out_shape=jax.ShapeDtypeStruct((m, d), BF16),
        grid=(batch, nt),
        in_specs=[blk] * 6 + [par] * 5,
        out_specs=blk,
        scratch_shapes=[pltpu.VMEM((npair, 2 * RWKV_HEAD_DIM, LANES), F32)],
        compiler_params=_cparams(("arbitrary", "arbitrary")),
        name="rwkv_core",
    )(r, k, v, lw, a, go, row1(k_k), row1(k_a), row1(r_k), row1(ln_w), row1(ln_b))


def _rwkv_layer(x, norm_g, mu, w_r, w_k, w_v, w0, w_w1, w_w2, a0, a_w1, a_w2, g_w1, g_w2,
                k_k, k_a, r_k, ln_w, ln_b, batch, seq_len):
    r, k, v, lw, a, go = _rwkv_proj(x, norm_g, mu, w_r, w_k, w_v, w0, w_w1, w_w2, a0, a_w1, a_w2,
                                    g_w1, g_w2, seq_len)
    return _rwkv_core(r, k, v, lw, a, go, k_k, k_a, r_k, ln_w, ln_b, batch, seq_len)


def kernel(x, norm_mixer, norm_ffn, ffn_w_up, ffn_conv_w, ffn_conv_b, ffn_w_down, nsa_w_in, nsa_pe_k, nsa_pe_v, nsa_cmp_k_w1, nsa_cmp_k_w2, nsa_cmp_v_w1, nsa_cmp_v_w2, nsa_b_gate, nsa_w_out, mlstm_w_in, mlstm_conv_w, mlstm_conv_b, mlstm_b_gates, mlstm_norm, mlstm_w_out, rwkv_mu, rwkv_w_r, rwkv_w_k, rwkv_w_v, rwkv_w_o, rwkv_w0, rwkv_w_w1, rwkv_w_w2, rwkv_a0, rwkv_a_w1, rwkv_a_w2, rwkv_g_w1, rwkv_g_w2, rwkv_k_k, rwkv_k_a, rwkv_r_k, rwkv_ln_w, rwkv_ln_b, final_norm):
    batch, seq_len, d = x.shape
    depth = norm_mixer.shape[0]
    rope = _rope_tables(seq_len)
    xf = x.reshape(batch * seq_len, d)
    for i in range(depth):
        kind, j = i % 3, i // 3
        if kind == 0:
            w_o = nsa_w_out[j]
            o = _nsa_layer(xf, norm_mixer[i], nsa_w_in[j], nsa_pe_k[j], nsa_pe_v[j], nsa_cmp_k_w1[j],
                           nsa_cmp_k_w2[j], nsa_cmp_v_w1[j], nsa_cmp_v_w2[j], nsa_b_gate[j], rope, batch, seq_len)
        elif kind == 1:
            w_o = mlstm_w_out[j]
            o = _mlstm_layer(xf, norm_mixer[i], mlstm_w_in[j], mlstm_conv_w[j], mlstm_conv_b[j],
                             mlstm_b_gates[j], mlstm_norm[j], batch, seq_len)
        else:
            w_o = rwkv_w_o[j]
            o = _rwkv_layer(xf, norm_mixer[i], rwkv_mu[j], rwkv_w_r[j], rwkv_w_k[j], rwkv_w_v[j],
                            rwkv_w0[j], rwkv_w_w1[j], rwkv_w_w2[j], rwkv_a0[j],
                            rwkv_a_w1[j], rwkv_a_w2[j], rwkv_g_w1[j], rwkv_g_w2[j], rwkv_k_k[j],
                            rwkv_k_a[j], rwkv_r_k[j], rwkv_ln_w[j], rwkv_ln_b[j], batch, seq_len)
        xf = _ffn(xf, o, w_o, norm_ffn[i], ffn_w_up[i], ffn_conv_w[i], ffn_conv_b[i], ffn_w_down[i],
                  seq_len, final_g=final_norm if i == depth - 1 else None)
    return xf.reshape(batch, seq_len, d)
```

```python
import functools
import math

import jax
import jax.numpy as jnp
import numpy as np
from jax import lax
from jax.experimental import pallas as pl
from jax.experimental.pallas import tpu as pltpu

F32 = jnp.float32
BF16 = jnp.bfloat16

D_MODEL = 1024
NORM_EPS = 1e-6
ROPE_THETA = 500000.0

NSA_HEAD_DIM = 64
NSA_HEADS = 16
NSA_GROUPS = 4
NSA_REP = NSA_HEADS // NSA_GROUPS
NSA_ROT_DIM = 16
CMP_BLOCK = 32
CMP_STRIDE = 16
CMP_HIDDEN = 256
SEL_BLOCK = 64
SEL_TOPK = 16
WINDOW = 512
NSA_KV = NSA_GROUPS * NSA_HEAD_DIM
BF16_SUBLANES = 16
NSA_VT_ROWS = NSA_HEAD_DIM + BF16_SUBLANES
NSA_GATE_ROWS = 16
NSA_WINDOW_QUERIES = 256

MLSTM_HEADS = 8
MLSTM_QK_DIM = 64
MLSTM_V_DIM = 128
MLSTM_TILE = 128
MLSTM_CONV = 4

RWKV_HEAD_DIM = 64
RWKV_GN_EPS = 64e-5
RWKV_CHUNK = 64
RWKV_INV_BLOCK = 16

FFN_DIM = 2816
FFN_CONV = 3
FFN_CHUNK = 256
FFN_ROW_BLOCK = 64

LOG2E = math.log2(math.e)
MASKED = -1e30

LANES = 128
SUBLANES = 8
VMEM_LIMIT = 56 * 1024 * 1024


def _dot(a, b):
    return jnp.dot(a, b, preferred_element_type=F32)


def _dot_nt(a, b):
    return lax.dot_general(a, b, (((1,), (1,)), ((), ())), preferred_element_type=F32)


def _dot_tn(a, b):
    return lax.dot_general(a, b, (((0,), (0,)), ((), ())), preferred_element_type=F32)


def _split2(x):
    hi = x.astype(BF16)
    return hi, (x - hi.astype(F32)).astype(BF16)


def _split3(x):
    hi = x.astype(BF16)
    r1 = x - hi.astype(F32)
    mid = r1.astype(BF16)
    return hi, mid, (r1 - mid.astype(F32)).astype(BF16)


def _cumsum_rows(tri3, x):
    return _dot(tri3, jnp.concatenate(_split3(x), axis=0))


def _cumsum_lanes(x, triu3):
    return _dot(jnp.concatenate(_split3(x), axis=1), triu3)


def _rms(x, g):
    ms = jnp.mean(x * x, axis=-1, keepdims=True)
    return x * lax.rsqrt(ms + NORM_EPS) * g


def _sigmoid(x):
    return 1.0 / (1.0 + jnp.exp(-x))


def _cparams(sem):
    return pltpu.CompilerParams(dimension_semantics=sem, vmem_limit_bytes=VMEM_LIMIT)


def _const_spec(shape):
    n = len(shape)
    return pl.BlockSpec(shape, lambda *_: (0,) * n, pipeline_mode=pl.Buffered(1))


def _ffn_kernel(res_ref, a_ref, wo_ref, g_ref, wu_ref, cw_ref, cb_ref, wd_ref, fg_ref, o_ref,
                h_scr, carry_scr, act_scr, ga_scr, va_scr, gb_scr, vb_scr,
                *, tiles_per_seq, n_chunks, final_norm):
    tm = res_ref.shape[0]
    fc = FFN_CHUNK
    halo = SUBLANES
    x = res_ref[...] + _dot(a_ref[...], wo_ref[...])
    h_scr[...] = _rms(x, g_ref[...]).astype(BF16)
    o_ref[...] = x
    seq_start = (pl.program_id(0) % tiles_per_seq) == 0
    cols = lambda c, base=0: pl.ds(pl.multiple_of(base + c * fc, LANES), fc)

    def up(c, g_scr, v_scr):
        h = h_scr[...]
        g_scr[halo:, :] = _dot(h, wu_ref[:, cols(c)])
        v_scr[...] = _dot(h, wu_ref[:, cols(c, FFN_DIM)])

    def down(c, g_scr, v_scr):
        g_scr[0:halo, :] = jnp.where(seq_start, 0.0, carry_scr[:, cols(c)])
        carry_scr[:, cols(c)] = g_scr[tm:tm + halo, :]
        cw = cw_ref[:, cols(c)]
        cb = cb_ref[:, cols(c)]
        for r0 in range(0, tm, FFN_ROW_BLOCK):
            y = cb + sum(cw[FFN_CONV - 1 - s:FFN_CONV - s, :] * g_scr[halo - s + r0:halo - s + r0 + FFN_ROW_BLOCK, :]
                         for s in range(FFN_CONV))
            act = y * _sigmoid(y) * v_scr[r0:r0 + FFN_ROW_BLOCK, :]
            act_scr[r0:r0 + FFN_ROW_BLOCK, :] = act.astype(BF16)
        o_ref[...] += _dot(act_scr[...], wd_ref[pl.ds(pl.multiple_of(c * fc, fc), fc), :])

    assert n_chunks % 2 == 1

    def chunk_pair(j, carry):
        up(2 * j + 1, gb_scr, vb_scr)
        down(2 * j, ga_scr, va_scr)
        up(2 * j + 2, ga_scr, va_scr)
        down(2 * j + 1, gb_scr, vb_scr)
        return carry

    up(0, ga_scr, va_scr)
    for j in range(n_chunks // 2):
        chunk_pair(j, 0)
    down(n_chunks - 1, ga_scr, va_scr)
    if final_norm:
        o_ref[...] = _rms(o_ref[...], fg_ref[...])


def _ffn(res, a, w_o, g, w_up, conv_w, conv_b, w_down, seq_len, final_g=None, tm=1024):
    m, d = res.shape
    nc = FFN_DIM // FFN_CHUNK
    cw = jnp.pad(conv_w, ((0, SUBLANES - FFN_CONV), (0, 0)))
    row = lambda i: (i, 0)
    fg = jnp.ones((d,), F32) if final_g is None else final_g
    kern = functools.partial(_ffn_kernel, tiles_per_seq=seq_len // tm, n_chunks=nc,
                             final_norm=final_g is not None)
    return pl.pallas_call(
        kern,
        out_shape=jax.ShapeDtypeStruct((m, d), F32),
        grid=(m // tm,),
        in_specs=[pl.BlockSpec((tm, d), row), pl.BlockSpec((tm, d), row), _const_spec((d, d)),
                  _const_spec((1, d)), _const_spec((d, 2 * FFN_DIM)),
                  _const_spec((SUBLANES, FFN_DIM)), _const_spec((1, FFN_DIM)),
                  _const_spec((FFN_DIM, d)), _const_spec((1, d))],
        out_specs=pl.BlockSpec((tm, d), row),
        scratch_shapes=[pltpu.VMEM((tm, d), BF16),
                        pltpu.VMEM((SUBLANES, FFN_DIM), F32),
                        pltpu.VMEM((tm, FFN_CHUNK), BF16)]
                       + [pltpu.VMEM((tm + SUBLANES, FFN_CHUNK), F32), pltpu.VMEM((tm, FFN_CHUNK), F32)] * 2,
        compiler_params=_cparams(("arbitrary",)),
        name="conv_ffn",
    )(res, a, w_o.astype(BF16), g.reshape(1, d), w_up.astype(BF16), cw, conv_b.reshape(1, FFN_DIM),
      w_down.astype(BF16), fg.reshape(1, d))


def _nsa_proj_kernel(x_ref, g_ref, w_ref, wvt_ref, bg_ref, rc_ref, rs1_ref, rs2_ref,
                     qp_ref, qr_ref, kc_ref, vc_ref, ks_ref, vs_ref, kw_ref, vw_ref, gate_ref,
                     cmp_scr, *, tiles_per_seq):
    hn = _rms(x_ref[...], g_ref[...]).astype(BF16)
    y = _dot(hn, w_ref[...])
    vt = _dot_nt(wvt_ref[...], hn)
    rc, rs1, rs2 = rc_ref[...], rs1_ref[...], rs2_ref[...]
    dh = NSA_HEAD_DIM

    def rope(z):
        half = NSA_ROT_DIM // 2
        return z * rc + pltpu.roll(z, half, 1) * rs1 + pltpu.roll(z, LANES - half, 1) * rs2

    scale = dh ** -0.5 * LOG2E
    for j in range(D_MODEL // LANES):
        q = y[:, j * LANES:(j + 1) * LANES] * scale
        qp_ref[:, j * LANES:(j + 1) * LANES] = q.astype(BF16)
        qr_ref[:, j * LANES:(j + 1) * LANES] = rope(q).astype(BF16)

    def kv_chunk(idx):
        return y[:, D_MODEL + idx * NSA_KV:D_MODEL + (idx + 1) * NSA_KV]

    def split_groups(z, ref, dtype):
        for g in range(NSA_GROUPS):
            ref[g] = z[:, g * dh:(g + 1) * dh].astype(dtype)

    def rope256(z):
        return jnp.concatenate([rope(z[:, :LANES]), rope(z[:, LANES:])], axis=1)

    tm = y.shape[0]
    nrow = tm // CMP_STRIDE
    for j in range(2 * NSA_KV // LANES):
        cmp_scr[j] = y[:, D_MODEL + j * LANES:D_MODEL + (j + 1) * LANES]
    for j in range(2 * NSA_KV // LANES):
        ref = kc_ref if j < NSA_KV // LANES else vc_ref
        toks = [cmp_scr[j, pl.ds(tok, nrow, stride=CMP_STRIDE), :] for tok in range(CMP_STRIDE)]
        for half in range(LANES // dh):
            g = (j % (NSA_KV // LANES)) * (LANES // dh) + half
            ref[g] = jnp.concatenate([t[:, half * dh:(half + 1) * dh] for t in toks], axis=1)
    t_pos = (pl.program_id(0) % tiles_per_seq) * tm + lax.broadcasted_iota(jnp.int32, (tm, LANES), 0)
    onehot = (lax.broadcasted_iota(jnp.int32, (tm, LANES), 1) == t_pos // SEL_BLOCK).astype(F32)
    ksel = rope256(kv_chunk(2))
    zpad = jnp.zeros((tm, LANES - dh), F32)
    for g in range(NSA_GROUPS):
        ks_ref[g] = jnp.concatenate([ksel[:, g * dh:(g + 1) * dh], zpad, onehot], axis=1).astype(BF16)
    split_groups(rope256(kv_chunk(3)), kw_ref, BF16)
    ones_pad = (lax.broadcasted_iota(jnp.int32, (NSA_VT_ROWS - dh, tm), 0) == 0).astype(F32)
    for g in range(NSA_GROUPS):
        vs_ref[g] = jnp.concatenate([vt[g * dh:(g + 1) * dh], ones_pad], axis=0).astype(BF16)
        vw_ref[g] = jnp.concatenate([vt[NSA_KV + g * dh:NSA_KV + (g + 1) * dh], ones_pad], axis=0).astype(BF16)
    gate = _sigmoid(vt[2 * NSA_KV:] + bg_ref[...])
    for g in range(NSA_GROUPS):
        gate_ref[g] = gate[g * NSA_GATE_ROWS:(g + 1) * NSA_GATE_ROWS]


def _rope_tables(seq_len):
    half = NSA_ROT_DIM // 2
    inv_freq = ROPE_THETA ** (-jnp.arange(half, dtype=F32) / half)
    ang = jnp.arange(seq_len, dtype=F32)[:, None] * inv_freq[None, :]
    cos, sin = jnp.cos(ang), jnp.sin(ang)
    zeros = jnp.zeros((seq_len, NSA_HEAD_DIM - NSA_ROT_DIM), F32)
    z8 = jnp.zeros((seq_len, half), F32)
    rc = jnp.concatenate([cos, cos, zeros + 1.0], axis=1)
    rs1 = jnp.concatenate([z8, sin, zeros], axis=1)
    rs2 = jnp.concatenate([-sin, z8, zeros], axis=1)
    two = lambda t: jnp.concatenate([t, t], axis=1)
    return two(rc), two(rs1), two(rs2)


def _nsa_proj(x, g, w_in, b_gate, rope, seq_len, tm=512):
    m, d = x.shape
    n_kv = 6 * NSA_KV
    kv = lambda idx: w_in[:, D_MODEL + idx * NSA_KV:D_MODEL + (idx + 1) * NSA_KV]
    pad_g = NSA_GATE_ROWS - NSA_REP * 3
    wg = w_in[:, D_MODEL + n_kv:].reshape(d, NSA_GROUPS, NSA_REP * 3)
    wg = jnp.pad(wg, ((0, 0), (0, 0), (0, pad_g))).reshape(d, NSA_GROUPS * NSA_GATE_ROWS)
    w = jnp.concatenate([w_in[:, :D_MODEL], kv(0), kv(1), kv(2), kv(4)], axis=1).astype(BF16)
    wvt = jnp.concatenate([kv(3), kv(5), wg], axis=1).T.astype(BF16)
    bg = jnp.pad(b_gate.reshape(NSA_GROUPS, NSA_REP * 3), ((0, 0), (0, pad_g)))
    bg = bg.reshape(NSA_GROUPS * NSA_GATE_ROWS, 1)
    n = w.shape[1]
    tps = seq_len // tm
    row = lambda i: (i, 0)
    rope_spec = pl.BlockSpec((tm, LANES), lambda i: (i % tps, 0))
    assert seq_len // SEL_BLOCK <= LANES
    g_out = lambda dt, w=NSA_HEAD_DIM: jax.ShapeDtypeStruct((NSA_GROUPS, m, w), dt)
    g_spec = pl.BlockSpec((NSA_GROUPS, tm, NSA_HEAD_DIM), lambda i: (0, i, 0))
    ks_spec = pl.BlockSpec((NSA_GROUPS, tm, 2 * LANES), lambda i: (0, i, 0))
    cmp_out = jax.ShapeDtypeStruct((NSA_GROUPS, m // CMP_STRIDE, CMP_STRIDE * NSA_HEAD_DIM), F32)
    cmp_spec = pl.BlockSpec((NSA_GROUPS, tm // CMP_STRIDE, CMP_STRIDE * NSA_HEAD_DIM), lambda i: (0, i, 0))
    vt_out = jax.ShapeDtypeStruct((NSA_GROUPS, NSA_VT_ROWS, m), BF16)
    vt_spec = pl.BlockSpec((NSA_GROUPS, NSA_VT_ROWS, tm), lambda i: (0, 0, i))
    return pl.pallas_call(
        functools.partial(_nsa_proj_kernel, tiles_per_seq=tps),
        out_shape=(jax.ShapeDtypeStruct((m, d), BF16), jax.ShapeDtypeStruct((m, d), BF16),
                   cmp_out, cmp_out, g_out(BF16, 2 * LANES), vt_out, g_out(BF16), vt_out,
                   jax.ShapeDtypeStruct((NSA_GROUPS, NSA_GATE_ROWS, m), F32)),
        grid=(m // tm,),
        in_specs=[pl.BlockSpec((tm, d), row), _const_spec((1, d)), _const_spec((d, n)),
                  _const_spec((2 * NSA_KV + NSA_GROUPS * NSA_GATE_ROWS, d)),
                  _const_spec((NSA_GROUPS * NSA_GATE_ROWS, 1)), rope_spec, rope_spec, rope_spec],
        out_specs=(pl.BlockSpec((tm, d), row), pl.BlockSpec((tm, d), row),
                   cmp_spec, cmp_spec, ks_spec, vt_spec, g_spec, vt_spec,
                   pl.BlockSpec((NSA_GROUPS, NSA_GATE_ROWS, tm), lambda i: (0, 0, i))),
        scratch_shapes=[pltpu.VMEM((2 * NSA_KV // LANES, tm, LANES), F32)],
        compiler_params=_cparams(("parallel",)),
        name="nsa_proj",
    )(x, g.reshape(1, d), w, wvt, bg, *rope)


def _gelu_tanh(x):
    return 0.5 * x * (1.0 + jnp.tanh(math.sqrt(2.0 / math.pi) * (x + 0.044715 * (x * x * x))))


def _compress_kernel(zk_ref, zv_ref, pek_ref, pev_ref, w1k_ref, w2k_ref, w1v_ref, w2v_ref,
                     kc_ref, vc_ref):
    nrow = zk_ref.shape[0]
    rows = lax.broadcasted_iota(jnp.int32, (nrow, NSA_HEAD_DIM), 0)

    def one(z_ref, pe_ref, w1_ref, w2_ref, o_ref):
        z = z_ref[...]
        a = _dot((z + pe_ref[0:1, :]).astype(BF16), w1_ref[0])
        b = _dot((z + pe_ref[1:2, :]).astype(BF16), w1_ref[1])
        hid = a + pltpu.roll(b, nrow - 1, 0)
        out = _dot(_gelu_tanh(hid).astype(BF16), w2_ref[...])
        o_ref[...] = jnp.where(rows == nrow - 1, 0.0, out).astype(o_ref.dtype)

    one(zk_ref, pek_ref, w1k_ref, w2k_ref, kc_ref)
    one(zv_ref, pev_ref, w1v_ref, w2v_ref, vc_ref)


def _compress(kc_raw, vc_raw, pe_k, pe_v, w1k, w2k, w1v, w2v, seq_len):
    g, nrows, half = kc_raw.shape
    dh = half // CMP_STRIDE
    nchunk = seq_len // CMP_STRIDE
    zk = kc_raw.reshape(g * nrows, half)
    zv = vc_raw.reshape(g * nrows, half)
    pe2 = lambda pe: pe.reshape(2, half)
    w1 = lambda w: w.astype(BF16).reshape(2, half, CMP_HIDDEN)
    nblk = zk.shape[0] // nchunk
    row = lambda i: (i, 0)
    return pl.pallas_call(
        _compress_kernel,
        out_shape=(jax.ShapeDtypeStruct((zk.shape[0], dh), BF16),
                   jax.ShapeDtypeStruct((zk.shape[0], dh), BF16)),
        grid=(nblk,),
        in_specs=[pl.BlockSpec((nchunk, half), row), pl.BlockSpec((nchunk, half), row),
                  _const_spec((2, half)), _const_spec((2, half)),
                  _const_spec((2, half, CMP_HIDDEN)), _const_spec((CMP_HIDDEN, dh)),
                  _const_spec((2, half, CMP_HIDDEN)), _const_spec((CMP_HIDDEN, dh))],
        out_specs=(pl.BlockSpec((nchunk, dh), row), pl.BlockSpec((nchunk, dh), row)),
        compiler_params=_cparams(("parallel",)),
        name="nsa_compress",
    )(zk, zv, pe2(pe_k), pe2(pe_v), w1(w1k), w2k.astype(BF16), w1(w1v), w2v.astype(BF16))


def _nsa_cmp_kernel(q_ref, kc_ref, vc_ref, gate_ref, ovt_ref, oc_ref, selt_ref, imp_scr, *, tq, n_classes):
    qi = pl.program_id(2)
    nq = pl.num_programs(2)
    dh = NSA_HEAD_DIM
    ncmp = kc_ref.shape[0]
    nsel = imp_scr.shape[0]
    heads = range(NSA_REP)

    def attend(nrows):
        q = q_ref[...]
        qs = jnp.concatenate([q[:, r * dh:(r + 1) * dh] for r in heads], axis=0)
        kc, vc = kc_ref[0:nrows, :], vc_ref[0:nrows, :]
        t = qi * tq + lax.broadcasted_iota(jnp.int32, (nrows, tq), 1)
        cmp_end = lax.broadcasted_iota(jnp.int32, (nrows, tq), 0) * CMP_STRIDE + (CMP_BLOCK - 1)
        neg = jnp.where(cmp_end <= t, 0.0, -jnp.inf)
        st = [_dot_nt(kc, qs[r * tq:(r + 1) * tq]) + neg for r in heads]
        mx = [jnp.max(st[r], axis=0, keepdims=True) for r in heads]
        mx = [jnp.where(mx[r] == -jnp.inf, 0.0, mx[r]) for r in heads]
        e = [jnp.exp2(st[r] - mx[r]) for r in heads]
        p = [e[r] / jnp.maximum(jnp.sum(e[r], axis=0, keepdims=True), 1e-30) for r in heads]
        o_t = [_dot_tn(vc, p[r].astype(BF16)) for r in heads]
        gate = gate_ref[...]
        o_t = [o_t[r] * gate[3 * r:3 * r + 1, :] for r in heads]
        for j in range(NSA_REP // 2):
            pair = jnp.concatenate([o_t[2 * j], o_t[2 * j + 1]], axis=0).T
            oc_ref[:, 2 * j * dh:2 * (j + 1) * dh] = pair
        psum = sum(p)
        ovt = jnp.concatenate([ovt_ref[:, j * ncmp:j * ncmp + nrows] for j in range(3)], axis=1)
        imp_scr[...] = _dot(ovt, jnp.concatenate(_split3(psum), axis=0))

    def select(nblk):
        blk = lax.broadcasted_iota(jnp.int32, (nblk, LANES), 0)
        blk_f = blk.astype(F32)
        if nblk < nsel:
            selt_ref[nblk:, :] = jnp.zeros((nsel - nblk, tq), selt_ref.dtype)
        for cb in range(tq // LANES):
            csl = slice(cb * LANES, (cb + 1) * LANES)
            tb = (qi * tq + cb * LANES + lax.broadcasted_iota(jnp.int32, (nblk, LANES), 1)) // SEL_BLOCK
            forced = (blk == 0) | (blk == tb) | (blk == tb - 1)
            vals = jnp.where(forced, -jnp.inf, jnp.where(blk <= tb, imp_scr[0:nblk, csl], -1.0))
            for _ in range(SEL_TOPK - 3):
                top = jnp.max(vals, axis=0, keepdims=True)
                first = jnp.min(jnp.where(vals == top, blk_f, float(nsel)), axis=0, keepdims=True)
                vals = jnp.where(blk_f == first, -jnp.inf, vals)
            selt_ref[0:nblk, csl] = jnp.where(vals == -jnp.inf, 1.0, 0.0).astype(selt_ref.dtype)

    for cls in range(n_classes):
        @pl.when((qi * n_classes) // nq == cls)
        def _(cls=cls):
            attend((cls + 1) * ncmp // n_classes)
            select((cls + 1) * nsel // n_classes)


def _overlap_matrix_t3(ncmp_pad, nsel):
    c = np.arange(ncmp_pad)[None, :]
    s = np.arange(nsel)[:, None]
    cmp_start = c * CMP_STRIDE
    cmp_end = cmp_start + CMP_BLOCK - 1
    blk_start = s * SEL_BLOCK
    ov = ((cmp_end >= blk_start) & (cmp_start <= blk_start + SEL_BLOCK - 1)).astype(np.float32)
    return jnp.asarray(np.concatenate([ov, ov, ov], axis=1), dtype=BF16)


def _nsa_cmp(qp, kc, vc, gates, batch, seq_len, tq=512):
    m, d = qp.shape
    nq = seq_len // tq
    ncmp = seq_len // CMP_STRIDE
    nsel = seq_len // SEL_BLOCK
    ovt3 = _overlap_matrix_t3(ncmp, nsel)
    qmap = lambda b, g, i: (b * nq + i, g)
    kmap = lambda b, g, i: (g * batch + b, 0)
    n_classes = math.gcd(nq, 4)
    return pl.pallas_call(
        functools.partial(_nsa_cmp_kernel, tq=tq, n_classes=n_classes),
        out_shape=(jax.ShapeDtypeStruct((m, d), F32),
                   jax.ShapeDtypeStruct((NSA_GROUPS, nsel, m), BF16)),
        grid=(batch, NSA_GROUPS, nq),
        in_specs=[pl.BlockSpec((tq, NSA_KV), qmap),
                  pl.BlockSpec((ncmp, NSA_HEAD_DIM), kmap), pl.BlockSpec((ncmp, NSA_HEAD_DIM), kmap),
                  pl.BlockSpec((None, NSA_GATE_ROWS, tq), lambda b, g, i: (g, 0, b * nq + i)),
                  _const_spec((nsel, 3 * ncmp))],
        out_specs=(pl.BlockSpec((tq, NSA_KV), qmap),
                   pl.BlockSpec((None, nsel, tq), lambda b, g, i: (g, 0, b * nq + i))),
        scratch_shapes=[pltpu.VMEM((nsel, tq), F32)],
        compiler_params=_cparams(("parallel", "parallel", "parallel")),
        name="nsa_cmp_topk",
    )(qp, kc, vc, gates, ovt3)


def _nsa_sel_kernel(q_ref, ks_ref, vs_ref, kw_ref, vw_ref, sel_ref, gate_ref, oc_ref, o_ref,
                    m_scr, acc_scr, ow_scr, sta_scr, stb_scr, *, tq, tk):
    qi = pl.program_id(2)
    dh = NSA_HEAD_DIM
    cols = NSA_REP * tq
    q = q_ref[...]
    qs = jnp.concatenate([q[:, r * dh:(r + 1) * dh] for r in range(NSA_REP)], axis=0)
    selt = sel_ref[...].astype(F32)
    nsel = selt.shape[0]
    bmask_t = jnp.where(selt > 0.5, 0.0, MASKED)
    if nsel < LANES:
        bmask_t = jnp.concatenate([bmask_t, jnp.zeros((LANES - nsel, tq), F32)], axis=0)
    bmask = bmask_t.T.astype(BF16)
    zpad = jnp.zeros((tq, LANES - dh), BF16)
    qa = jnp.concatenate([jnp.concatenate([q[:, r * dh:(r + 1) * dh], zpad, bmask], axis=1)
                          for r in range(NSA_REP)], axis=0)
    q0 = qi * tq
    key_iota = lax.broadcasted_iota(jnp.int32, (tk, tq), 0)
    t_pos = q0 + lax.broadcasted_iota(jnp.int32, (tk, tq), 1)

    m_scr[...] = jnp.full((1, cols), -jnp.inf, F32)
    acc_scr[...] = jnp.zeros((NSA_VT_ROWS, cols), F32)

    heads = range(NSA_REP)
    hsl = [slice(r * tq, (r + 1) * tq) for r in heads]

    def put_scores(scr, ki):
        k = ks_ref[pl.ds(pl.multiple_of(ki * tk, tk), tk), :]
        for r in heads:
            scr[r] = _dot_nt(k, qa[hsl[r]])

    def get_scores(scr):
        return [scr[r] for r in heads]

    def consume(ki, st, causal):
        k0 = pl.multiple_of(ki * tk, tk)
        vt = vs_ref[:, pl.ds(k0, tk)]
        if causal:
            neg = jnp.where(k0 + key_iota <= t_pos, 0.0, -jnp.inf)
            st = [s + neg for s in st]
        m_old = [m_scr[:, hsl[r]] for r in heads]
        m_new = [jnp.maximum(m_old[r], jnp.max(st[r], axis=0, keepdims=True)) for r in heads]
        alpha = [jnp.exp2(m_old[r] - m_new[r]) for r in heads]
        p = [jnp.exp2(st[r] - m_new[r]).astype(BF16) for r in heads]
        pv = [_dot(vt, p[r]) for r in heads]
        for r in heads:
            acc_scr[:, hsl[r]] = alpha[r] * acc_scr[:, hsl[r]] + pv[r]
            m_scr[:, hsl[r]] = m_new[r]

    def tile_pair(j, carry):
        even = get_scores(sta_scr)
        put_scores(stb_scr, 2 * j + 1)
        consume(2 * j, even, False)
        odd = get_scores(stb_scr)
        put_scores(sta_scr, 2 * j + 2)
        consume(2 * j + 1, odd, False)
        return carry

    last = (q0 + tq - 1) // tk

    gate = gate_ref[...]
    wq = min(tq, NSA_WINDOW_QUERIES)
    wsub = wq + WINDOW
    nsub = tq // wq
    rel_iota = (lax.broadcasted_iota(jnp.int32, (wsub, wq), 0)
                - lax.broadcasted_iota(jnp.int32, (wsub, wq), 1))
    starts = [pl.multiple_of(jnp.maximum(q0 + wq * u - WINDOW, 0), wq) for u in range(nsub)]
    kw = [kw_ref[pl.ds(starts[u], wsub), :] for u in range(nsub)]
    vwt = [vw_ref[:, pl.ds(starts[u], wsub)] for u in range(nsub)]
    rel = [starts[u] - (q0 + wq * u) + rel_iota for u in range(nsub)]
    neg_w = [jnp.where((rel[u] <= 0) & (rel[u] > -WINDOW), 0.0, -jnp.inf) for u in range(nsub)]
    subs = [(r, u) for r in heads for u in range(nsub)]
    sw = [_dot_nt(kw[u], qs[r * tq + wq * u:r * tq + wq * (u + 1)]) + neg_w[u] for r, u in subs]
    put_scores(sta_scr, 0)
    mw = [jnp.max(s, axis=0, keepdims=True) for s in sw]
    mw = [jnp.where(m == -jnp.inf, 0.0, m) for m in mw]
    ew = [jnp.exp2(s - m).astype(BF16) for s, m in zip(sw, mw)]
    pvw = [_dot(vwt[u], e) for (r, u), e in zip(subs, ew)]
    pvw = [p[0:dh] / jnp.maximum(p[dh:dh + 1], 1e-30) for p in pvw]
    for r in heads:
        o_win = jnp.concatenate(pvw[r * nsub:(r + 1) * nsub], axis=1)
        ow_scr[:, hsl[r]] = o_win * gate[3 * r + 2:3 * r + 3, :]

    n_pairs = last // 2

    def tile_quad(jq, carry):
        tile_pair(2 * jq, carry)
        tile_pair(2 * jq + 1, carry)
        return carry

    lax.fori_loop(0, n_pairs // 2, tile_quad, 0)

    @pl.when(n_pairs % 2 == 1)
    def _():
        tile_pair(n_pairs - 1, 0)

    @pl.when(last % 2 == 0)
    def _():
        consume(last, get_scores(sta_scr), True)

    @pl.when(last % 2 == 1)
    def _():
        even = get_scores(sta_scr)
        put_scores(stb_scr, last)
        consume(last - 1, even, False)
        consume(last, get_scores(stb_scr), True)
    o_sel = acc_scr[0:dh, :] / jnp.maximum(acc_scr[dh:dh + 1, :], 1e-30)

    mix = [o_sel[:, hsl[r]] * gate[3 * r + 1:3 * r + 2, :] + ow_scr[:, hsl[r]] for r in heads]
    for j in range(NSA_REP // 2):
        psl = slice(2 * j * dh, 2 * (j + 1) * dh)
        pair = jnp.concatenate([mix[2 * j], mix[2 * j + 1]], axis=0).T
        o_ref[:, psl] = (oc_ref[:, psl] + pair).astype(o_ref.dtype)


def _nsa_sel(qr, ks, vs, kw, vw, sel, gates, oc, batch, seq_len, tq=512, tk=512):
    m, d = qr.shape
    nq = seq_len // tq
    nsel = seq_len // SEL_BLOCK
    qmap = lambda b, g, i: (b * nq + i, g)
    kvmap = lambda b, g, i: (g, b, 0)
    kv_spec = pl.BlockSpec((None, seq_len, NSA_HEAD_DIM), kvmap)
    ks_spec = pl.BlockSpec((None, seq_len, 2 * LANES), kvmap)
    vt_spec = pl.BlockSpec((None, NSA_VT_ROWS, seq_len), lambda b, g, i: (g, 0, b))
    cols = NSA_REP * tq
    assert tk % tq == 0
    return pl.pallas_call(
        functools.partial(_nsa_sel_kernel, tq=tq, tk=tk),
        out_shape=jax.ShapeDtypeStruct((m, d), BF16),
        grid=(batch, NSA_GROUPS, nq),
        in_specs=[pl.BlockSpec((tq, NSA_KV), qmap), ks_spec, vt_spec, kv_spec, vt_spec,
                  pl.BlockSpec((None, nsel, tq), lambda b, g, i: (g, 0, b * nq + i)),
                  pl.BlockSpec((None, NSA_GATE_ROWS, tq), lambda b, g, i: (g, 0, b * nq + i)),
                  pl.BlockSpec((tq, NSA_KV), qmap)],
        out_specs=pl.BlockSpec((tq, NSA_KV), qmap),
        scratch_shapes=[pltpu.VMEM((1, cols), F32), pltpu.VMEM((NSA_VT_ROWS, cols), F32),
                        pltpu.VMEM((NSA_HEAD_DIM, cols), F32),
                        pltpu.VMEM((NSA_REP, tk, tq), F32), pltpu.VMEM((NSA_REP, tk, tq), F32)],
        compiler_params=_cparams(("parallel", "parallel", "parallel")),
        name="nsa_sel_win",
    )(qr, ks, vs, kw, vw, sel, gates, oc)


def _nsa_layer(x, norm_g, w_in, pe_k, pe_v, w1k, w2k, w1v, w2v, b_gate, rope, batch, seq_len):
    qp, qr, kc_raw, vc_raw, ks, vs, kw, vw, gates = _nsa_proj(x, norm_g, w_in, b_gate, rope, seq_len)
    kc, vc = _compress(kc_raw, vc_raw, pe_k, pe_v, w1k, w2k, w1v, w2v, seq_len)
    oc, sel = _nsa_cmp(qp, kc, vc, gates, batch, seq_len)
    return _nsa_sel(qr, ks, vs, kw, vw, sel, gates, oc, batch, seq_len)


def _mlstm_proj_kernel(x_ref, g_ref, w_ref, wot_ref, wt_ref, bcol_ref, brow_ref,
                       qk_ref, v_ref, ot_ref, gc_ref, gr_ref):
    hn = _rms(x_ref[...], g_ref[...]).astype(BF16)
    y = _dot(hn, w_ref[...])
    d = D_MODEL
    qk_ref[...] = y[:, :d]
    v_ref[...] = y[:, d:2 * d].astype(BF16)
    ot_ref[...] = _sigmoid(_dot_nt(wot_ref[...], hn))
    gc_ref[...] = y[:, 2 * d:] + bcol_ref[...]
    gr_ref[...] = _dot_nt(wt_ref[...], hn) + brow_ref[...]


def _mlstm_proj(x, g, w_in, b_gates, tm=512):
    m, d = x.shape
    h, dk = MLSTM_HEADS, MLSTM_QK_DIM
    wq = w_in[:, :h * dk].reshape(d, h, dk)
    wk = w_in[:, h * dk:2 * h * dk].reshape(d, h, dk)
    wqk = jnp.concatenate([wq, wk], axis=2).reshape(d, 2 * h * dk)
    wv = w_in[:, d:2 * d]
    wif = w_in[:, 2 * d:2 * d + 2 * h]
    wo = w_in[:, 2 * d + 2 * h:]
    w = jnp.concatenate([wqk, wv, jnp.pad(wif, ((0, 0), (0, LANES - 2 * h)))], axis=1).astype(BF16)
    wot = wo.T.astype(BF16)
    wt = wif.T.astype(BF16)
    bcol = jnp.pad(b_gates, (0, LANES - 2 * h)).reshape(1, LANES)
    brow = b_gates.reshape(2 * h, 1)
    n = w.shape[1]
    row = lambda i: (i, 0)
    col = lambda i: (0, i)
    return pl.pallas_call(
        _mlstm_proj_kernel,
        out_shape=(jax.ShapeDtypeStruct((m, d), F32), jax.ShapeDtypeStruct((m, d), BF16),
                   jax.ShapeDtypeStruct((d, m), F32), jax.ShapeDtypeStruct((m, LANES), F32),
                   jax.ShapeDtypeStruct((2 * h, m), F32)),
        grid=(m // tm,),
        in_specs=[pl.BlockSpec((tm, d), row), _const_spec((1, d)), _const_spec((d, n)), _const_spec((d, d)),
                  _const_spec((2 * h, d)), _const_spec((1, LANES)), _const_spec((2 * h, 1))],
        out_specs=(pl.BlockSpec((tm, d), row), pl.BlockSpec((tm, d), row), pl.BlockSpec((d, tm), col),
                   pl.BlockSpec((tm, LANES), row), pl.BlockSpec((2 * h, tm), col)),
        compiler_params=_cparams(("parallel",)),
        name="mlstm_proj",
    )(x, g.reshape(1, d), w, wot, wt, bcol, brow)


def _log_sigmoid(x):
    return jnp.minimum(x, 0.0) - jnp.log(1.0 + jnp.exp(-jnp.abs(x)))


def _mlstm_core_kernel(qk_ref, v_ref, og_ref, gc_ref, gr_ref, cw_ref, cb_ref, ng_ref, o_ref,
                       qkc_scr, ext_scr, c_scr, n_scr, m_scr, *, tt):
    L = MLSTM_TILE
    H, dk, dv = MLSTM_HEADS, MLSTM_QK_DIM, MLSTM_V_DIM
    seq_start = pl.program_id(1) == 0

    @pl.when(seq_start)
    def _():
        c_scr[...] = jnp.zeros_like(c_scr)
        n_scr[...] = jnp.zeros_like(n_scr)
        m_scr[...] = jnp.zeros_like(m_scr)
        ext_scr[0:SUBLANES, :] = jnp.zeros((SUBLANES, ext_scr.shape[1]), F32)

    ext_scr[SUBLANES:, :] = qk_ref[...]
    is_k = lax.broadcasted_iota(jnp.int32, (tt, 2 * dk), 1) >= dk
    for h in range(H):
        hsl = slice(h * 2 * dk, (h + 1) * 2 * dk)
        cw = cw_ref[:, hsl]
        acc = cb_ref[:, hsl] + cw[MLSTM_CONV - 1:MLSTM_CONV, :] * ext_scr[SUBLANES:, hsl]
        for s in range(1, MLSTM_CONV):
            acc = acc + cw[MLSTM_CONV - 1 - s:MLSTM_CONV - s, :] * ext_scr[SUBLANES - s:SUBLANES - s + tt, hsl]
        act = acc * _sigmoid(acc)
        qkc_scr[:, hsl] = jnp.where(is_k, act * dk ** -0.5, act)
    ext_scr[0:SUBLANES, :] = ext_scr[tt:tt + SUBLANES, :]

    src = lax.broadcasted_iota(jnp.int32, (L, L), 0)
    tgt = lax.broadcasted_iota(jnp.int32, (L, L), 1)
    causal = src <= tgt
    tri3 = jnp.concatenate([(src >= tgt).astype(BF16)] * 3, axis=1)
    triu3 = jnp.concatenate([causal.astype(BF16)] * 3, axis=0)
    n_pad = jnp.zeros((SUBLANES - 3, dk), BF16)

    def chunk(c, carry):
        r0 = pl.multiple_of(c * L, L)
        gcol = gc_ref[pl.ds(r0, L), :]
        grow = gr_ref[c]
        b_col = _cumsum_rows(tri3, _log_sigmoid(gcol))
        b_row = _cumsum_lanes(_log_sigmoid(grow), triu3)
        hs = range(H)
        vsl = [slice(h * dv, (h + 1) * dv) for h in hs]
        qk = [qkc_scr[pl.ds(r0, L), h * 2 * dk:(h + 1) * 2 * dk] for h in hs]
        q = [qk[h][:, :dk].astype(BF16) for h in hs]
        k = [qk[h][:, dk:] for h in hs]
        v = [v_ref[pl.ds(r0, L), vsl[h]] for h in hs]
        col = [gcol[:, h:h + 1] - b_col[:, H + h:H + h + 1] for h in hs]
        li_r = [grow[h:h + 1, :] for h in hs]
        b_r = [b_row[H + h:H + h + 1, :] for h in hs]
        b_end = [b_r[h][:, L - 1:L] for h in hs]
        dmat = [jnp.where(causal, b_r[h] + col[h], -jnp.inf) for h in hs]
        d_max = [jnp.max(dmat[h], axis=0, keepdims=True) for h in hs]
        att = [jnp.exp(dmat[h] - d_max[h]) * _dot_nt(k[h].astype(BF16), q[h]) for h in hs]
        a_sum = [jnp.sum(att[h], axis=0, keepdims=True) for h in hs]
        intra = [_dot_tn(v[h], att[h].astype(BF16)) for h in hs]
        g_max = [jnp.max(b_end[h] - b_r[h] + li_r[h], axis=-1, keepdims=True) for h in hs]
        kw = [k[h] * jnp.exp(b_end[h] + col[h] - g_max[h]) for h in hs]
        c_loc = [_dot_tn(v[h], kw[h].astype(BF16)) for h in hs]
        n_loc = [jnp.sum(kw[h], axis=0, keepdims=True) for h in hs]
        m_prev = [m_scr[h:h + 1, 0:1] for h in hs]
        c_prev = [c_scr[h] for h in hs]
        n_prev = [n_scr[h:h + 1, :] for h in hs]
        m_inter = [b_r[h] + m_prev[h] for h in hs]
        m_t = [jnp.maximum(m_inter[h], d_max[h]) for h in hs]
        w_loc = [jnp.exp(d_max[h] - m_t[h]) for h in hs]
        w_int = [jnp.exp(m_inter[h] - m_t[h]) for h in hs]
        q_c = [_dot_nt(c_prev[h].astype(BF16), q[h]) for h in hs]
        q_n = [_dot_nt(jnp.concatenate(list(_split3(n_prev[h])) + [n_pad], axis=0), q[h]) for h in hs]
        q_n = [q_n[h][0:1] + q_n[h][1:2] + q_n[h][2:3] for h in hs]
        num = [w_loc[h] * intra[h] + w_int[h] * q_c[h] for h in hs]
        den = [w_loc[h] * a_sum[h] + w_int[h] * q_n[h] for h in hs]
        h_t = [num[h] / jnp.maximum(jnp.abs(den[h]), jnp.exp(-m_t[h])) for h in hs]
        h_t = [h_t[h] * lax.rsqrt(jnp.mean(h_t[h] * h_t[h], axis=0, keepdims=True) + NORM_EPS) for h in hs]
        m_new = [jnp.maximum(b_end[h] + m_prev[h], g_max[h]) for h in hs]
        a = [jnp.exp(b_end[h] + m_prev[h] - m_new[h]) for h in hs]
        sc = [jnp.exp(g_max[h] - m_new[h]) for h in hs]
        for h in hs:
            out = h_t[h] * ng_ref[vsl[h], :] * og_ref[vsl[h], pl.ds(r0, L)]
            o_ref[pl.ds(r0, L), vsl[h]] = out.T.astype(o_ref.dtype)
            c_scr[h] = a[h] * c_prev[h] + sc[h] * c_loc[h]
            n_scr[h:h + 1, :] = a[h] * n_prev[h] + sc[h] * n_loc[h]
            m_scr[h:h + 1, :] = jnp.broadcast_to(m_new[h], (1, LANES))
        return carry

    lax.fori_loop(0, tt // L, chunk, 0)


def _mlstm_core(qk, v, ogt, gcol, grow3, conv_w, conv_b, norm_gb, batch, seq_len, tt=256):
    m, d = qk.shape
    H, dk, dv = MLSTM_HEADS, MLSTM_QK_DIM, MLSTM_V_DIM
    nt = seq_len // tt
    ncs = tt // MLSTM_TILE
    row = lambda b, i: (b * nt + i, 0)
    return pl.pallas_call(
        functools.partial(_mlstm_core_kernel, tt=tt),
        out_shape=jax.ShapeDtypeStruct((m, d), BF16),
        grid=(batch, nt),
        in_specs=[pl.BlockSpec((tt, d), row), pl.BlockSpec((tt, d), row),
                  pl.BlockSpec((d, tt), lambda b, i: (0, b * nt + i)),
                  pl.BlockSpec((tt, LANES), row),
                  pl.BlockSpec((ncs, 2 * H, MLSTM_TILE), lambda b, i: (b * nt + i, 0, 0)),
                  _const_spec((SUBLANES, d)), _const_spec((1, d)), _const_spec((d, LANES))],
        out_specs=pl.BlockSpec((tt, d), row),
        scratch_shapes=[pltpu.VMEM((tt, d), F32), pltpu.VMEM((tt + SUBLANES, d), F32),
                        pltpu.VMEM((H, dv, dk), F32), pltpu.VMEM((H, dk), F32),
                        pltpu.VMEM((H, LANES), F32)],
        compiler_params=_cparams(("arbitrary", "arbitrary")),
        name="mlstm_core",
    )(qk, v, ogt, gcol, grow3, conv_w, conv_b, norm_gb)


def _mlstm_layer(x, norm_g, w_in, conv_w, conv_b, b_gates, hnorm, batch, seq_len):
    H, dk = MLSTM_HEADS, MLSTM_QK_DIM
    qk, v, ogt, gcol, grow = _mlstm_proj(x, norm_g, w_in, b_gates)
    m = x.shape[0]
    grow3 = grow.reshape(2 * H, m // MLSTM_TILE, MLSTM_TILE).transpose(1, 0, 2)
    norm_gb = jnp.broadcast_to(hnorm.reshape(-1, 1), (hnorm.shape[0], LANES))

    def perm(z):
        lead = z.shape[:-1]
        zq = z[..., :H * dk].reshape(*lead, H, dk)
        zk = z[..., H * dk:].reshape(*lead, H, dk)
        return jnp.concatenate([zq, zk], axis=-1).reshape(*lead, 2 * H * dk)

    cw = jnp.pad(perm(conv_w), ((0, SUBLANES - MLSTM_CONV), (0, 0)))
    return _mlstm_core(qk, v, ogt, gcol, grow3, cw, perm(conv_b).reshape(1, -1), norm_gb,
                       batch, seq_len)


def _softplus(x):
    return jnp.maximum(x, 0.0) + jnp.log(1.0 + jnp.exp(-jnp.abs(x)))


def _rwkv_proj_kernel(x_ref, g_ref, mu_ref, wr_ref, wk_ref, wv_ref, ww1_ref, ww2_ref, w0_ref,
                      aw1_ref, aw2_ref, a0_ref, gw1_ref, gw2_ref,
                      r_ref, k_ref, v_ref, lw_ref, a_ref, go_ref, prev_scr, *, tiles_per_seq):
    tm = x_ref.shape[0]
    h = _rms(x_ref[...], g_ref[...])
    seq_start = (pl.program_id(0) % tiles_per_seq) == 0
    prev = jnp.where(seq_start, 0.0, prev_scr[SUBLANES - 1:SUBLANES, :])
    prev_scr[...] = h[tm - SUBLANES:, :]
    rows = lax.broadcasted_iota(jnp.int32, h.shape, 0)
    xx = jnp.where(rows == 0, prev, pltpu.roll(h, 1, 0)) - h
    mix = lambda j: (h + xx * mu_ref[j:j + 1, :]).astype(BF16)
    r_ref[...] = _dot(mix(0), wr_ref[...])
    k_ref[...] = _dot(mix(2), wk_ref[...])
    v_ref[...] = _dot(mix(3), wv_ref[...])
    wl = _dot(jnp.tanh(_dot(mix(1), ww1_ref[...])).astype(BF16), ww2_ref[...]) + w0_ref[...]
    w_log = -_softplus(-wl) - 0.5
    lw_ref[...] = -jnp.exp(w_log)
    al = _dot(_dot(mix(4), aw1_ref[...]).astype(BF16), aw2_ref[...]) + a0_ref[...]
    a_ref[...] = _sigmoid(al)
    go_ref[...] = _dot(_sigmoid(_dot(mix(5), gw1_ref[...])).astype(BF16), gw2_ref[...])


def _rwkv_proj(x, g, mu, w_r, w_k, w_v, w0, w_w1, w_w2, a0, a_w1, a_w2, g_w1, g_w2, seq_len, tm=512):
    m, d = x.shape
    row = lambda i: (i, 0)
    bf = lambda w: w.astype(BF16)
    mu8 = jnp.pad(mu, ((0, SUBLANES - mu.shape[0]), (0, 0)))
    consts = [g.reshape(1, d), mu8, bf(w_r), bf(w_k), bf(w_v), bf(w_w1), bf(w_w2), w0.reshape(1, d),
              bf(a_w1), bf(a_w2), a0.reshape(1, d), bf(g_w1), bf(g_w2)]
    out = jax.ShapeDtypeStruct((m, d), F32)
    return pl.pallas_call(
        functools.partial(_rwkv_proj_kernel, tiles_per_seq=seq_len // tm),
        out_shape=(out,) * 6,
        grid=(m // tm,),
        in_specs=[pl.BlockSpec((tm, d), row)] + [_const_spec(c.shape) for c in consts],
        out_specs=(pl.BlockSpec((tm, d), row),) * 6,
        scratch_shapes=[pltpu.VMEM((SUBLANES, d), F32)],
        compiler_params=_cparams(("arbitrary",)),
        name="rwkv_proj",
    )(x, *consts)


def _block_diag(x, lo):
    zero = jnp.zeros_like(x)
    return jnp.concatenate([jnp.where(lo, x, zero), jnp.where(lo, zero, x)], axis=0)


class _PairMat:
    def __init__(self, x, lo):
        self.x, self.lo = x, lo
        self._lhs = self._rhs = None

    def lhs(self):
        if self._lhs is None:
            hi, lo = _split2(self.x)
            self._lhs = jnp.concatenate([hi, lo, hi], axis=1)
        return self._lhs

    def rhs(self):
        if self._rhs is None:
            hi, lo = _split2(self.x)
            bh = _block_diag(hi, self.lo)
            self._rhs = jnp.concatenate([bh, bh, _block_diag(lo, self.lo)], axis=0)
        return self._rhs


def _pair_mm(p, q):
    return _dot(p.lhs(), q.rhs())


def _rwkv_core_kernel(r_ref, k_ref, v_ref, lw_ref, a_ref, go_ref, kk_ref, ka_ref, rk_ref,
                      lnw_ref, lnb_ref, o_ref, z_scr, *, tt):
    L, N = RWKV_CHUNK, RWKV_HEAD_DIM
    npair = z_scr.shape[0]
    pairs = range(npair)

    @pl.when(pl.program_id(1) == 0)
    def _():
        z_scr[...] = jnp.zeros_like(z_scr)

    ri = lax.broadcasted_iota(jnp.int32, (L, LANES), 0)
    ln = lax.broadcasted_iota(jnp.int32, (L, LANES), 1)
    si = ln % N
    lo = ln < N
    lower_incl = ri >= si
    lower_strict = ri > si
    blk_diag = (ri // RWKV_INV_BLOCK) == (si // RWKV_INV_BLOCK)
    eye = (ri == si).astype(F32)
    tri = (lax.broadcasted_iota(jnp.int32, (L, L), 0) >= lax.broadcasted_iota(jnp.int32, (L, L), 1))
    tri3 = jnp.concatenate([tri.astype(BF16)] * 3, axis=1)
    z_mask = ((lax.broadcasted_iota(jnp.int32, (2 * N, LANES), 0) // N)
              == (lax.broadcasted_iota(jnp.int32, (2 * N, LANES), 1) // N))

    def half_sum(x):
        s0 = jnp.sum(jnp.where(lo, x, 0.0), axis=-1, keepdims=True)
        s1 = jnp.sum(jnp.where(lo, 0.0, x), axis=-1, keepdims=True)
        return jnp.where(lo, s0, s1)

    bd = lambda x: _block_diag(x, lo)
    mk = lambda xs: [_PairMat(x, lo) for x in xs]
    mm = lambda ps, qs: [_pair_mm(p, q) for p, q in zip(ps, qs)]

    def prep(grp, r0):
        idx = range(len(grp))
        sl = [slice(p * LANES, (p + 1) * LANES) for p in grp]
        ld = lambda ref: [ref[pl.ds(r0, L), s] for s in sl]
        r, k, v, lw, a = ld(r_ref), ld(k_ref), ld(v_ref), ld(lw_ref), ld(a_ref)
        kk = [k[i] * kk_ref[:, sl[i]] for i in idx]
        kk = [kk[i] / jnp.maximum(jnp.sqrt(half_sum(kk[i] * kk[i])), 1e-12) for i in idx]
        km = [k[i] * (1.0 + (a[i] - 1.0) * ka_ref[:, sl[i]]) for i in idx]
        bv = [kk[i] * a[i] for i in idx]
        cum = [_cumsum_rows(tri3, lw[i]) for i in idx]
        cum_end = [cum[i][L - 1:L, :] for i in idx]
        w_inv = [jnp.exp(-cum[i]) for i in idx]
        w_out = [jnp.exp(cum_end[i] - cum[i]) for i in idx]
        kk_h = [(kk[i] * jnp.exp(cum[i] - lw[i])).astype(BF16) for i in idx]
        r_h = [(r[i] * jnp.exp(cum[i])).astype(BF16) for i in idx]
        b_t = [(bv[i] * w_inv[i]).astype(BF16) for i in idx]
        k_t = [(km[i] * w_inv[i]).astype(BF16) for i in idx]
        bbar = [(bv[i] * w_out[i]).astype(BF16) for i in idx]
        kbar = [(km[i] * w_out[i]).astype(BF16) for i in idx]
        vb = [v[i].astype(BF16) for i in idx]
        lhs = [jnp.concatenate([kk_h[i], r_h[i]], axis=0) for i in idx]
        ab = [_dot_nt(lhs[i], bd(b_t[i])) for i in idx]
        ak = [_dot_nt(lhs[i], bd(k_t[i])) for i in idx]
        return dict(
            sl=sl, r=r, v=v, km=km, kk_h=kk_h, r_h=r_h, bbar=bbar, kbar=kbar, vb=vb, cum_end=cum_end,
            a_ub=[jnp.where(lower_strict, ab[i][:L], 0.0) for i in idx],
            a_rb=[jnp.where(lower_incl, ab[i][L:], 0.0).astype(BF16) for i in idx],
            a_uk=[jnp.where(lower_strict, ak[i][:L], 0.0).astype(BF16) for i in idx],
            a_rk=[jnp.where(lower_incl, ak[i][L:], 0.0).astype(BF16) for i in idx])

    def inverse(s):
        a_ub = s["a_ub"]
        idx = range(len(a_ub))
        n1 = mk([jnp.where(blk_diag, -a_ub[i], 0.0) for i in idx])
        n2 = mk(mm(n1, n1))
        n4 = mk(mm(n2, n2))
        n8 = mk(mm(n4, n4))
        acc = mk([eye + n1[i].x for i in idx])
        for nk in (n2, n4):
            prod = mm(acc, nk)
            acc = mk([acc[i].x + prod[i] for i in idx])
        prod = mm(acc, n8)
        d_inv = mk([acc[i].x + prod[i] for i in idx])
        l_off = mk([jnp.where(blk_diag, 0.0, a_ub[i]) for i in idx])
        e1 = mk([-x for x in mm(d_inv, l_off)])
        e2 = mk(mm(e1, e1))
        qm = mk([eye + e1[i].x for i in idx])
        prod = mm(qm, e2)
        qm = mk([qm[i].x + prod[i] for i in idx])
        s["t_inv"] = mk(mm(qm, d_inv))

    def state(grp, s):
        idx = range(len(grp))
        z = [z_scr[p] for p in grp]
        zb = [z[i].astype(BF16) for i in idx]
        vb = s["vb"]
        bdv = [bd(vb[i]) for i in idx]
        rhs_u = mk([_dot_nt(s["kk_h"][i], zb[i]) + _dot(s["a_uk"][i], bdv[i]) for i in idx])
        u = [-x for x in mm(s["t_inv"], rhs_u)]
        ub = [u[i].astype(BF16) for i in idx]
        s["y"] = [_dot_nt(s["r_h"][i], zb[i])
                  + _dot(jnp.concatenate([s["a_rb"][i], s["a_rk"][i]], axis=1),
                         jnp.concatenate([bd(ub[i]), bdv[i]], axis=0)) for i in idx]
        upd = [_dot_tn(jnp.concatenate([ub[i], vb[i]], axis=0),
                       jnp.concatenate([s["bbar"][i], s["kbar"][i]], axis=0)) for i in idx]
        for i, p in enumerate(grp):
            z_scr[p] = z[i] * jnp.exp(s["cum_end"][i]) + jnp.where(z_mask, upd[i], 0.0)

    def post(s, r0):
        y, sl, r, v, km = s["y"], s["sl"], s["r"], s["v"], s["km"]
        idx = range(len(y))
        inv_n = 1.0 / N
        mean = [half_sum(y[i]) * inv_n for i in idx]
        yc = [y[i] - mean[i] for i in idx]
        var = [half_sum(yc[i] * yc[i]) * inv_n for i in idx]
        bonus = [half_sum(r[i] * km[i] * rk_ref[:, sl[i]]) * v[i] for i in idx]
        for i in idx:
            yn = yc[i] * lax.rsqrt(var[i] + RWKV_GN_EPS) * lnw_ref[:, sl[i]] + lnb_ref[:, sl[i]]
            o_ref[pl.ds(r0, L), sl[i]] = ((yn + bonus[i]) * go_ref[pl.ds(r0, L), sl[i]]).astype(o_ref.dtype)

    group = list(pairs)

    n_chunks = tt // L
    cur = prep(group, 0)
    inverse(cur)
    for c in range(1, n_chunks):
        nxt = prep(group, c * L)
        state(group, cur)
        inverse(nxt)
        post(cur, (c - 1) * L)
        cur = nxt
    state(group, cur)
    post(cur, (n_chunks - 1) * L)


def _rwkv_core(r, k, v, lw, a, go, k_k, k_a, r_k, ln_w, ln_b, batch, seq_len, tt=256):
    m, d = r.shape
    nt = seq_len // tt
    npair = d // LANES
    blk = pl.BlockSpec((tt, d), lambda b, i: (b * nt + i, 0))
    par = _const_spec((1, d))
    row1 = lambda z: z.reshape(1, d)
    return pl.pallas_call(
        functools.partial(_rwkv_core_kernel, tt=tt),
        out_shape=jax.ShapeDtypeStruct((m, d), BF16),
        grid=(batch, nt),
        in_specs=[blk] * 6 + [par] * 5,
        out_specs=blk,
        scratch_shapes=[pltpu.VMEM((npair, 2 * RWKV_HEAD_DIM, LANES), F32)],
        compiler_params=_cparams(("arbitrary", "arbitrary")),
        name="rwkv_core",
    )(r, k, v, lw, a, go, row1(k_k), row1(k_a), row1(r_k), row1(ln_w), row1(ln_b))


def _rwkv_layer(x, norm_g, mu, w_r, w_k, w_v, w0, w_w1, w_w2, a0, a_w1, a_w2, g_w1, g_w2,
                k_k, k_a, r_k, ln_w, ln_b, batch, seq_len):
    r, k, v, lw, a, go = _rwkv_proj(x, norm_g, mu, w_r, w_k, w_v, w0, w_w1, w_w2, a0, a_w1, a_w2,
                                    g_w1, g_w2, seq_len)
    return _rwkv_core(r, k, v, lw, a, go, k_k, k_a, r_k, ln_w, ln_b, batch, seq_len)


def kernel(x, norm_mixer, norm_ffn, ffn_w_up, ffn_conv_w, ffn_conv_b, ffn_w_down, nsa_w_in, nsa_pe_k, nsa_pe_v, nsa_cmp_k_w1, nsa_cmp_k_w2, nsa_cmp_v_w1, nsa_cmp_v_w2, nsa_b_gate, nsa_w_out, mlstm_w_in, mlstm_conv_w, mlstm_conv_b, mlstm_b_gates, mlstm_norm, mlstm_w_out, rwkv_mu, rwkv_w_r, rwkv_w_k, rwkv_w_v, rwkv_w_o, rwkv_w0, rwkv_w_w1, rwkv_w_w2, rwkv_a0, rwkv_a_w1, rwkv_a_w2, rwkv_g_w1, rwkv_g_w2, rwkv_k_k, rwkv_k_a, rwkv_r_k, rwkv_ln_w, rwkv_ln_b, final_norm):
    batch, seq_len, d = x.shape
    depth = norm_mixer.shape[0]
    rope = _rope_tables(seq_len)
    xf = x.reshape(batch * seq_len, d)
    for i in range(depth):
        kind, j = i % 3, i // 3
        if kind == 0:
            w_o = nsa_w_out[j]
            o = _nsa_layer(xf, norm_mixer[i], nsa_w_in[j], nsa_pe_k[j], nsa_pe_v[j], nsa_cmp_k_w1[j],
                           nsa_cmp_k_w2[j], nsa_cmp_v_w1[j], nsa_cmp_v_w2[j], nsa_b_gate[j], rope, batch, seq_len)
        elif kind == 1:
            w_o = mlstm_w_out[j]
            o = _mlstm_layer(xf, norm_mixer[i], mlstm_w_in[j], mlstm_conv_w[j], mlstm_conv_b[j],
                             mlstm_b_gates[j], mlstm_norm[j], batch, seq_len)
        else:
            w_o = rwkv_w_o[j]
            o = _rwkv_layer(xf, norm_mixer[i], rwkv_mu[j], rwkv_w_r[j], rwkv_w_k[j], rwkv_w_v[j],
                            rwkv_w0[j], rwkv_w_w1[j], rwkv_w_w2[j], rwkv_a0[j],
                            rwkv_a_w1[j], rwkv_a_w2[j], rwkv_g_w1[j], rwkv_g_w2[j], rwkv_k_k[j],
                            rwkv_k_a[j], rwkv_r_k[j], rwkv_ln_w[j], rwkv_ln_b[j], batch, seq_len)
        xf = _ffn(xf, o, w_o, norm_ffn[i], ffn_w_up[i], ffn_conv_w[i], ffn_conv_b[i], ffn_w_down[i],
                  seq_len, final_g=final_norm if i == depth - 1 else None)
    return xf.reshape(batch, seq_len, d)
```

```python
import functools
import math

import jax
import jax.numpy as jnp
import numpy as np
from jax import lax
from jax.experimental import pallas as pl
from jax.experimental.pallas import tpu as pltpu

F32 = jnp.float32
BF16 = jnp.bfloat16

D_MODEL = 1024
NORM_EPS = 1e-6
ROPE_THETA = 500000.0

NSA_HEAD_DIM = 64
NSA_HEADS = 16
NSA_GROUPS = 4
NSA_REP = NSA_HEADS // NSA_GROUPS
NSA_ROT_DIM = 16
CMP_BLOCK = 32
CMP_STRIDE = 16
CMP_HIDDEN = 256
SEL_BLOCK = 64
SEL_TOPK = 16
WINDOW = 512
NSA_KV = NSA_GROUPS * NSA_HEAD_DIM
BF16_SUBLANES = 16
NSA_VT_ROWS = NSA_HEAD_DIM + BF16_SUBLANES
NSA_GATE_ROWS = 16
NSA_WINDOW_QUERIES = 256

MLSTM_HEADS = 8
MLSTM_QK_DIM = 64
MLSTM_V_DIM = 128
MLSTM_TILE = 128
MLSTM_CONV = 4

RWKV_HEAD_DIM = 64
RWKV_GN_EPS = 64e-5
RWKV_CHUNK = 64
RWKV_INV_BLOCK = 16

FFN_DIM = 2816
FFN_CONV = 3
FFN_CHUNK = 256
FFN_ROW_BLOCK = 64

LOG2E = math.log2(math.e)
MASKED = -1e30

LANES = 128
SUBLANES = 8
VMEM_LIMIT = 56 * 1024 * 1024


def _dot(a, b):
    return jnp.dot(a, b, preferred_element_type=F32)


def _dot_nt(a, b):
    return lax.dot_general(a, b, (((1,), (1,)), ((), ())), preferred_element_type=F32)


def _dot_tn(a, b):
    return lax.dot_general(a, b, (((0,), (0,)), ((), ())), preferred_element_type=F32)


def _split2(x):
    hi = x.astype(BF16)
    return hi, (x - hi.astype(F32)).astype(BF16)


def _split3(x):
    hi = x.astype(BF16)
    r1 = x - hi.astype(F32)
    mid = r1.astype(BF16)
    return hi, mid, (r1 - mid.astype(F32)).astype(BF16)


def _cumsum_rows(tri3, x):
    return _dot(tri3, jnp.concatenate(_split3(x), axis=0))


def _cumsum_lanes(x, triu3):
    return _dot(jnp.concatenate(_split3(x), axis=1), triu3)


def _rms(x, g):
    ms = jnp.mean(x * x, axis=-1, keepdims=True)
    return x * lax.rsqrt(ms + NORM_EPS) * g


def _sigmoid(x):
    return 1.0 / (1.0 + jnp.exp(-x))


def _cparams(sem):
    return pltpu.CompilerParams(dimension_semantics=sem, vmem_limit_bytes=VMEM_LIMIT)


def _const_spec(shape):
    n = len(shape)
    return pl.BlockSpec(shape, lambda *_: (0,) * n, pipeline_mode=pl.Buffered(1))


def _ffn_kernel(res_ref, a_ref, wo_ref, g_ref, wu_ref, cw_ref, cb_ref, wd_ref, fg_ref, o_ref,
                h_scr, carry_scr, act_scr, ga_scr, va_scr, gb_scr, vb_scr,
                *, tiles_per_seq, n_chunks, final_norm):
    tm = res_ref.shape[0]
    fc = FFN_CHUNK
    halo = SUBLANES
    x = res_ref[...] + _dot(a_ref[...], wo_ref[...])
    h_scr[...] = _rms(x, g_ref[...]).astype(BF16)
    o_ref[...] = x
    seq_start = (pl.program_id(0) % tiles_per_seq) == 0
    cols = lambda c, base=0: pl.ds(pl.multiple_of(base + c * fc, LANES), fc)

    def up(c, g_scr, v_scr):
        h = h_scr[...]
        g_scr[halo:, :] = _dot(h, wu_ref[:, cols(c)])
        v_scr[...] = _dot(h, wu_ref[:, cols(c, FFN_DIM)])

    def activate(c, g_scr, v_scr, slot):
        g_scr[0:halo, :] = jnp.where(seq_start, 0.0, carry_scr[:, cols(c)])
        carry_scr[:, cols(c)] = g_scr[tm:tm + halo, :]
        cw = cw_ref[:, cols(c)]
        cb = cb_ref[:, cols(c)]
        for r0 in range(0, tm, FFN_ROW_BLOCK):
            y = cb + sum(cw[FFN_CONV - 1 - s:FFN_CONV - s, :] * g_scr[halo - s + r0:halo - s + r0 + FFN_ROW_BLOCK, :]
                         for s in range(FFN_CONV))
            act = y * _sigmoid(y) * v_scr[r0:r0 + FFN_ROW_BLOCK, :]
            act_scr[r0:r0 + FFN_ROW_BLOCK, slot * fc:(slot + 1) * fc] = act.astype(BF16)

    def down(c0, n):
        o_ref[...] += _dot(act_scr[:, 0:n * fc], wd_ref[c0 * fc:(c0 + n) * fc, :])

    assert n_chunks % 2 == 1
    up(0, ga_scr, va_scr)
    for j in range(n_chunks // 2):
        up(2 * j + 1, gb_scr, vb_scr)
        activate(2 * j, ga_scr, va_scr, 0)
        up(2 * j + 2, ga_scr, va_scr)
        activate(2 * j + 1, gb_scr, vb_scr, 1)
        down(2 * j, 2)
    activate(n_chunks - 1, ga_scr, va_scr, 0)
    down(n_chunks - 1, 1)
    if final_norm:
        o_ref[...] = _rms(o_ref[...], fg_ref[...])


def _ffn(res, a, w_o, g, w_up, conv_w, conv_b, w_down, seq_len, final_g=None, tm=1024):
    m, d = res.shape
    nc = FFN_DIM // FFN_CHUNK
    cw = jnp.pad(conv_w, ((0, SUBLANES - FFN_CONV), (0, 0)))
    row = lambda i: (i, 0)
    fg = jnp.ones((d,), F32) if final_g is None else final_g
    kern = functools.partial(_ffn_kernel, tiles_per_seq=seq_len // tm, n_chunks=nc,
                             final_norm=final_g is not None)
    return pl.pallas_call(
        kern,
        out_shape=jax.ShapeDtypeStruct((m, d), F32),
        grid=(m // tm,),
        in_specs=[pl.BlockSpec((tm, d), row), pl.BlockSpec((tm, d), row), _const_spec((d, d)),
                  _const_spec((1, d)), _const_spec((d, 2 * FFN_DIM)),
                  _const_spec((SUBLANES, FFN_DIM)), _const_spec((1, FFN_DIM)),
                  _const_spec((FFN_DIM, d)), _const_spec((1, d))],
        out_specs=pl.BlockSpec((tm, d), row),
        scratch_shapes=[pltpu.VMEM((tm, d), BF16),
                        pltpu.VMEM((SUBLANES, FFN_DIM), F32),
                        pltpu.VMEM((tm, 2 * FFN_CHUNK), BF16)]
                       + [pltpu.VMEM((tm + SUBLANES, FFN_CHUNK), F32), pltpu.VMEM((tm, FFN_CHUNK), F32)] * 2,
        compiler_params=_cparams(("arbitrary",)),
        name="conv_ffn",
    )(res, a, w_o.astype(BF16), g.reshape(1, d), w_up.astype(BF16), cw, conv_b.reshape(1, FFN_DIM),
      w_down.astype(BF16), fg.reshape(1, d))


def _nsa_proj_kernel(x_ref, g_ref, w_ref, wvt_ref, bg_ref, rc_ref, rs1_ref, rs2_ref,
                     qp_ref, qr_ref, kc_ref, vc_ref, ks_ref, vs_ref, kw_ref, vw_ref, gate_ref,
                     cmp_scr, *, tiles_per_seq):
    hn = _rms(x_ref[...], g_ref[...]).astype(BF16)
    y = _dot(hn, w_ref[...])
    vt = _dot_nt(wvt_ref[...], hn)
    rc, rs1, rs2 = rc_ref[...], rs1_ref[...], rs2_ref[...]
    dh = NSA_HEAD_DIM

    def rope(z):
        half = NSA_ROT_DIM // 2
        return z * rc + pltpu.roll(z, half, 1) * rs1 + pltpu.roll(z, LANES - half, 1) * rs2

    scale = dh ** -0.5 * LOG2E
    for j in range(D_MODEL // LANES):
        q = y[:, j * LANES:(j + 1) * LANES] * scale
        qp_ref[:, j * LANES:(j + 1) * LANES] = q.astype(BF16)
        qr_ref[:, j * LANES:(j + 1) * LANES] = rope(q).astype(BF16)

    def kv_chunk(idx):
        return y[:, D_MODEL + idx * NSA_KV:D_MODEL + (idx + 1) * NSA_KV]

    def split_groups(z, ref, dtype):
        for g in range(NSA_GROUPS):
            ref[g] = z[:, g * dh:(g + 1) * dh].astype(dtype)

    def rope256(z):
        return jnp.concatenate([rope(z[:, :LANES]), rope(z[:, LANES:])], axis=1)

    tm = y.shape[0]
    nrow = tm // CMP_STRIDE
    for j in range(2 * NSA_KV // LANES):
        cmp_scr[j] = y[:, D_MODEL + j * LANES:D_MODEL + (j + 1) * LANES]
    for j in range(2 * NSA_KV // LANES):
        ref = kc_ref if j < NSA_KV // LANES else vc_ref
        toks = [cmp_scr[j, pl.ds(tok, nrow, stride=CMP_STRIDE), :] for tok in range(CMP_STRIDE)]
        for half in range(LANES // dh):
            g = (j % (NSA_KV // LANES)) * (LANES // dh) + half
            ref[g] = jnp.concatenate([t[:, half * dh:(half + 1) * dh] for t in toks], axis=1)
    t_pos = (pl.program_id(0) % tiles_per_seq) * tm + lax.broadcasted_iota(jnp.int32, (tm, LANES), 0)
    onehot = (lax.broadcasted_iota(jnp.int32, (tm, LANES), 1) == t_pos // SEL_BLOCK).astype(F32)
    ksel = rope256(kv_chunk(2))
    zpad = jnp.zeros((tm, LANES - dh), F32)
    for g in range(NSA_GROUPS):
        ks_ref[g] = jnp.concatenate([ksel[:, g * dh:(g + 1) * dh], zpad, onehot], axis=1).astype(BF16)
    split_groups(rope256(kv_chunk(3)), kw_ref, BF16)
    ones_pad = (lax.broadcasted_iota(jnp.int32, (NSA_VT_ROWS - dh, tm), 0) == 0).astype(F32)
    for g in range(NSA_GROUPS):
        vs_ref[g] = jnp.concatenate([vt[g * dh:(g + 1) * dh], ones_pad], axis=0).astype(BF16)
        vw_ref[g] = jnp.concatenate([vt[NSA_KV + g * dh:NSA_KV + (g + 1) * dh], ones_pad], axis=0).astype(BF16)
    gate = _sigmoid(vt[2 * NSA_KV:] + bg_ref[...])
    for g in range(NSA_GROUPS):
        gate_ref[g] = gate[g * NSA_GATE_ROWS:(g + 1) * NSA_GATE_ROWS]


def _rope_tables(seq_len):
    half = NSA_ROT_DIM // 2
    inv_freq = ROPE_THETA ** (-jnp.arange(half, dtype=F32) / half)
    ang = jnp.arange(seq_len, dtype=F32)[:, None] * inv_freq[None, :]
    cos, sin = jnp.cos(ang), jnp.sin(ang)
    zeros = jnp.zeros((seq_len, NSA_HEAD_DIM - NSA_ROT_DIM), F32)
    z8 = jnp.zeros((seq_len, half), F32)
    rc = jnp.concatenate([cos, cos, zeros + 1.0], axis=1)
    rs1 = jnp.concatenate([z8, sin, zeros], axis=1)
    rs2 = jnp.concatenate([-sin, z8, zeros], axis=1)
    two = lambda t: jnp.concatenate([t, t], axis=1)
    return two(rc), two(rs1), two(rs2)


def _nsa_proj(x, g, w_in, b_gate, rope, seq_len, tm=512):
    m, d = x.shape
    n_kv = 6 * NSA_KV
    kv = lambda idx: w_in[:, D_MODEL + idx * NSA_KV:D_MODEL + (idx + 1) * NSA_KV]
    pad_g = NSA_GATE_ROWS - NSA_REP * 3
    wg = w_in[:, D_MODEL + n_kv:].reshape(d, NSA_GROUPS, NSA_REP * 3)
    wg = jnp.pad(wg, ((0, 0), (0, 0), (0, pad_g))).reshape(d, NSA_GROUPS * NSA_GATE_ROWS)
    w = jnp.concatenate([w_in[:, :D_MODEL], kv(0), kv(1), kv(2), kv(4)], axis=1).astype(BF16)
    wvt = jnp.concatenate([kv(3), kv(5), wg], axis=1).T.astype(BF16)
    bg = jnp.pad(b_gate.reshape(NSA_GROUPS, NSA_REP * 3), ((0, 0), (0, pad_g)))
    bg = bg.reshape(NSA_GROUPS * NSA_GATE_ROWS, 1)
    n = w.shape[1]
    tps = seq_len // tm
    row = lambda i: (i, 0)
    rope_spec = pl.BlockSpec((tm, LANES), lambda i: (i % tps, 0))
    assert seq_len // SEL_BLOCK <= LANES
    g_out = lambda dt, w=NSA_HEAD_DIM: jax.ShapeDtypeStruct((NSA_GROUPS, m, w), dt)
    g_spec = pl.BlockSpec((NSA_GROUPS, tm, NSA_HEAD_DIM), lambda i: (0, i, 0))
    ks_spec = pl.BlockSpec((NSA_GROUPS, tm, 2 * LANES), lambda i: (0, i, 0))
    cmp_out = jax.ShapeDtypeStruct((NSA_GROUPS, m // CMP_STRIDE, CMP_STRIDE * NSA_HEAD_DIM), F32)
    cmp_spec = pl.BlockSpec((NSA_GROUPS, tm // CMP_STRIDE, CMP_STRIDE * NSA_HEAD_DIM), lambda i: (0, i, 0))
    vt_out = jax.ShapeDtypeStruct((NSA_GROUPS, NSA_VT_ROWS, m), BF16)
    vt_spec = pl.BlockSpec((NSA_GROUPS, NSA_VT_ROWS, tm), lambda i: (0, 0, i))
    return pl.pallas_call(
        functools.partial(_nsa_proj_kernel, tiles_per_seq=tps),
        out_shape=(jax.ShapeDtypeStruct((m, d), BF16), jax.ShapeDtypeStruct((m, d), BF16),
                   cmp_out, cmp_out, g_out(BF16, 2 * LANES), vt_out, g_out(BF16), vt_out,
                   jax.ShapeDtypeStruct((NSA_GROUPS, NSA_GATE_ROWS, m), F32)),
        grid=(m // tm,),
        in_specs=[pl.BlockSpec((tm, d), row), _const_spec((1, d)), _const_spec((d, n)),
                  _const_spec((2 * NSA_KV + NSA_GROUPS * NSA_GATE_ROWS, d)),
                  _const_spec((NSA_GROUPS * NSA_GATE_ROWS, 1)), rope_spec, rope_spec, rope_spec],
        out_specs=(pl.BlockSpec((tm, d), row), pl.BlockSpec((tm, d), row),
                   cmp_spec, cmp_spec, ks_spec, vt_spec, g_spec, vt_spec,
                   pl.BlockSpec((NSA_GROUPS, NSA_GATE_ROWS, tm), lambda i: (0, 0, i))),
        scratch_shapes=[pltpu.VMEM((2 * NSA_KV // LANES, tm, LANES), F32)],
        compiler_params=_cparams(("parallel",)),
        name="nsa_proj",
    )(x, g.reshape(1, d), w, wvt, bg, *rope)


def _gelu_tanh(x):
    return 0.5 * x * (1.0 + jnp.tanh(math.sqrt(2.0 / math.pi) * (x + 0.044715 * (x * x * x))))


def _compress_kernel(zk_ref, zv_ref, pek_ref, pev_ref, w1k_ref, w2k_ref, w1v_ref, w2v_ref,
                     kc_ref, vc_ref):
    nrow = zk_ref.shape[0]
    rows = lax.broadcasted_iota(jnp.int32, (nrow, NSA_HEAD_DIM), 0)

    def one(z_ref, pe_ref, w1_ref, w2_ref, o_ref):
        z = z_ref[...]
        a = _dot((z + pe_ref[0:1, :]).astype(BF16), w1_ref[0])
        b = _dot((z + pe_ref[1:2, :]).astype(BF16), w1_ref[1])
        hid = a + pltpu.roll(b, nrow - 1, 0)
        out = _dot(_gelu_tanh(hid).astype(BF16), w2_ref[...])
        o_ref[...] = jnp.where(rows == nrow - 1, 0.0, out).astype(o_ref.dtype)

    one(zk_ref, pek_ref, w1k_ref, w2k_ref, kc_ref)
    one(zv_ref, pev_ref, w1v_ref, w2v_ref, vc_ref)


def _compress(kc_raw, vc_raw, pe_k, pe_v, w1k, w2k, w1v, w2v, seq_len):
    g, nrows, half = kc_raw.shape
    dh = half // CMP_STRIDE
    nchunk = seq_len // CMP_STRIDE
    zk = kc_raw.reshape(g * nrows, half)
    zv = vc_raw.reshape(g * nrows, half)
    pe2 = lambda pe: pe.reshape(2, half)
    w1 = lambda w: w.astype(BF16).reshape(2, half, CMP_HIDDEN)
    nblk = zk.shape[0] // nchunk
    row = lambda i: (i, 0)
    return pl.pallas_call(
        _compress_kernel,
        out_shape=(jax.ShapeDtypeStruct((zk.shape[0], dh), BF16),
                   jax.ShapeDtypeStruct((zk.shape[0], dh), BF16)),
        grid=(nblk,),
        in_specs=[pl.BlockSpec((nchunk, half), row), pl.BlockSpec((nchunk, half), row),
                  _const_spec((2, half)), _const_spec((2, half)),
                  _const_spec((2, half, CMP_HIDDEN)), _const_spec((CMP_HIDDEN, dh)),
                  _const_spec((2, half, CMP_HIDDEN)), _const_spec((CMP_HIDDEN, dh))],
        out_specs=(pl.BlockSpec((nchunk, dh), row), pl.BlockSpec((nchunk, dh), row)),
        compiler_params=_cparams(("parallel",)),
        name="nsa_compress",
    )(zk, zv, pe2(pe_k), pe2(pe_v), w1(w1k), w2k.astype(BF16), w1(w1v), w2v.astype(BF16))


def _nsa_cmp_kernel(q_ref, kc_ref, vc_ref, gate_ref, ovt_ref, oc_ref, selt_ref, imp_scr, *, tq, n_classes):
    qi = pl.program_id(2)
    nq = pl.num_programs(2)
    dh = NSA_HEAD_DIM
    ncmp = kc_ref.shape[0]
    nsel = imp_scr.shape[0]
    heads = range(NSA_REP)

    def attend(nrows):
        q = q_ref[...]
        qs = jnp.concatenate([q[:, r * dh:(r + 1) * dh] for r in heads], axis=0)
        kc, vc = kc_ref[0:nrows, :], vc_ref[0:nrows, :]
        t = qi * tq + lax.broadcasted_iota(jnp.int32, (nrows, tq), 1)
        cmp_end = lax.broadcasted_iota(jnp.int32, (nrows, tq), 0) * CMP_STRIDE + (CMP_BLOCK - 1)
        neg = jnp.where(cmp_end <= t, 0.0, -jnp.inf)
        st = [_dot_nt(kc, qs[r * tq:(r + 1) * tq]) + neg for r in heads]
        mx = [jnp.max(st[r], axis=0, keepdims=True) for r in heads]
        mx = [jnp.where(mx[r] == -jnp.inf, 0.0, mx[r]) for r in heads]
        e = [jnp.exp2(st[r] - mx[r]) for r in heads]
        p = [e[r] / jnp.maximum(jnp.sum(e[r], axis=0, keepdims=True), 1e-30) for r in heads]
        o_t = [_dot_tn(vc, p[r].astype(BF16)) for r in heads]
        gate = gate_ref[...]
        o_t = [o_t[r] * gate[3 * r:3 * r + 1, :] for r in heads]
        for j in range(NSA_REP // 2):
            pair = jnp.concatenate([o_t[2 * j], o_t[2 * j + 1]], axis=0).T
            oc_ref[:, 2 * j * dh:2 * (j + 1) * dh] = pair
        psum = sum(p)
        ovt = jnp.concatenate([ovt_ref[:, j * ncmp:j * ncmp + nrows] for j in range(3)], axis=1)
        imp_scr[...] = _dot(ovt, jnp.concatenate(_split3(psum), axis=0))

    def select(nblk):
        blk = lax.broadcasted_iota(jnp.int32, (nblk, LANES), 0)
        blk_f = blk.astype(F32)
        if nblk < nsel:
            selt_ref[nblk:, :] = jnp.zeros((nsel - nblk, tq), selt_ref.dtype)
        for cb in range(tq // LANES):
            csl = slice(cb * LANES, (cb + 1) * LANES)
            tb = (qi * tq + cb * LANES + lax.broadcasted_iota(jnp.int32, (nblk, LANES), 1)) // SEL_BLOCK
            forced = (blk == 0) | (blk == tb) | (blk == tb - 1)
            vals = jnp.where(forced, -jnp.inf, jnp.where(blk <= tb, imp_scr[0:nblk, csl], -1.0))
            for _ in range(SEL_TOPK - 3):
                top = jnp.max(vals, axis=0, keepdims=True)
                first = jnp.min(jnp.where(vals == top, blk_f, float(nsel)), axis=0, keepdims=True)
                vals = jnp.where(blk_f == first, -jnp.inf, vals)
            selt_ref[0:nblk, csl] = jnp.where(vals == -jnp.inf, 1.0, 0.0).astype(selt_ref.dtype)

    for cls in range(n_classes):
        @pl.when((qi * n_classes) // nq == cls)
        def _(cls=cls):
            attend((cls + 1) * ncmp // n_classes)
            select((cls + 1) * nsel // n_classes)


def _overlap_matrix_t3(ncmp_pad, nsel):
    c = np.arange(ncmp_pad)[None, :]
    s = np.arange(nsel)[:, None]
    cmp_start = c * CMP_STRIDE
    cmp_end = cmp_start + CMP_BLOCK - 1
    blk_start = s * SEL_BLOCK
    ov = ((cmp_end >= blk_start) & (cmp_start <= blk_start + SEL_BLOCK - 1)).astype(np.float32)
    return jnp.asarray(np.concatenate([ov, ov, ov], axis=1), dtype=BF16)


def _nsa_cmp(qp, kc, vc, gates, batch, seq_len, tq=512):
    m, d = qp.shape
    nq = seq_len // tq
    ncmp = seq_len // CMP_STRIDE
    nsel = seq_len // SEL_BLOCK
    ovt3 = _overlap_matrix_t3(ncmp, nsel)
    qmap = lambda b, g, i: (b * nq + i, g)
    kmap = lambda b, g, i: (g * batch + b, 0)
    n_classes = math.gcd(nq, 4)
    return pl.pallas_call(
        functools.partial(_nsa_cmp_kernel, tq=tq, n_classes=n_classes),
        out_shape=(jax.ShapeDtypeStruct((m, d), F32),
                   jax.ShapeDtypeStruct((NSA_GROUPS, nsel, m), BF16)),
        grid=(batch, NSA_GROUPS, nq),
        in_specs=[pl.BlockSpec((tq, NSA_KV), qmap),
                  pl.BlockSpec((ncmp, NSA_HEAD_DIM), kmap), pl.BlockSpec((ncmp, NSA_HEAD_DIM), kmap),
                  pl.BlockSpec((None, NSA_GATE_ROWS, tq), lambda b, g, i: (g, 0, b * nq + i)),
                  _const_spec((nsel, 3 * ncmp))],
        out_specs=(pl.BlockSpec((tq, NSA_KV), qmap),
                   pl.BlockSpec((None, nsel, tq), lambda b, g, i: (g, 0, b * nq + i))),
        scratch_shapes=[pltpu.VMEM((nsel, tq), F32)],
        compiler_params=_cparams(("parallel", "parallel", "parallel")),
        name="nsa_cmp_topk",
    )(qp, kc, vc, gates, ovt3)


def _nsa_sel_kernel(q_ref, ks_ref, vs_ref, kw_ref, vw_ref, sel_ref, gate_ref, oc_ref, o_ref,
                    m_scr, acc_scr, ow_scr, sta_scr, stb_scr, *, tq, tk):
    qi = pl.program_id(2)
    dh = NSA_HEAD_DIM
    cols = NSA_REP * tq
    q = q_ref[...]
    qs = jnp.concatenate([q[:, r * dh:(r + 1) * dh] for r in range(NSA_REP)], axis=0)
    selt = sel_ref[...].astype(F32)
    nsel = selt.shape[0]
    bmask_t = jnp.where(selt > 0.5, 0.0, MASKED)
    if nsel < LANES:
        bmask_t = jnp.concatenate([bmask_t, jnp.zeros((LANES - nsel, tq), F32)], axis=0)
    bmask = bmask_t.T.astype(BF16)
    zpad = jnp.zeros((tq, LANES - dh), BF16)
    qa = jnp.concatenate([jnp.concatenate([q[:, r * dh:(r + 1) * dh], zpad, bmask], axis=1)
                          for r in range(NSA_REP)], axis=0)
    q0 = qi * tq
    key_iota = lax.broadcasted_iota(jnp.int32, (tk, tq), 0)
    t_pos = q0 + lax.broadcasted_iota(jnp.int32, (tk, tq), 1)

    m_scr[...] = jnp.full((1, cols), -jnp.inf, F32)
    acc_scr[...] = jnp.zeros((NSA_VT_ROWS, cols), F32)

    heads = range(NSA_REP)
    hsl = [slice(r * tq, (r + 1) * tq) for r in heads]

    def put_scores(scr, ki):
        k = ks_ref[pl.ds(pl.multiple_of(ki * tk, tk), tk), :]
        for r in heads:
            scr[r] = _dot_nt(k, qa[hsl[r]])

    def get_scores(scr):
        return [scr[r] for r in heads]

    def consume(ki, st, causal):
        k0 = pl.multiple_of(ki * tk, tk)
        vt = vs_ref[:, pl.ds(k0, tk)]
        if causal:
            neg = jnp.where(k0 + key_iota <= t_pos, 0.0, -jnp.inf)
            st = [s + neg for s in st]
        m_old = [m_scr[:, hsl[r]] for r in heads]
        m_new = [jnp.maximum(m_old[r], jnp.max(st[r], axis=0, keepdims=True)) for r in heads]
        alpha = [jnp.exp2(m_old[r] - m_new[r]) for r in heads]
        p = [jnp.exp2(st[r] - m_new[r]).astype(BF16) for r in heads]
        pv = [_dot(vt, p[r]) for r in heads]
        for r in heads:
            acc_scr[:, hsl[r]] = alpha[r] * acc_scr[:, hsl[r]] + pv[r]
            m_scr[:, hsl[r]] = m_new[r]

    def tile_pair(j, carry):
        even = get_scores(sta_scr)
        put_scores(stb_scr, 2 * j + 1)
        consume(2 * j, even, False)
        odd = get_scores(stb_scr)
        put_scores(sta_scr, 2 * j + 2)
        consume(2 * j + 1, odd, False)
        return carry

    last = (q0 + tq - 1) // tk

    gate = gate_ref[...]
    wq = min(tq, NSA_WINDOW_QUERIES)
    wsub = wq + WINDOW
    nsub = tq // wq
    rel_iota = (lax.broadcasted_iota(jnp.int32, (wsub, wq), 0)
                - lax.broadcasted_iota(jnp.int32, (wsub, wq), 1))
    starts = [pl.multiple_of(jnp.maximum(q0 + wq * u - WINDOW, 0), wq) for u in range(nsub)]
    kw = [kw_ref[pl.ds(starts[u], wsub), :] for u in range(nsub)]
    vwt = [vw_ref[:, pl.ds(starts[u], wsub)] for u in range(nsub)]
    rel = [starts[u] - (q0 + wq * u) + rel_iota for u in range(nsub)]
    neg_w = [jnp.where((rel[u] <= 0) & (rel[u] > -WINDOW), 0.0, -jnp.inf) for u in range(nsub)]
    subs = [(r, u) for r in heads for u in range(nsub)]
    sw = [_dot_nt(kw[u], qs[r * tq + wq * u:r * tq + wq * (u + 1)]) + neg_w[u] for r, u in subs]
    put_scores(sta_scr, 0)
    mw = [jnp.max(s, axis=0, keepdims=True) for s in sw]
    mw = [jnp.where(m == -jnp.inf, 0.0, m) for m in mw]
    ew = [jnp.exp2(s - m).astype(BF16) for s, m in zip(sw, mw)]
    pvw = [_dot(vwt[u], e) for (r, u), e in zip(subs, ew)]
    pvw = [p[0:dh] / jnp.maximum(p[dh:dh + 1], 1e-30) for p in pvw]
    for r in heads:
        o_win = jnp.concatenate(pvw[r * nsub:(r + 1) * nsub], axis=1)
        ow_scr[:, hsl[r]] = o_win * gate[3 * r + 2:3 * r + 3, :]

    n_pairs = last // 2

    def tile_quad(jq, carry):
        tile_pair(2 * jq, carry)
        tile_pair(2 * jq + 1, carry)
        return carry

    lax.fori_loop(0, n_pairs // 2, tile_quad, 0)

    @pl.when(n_pairs % 2 == 1)
    def _():
        tile_pair(n_pairs - 1, 0)

    @pl.when(last % 2 == 0)
    def _():
        consume(last, get_scores(sta_scr), True)

    @pl.when(last % 2 == 1)
    def _():
        even = get_scores(sta_scr)
        put_scores(stb_scr, last)
        consume(last - 1, even, False)
        consume(last, get_scores(stb_scr), True)
    o_sel = acc_scr[0:dh, :] / jnp.maximum(acc_scr[dh:dh + 1, :], 1e-30)

    mix = [o_sel[:, hsl[r]] * gate[3 * r + 1:3 * r + 2, :] + ow_scr[:, hsl[r]] for r in heads]
    for j in range(NSA_REP // 2):
        psl = slice(2 * j * dh, 2 * (j + 1) * dh)
        pair = jnp.concatenate([mix[2 * j], mix[2 * j + 1]], axis=0).T
        o_ref[:, psl] = (oc_ref[:, psl] + pair).astype(o_ref.dtype)


def _nsa_sel(qr, ks, vs, kw, vw, sel, gates, oc, batch, seq_len, tq=512, tk=512):
    m, d = qr.shape
    nq = seq_len // tq
    nsel = seq_len // SEL_BLOCK
    qmap = lambda b, g, i: (b * nq + i, g)
    kvmap = lambda b, g, i: (g, b, 0)
    kv_spec = pl.BlockSpec((None, seq_len, NSA_HEAD_DIM), kvmap)
    ks_spec = pl.BlockSpec((None, seq_len, 2 * LANES), kvmap)
    vt_spec = pl.BlockSpec((None, NSA_VT_ROWS, seq_len), lambda b, g, i: (g, 0, b))
    cols = NSA_REP * tq
    assert tk % tq == 0
    return pl.pallas_call(
        functools.partial(_nsa_sel_kernel, tq=tq, tk=tk),
        out_shape=jax.ShapeDtypeStruct((m, d), BF16),
        grid=(batch, NSA_GROUPS, nq),
        in_specs=[pl.BlockSpec((tq, NSA_KV), qmap), ks_spec, vt_spec, kv_spec, vt_spec,
                  pl.BlockSpec((None, nsel, tq), lambda b, g, i: (g, 0, b * nq + i)),
                  pl.BlockSpec((None, NSA_GATE_ROWS, tq), lambda b, g, i: (g, 0, b * nq + i)),
                  pl.BlockSpec((tq, NSA_KV), qmap)],
        out_specs=pl.BlockSpec((tq, NSA_KV), qmap),
        scratch_shapes=[pltpu.VMEM((1, cols), F32), pltpu.VMEM((NSA_VT_ROWS, cols), F32),
                        pltpu.VMEM((NSA_HEAD_DIM, cols), F32),
                        pltpu.VMEM((NSA_REP, tk, tq), F32), pltpu.VMEM((NSA_REP, tk, tq), F32)],
        compiler_params=_cparams(("parallel", "parallel", "parallel")),
        name="nsa_sel_win",
    )(qr, ks, vs, kw, vw, sel, gates, oc)


def _nsa_layer(x, norm_g, w_in, pe_k, pe_v, w1k, w2k, w1v, w2v, b_gate, rope, batch, seq_len):
    qp, qr, kc_raw, vc_raw, ks, vs, kw, vw, gates = _nsa_proj(x, norm_g, w_in, b_gate, rope, seq_len)
    kc, vc = _compress(kc_raw, vc_raw, pe_k, pe_v, w1k, w2k, w1v, w2v, seq_len)
    oc, sel = _nsa_cmp(qp, kc, vc, gates, batch, seq_len)
    return _nsa_sel(qr, ks, vs, kw, vw, sel, gates, oc, batch, seq_len)


def _mlstm_proj_kernel(x_ref, g_ref, w_ref, wot_ref, wt_ref, bcol_ref, brow_ref,
                       qk_ref, v_ref, ot_ref, gc_ref, gr_ref):
    hn = _rms(x_ref[...], g_ref[...]).astype(BF16)
    y = _dot(hn, w_ref[...])
    d = D_MODEL
    qk_ref[...] = y[:, :d]
    v_ref[...] = y[:, d:2 * d].astype(BF16)
    ot_ref[...] = _sigmoid(_dot_nt(wot_ref[...], hn))
    gc_ref[...] = y[:, 2 * d:] + bcol_ref[...]
    gr_ref[...] = _dot_nt(wt_ref[...], hn) + brow_ref[...]


def _mlstm_proj(x, g, w_in, b_gates, tm=512):
    m, d = x.shape
    h, dk = MLSTM_HEADS, MLSTM_QK_DIM
    wq = w_in[:, :h * dk].reshape(d, h, dk)
    wk = w_in[:, h * dk:2 * h * dk].reshape(d, h, dk)
    wqk = jnp.concatenate([wq, wk], axis=2).reshape(d, 2 * h * dk)
    wv = w_in[:, d:2 * d]
    wif = w_in[:, 2 * d:2 * d + 2 * h]
    wo = w_in[:, 2 * d + 2 * h:]
    w = jnp.concatenate([wqk, wv, jnp.pad(wif, ((0, 0), (0, LANES - 2 * h)))], axis=1).astype(BF16)
    wot = wo.T.astype(BF16)
    wt = wif.T.astype(BF16)
    bcol = jnp.pad(b_gates, (0, LANES - 2 * h)).reshape(1, LANES)
    brow = b_gates.reshape(2 * h, 1)
    n = w.shape[1]
    row = lambda i: (i, 0)
    col = lambda i: (0, i)
    return pl.pallas_call(
        _mlstm_proj_kernel,
        out_shape=(jax.ShapeDtypeStruct((m, d), F32), jax.ShapeDtypeStruct((m, d), BF16),
                   jax.ShapeDtypeStruct((d, m), F32), jax.ShapeDtypeStruct((m, LANES), F32),
                   jax.ShapeDtypeStruct((2 * h, m), F32)),
        grid=(m // tm,),
        in_specs=[pl.BlockSpec((tm, d), row), _const_spec((1, d)), _const_spec((d, n)), _const_spec((d, d)),
                  _const_spec((2 * h, d)), _const_spec((1, LANES)), _const_spec((2 * h, 1))],
        out_specs=(pl.BlockSpec((tm, d), row), pl.BlockSpec((tm, d), row), pl.BlockSpec((d, tm), col),
                   pl.BlockSpec((tm, LANES), row), pl.BlockSpec((2 * h, tm), col)),
        compiler_params=_cparams(("parallel",)),
        name="mlstm_proj",
    )(x, g.reshape(1, d), w, wot, wt, bcol, brow)


def _log_sigmoid(x):
    return jnp.minimum(x, 0.0) - jnp.log(1.0 + jnp.exp(-jnp.abs(x)))


def _mlstm_core_kernel(qk_ref, v_ref, og_ref, gc_ref, gr_ref, cw_ref, cb_ref, ng_ref, o_ref,
                       qkc_scr, ext_scr, c_scr, n_scr, m_scr, *, tt):
    L = MLSTM_TILE
    H, dk, dv = MLSTM_HEADS, MLSTM_QK_DIM, MLSTM_V_DIM
    seq_start = pl.program_id(1) == 0

    @pl.when(seq_start)
    def _():
        c_scr[...] = jnp.zeros_like(c_scr)
        n_scr[...] = jnp.zeros_like(n_scr)
        m_scr[...] = jnp.zeros_like(m_scr)
        ext_scr[0:SUBLANES, :] = jnp.zeros((SUBLANES, ext_scr.shape[1]), F32)

    ext_scr[SUBLANES:, :] = qk_ref[...]
    is_k = lax.broadcasted_iota(jnp.int32, (tt, 2 * dk), 1) >= dk
    for h in range(H):
        hsl = slice(h * 2 * dk, (h + 1) * 2 * dk)
        cw = cw_ref[:, hsl]
        acc = cb_ref[:, hsl] + cw[MLSTM_CONV - 1:MLSTM_CONV, :] * ext_scr[SUBLANES:, hsl]
        for s in range(1, MLSTM_CONV):
            acc = acc + cw[MLSTM_CONV - 1 - s:MLSTM_CONV - s, :] * ext_scr[SUBLANES - s:SUBLANES - s + tt, hsl]
        act = acc * _sigmoid(acc)
        qkc_scr[:, hsl] = jnp.where(is_k, act * dk ** -0.5, act)
    ext_scr[0:SUBLANES, :] = ext_scr[tt:tt + SUBLANES, :]

    src = lax.broadcasted_iota(jnp.int32, (L, L), 0)
    tgt = lax.broadcasted_iota(jnp.int32, (L, L), 1)
    causal = src <= tgt
    tri3 = jnp.concatenate([(src >= tgt).astype(BF16)] * 3, axis=1)
    triu3 = jnp.concatenate([causal.astype(BF16)] * 3, axis=0)
    n_pad = jnp.zeros((SUBLANES - 3, dk), BF16)

    def chunk(c, carry):
        r0 = pl.multiple_of(c * L, L)
        gcol = gc_ref[pl.ds(r0, L), :]
        grow = gr_ref[c]
        b_col = _cumsum_rows(tri3, _log_sigmoid(gcol))
        b_row = _cumsum_lanes(_log_sigmoid(grow), triu3)
        hs = range(H)
        vsl = [slice(h * dv, (h + 1) * dv) for h in hs]
        qk = [qkc_scr[pl.ds(r0, L), h * 2 * dk:(h + 1) * 2 * dk] for h in hs]
        q = [qk[h][:, :dk].astype(BF16) for h in hs]
        k = [qk[h][:, dk:] for h in hs]
        v = [v_ref[pl.ds(r0, L), vsl[h]] for h in hs]
        col = [gcol[:, h:h + 1] - b_col[:, H + h:H + h + 1] for h in hs]
        li_r = [grow[h:h + 1, :] for h in hs]
        b_r = [b_row[H + h:H + h + 1, :] for h in hs]
        b_end = [b_r[h][:, L - 1:L] for h in hs]
        dmat = [jnp.where(causal, b_r[h] + col[h], -jnp.inf) for h in hs]
        d_max = [jnp.max(dmat[h], axis=0, keepdims=True) for h in hs]
        att = [jnp.exp(dmat[h] - d_max[h]) * _dot_nt(k[h].astype(BF16), q[h]) for h in hs]
        a_sum = [jnp.sum(att[h], axis=0, keepdims=True) for h in hs]
        intra = [_dot_tn(v[h], att[h].astype(BF16)) for h in hs]
        g_max = [jnp.max(b_end[h] - b_r[h] + li_r[h], axis=-1, keepdims=True) for h in hs]
        kw = [k[h] * jnp.exp(b_end[h] + col[h] - g_max[h]) for h in hs]
        c_loc = [_dot_tn(v[h], kw[h].astype(BF16)) for h in hs]
        n_loc = [jnp.sum(kw[h], axis=0, keepdims=True) for h in hs]
        m_prev = [m_scr[h:h + 1, 0:1] for h in hs]
        c_prev = [c_scr[h] for h in hs]
        n_prev = [n_scr[h:h + 1, :] for h in hs]
        m_inter = [b_r[h] + m_prev[h] for h in hs]
        m_t = [jnp.maximum(m_inter[h], d_max[h]) for h in hs]
        w_loc = [jnp.exp(d_max[h] - m_t[h]) for h in hs]
        w_int = [jnp.exp(m_inter[h] - m_t[h]) for h in hs]
        q_c = [_dot_nt(c_prev[h].astype(BF16), q[h]) for h in hs]
        q_n = [_dot_nt(jnp.concatenate(list(_split3(n_prev[h])) + [n_pad], axis=0), q[h]) for h in hs]
        q_n = [q_n[h][0:1] + q_n[h][1:2] + q_n[h][2:3] for h in hs]
        num = [w_loc[h] * intra[h] + w_int[h] * q_c[h] for h in hs]
        den = [w_loc[h] * a_sum[h] + w_int[h] * q_n[h] for h in hs]
        h_t = [num[h] / jnp.maximum(jnp.abs(den[h]), jnp.exp(-m_t[h])) for h in hs]
        h_t = [h_t[h] * lax.rsqrt(jnp.mean(h_t[h] * h_t[h], axis=0, keepdims=True) + NORM_EPS) for h in hs]
        m_new = [jnp.maximum(b_end[h] + m_prev[h], g_max[h]) for h in hs]
        a = [jnp.exp(b_end[h] + m_prev[h] - m_new[h]) for h in hs]
        sc = [jnp.exp(g_max[h] - m_new[h]) for h in hs]
        for h in hs:
            out = h_t[h] * ng_ref[vsl[h], :] * og_ref[vsl[h], pl.ds(r0, L)]
            o_ref[pl.ds(r0, L), vsl[h]] = out.T.astype(o_ref.dtype)
            c_scr[h] = a[h] * c_prev[h] + sc[h] * c_loc[h]
            n_scr[h:h + 1, :] = a[h] * n_prev[h] + sc[h] * n_loc[h]
            m_scr[h:h + 1, :] = jnp.broadcast_to(m_new[h], (1, LANES))
        return carry

    lax.fori_loop(0, tt // L, chunk, 0)


def _mlstm_core(qk, v, ogt, gcol, grow3, conv_w, conv_b, norm_gb, batch, seq_len, tt=256):
    m, d = qk.shape
    H, dk, dv = MLSTM_HEADS, MLSTM_QK_DIM, MLSTM_V_DIM
    nt = seq_len // tt
    ncs = tt // MLSTM_TILE
    row = lambda b, i: (b * nt + i, 0)
    return pl.pallas_call(
        functools.partial(_mlstm_core_kernel, tt=tt),
        out_shape=jax.ShapeDtypeStruct((m, d), BF16),
        grid=(batch, nt),
        in_specs=[pl.BlockSpec((tt, d), row), pl.BlockSpec((tt, d), row),
                  pl.BlockSpec((d, tt), lambda b, i: (0, b * nt + i)),
                  pl.BlockSpec((tt, LANES), row),
                  pl.BlockSpec((ncs, 2 * H, MLSTM_TILE), lambda b, i: (b * nt + i, 0, 0)),
                  _const_spec((SUBLANES, d)), _const_spec((1, d)), _const_spec((d, LANES))],
        out_specs=pl.BlockSpec((tt, d), row),
        scratch_shapes=[pltpu.VMEM((tt, d), F32), pltpu.VMEM((tt + SUBLANES, d), F32),
                        pltpu.VMEM((H, dv, dk), F32), pltpu.VMEM((H, dk), F32),
                        pltpu.VMEM((H, LANES), F32)],
        compiler_params=_cparams(("arbitrary", "arbitrary")),
        name="mlstm_core",
    )(qk, v, ogt, gcol, grow3, conv_w, conv_b, norm_gb)


def _mlstm_layer(x, norm_g, w_in, conv_w, conv_b, b_gates, hnorm, batch, seq_len):
    H, dk = MLSTM_HEADS, MLSTM_QK_DIM
    qk, v, ogt, gcol, grow = _mlstm_proj(x, norm_g, w_in, b_gates)
    m = x.shape[0]
    grow3 = grow.reshape(2 * H, m // MLSTM_TILE, MLSTM_TILE).transpose(1, 0, 2)
    norm_gb = jnp.broadcast_to(hnorm.reshape(-1, 1), (hnorm.shape[0], LANES))

    def perm(z):
        lead = z.shape[:-1]
        zq = z[..., :H * dk].reshape(*lead, H, dk)
        zk = z[..., H * dk:].reshape(*lead, H, dk)
        return jnp.concatenate([zq, zk], axis=-1).reshape(*lead, 2 * H * dk)

    cw = jnp.pad(perm(conv_w), ((0, SUBLANES - MLSTM_CONV), (0, 0)))
    return _mlstm_core(qk, v, ogt, gcol, grow3, cw, perm(conv_b).reshape(1, -1), norm_gb,
                       batch, seq_len)


def _softplus(x):
    return jnp.maximum(x, 0.0) + jnp.log(1.0 + jnp.exp(-jnp.abs(x)))


def _rwkv_proj_kernel(x_ref, g_ref, mu_ref, wr_ref, wk_ref, wv_ref, ww1_ref, ww2_ref, w0_ref,
                      aw1_ref, aw2_ref, a0_ref, gw1_ref, gw2_ref,
                      r_ref, k_ref, v_ref, lw_ref, a_ref, go_ref, prev_scr, *, tiles_per_seq):
    tm = x_ref.shape[0]
    h = _rms(x_ref[...], g_ref[...])
    seq_start = (pl.program_id(0) % tiles_per_seq) == 0
    prev = jnp.where(seq_start, 0.0, prev_scr[SUBLANES - 1:SUBLANES, :])
    prev_scr[...] = h[tm - SUBLANES:, :]
    rows = lax.broadcasted_iota(jnp.int32, h.shape, 0)
    xx = jnp.where(rows == 0, prev, pltpu.roll(h, 1, 0)) - h
    mix = lambda j: (h + xx * mu_ref[j:j + 1, :]).astype(BF16)
    r_ref[...] = _dot(mix(0), wr_ref[...])
    k_ref[...] = _dot(mix(2), wk_ref[...])
    v_ref[...] = _dot(mix(3), wv_ref[...])
    wl = _dot(jnp.tanh(_dot(mix(1), ww1_ref[...])).astype(BF16), ww2_ref[...]) + w0_ref[...]
    w_log = -_softplus(-wl) - 0.5
    lw_ref[...] = -jnp.exp(w_log)
    al = _dot(_dot(mix(4), aw1_ref[...]).astype(BF16), aw2_ref[...]) + a0_ref[...]
    a_ref[...] = _sigmoid(al)
    go_ref[...] = _dot(_sigmoid(_dot(mix(5), gw1_ref[...])).astype(BF16), gw2_ref[...])


def _rwkv_proj(x, g, mu, w_r, w_k, w_v, w0, w_w1, w_w2, a0, a_w1, a_w2, g_w1, g_w2, seq_len, tm=512):
    m, d = x.shape
    row = lambda i: (i, 0)
    bf = lambda w: w.astype(BF16)
    mu8 = jnp.pad(mu, ((0, SUBLANES - mu.shape[0]), (0, 0)))
    consts = [g.reshape(1, d), mu8, bf(w_r), bf(w_k), bf(w_v), bf(w_w1), bf(w_w2), w0.reshape(1, d),
              bf(a_w1), bf(a_w2), a0.reshape(1, d), bf(g_w1), bf(g_w2)]
    out = jax.ShapeDtypeStruct((m, d), F32)
    return pl.pallas_call(
        functools.partial(_rwkv_proj_kernel, tiles_per_seq=seq_len // tm),
        out_shape=(out,) * 6,
        grid=(m // tm,),
        in_specs=[pl.BlockSpec((tm, d), row)] + [_const_spec(c.shape) for c in consts],
        out_specs=(pl.BlockSpec((tm, d), row),) * 6,
        scratch_shapes=[pltpu.VMEM((SUBLANES, d), F32)],
        compiler_params=_cparams(("arbitrary",)),
        name="rwkv_proj",
    )(x, *consts)


def _block_diag(x, lo):
    zero = jnp.zeros_like(x)
    return jnp.concatenate([jnp.where(lo, x, zero), jnp.where(lo, zero, x)], axis=0)


class _PairMat:
    def __init__(self, x, lo):
        self.x, self.lo = x, lo
        self._lhs = self._rhs = None

    def lhs(self):
        if self._lhs is None:
            hi, lo = _split2(self.x)
            self._lhs = jnp.concatenate([hi, lo, hi], axis=1)
        return self._lhs

    def rhs(self):
        if self._rhs is None:
            hi, lo = _split2(self.x)
            bh = _block_diag(hi, self.lo)
            self._rhs = jnp.concatenate([bh, bh, _block_diag(lo, self.lo)], axis=0)
        return self._rhs


def _pair_mm(p, q):
    return _dot(p.lhs(), q.rhs())


def _rwkv_core_kernel(r_ref, k_ref, v_ref, lw_ref, a_ref, go_ref, kk_ref, ka_ref, rk_ref,
                      lnw_ref, lnb_ref, o_ref, z_scr, *, tt):
    L, N = RWKV_CHUNK, RWKV_HEAD_DIM
    npair = z_scr.shape[0]
    pairs = range(npair)

    @pl.when(pl.program_id(1) == 0)
    def _():
        z_scr[...] = jnp.zeros_like(z_scr)

    ri = lax.broadcasted_iota(jnp.int32, (L, LANES), 0)
    ln = lax.broadcasted_iota(jnp.int32, (L, LANES), 1)
    si = ln % N
    lo = ln < N
    lower_incl = ri >= si
    lower_strict = ri > si
    blk_diag = (ri // RWKV_INV_BLOCK) == (si // RWKV_INV_BLOCK)
    eye = (ri == si).astype(F32)
    tri = (lax.broadcasted_iota(jnp.int32, (L, L), 0) >= lax.broadcasted_iota(jnp.int32, (L, L), 1))
    tri3 = jnp.concatenate([tri.astype(BF16)] * 3, axis=1)
    z_mask = ((lax.broadcasted_iota(jnp.int32, (2 * N, LANES), 0) // N)
              == (lax.broadcasted_iota(jnp.int32, (2 * N, LANES), 1) // N))

    def half_sum(x):
        s0 = jnp.sum(jnp.where(lo, x, 0.0), axis=-1, keepdims=True)
        s1 = jnp.sum(jnp.where(lo, 0.0, x), axis=-1, keepdims=True)
        return jnp.where(lo, s0, s1)

    bd = lambda x: _block_diag(x, lo)
    mk = lambda xs: [_PairMat(x, lo) for x in xs]
    mm = lambda ps, qs: [_pair_mm(p, q) for p, q in zip(ps, qs)]

    def prep(grp, r0):
        idx = range(len(grp))
        sl = [slice(p * LANES, (p + 1) * LANES) for p in grp]
        ld = lambda ref: [ref[pl.ds(r0, L), s] for s in sl]
        r, k, v, lw, a = ld(r_ref), ld(k_ref), ld(v_ref), ld(lw_ref), ld(a_ref)
        kk = [k[i] * kk_ref[:, sl[i]] for i in idx]
        kk = [kk[i] / jnp.maximum(jnp.sqrt(half_sum(kk[i] * kk[i])), 1e-12) for i in idx]
        km = [k[i] * (1.0 + (a[i] - 1.0) * ka_ref[:, sl[i]]) for i in idx]
        bv = [kk[i] * a[i] for i in idx]
        cum = [_cumsum_rows(tri3, lw[i]) for i in idx]
        cum_end = [cum[i][L - 1:L, :] for i in idx]
        w_inv = [jnp.exp(-cum[i]) for i in idx]
        w_out = [jnp.exp(cum_end[i] - cum[i]) for i in idx]
        kk_h = [(kk[i] * jnp.exp(cum[i] - lw[i])).astype(BF16) for i in idx]
        r_h = [(r[i] * jnp.exp(cum[i])).astype(BF16) for i in idx]
        b_t = [(bv[i] * w_inv[i]).astype(BF16) for i in idx]
        k_t = [(km[i] * w_inv[i]).astype(BF16) for i in idx]
        bbar = [(bv[i] * w_out[i]).astype(BF16) for i in idx]
        kbar = [(km[i] * w_out[i]).astype(BF16) for i in idx]
        vb = [v[i].astype(BF16) for i in idx]
        lhs = [jnp.concatenate([kk_h[i], r_h[i]], axis=0) for i in idx]
        ab = [_dot_nt(lhs[i], bd(b_t[i])) for i in idx]
        ak = [_dot_nt(lhs[i], bd(k_t[i])) for i in idx]
        return dict(
            sl=sl, r=r, v=v, km=km, kk_h=kk_h, r_h=r_h, bbar=bbar, kbar=kbar, vb=vb, cum_end=cum_end,
            a_ub=[jnp.where(lower_strict, ab[i][:L], 0.0) for i in idx],
            a_rb=[jnp.where(lower_incl, ab[i][L:], 0.0).astype(BF16) for i in idx],
            a_uk=[jnp.where(lower_strict, ak[i][:L], 0.0).astype(BF16) for i in idx],
            a_rk=[jnp.where(lower_incl, ak[i][L:], 0.0).astype(BF16) for i in idx])

    def inverse(s):
        a_ub = s["a_ub"]
        idx = range(len(a_ub))
        n1 = mk([jnp.where(blk_diag, -a_ub[i], 0.0) for i in idx])
        n2 = mk(mm(n1, n1))
        n4 = mk(mm(n2, n2))
        n8 = mk(mm(n4, n4))
        acc = mk([eye + n1[i].x for i in idx])
        for nk in (n2, n4):
            prod = mm(acc, nk)
            acc = mk([acc[i].x + prod[i] for i in idx])
        prod = mm(acc, n8)
        d_inv = mk([acc[i].x + prod[i] for i in idx])
        l_off = mk([jnp.where(blk_diag, 0.0, a_ub[i]) for i in idx])
        e1 = mk([-x for x in mm(d_inv, l_off)])
        e2 = mk(mm(e1, e1))
        qm = mk([eye + e1[i].x for i in idx])
        prod = mm(qm, e2)
        qm = mk([qm[i].x + prod[i] for i in idx])
        s["t_inv"] = mk(mm(qm, d_inv))

    def state(grp, s):
        idx = range(len(grp))
        z = [z_scr[p] for p in grp]
        zb = [z[i].astype(BF16) for i in idx]
        vb = s["vb"]
        bdv = [bd(vb[i]) for i in idx]
        rhs_u = mk([_dot_nt(s["kk_h"][i], zb[i]) + _dot(s["a_uk"][i], bdv[i]) for i in idx])
        u = [-x for x in mm(s["t_inv"], rhs_u)]
        ub = [u[i].astype(BF16) for i in idx]
        s["y"] = [_dot_nt(s["r_h"][i], zb[i])
                  + _dot(jnp.concatenate([s["a_rb"][i], s["a_rk"][i]], axis=1),
                         jnp.concatenate([bd(ub[i]), bdv[i]], axis=0)) for i in idx]
        upd = [_dot_tn(jnp.concatenate([ub[i], vb[i]], axis=0),
                       jnp.concatenate([s["bbar"][i], s["kbar"][i]], axis=0)) for i in idx]
        for i, p in enumerate(grp):
            z_scr[p] = z[i] * jnp.exp(s["cum_end"][i]) + jnp.where(z_mask, upd[i], 0.0)

    def post(s, r0):
        y, sl, r, v, km = s["y"], s["sl"], s["r"], s["v"], s["km"]
        idx = range(len(y))
        inv_n = 1.0 / N
        mean = [half_sum(y[i]) * inv_n for i in idx]
        yc = [y[i] - mean[i] for i in idx]
        var = [half_sum(yc[i] * yc[i]) * inv_n for i in idx]
        bonus = [half_sum(r[i] * km[i] * rk_ref[:, sl[i]]) * v[i] for i in idx]
        for i in idx:
            yn = yc[i] * lax.rsqrt(var[i] + RWKV_GN_EPS) * lnw_ref[:, sl[i]] + lnb_ref[:, sl[i]]
            o_ref[pl.ds(r0, L), sl[i]] = ((yn + bonus[i]) * go_ref[pl.ds(r0, L), sl[i]]).astype(o_ref.dtype)

    group = list(pairs)

    n_chunks = tt // L
    cur = prep(group, 0)
    inverse(cur)
    for c in range(1, n_chunks):
        nxt = prep(group, c * L)
        state(group, cur)
        inverse(nxt)
        post(cur, (c - 1) * L)
        cur = nxt
    state(group, cur)
    post(cur, (n_chunks - 1) * L)


def _rwkv_core(r, k, v, lw, a, go, k_k, k_a, r_k, ln_w, ln_b, batch, seq_len, tt=256):
    m, d = r.shape
    nt = seq_len // tt
    npair = d // LANES
    blk = pl.BlockSpec((tt, d), lambda b, i: (b * nt + i, 0))
    par = _const_spec((1, d))
    row1 = lambda z: z.reshape(1, d)
    return pl.pallas_call(
        functools.partial(_rwkv_core_kernel, tt=tt),
        out_shape=jax.ShapeDtypeStruct((m, d), BF16),
        grid=(batch, nt),
        in_specs=[blk] * 6 + [par] * 5,
        out_specs=blk,
        scratch_shapes=[pltpu.VMEM((npair, 2 * RWKV_HEAD_DIM, LANES), F32)],
        compiler_params=_cparams(("arbitrary", "arbitrary")),
        name="rwkv_core",
    )(r, k, v, lw, a, go, row1(k_k), row1(k_a), row1(r_k), row1(ln_w), row1(ln_b))


def _rwkv_layer(x, norm_g, mu, w_r, w_k, w_v, w0, w_w1, w_w2, a0, a_w1, a_w2, g_w1, g_w2,
                k_k, k_a, r_k, ln_w, ln_b, batch, seq_len):
    r, k, v, lw, a, go = _rwkv_proj(x, norm_g, mu, w_r, w_k, w_v, w0, w_w1, w_w2, a0, a_w1, a_w2,
                                    g_w1, g_w2, seq_len)
    return _rwkv_core(r, k, v, lw, a, go, k_k, k_a, r_k, ln_w, ln_b, batch, seq_len)


def kernel(x, norm_mixer, norm_ffn, ffn_w_up, ffn_conv_w, ffn_conv_b, ffn_w_down, nsa_w_in, nsa_pe_k, nsa_pe_v, nsa_cmp_k_w1, nsa_cmp_k_w2, nsa_cmp_v_w1, nsa_cmp_v_w2, nsa_b_gate, nsa_w_out, mlstm_w_in, mlstm_conv_w, mlstm_conv_b, mlstm_b_gates, mlstm_norm, mlstm_w_out, rwkv_mu, rwkv_w_r, rwkv_w_k, rwkv_w_v, rwkv_w_o, rwkv_w0, rwkv_w_w1, rwkv_w_w2, rwkv_a0, rwkv_a_w1, rwkv_a_w2, rwkv_g_w1, rwkv_g_w2, rwkv_k_k, rwkv_k_a, rwkv_r_k, rwkv_ln_w, rwkv_ln_b, final_norm):
    batch, seq_len, d = x.shape
    depth = norm_mixer.shape[0]
    rope = _rope_tables(seq_len)
    xf = x.reshape(batch * seq_len, d)
    for i in range(depth):
        kind, j = i % 3, i // 3
        if kind == 0:
            w_o = nsa_w_out[j]
            o = _nsa_layer(xf, norm_mixer[i], nsa_w_in[j], nsa_pe_k[j], nsa_pe_v[j], nsa_cmp_k_w1[j],
                           nsa_cmp_k_w2[j], nsa_cmp_v_w1[j], nsa_cmp_v_w2[j], nsa_b_gate[j], rope, batch, seq_len)
        elif kind == 1:
            w_o = mlstm_w_out[j]
            o = _mlstm_layer(xf, norm_mixer[i], mlstm_w_in[j], mlstm_conv_w[j], mlstm_conv_b[j],
                             mlstm_b_gates[j], mlstm_norm[j], batch, seq_len)
        else:
            w_o = rwkv_w_o[j]
            o = _rwkv_layer(xf, norm_mixer[i], rwkv_mu[j], rwkv_w_r[j], rwkv_w_k[j], rwkv_w_v[j],
                            rwkv_w0[j], rwkv_w_w1[j], rwkv_w_w2[j], rwkv_a0[j],
                            rwkv_a_w1[j], rwkv_a_w2[j], rwkv_g_w1[j], rwkv_g_w2[j], rwkv_k_k[j],
                            rwkv_k_a[j], rwkv_r_k[j], rwkv_ln_w[j], rwkv_ln_b[j], batch, seq_len)
        xf = _ffn(xf, o, w_o, norm_ffn[i], ffn_w_up[i], ffn_conv_w[i], ffn_conv_b[i], ffn_w_down[i],
                  seq_len, final_g=final_norm if i == depth - 1 else None)
    return xf.reshape(batch, seq_len, d)
```

```python
import functools
import math

import jax
import jax.numpy as jnp
import numpy as np
from jax import lax
from jax.experimental import pallas as pl
from jax.experimental.pallas import tpu as pltpu

F32 = jnp.float32
BF16 = jnp.bfloat16

D_MODEL = 1024
NORM_EPS = 1e-6
ROPE_THETA = 500000.0

NSA_HEAD_DIM = 64
NSA_HEADS = 16
NSA_GROUPS = 4
NSA_REP = NSA_HEADS // NSA_GROUPS
NSA_ROT_DIM = 16
CMP_BLOCK = 32
CMP_STRIDE = 16
CMP_HIDDEN = 256
SEL_BLOCK = 64
SEL_TOPK = 16
WINDOW = 512
NSA_KV = NSA_GROUPS * NSA_HEAD_DIM
BF16_SUBLANES = 16
NSA_VT_ROWS = NSA_HEAD_DIM + BF16_SUBLANES
NSA_GATE_ROWS = 16
NSA_WINDOW_QUERIES = 256

MLSTM_HEADS = 8
MLSTM_QK_DIM = 64
MLSTM_V_DIM = 128
MLSTM_TILE = 128
MLSTM_CONV = 4

RWKV_HEAD_DIM = 64
RWKV_GN_EPS = 64e-5
RWKV_CHUNK = 64
RWKV_INV_BLOCK = 16

FFN_DIM = 2816
FFN_CONV = 3
FFN_CHUNK = 256
FFN_ROW_BLOCK = 64
FFN_DOWN_GROUP = 4

LOG2E = math.log2(math.e)
MASKED = -1e30

LANES = 128
SUBLANES = 8
VMEM_LIMIT = 56 * 1024 * 1024


def _dot(a, b):
    return jnp.dot(a, b, preferred_element_type=F32)


def _dot_nt(a, b):
    return lax.dot_general(a, b, (((1,), (1,)), ((), ())), preferred_element_type=F32)


def _dot_tn(a, b):
    return lax.dot_general(a, b, (((0,), (0,)), ((), ())), preferred_element_type=F32)


def _split2(x):
    hi = x.astype(BF16)
    return hi, (x - hi.astype(F32)).astype(BF16)


def _split3(x):
    hi = x.astype(BF16)
    r1 = x - hi.astype(F32)
    mid = r1.astype(BF16)
    return hi, mid, (r1 - mid.astype(F32)).astype(BF16)


def _cumsum_rows(tri3, x):
    return _dot(tri3, jnp.concatenate(_split3(x), axis=0))


def _cumsum_lanes(x, triu3):
    return _dot(jnp.concatenate(_split3(x), axis=1), triu3)


def _rms(x, g):
    ms = jnp.mean(x * x, axis=-1, keepdims=True)
    return x * lax.rsqrt(ms + NORM_EPS) * g


def _sigmoid(x):
    return 1.0 / (1.0 + jnp.exp(-x))


def _cparams(sem):
    return pltpu.CompilerParams(dimension_semantics=sem, vmem_limit_bytes=VMEM_LIMIT)


def _const_spec(shape):
    n = len(shape)
    return pl.BlockSpec(shape, lambda *_: (0,) * n, pipeline_mode=pl.Buffered(1))


def _ffn_kernel(res_ref, a_ref, wo_ref, g_ref, wu_ref, cw_ref, cb_ref, wd_ref, fg_ref, o_ref,
                h_scr, carry_scr, acta_scr, actb_scr, ga_scr, va_scr, gb_scr, vb_scr,
                *, tiles_per_seq, n_chunks, final_norm):
    tm = res_ref.shape[0]
    fc = FFN_CHUNK
    halo = SUBLANES
    x = res_ref[...] + _dot(a_ref[...], wo_ref[...])
    h_scr[...] = _rms(x, g_ref[...]).astype(BF16)
    o_ref[...] = x
    seq_start = (pl.program_id(0) % tiles_per_seq) == 0
    cols = lambda c, base=0: pl.ds(pl.multiple_of(base + c * fc, LANES), fc)

    def up(c, g_scr, v_scr):
        h = h_scr[...]
        g_scr[halo:, :] = _dot(h, wu_ref[:, cols(c)])
        v_scr[...] = _dot(h, wu_ref[:, cols(c, FFN_DIM)])

    def activate(c, g_scr, v_scr, act_scr, slot):
        g_scr[0:halo, :] = jnp.where(seq_start, 0.0, carry_scr[:, cols(c)])
        carry_scr[:, cols(c)] = g_scr[tm:tm + halo, :]
        cw = cw_ref[:, cols(c)]
        cb = cb_ref[:, cols(c)]
        for r0 in range(0, tm, FFN_ROW_BLOCK):
            y = cb + sum(cw[FFN_CONV - 1 - s:FFN_CONV - s, :] * g_scr[halo - s + r0:halo - s + r0 + FFN_ROW_BLOCK, :]
                         for s in range(FFN_CONV))
            act = y * _sigmoid(y) * v_scr[r0:r0 + FFN_ROW_BLOCK, :]
            act_scr[r0:r0 + FFN_ROW_BLOCK, slot * fc:(slot + 1) * fc] = act.astype(BF16)

    def down(c0, n, act_scr):
        o_ref[...] += _dot(act_scr[:, 0:n * fc], wd_ref[c0 * fc:(c0 + n) * fc, :])

    bufs = ((ga_scr, va_scr), (gb_scr, vb_scr))
    acts = (acta_scr, actb_scr)
    up(0, *bufs[0])
    for c in range(n_chunks):
        if c + 1 < n_chunks:
            up(c + 1, *bufs[(c + 1) % 2])
        grp, slot = divmod(c, FFN_DOWN_GROUP)
        activate(c, *bufs[c % 2], acts[grp % 2], slot)
        if slot == FFN_DOWN_GROUP - 1 or c == n_chunks - 1:
            down(grp * FFN_DOWN_GROUP, slot + 1, acts[grp % 2])
    if final_norm:
        o_ref[...] = _rms(o_ref[...], fg_ref[...])


def _ffn(res, a, w_o, g, w_up, conv_w, conv_b, w_down, seq_len, final_g=None, tm=1024):
    m, d = res.shape
    nc = FFN_DIM // FFN_CHUNK
    cw = jnp.pad(conv_w, ((0, SUBLANES - FFN_CONV), (0, 0)))
    row = lambda i: (i, 0)
    fg = jnp.ones((d,), F32) if final_g is None else final_g
    kern = functools.partial(_ffn_kernel, tiles_per_seq=seq_len // tm, n_chunks=nc,
                             final_norm=final_g is not None)
    return pl.pallas_call(
        kern,
        out_shape=jax.ShapeDtypeStruct((m, d), F32),
        grid=(m // tm,),
        in_specs=[pl.BlockSpec((tm, d), row), pl.BlockSpec((tm, d), row), _const_spec((d, d)),
                  _const_spec((1, d)), _const_spec((d, 2 * FFN_DIM)),
                  _const_spec((SUBLANES, FFN_DIM)), _const_spec((1, FFN_DIM)),
                  _const_spec((FFN_DIM, d)), _const_spec((1, d))],
        out_specs=pl.BlockSpec((tm, d), row),
        scratch_shapes=[pltpu.VMEM((tm, d), BF16),
                        pltpu.VMEM((SUBLANES, FFN_DIM), F32),
                        pltpu.VMEM((tm, FFN_DOWN_GROUP * FFN_CHUNK), BF16),
                        pltpu.VMEM((tm, FFN_DOWN_GROUP * FFN_CHUNK), BF16)]
                       + [pltpu.VMEM((tm + SUBLANES, FFN_CHUNK), F32), pltpu.VMEM((tm, FFN_CHUNK), F32)] * 2,
        compiler_params=_cparams(("arbitrary",)),
        name="conv_ffn",
    )(res, a, w_o.astype(BF16), g.reshape(1, d), w_up.astype(BF16), cw, conv_b.reshape(1, FFN_DIM),
      w_down.astype(BF16), fg.reshape(1, d))


def _nsa_proj_kernel(x_ref, g_ref, w_ref, wvt_ref, bg_ref, rc_ref, rs1_ref, rs2_ref,
                     qp_ref, qr_ref, kc_ref, vc_ref, ks_ref, vs_ref, kw_ref, vw_ref, gate_ref,
                     cmp_scr, *, tiles_per_seq):
    hn = _rms(x_ref[...], g_ref[...]).astype(BF16)
    y = _dot(hn, w_ref[...])
    vt = _dot_nt(wvt_ref[...], hn)
    rc, rs1, rs2 = rc_ref[...], rs1_ref[...], rs2_ref[...]
    dh = NSA_HEAD_DIM

    def rope(z):
        half = NSA_ROT_DIM // 2
        return z * rc + pltpu.roll(z, half, 1) * rs1 + pltpu.roll(z, LANES - half, 1) * rs2

    scale = dh ** -0.5 * LOG2E
    for j in range(D_MODEL // LANES):
        q = y[:, j * LANES:(j + 1) * LANES] * scale
        qp_ref[:, j * LANES:(j + 1) * LANES] = q.astype(BF16)
        qr_ref[:, j * LANES:(j + 1) * LANES] = rope(q).astype(BF16)

    def kv_chunk(idx):
        return y[:, D_MODEL + idx * NSA_KV:D_MODEL + (idx + 1) * NSA_KV]

    def split_groups(z, ref, dtype):
        for g in range(NSA_GROUPS):
            ref[g] = z[:, g * dh:(g + 1) * dh].astype(dtype)

    def rope256(z):
        return jnp.concatenate([rope(z[:, :LANES]), rope(z[:, LANES:])], axis=1)

    tm = y.shape[0]
    nrow = tm // CMP_STRIDE
    for j in range(2 * NSA_KV // LANES):
        cmp_scr[j] = y[:, D_MODEL + j * LANES:D_MODEL + (j + 1) * LANES]
    for j in range(2 * NSA_KV // LANES):
        ref = kc_ref if j < NSA_KV // LANES else vc_ref
        toks = [cmp_scr[j, pl.ds(tok, nrow, stride=CMP_STRIDE), :] for tok in range(CMP_STRIDE)]
        for half in range(LANES // dh):
            g = (j % (NSA_KV // LANES)) * (LANES // dh) + half
            ref[g] = jnp.concatenate([t[:, half * dh:(half + 1) * dh] for t in toks], axis=1)
    t_pos = (pl.program_id(0) % tiles_per_seq) * tm + lax.broadcasted_iota(jnp.int32, (tm, LANES), 0)
    onehot = (lax.broadcasted_iota(jnp.int32, (tm, LANES), 1) == t_pos // SEL_BLOCK).astype(F32)
    ksel = rope256(kv_chunk(2))
    zpad = jnp.zeros((tm, LANES - dh), F32)
    for g in range(NSA_GROUPS):
        ks_ref[g] = jnp.concatenate([ksel[:, g * dh:(g + 1) * dh], zpad, onehot], axis=1).astype(BF16)
    split_groups(rope256(kv_chunk(3)), kw_ref, BF16)
    ones_pad = (lax.broadcasted_iota(jnp.int32, (NSA_VT_ROWS - dh, tm), 0) == 0).astype(F32)
    for g in range(NSA_GROUPS):
        vs_ref[g] = jnp.concatenate([vt[g * dh:(g + 1) * dh], ones_pad], axis=0).astype(BF16)
        vw_ref[g] = jnp.concatenate([vt[NSA_KV + g * dh:NSA_KV + (g + 1) * dh], ones_pad], axis=0).astype(BF16)
    gate = _sigmoid(vt[2 * NSA_KV:] + bg_ref[...])
    for g in range(NSA_GROUPS):
        gate_ref[g] = gate[g * NSA_GATE_ROWS:(g + 1) * NSA_GATE_ROWS]


def _rope_tables(seq_len):
    half = NSA_ROT_DIM // 2
    inv_freq = ROPE_THETA ** (-jnp.arange(half, dtype=F32) / half)
    ang = jnp.arange(seq_len, dtype=F32)[:, None] * inv_freq[None, :]
    cos, sin = jnp.cos(ang), jnp.sin(ang)
    zeros = jnp.zeros((seq_len, NSA_HEAD_DIM - NSA_ROT_DIM), F32)
    z8 = jnp.zeros((seq_len, half), F32)
    rc = jnp.concatenate([cos, cos, zeros + 1.0], axis=1)
    rs1 = jnp.concatenate([z8, sin, zeros], axis=1)
    rs2 = jnp.concatenate([-sin, z8, zeros], axis=1)
    two = lambda t: jnp.concatenate([t, t], axis=1)
    return two(rc), two(rs1), two(rs2)


def _nsa_proj(x, g, w_in, b_gate, rope, seq_len, tm=512):
    m, d = x.shape
    n_kv = 6 * NSA_KV
    kv = lambda idx: w_in[:, D_MODEL + idx * NSA_KV:D_MODEL + (idx + 1) * NSA_KV]
    pad_g = NSA_GATE_ROWS - NSA_REP * 3
    wg = w_in[:, D_MODEL + n_kv:].reshape(d, NSA_GROUPS, NSA_REP * 3)
    wg = jnp.pad(wg, ((0, 0), (0, 0), (0, pad_g))).reshape(d, NSA_GROUPS * NSA_GATE_ROWS)
    w = jnp.concatenate([w_in[:, :D_MODEL], kv(0), kv(1), kv(2), kv(4)], axis=1).astype(BF16)
    wvt = jnp.concatenate([kv(3), kv(5), wg], axis=1).T.astype(BF16)
    bg = jnp.pad(b_gate.reshape(NSA_GROUPS, NSA_REP * 3), ((0, 0), (0, pad_g)))
    bg = bg.reshape(NSA_GROUPS * NSA_GATE_ROWS, 1)
    n = w.shape[1]
    tps = seq_len // tm
    row = lambda i: (i, 0)
    rope_spec = pl.BlockSpec((tm, LANES), lambda i: (i % tps, 0))
    assert seq_len // SEL_BLOCK <= LANES
    g_out = lambda dt, w=NSA_HEAD_DIM: jax.ShapeDtypeStruct((NSA_GROUPS, m, w), dt)
    g_spec = pl.BlockSpec((NSA_GROUPS, tm, NSA_HEAD_DIM), lambda i: (0, i, 0))
    ks_spec = pl.BlockSpec((NSA_GROUPS, tm, 2 * LANES), lambda i: (0, i, 0))
    cmp_out = jax.ShapeDtypeStruct((NSA_GROUPS, m // CMP_STRIDE, CMP_STRIDE * NSA_HEAD_DIM), F32)
    cmp_spec = pl.BlockSpec((NSA_GROUPS, tm // CMP_STRIDE, CMP_STRIDE * NSA_HEAD_DIM), lambda i: (0, i, 0))
    vt_out = jax.ShapeDtypeStruct((NSA_GROUPS, NSA_VT_ROWS, m), BF16)
    vt_spec = pl.BlockSpec((NSA_GROUPS, NSA_VT_ROWS, tm), lambda i: (0, 0, i))
    return pl.pallas_call(
        functools.partial(_nsa_proj_kernel, tiles_per_seq=tps),
        out_shape=(jax.ShapeDtypeStruct((m, d), BF16), jax.ShapeDtypeStruct((m, d), BF16),
                   cmp_out, cmp_out, g_out(BF16, 2 * LANES), vt_out, g_out(BF16), vt_out,
                   jax.ShapeDtypeStruct((NSA_GROUPS, NSA_GATE_ROWS, m), F32)),
        grid=(m // tm,),
        in_specs=[pl.BlockSpec((tm, d), row), _const_spec((1, d)), _const_spec((d, n)),
                  _const_spec((2 * NSA_KV + NSA_GROUPS * NSA_GATE_ROWS, d)),
                  _const_spec((NSA_GROUPS * NSA_GATE_ROWS, 1)), rope_spec, rope_spec, rope_spec],
        out_specs=(pl.BlockSpec((tm, d), row), pl.BlockSpec((tm, d), row),
                   cmp_spec, cmp_spec, ks_spec, vt_spec, g_spec, vt_spec,
                   pl.BlockSpec((NSA_GROUPS, NSA_GATE_ROWS, tm), lambda i: (0, 0, i))),
        scratch_shapes=[pltpu.VMEM((2 * NSA_KV // LANES, tm, LANES), F32)],
        compiler_params=_cparams(("parallel",)),
        name="nsa_proj",
    )(x, g.reshape(1, d), w, wvt, bg, *rope)


def _gelu_tanh(x):
    return 0.5 * x * (1.0 + jnp.tanh(math.sqrt(2.0 / math.pi) * (x + 0.044715 * (x * x * x))))


def _compress_kernel(zk_ref, zv_ref, pek_ref, pev_ref, w1k_ref, w2k_ref, w1v_ref, w2v_ref,
                     kc_ref, vc_ref):
    nrow = zk_ref.shape[0]
    rows = lax.broadcasted_iota(jnp.int32, (nrow, NSA_HEAD_DIM), 0)

    def one(z_ref, pe_ref, w1_ref, w2_ref, o_ref):
        z = z_ref[...]
        a = _dot((z + pe_ref[0:1, :]).astype(BF16), w1_ref[0])
        b = _dot((z + pe_ref[1:2, :]).astype(BF16), w1_ref[1])
        hid = a + pltpu.roll(b, nrow - 1, 0)
        out = _dot(_gelu_tanh(hid).astype(BF16), w2_ref[...])
        o_ref[...] = jnp.where(rows == nrow - 1, 0.0, out).astype(o_ref.dtype)

    one(zk_ref, pek_ref, w1k_ref, w2k_ref, kc_ref)
    one(zv_ref, pev_ref, w1v_ref, w2v_ref, vc_ref)


def _compress(kc_raw, vc_raw, pe_k, pe_v, w1k, w2k, w1v, w2v, seq_len):
    g, nrows, half = kc_raw.shape
    dh = half // CMP_STRIDE
    nchunk = seq_len // CMP_STRIDE
    zk = kc_raw.reshape(g * nrows, half)
    zv = vc_raw.reshape(g * nrows, half)
    pe2 = lambda pe: pe.reshape(2, half)
    w1 = lambda w: w.astype(BF16).reshape(2, half, CMP_HIDDEN)
    nblk = zk.shape[0] // nchunk
    row = lambda i: (i, 0)
    return pl.pallas_call(
        _compress_kernel,
        out_shape=(jax.ShapeDtypeStruct((zk.shape[0], dh), BF16),
                   jax.ShapeDtypeStruct((zk.shape[0], dh), BF16)),
        grid=(nblk,),
        in_specs=[pl.BlockSpec((nchunk, half), row), pl.BlockSpec((nchunk, half), row),
                  _const_spec((2, half)), _const_spec((2, half)),
                  _const_spec((2, half, CMP_HIDDEN)), _const_spec((CMP_HIDDEN, dh)),
                  _const_spec((2, half, CMP_HIDDEN)), _const_spec((CMP_HIDDEN, dh))],
        out_specs=(pl.BlockSpec((nchunk, dh), row), pl.BlockSpec((nchunk, dh), row)),
        compiler_params=_cparams(("parallel",)),
        name="nsa_compress",
    )(zk, zv, pe2(pe_k), pe2(pe_v), w1(w1k), w2k.astype(BF16), w1(w1v), w2v.astype(BF16))


def _nsa_cmp_kernel(q_ref, kc_ref, vc_ref, gate_ref, ovt_ref, oc_ref, selt_ref, imp_scr, *, tq, n_classes):
    qi = pl.program_id(2)
    nq = pl.num_programs(2)
    dh = NSA_HEAD_DIM
    ncmp = kc_ref.shape[0]
    nsel = imp_scr.shape[0]
    heads = range(NSA_REP)

    def attend(nrows):
        q = q_ref[...]
        qs = jnp.concatenate([q[:, r * dh:(r + 1) * dh] for r in heads], axis=0)
        kc, vc = kc_ref[0:nrows, :], vc_ref[0:nrows, :]
        t = qi * tq + lax.broadcasted_iota(jnp.int32, (nrows, tq), 1)
        cmp_end = lax.broadcasted_iota(jnp.int32, (nrows, tq), 0) * CMP_STRIDE + (CMP_BLOCK - 1)
        neg = jnp.where(cmp_end <= t, 0.0, -jnp.inf)
        st = [_dot_nt(kc, qs[r * tq:(r + 1) * tq]) + neg for r in heads]
        mx = [jnp.max(st[r], axis=0, keepdims=True) for r in heads]
        mx = [jnp.where(mx[r] == -jnp.inf, 0.0, mx[r]) for r in heads]
        e = [jnp.exp2(st[r] - mx[r]) for r in heads]
        p = [e[r] / jnp.maximum(jnp.sum(e[r], axis=0, keepdims=True), 1e-30) for r in heads]
        o_t = [_dot_tn(vc, p[r].astype(BF16)) for r in heads]
        gate = gate_ref[...]
        o_t = [o_t[r] * gate[3 * r:3 * r + 1, :] for r in heads]
        for j in range(NSA_REP // 2):
            pair = jnp.concatenate([o_t[2 * j], o_t[2 * j + 1]], axis=0).T
            oc_ref[:, 2 * j * dh:2 * (j + 1) * dh] = pair
        psum = sum(p)
        ovt = jnp.concatenate([ovt_ref[:, j * ncmp:j * ncmp + nrows] for j in range(3)], axis=1)
        imp_scr[...] = _dot(ovt, jnp.concatenate(_split3(psum), axis=0))

    def select(nblk):
        blk = lax.broadcasted_iota(jnp.int32, (nblk, LANES), 0)
        blk_f = blk.astype(F32)
        if nblk < nsel:
            selt_ref[nblk:, :] = jnp.zeros((nsel - nblk, tq), selt_ref.dtype)
        for cb in range(tq // LANES):
            csl = slice(cb * LANES, (cb + 1) * LANES)
            tb = (qi * tq + cb * LANES + lax.broadcasted_iota(jnp.int32, (nblk, LANES), 1)) // SEL_BLOCK
            forced = (blk == 0) | (blk == tb) | (blk == tb - 1)
            vals = jnp.where(forced, -jnp.inf, jnp.where(blk <= tb, imp_scr[0:nblk, csl], -1.0))
            for _ in range(SEL_TOPK - 3):
                top = jnp.max(vals, axis=0, keepdims=True)
                first = jnp.min(jnp.where(vals == top, blk_f, float(nsel)), axis=0, keepdims=True)
                vals = jnp.where(blk_f == first, -jnp.inf, vals)
            selt_ref[0:nblk, csl] = jnp.where(vals == -jnp.inf, 1.0, 0.0).astype(selt_ref.dtype)

    for cls in range(n_classes):
        @pl.when((qi * n_classes) // nq == cls)
        def _(cls=cls):
            attend((cls + 1) * ncmp // n_classes)
            select((cls + 1) * nsel // n_classes)


def _overlap_matrix_t3(ncmp_pad, nsel):
    c = np.arange(ncmp_pad)[None, :]
    s = np.arange(nsel)[:, None]
    cmp_start = c * CMP_STRIDE
    cmp_end = cmp_start + CMP_BLOCK - 1
    blk_start = s * SEL_BLOCK
    ov = ((cmp_end >= blk_start) & (cmp_start <= blk_start + SEL_BLOCK - 1)).astype(np.float32)
    return jnp.asarray(np.concatenate([ov, ov, ov], axis=1), dtype=BF16)


def _nsa_cmp(qp, kc, vc, gates, batch, seq_len, tq=512):
    m, d = qp.shape
    nq = seq_len // tq
    ncmp = seq_len // CMP_STRIDE
    nsel = seq_len // SEL_BLOCK
    ovt3 = _overlap_matrix_t3(ncmp, nsel)
    qmap = lambda b, g, i: (b * nq + i, g)
    kmap = lambda b, g, i: (g * batch + b, 0)
    n_classes = math.gcd(nq, 4)
    return pl.pallas_call(
        functools.partial(_nsa_cmp_kernel, tq=tq, n_classes=n_classes),
        out_shape=(jax.ShapeDtypeStruct((m, d), F32),
                   jax.ShapeDtypeStruct((NSA_GROUPS, nsel, m), BF16)),
        grid=(batch, NSA_GROUPS, nq),
        in_specs=[pl.BlockSpec((tq, NSA_KV), qmap),
                  pl.BlockSpec((ncmp, NSA_HEAD_DIM), kmap), pl.BlockSpec((ncmp, NSA_HEAD_DIM), kmap),
                  pl.BlockSpec((None, NSA_GATE_ROWS, tq), lambda b, g, i: (g, 0, b * nq + i)),
                  _const_spec((nsel, 3 * ncmp))],
        out_specs=(pl.BlockSpec((tq, NSA_KV), qmap),
                   pl.BlockSpec((None, nsel, tq), lambda b, g, i: (g, 0, b * nq + i))),
        scratch_shapes=[pltpu.VMEM((nsel, tq), F32)],
        compiler_params=_cparams(("parallel", "parallel", "parallel")),
        name="nsa_cmp_topk",
    )(qp, kc, vc, gates, ovt3)


def _nsa_sel_kernel(q_ref, ks_ref, vs_ref, kw_ref, vw_ref, sel_ref, gate_ref, oc_ref, o_ref,
                    m_scr, acc_scr, ow_scr, sta_scr, stb_scr, *, tq, tk):
    qi = pl.program_id(2)
    dh = NSA_HEAD_DIM
    cols = NSA_REP * tq
    q = q_ref[...]
    qs = jnp.concatenate([q[:, r * dh:(r + 1) * dh] for r in range(NSA_REP)], axis=0)
    selt = sel_ref[...].astype(F32)
    nsel = selt.shape[0]
    bmask_t = jnp.where(selt > 0.5, 0.0, MASKED)
    if nsel < LANES:
        bmask_t = jnp.concatenate([bmask_t, jnp.zeros((LANES - nsel, tq), F32)], axis=0)
    bmask = bmask_t.T.astype(BF16)
    zpad = jnp.zeros((tq, LANES - dh), BF16)
    qa = jnp.concatenate([jnp.concatenate([q[:, r * dh:(r + 1) * dh], zpad, bmask], axis=1)
                          for r in range(NSA_REP)], axis=0)
    q0 = qi * tq
    key_iota = lax.broadcasted_iota(jnp.int32, (tk, tq), 0)
    t_pos = q0 + lax.broadcasted_iota(jnp.int32, (tk, tq), 1)

    m_scr[...] = jnp.full((1, cols), -jnp.inf, F32)
    acc_scr[...] = jnp.zeros((NSA_VT_ROWS, cols), F32)

    heads = range(NSA_REP)
    hsl = [slice(r * tq, (r + 1) * tq) for r in heads]

    def put_scores(scr, ki):
        k = ks_ref[pl.ds(pl.multiple_of(ki * tk, tk), tk), :]
        for r in heads:
            scr[r] = _dot_nt(k, qa[hsl[r]])

    def get_scores(scr):
        return [scr[r] for r in heads]

    def consume(ki, st, causal):
        k0 = pl.multiple_of(ki * tk, tk)
        vt = vs_ref[:, pl.ds(k0, tk)]
        if causal:
            neg = jnp.where(k0 + key_iota <= t_pos, 0.0, -jnp.inf)
            st = [s + neg for s in st]
        m_old = [m_scr[:, hsl[r]] for r in heads]
        m_new = [jnp.maximum(m_old[r], jnp.max(st[r], axis=0, keepdims=True)) for r in heads]
        alpha = [jnp.exp2(m_old[r] - m_new[r]) for r in heads]
        p = [jnp.exp2(st[r] - m_new[r]).astype(BF16) for r in heads]
        pv = [_dot(vt, p[r]) for r in heads]
        for r in heads:
            acc_scr[:, hsl[r]] = alpha[r] * acc_scr[:, hsl[r]] + pv[r]
            m_scr[:, hsl[r]] = m_new[r]

    def tile_pair(j, carry):
        even = get_scores(sta_scr)
        put_scores(stb_scr, 2 * j + 1)
        consume(2 * j, even, False)
        odd = get_scores(stb_scr)
        put_scores(sta_scr, 2 * j + 2)
        consume(2 * j + 1, odd, False)
        return carry

    last = (q0 + tq - 1) // tk

    gate = gate_ref[...]
    wq = min(tq, NSA_WINDOW_QUERIES)
    wsub = wq + WINDOW
    nsub = tq // wq
    rel_iota = (lax.broadcasted_iota(jnp.int32, (wsub, wq), 0)
                - lax.broadcasted_iota(jnp.int32, (wsub, wq), 1))
    starts = [pl.multiple_of(jnp.maximum(q0 + wq * u - WINDOW, 0), wq) for u in range(nsub)]
    kw = [kw_ref[pl.ds(starts[u], wsub), :] for u in range(nsub)]
    vwt = [vw_ref[:, pl.ds(starts[u], wsub)] for u in range(nsub)]
    rel = [starts[u] - (q0 + wq * u) + rel_iota for u in range(nsub)]
    neg_w = [jnp.where((rel[u] <= 0) & (rel[u] > -WINDOW), 0.0, -jnp.inf) for u in range(nsub)]
    subs = [(r, u) for r in heads for u in range(nsub)]
    sw = [_dot_nt(kw[u], qs[r * tq + wq * u:r * tq + wq * (u + 1)]) + neg_w[u] for r, u in subs]
    put_scores(sta_scr, 0)
    mw = [jnp.max(s, axis=0, keepdims=True) for s in sw]
    mw = [jnp.where(m == -jnp.inf, 0.0, m) for m in mw]
    ew = [jnp.exp2(s - m).astype(BF16) for s, m in zip(sw, mw)]
    pvw = [_dot(vwt[u], e) for (r, u), e in zip(subs, ew)]
    pvw = [p[0:dh] / jnp.maximum(p[dh:dh + 1], 1e-30) for p in pvw]
    for r in heads:
        o_win = jnp.concatenate(pvw[r * nsub:(r + 1) * nsub], axis=1)
        ow_scr[:, hsl[r]] = o_win * gate[3 * r + 2:3 * r + 3, :]

    n_pairs = last // 2

    def tile_quad(jq, carry):
        tile_pair(2 * jq, carry)
        tile_pair(2 * jq + 1, carry)
        return carry

    lax.fori_loop(0, n_pairs // 2, tile_quad, 0)

    @pl.when(n_pairs % 2 == 1)
    def _():
        tile_pair(n_pairs - 1, 0)

    @pl.when(last % 2 == 0)
    def _():
        consume(last, get_scores(sta_scr), True)

    @pl.when(last % 2 == 1)
    def _():
        even = get_scores(sta_scr)
        put_scores(stb_scr, last)
        consume(last - 1, even, False)
        consume(last, get_scores(stb_scr), True)
    o_sel = acc_scr[0:dh, :] / jnp.maximum(acc_scr[dh:dh + 1, :], 1e-30)

    mix = [o_sel[:, hsl[r]] * gate[3 * r + 1:3 * r + 2, :] + ow_scr[:, hsl[r]] for r in heads]
    for j in range(NSA_REP // 2):
        psl = slice(2 * j * dh, 2 * (j + 1) * dh)
        pair = jnp.concatenate([mix[2 * j], mix[2 * j + 1]], axis=0).T
        o_ref[:, psl] = (oc_ref[:, psl] + pair).astype(o_ref.dtype)


def _nsa_sel(qr, ks, vs, kw, vw, sel, gates, oc, batch, seq_len, tq=512, tk=512):
    m, d = qr.shape
    nq = seq_len // tq
    nsel = seq_len // SEL_BLOCK
    qmap = lambda b, g, i: (b * nq + i, g)
    kvmap = lambda b, g, i: (g, b, 0)
    kv_spec = pl.BlockSpec((None, seq_len, NSA_HEAD_DIM), kvmap)
    ks_spec = pl.BlockSpec((None, seq_len, 2 * LANES), kvmap)
    vt_spec = pl.BlockSpec((None, NSA_VT_ROWS, seq_len), lambda b, g, i: (g, 0, b))
    cols = NSA_REP * tq
    assert tk % tq == 0
    return pl.pallas_call(
        functools.partial(_nsa_sel_kernel, tq=tq, tk=tk),
        out_shape=jax.ShapeDtypeStruct((m, d), BF16),
        grid=(batch, NSA_GROUPS, nq),
        in_specs=[pl.BlockSpec((tq, NSA_KV), qmap), ks_spec, vt_spec, kv_spec, vt_spec,
                  pl.BlockSpec((None, nsel, tq), lambda b, g, i: (g, 0, b * nq + i)),
                  pl.BlockSpec((None, NSA_GATE_ROWS, tq), lambda b, g, i: (g, 0, b * nq + i)),
                  pl.BlockSpec((tq, NSA_KV), qmap)],
        out_specs=pl.BlockSpec((tq, NSA_KV), qmap),
        scratch_shapes=[pltpu.VMEM((1, cols), F32), pltpu.VMEM((NSA_VT_ROWS, cols), F32),
                        pltpu.VMEM((NSA_HEAD_DIM, cols), F32),
                        pltpu.VMEM((NSA_REP, tk, tq), F32), pltpu.VMEM((NSA_REP, tk, tq), F32)],
        compiler_params=_cparams(("parallel", "parallel", "parallel")),
        name="nsa_sel_win",
    )(qr, ks, vs, kw, vw, sel, gates, oc)


def _nsa_layer(x, norm_g, w_in, pe_k, pe_v, w1k, w2k, w1v, w2v, b_gate, rope, batch, seq_len):
    qp, qr, kc_raw, vc_raw, ks, vs, kw, vw, gates = _nsa_proj(x, norm_g, w_in, b_gate, rope, seq_len)
    kc, vc = _compress(kc_raw, vc_raw, pe_k, pe_v, w1k, w2k, w1v, w2v, seq_len)
    oc, sel = _nsa_cmp(qp, kc, vc, gates, batch, seq_len)
    return _nsa_sel(qr, ks, vs, kw, vw, sel, gates, oc, batch, seq_len)


def _mlstm_proj_kernel(x_ref, g_ref, w_ref, wot_ref, wt_ref, bcol_ref, brow_ref,
                       qk_ref, v_ref, ot_ref, gc_ref, gr_ref):
    hn = _rms(x_ref[...], g_ref[...]).astype(BF16)
    y = _dot(hn, w_ref[...])
    d = D_MODEL
    qk_ref[...] = y[:, :d]
    v_ref[...] = y[:, d:2 * d].astype(BF16)
    ot_ref[...] = _sigmoid(_dot_nt(wot_ref[...], hn))
    gc_ref[...] = y[:, 2 * d:] + bcol_ref[...]
    gr_ref[...] = _dot_nt(wt_ref[...], hn) + brow_ref[...]


def _mlstm_proj(x, g, w_in, b_gates, tm=512):
    m, d = x.shape
    h, dk = MLSTM_HEADS, MLSTM_QK_DIM
    wq = w_in[:, :h * dk].reshape(d, h, dk)
    wk = w_in[:, h * dk:2 * h * dk].reshape(d, h, dk)
    wqk = jnp.concatenate([wq, wk], axis=2).reshape(d, 2 * h * dk)
    wv = w_in[:, d:2 * d]
    wif = w_in[:, 2 * d:2 * d + 2 * h]
    wo = w_in[:, 2 * d + 2 * h:]
    w = jnp.concatenate([wqk, wv, jnp.pad(wif, ((0, 0), (0, LANES - 2 * h)))], axis=1).astype(BF16)
    wot = wo.T.astype(BF16)
    wt = wif.T.astype(BF16)
    bcol = jnp.pad(b_gates, (0, LANES - 2 * h)).reshape(1, LANES)
    brow = b_gates.reshape(2 * h, 1)
    n = w.shape[1]
    row = lambda i: (i, 0)
    col = lambda i: (0, i)
    return pl.pallas_call(
        _mlstm_proj_kernel,
        out_shape=(jax.ShapeDtypeStruct((m, d), F32), jax.ShapeDtypeStruct((m, d), BF16),
                   jax.ShapeDtypeStruct((d, m), F32), jax.ShapeDtypeStruct((m, LANES), F32),
                   jax.ShapeDtypeStruct((2 * h, m), F32)),
        grid=(m // tm,),
        in_specs=[pl.BlockSpec((tm, d), row), _const_spec((1, d)), _const_spec((d, n)), _const_spec((d, d)),
                  _const_spec((2 * h, d)), _const_spec((1, LANES)), _const_spec((2 * h, 1))],
        out_specs=(pl.BlockSpec((tm, d), row), pl.BlockSpec((tm, d), row), pl.BlockSpec((d, tm), col),
                   pl.BlockSpec((tm, LANES), row), pl.BlockSpec((2 * h, tm), col)),
        compiler_params=_cparams(("parallel",)),
        name="mlstm_proj",
    )(x, g.reshape(1, d), w, wot, wt, bcol, brow)


def _log_sigmoid(x):
    return jnp.minimum(x, 0.0) - jnp.log(1.0 + jnp.exp(-jnp.abs(x)))


def _mlstm_core_kernel(qk_ref, v_ref, og_ref, gc_ref, gr_ref, cw_ref, cb_ref, ng_ref, o_ref,
                       qkc_scr, ext_scr, c_scr, n_scr, m_scr, *, tt):
    L = MLSTM_TILE
    H, dk, dv = MLSTM_HEADS, MLSTM_QK_DIM, MLSTM_V_DIM
    seq_start = pl.program_id(1) == 0

    @pl.when(seq_start)
    def _():
        c_scr[...] = jnp.zeros_like(c_scr)
        n_scr[...] = jnp.zeros_like(n_scr)
        m_scr[...] = jnp.zeros_like(m_scr)
        ext_scr[0:SUBLANES, :] = jnp.zeros((SUBLANES, ext_scr.shape[1]), F32)

    ext_scr[SUBLANES:, :] = qk_ref[...]
    is_k = lax.broadcasted_iota(jnp.int32, (tt, 2 * dk), 1) >= dk
    for h in range(H):
        hsl = slice(h * 2 * dk, (h + 1) * 2 * dk)
        cw = cw_ref[:, hsl]
        acc = cb_ref[:, hsl] + cw[MLSTM_CONV - 1:MLSTM_CONV, :] * ext_scr[SUBLANES:, hsl]
        for s in range(1, MLSTM_CONV):
            acc = acc + cw[MLSTM_CONV - 1 - s:MLSTM_CONV - s, :] * ext_scr[SUBLANES - s:SUBLANES - s + tt, hsl]
        act = acc * _sigmoid(acc)
        qkc_scr[:, hsl] = jnp.where(is_k, act * dk ** -0.5, act)
    ext_scr[0:SUBLANES, :] = ext_scr[tt:tt + SUBLANES, :]

    src = lax.broadcasted_iota(jnp.int32, (L, L), 0)
    tgt = lax.broadcasted_iota(jnp.int32, (L, L), 1)
    causal = src <= tgt
    tri3 = jnp.concatenate([(src >= tgt).astype(BF16)] * 3, axis=1)
    triu3 = jnp.concatenate([causal.astype(BF16)] * 3, axis=0)
    n_pad = jnp.zeros((SUBLANES - 3, dk), BF16)

    def chunk(c, carry):
        r0 = pl.multiple_of(c * L, L)
        gcol = gc_ref[pl.ds(r0, L), :]
        grow = gr_ref[c]
        b_col = _cumsum_rows(tri3, _log_sigmoid(gcol))
        b_row = _cumsum_lanes(_log_sigmoid(grow), triu3)
        hs = range(H)
        vsl = [slice(h * dv, (h + 1) * dv) for h in hs]
        qk = [qkc_scr[pl.ds(r0, L), h * 2 * dk:(h + 1) * 2 * dk] for h in hs]
        q = [qk[h][:, :dk].astype(BF16) for h in hs]
        k = [qk[h][:, dk:] for h in hs]
        v = [v_ref[pl.ds(r0, L), vsl[h]] for h in hs]
        col = [gcol[:, h:h + 1] - b_col[:, H + h:H + h + 1] for h in hs]
        li_r = [grow[h:h + 1, :] for h in hs]
        b_r = [b_row[H + h:H + h + 1, :] for h in hs]
        b_end = [b_r[h][:, L - 1:L] for h in hs]
        dmat = [jnp.where(causal, b_r[h] + col[h], -jnp.inf) for h in hs]
        d_max = [jnp.max(dmat[h], axis=0, keepdims=True) for h in hs]
        att = [jnp.exp(dmat[h] - d_max[h]) * _dot_nt(k[h].astype(BF16), q[h]) for h in hs]
        a_sum = [jnp.sum(att[h], axis=0, keepdims=True) for h in hs]
        intra = [_dot_tn(v[h], att[h].astype(BF16)) for h in hs]
        g_max = [jnp.max(b_end[h] - b_r[h] + li_r[h], axis=-1, keepdims=True) for h in hs]
        kw = [k[h] * jnp.exp(b_end[h] + col[h] - g_max[h]) for h in hs]
        c_loc = [_dot_tn(v[h], kw[h].astype(BF16)) for h in hs]
        n_loc = [jnp.sum(kw[h], axis=0, keepdims=True) for h in hs]
        m_prev = [m_scr[h:h + 1, 0:1] for h in hs]
        c_prev = [c_scr[h] for h in hs]
        n_prev = [n_scr[h:h + 1, :] for h in hs]
        m_inter = [b_r[h] + m_prev[h] for h in hs]
        m_t = [jnp.maximum(m_inter[h], d_max[h]) for h in hs]
        w_loc = [jnp.exp(d_max[h] - m_t[h]) for h in hs]
        w_int = [jnp.exp(m_inter[h] - m_t[h]) for h in hs]
        q_c = [_dot_nt(c_prev[h].astype(BF16), q[h]) for h in hs]
        q_n = [_dot_nt(jnp.concatenate(list(_split3(n_prev[h])) + [n_pad], axis=0), q[h]) for h in hs]
        q_n = [q_n[h][0:1] + q_n[h][1:2] + q_n[h][2:3] for h in hs]
        num = [w_loc[h] * intra[h] + w_int[h] * q_c[h] for h in hs]
        den = [w_loc[h] * a_sum[h] + w_int[h] * q_n[h] for h in hs]
        h_t = [num[h] / jnp.maximum(jnp.abs(den[h]), jnp.exp(-m_t[h])) for h in hs]
        h_t = [h_t[h] * lax.rsqrt(jnp.mean(h_t[h] * h_t[h], axis=0, keepdims=True) + NORM_EPS) for h in hs]
        m_new = [jnp.maximum(b_end[h] + m_prev[h], g_max[h]) for h in hs]
        a = [jnp.exp(b_end[h] + m_prev[h] - m_new[h]) for h in hs]
        sc = [jnp.exp(g_max[h] - m_new[h]) for h in hs]
        for h in hs:
            out = h_t[h] * ng_ref[vsl[h], :] * og_ref[vsl[h], pl.ds(r0, L)]
            o_ref[pl.ds(r0, L), vsl[h]] = out.T.astype(o_ref.dtype)
            c_scr[h] = a[h] * c_prev[h] + sc[h] * c_loc[h]
            n_scr[h:h + 1, :] = a[h] * n_prev[h] + sc[h] * n_loc[h]
            m_scr[h:h + 1, :] = jnp.broadcast_to(m_new[h], (1, LANES))
        return carry

    lax.fori_loop(0, tt // L, chunk, 0)


def _mlstm_core(qk, v, ogt, gcol, grow3, conv_w, conv_b, norm_gb, batch, seq_len, tt=256):
    m, d = qk.shape
    H, dk, dv = MLSTM_HEADS, MLSTM_QK_DIM, MLSTM_V_DIM
    nt = seq_len // tt
    ncs = tt // MLSTM_TILE
    row = lambda b, i: (b * nt + i, 0)
    return pl.pallas_call(
        functools.partial(_mlstm_core_kernel, tt=tt),
        out_shape=jax.ShapeDtypeStruct((m, d), BF16),
        grid=(batch, nt),
        in_specs=[pl.BlockSpec((tt, d), row), pl.BlockSpec((tt, d), row),
                  pl.BlockSpec((d, tt), lambda b, i: (0, b * nt + i)),
                  pl.BlockSpec((tt, LANES), row),
                  pl.BlockSpec((ncs, 2 * H, MLSTM_TILE), lambda b, i: (b * nt + i, 0, 0)),
                  _const_spec((SUBLANES, d)), _const_spec((1, d)), _const_spec((d, LANES))],
        out_specs=pl.BlockSpec((tt, d), row),
        scratch_shapes=[pltpu.VMEM((tt, d), F32), pltpu.VMEM((tt + SUBLANES, d), F32),
                        pltpu.VMEM((H, dv, dk), F32), pltpu.VMEM((H, dk), F32),
                        pltpu.VMEM((H, LANES), F32)],
        compiler_params=_cparams(("arbitrary", "arbitrary")),
        name="mlstm_core",
    )(qk, v, ogt, gcol, grow3, conv_w, conv_b, norm_gb)


def _mlstm_layer(x, norm_g, w_in, conv_w, conv_b, b_gates, hnorm, batch, seq_len):
    H, dk = MLSTM_HEADS, MLSTM_QK_DIM
    qk, v, ogt, gcol, grow = _mlstm_proj(x, norm_g, w_in, b_gates)
    m = x.shape[0]
    grow3 = grow.reshape(2 * H, m // MLSTM_TILE, MLSTM_TILE).transpose(1, 0, 2)
    norm_gb = jnp.broadcast_to(hnorm.reshape(-1, 1), (hnorm.shape[0], LANES))

    def perm(z):
        lead = z.shape[:-1]
        zq = z[..., :H * dk].reshape(*lead, H, dk)
        zk = z[..., H * dk:].reshape(*lead, H, dk)
        return jnp.concatenate([zq, zk], axis=-1).reshape(*lead, 2 * H * dk)

    cw = jnp.pad(perm(conv_w), ((0, SUBLANES - MLSTM_CONV), (0, 0)))
    return _mlstm_core(qk, v, ogt, gcol, grow3, cw, perm(conv_b).reshape(1, -1), norm_gb,
                       batch, seq_len)


def _softplus(x):
    return jnp.maximum(x, 0.0) + jnp.log(1.0 + jnp.exp(-jnp.abs(x)))


def _rwkv_proj_kernel(x_ref, g_ref, mu_ref, wr_ref, wk_ref, wv_ref, ww1_ref, ww2_ref, w0_ref,
                      aw1_ref, aw2_ref, a0_ref, gw1_ref, gw2_ref,
                      r_ref, k_ref, v_ref, lw_ref, a_ref, go_ref, prev_scr, *, tiles_per_seq):
    tm = x_ref.shape[0]
    h = _rms(x_ref[...], g_ref[...])
    seq_start = (pl.program_id(0) % tiles_per_seq) == 0
    prev = jnp.where(seq_start, 0.0, prev_scr[SUBLANES - 1:SUBLANES, :])
    prev_scr[...] = h[tm - SUBLANES:, :]
    rows = lax.broadcasted_iota(jnp.int32, h.shape, 0)
    xx = jnp.where(rows == 0, prev, pltpu.roll(h, 1, 0)) - h
    mix = lambda j: (h + xx * mu_ref[j:j + 1, :]).astype(BF16)
    r_ref[...] = _dot(mix(0), wr_ref[...])
    k_ref[...] = _dot(mix(2), wk_ref[...])
    v_ref[...] = _dot(mix(3), wv_ref[...])
    wl = _dot(jnp.tanh(_dot(mix(1), ww1_ref[...])).astype(BF16), ww2_ref[...]) + w0_ref[...]
    w_log = -_softplus(-wl) - 0.5
    lw_ref[...] = -jnp.exp(w_log)
    al = _dot(_dot(mix(4), aw1_ref[...]).astype(BF16), aw2_ref[...]) + a0_ref[...]
    a_ref[...] = _sigmoid(al)
    go_ref[...] = _dot(_sigmoid(_dot(mix(5), gw1_ref[...])).astype(BF16), gw2_ref[...])


def _rwkv_proj(x, g, mu, w_r, w_k, w_v, w0, w_w1, w_w2, a0, a_w1, a_w2, g_w1, g_w2, seq_len, tm=512):
    m, d = x.shape
    row = lambda i: (i, 0)
    bf = lambda w: w.astype(BF16)
    mu8 = jnp.pad(mu, ((0, SUBLANES - mu.shape[0]), (0, 0)))
    consts = [g.reshape(1, d), mu8, bf(w_r), bf(w_k), bf(w_v), bf(w_w1), bf(w_w2), w0.reshape(1, d),
              bf(a_w1), bf(a_w2), a0.reshape(1, d), bf(g_w1), bf(g_w2)]
    out = jax.ShapeDtypeStruct((m, d), F32)
    return pl.pallas_call(
        functools.partial(_rwkv_proj_kernel, tiles_per_seq=seq_len // tm),
        out_shape=(out,) * 6,
        grid=(m // tm,),
        in_specs=[pl.BlockSpec((tm, d), row)] + [_const_spec(c.shape) for c in consts],
        out_specs=(pl.BlockSpec((tm, d), row),) * 6,
        scratch_shapes=[pltpu.VMEM((SUBLANES, d), F32)],
        compiler_params=_cparams(("arbitrary",)),
        name="rwkv_proj",
    )(x, *consts)


def _block_diag(x, lo):
    zero = jnp.zeros_like(x)
    return jnp.concatenate([jnp.where(lo, x, zero), jnp.where(lo, zero, x)], axis=0)


class _PairMat:
    def __init__(self, x, lo):
        self.x, self.lo = x, lo
        self._lhs = self._rhs = None

    def lhs(self):
        if self._lhs is None:
            hi, lo = _split2(self.x)
            self._lhs = jnp.concatenate([hi, lo, hi], axis=1)
        return self._lhs

    def rhs(self):
        if self._rhs is None:
            hi, lo = _split2(self.x)
            bh = _block_diag(hi, self.lo)
            self._rhs = jnp.concatenate([bh, bh, _block_diag(lo, self.lo)], axis=0)
        return self._rhs


def _pair_mm(p, q):
    return _dot(p.lhs(), q.rhs())


def _rwkv_core_kernel(r_ref, k_ref, v_ref, lw_ref, a_ref, go_ref, kk_ref, ka_ref, rk_ref,
                      lnw_ref, lnb_ref, o_ref, z_scr, *, tt):
    L, N = RWKV_CHUNK, RWKV_HEAD_DIM
    npair = z_scr.shape[0]
    pairs = range(npair)

    @pl.when(pl.program_id(1) == 0)
    def _():
        z_scr[...] = jnp.zeros_like(z_scr)

    ri = lax.broadcasted_iota(jnp.int32, (L, LANES), 0)
    ln = lax.broadcasted_iota(jnp.int32, (L, LANES), 1)
    si = ln % N
    lo = ln < N
    lower_incl = ri >= si
    lower_strict = ri > si
    blk_diag = (ri // RWKV_INV_BLOCK) == (si // RWKV_INV_BLOCK)
    eye = (ri == si).astype(F32)
    tri = (lax.broadcasted_iota(jnp.int32, (L, L), 0) >= lax.broadcasted_iota(jnp.int32, (L, L), 1))
    tri3 = jnp.concatenate([tri.astype(BF16)] * 3, axis=1)
    z_mask = ((lax.broadcasted_iota(jnp.int32, (2 * N, LANES), 0) // N)
              == (lax.broadcasted_iota(jnp.int32, (2 * N, LANES), 1) // N))

    def half_sum(x):
        s0 = jnp.sum(jnp.where(lo, x, 0.0), axis=-1, keepdims=True)
        s1 = jnp.sum(jnp.where(lo, 0.0, x), axis=-1, keepdims=True)
        return jnp.where(lo, s0, s1)

    bd = lambda x: _block_diag(x, lo)
    mk = lambda xs: [_PairMat(x, lo) for x in xs]
    mm = lambda ps, qs: [_pair_mm(p, q) for p, q in zip(ps, qs)]

    def prep(grp, r0):
        idx = range(len(grp))
        sl = [slice(p * LANES, (p + 1) * LANES) for p in grp]
        ld = lambda ref: [ref[pl.ds(r0, L), s] for s in sl]
        r, k, v, lw, a = ld(r_ref), ld(k_ref), ld(v_ref), ld(lw_ref), ld(a_ref)
        kk = [k[i] * kk_ref[:, sl[i]] for i in idx]
        kk = [kk[i] / jnp.maximum(jnp.sqrt(half_sum(kk[i] * kk[i])), 1e-12) for i in idx]
        km = [k[i] * (1.0 + (a[i] - 1.0) * ka_ref[:, sl[i]]) for i in idx]
        bv = [kk[i] * a[i] for i in idx]
        cum = [_cumsum_rows(tri3, lw[i]) for i in idx]
        cum_end = [cum[i][L - 1:L, :] for i in idx]
        w_inv = [jnp.exp(-cum[i]) for i in idx]
        w_out = [jnp.exp(cum_end[i] - cum[i]) for i in idx]
        kk_h = [(kk[i] * jnp.exp(cum[i] - lw[i])).astype(BF16) for i in idx]
        r_h = [(r[i] * jnp.exp(cum[i])).astype(BF16) for i in idx]
        b_t = [(bv[i] * w_inv[i]).astype(BF16) for i in idx]
        k_t = [(km[i] * w_inv[i]).astype(BF16) for i in idx]
        bbar = [(bv[i] * w_out[i]).astype(BF16) for i in idx]
        kbar = [(km[i] * w_out[i]).astype(BF16) for i in idx]
        vb = [v[i].astype(BF16) for i in idx]
        lhs = [jnp.concatenate([kk_h[i], r_h[i]], axis=0) for i in idx]
        ab = [_dot_nt(lhs[i], bd(b_t[i])) for i in idx]
        ak = [_dot_nt(lhs[i], bd(k_t[i])) for i in idx]
        return dict(
            sl=sl, r=r, v=v, km=km, kk_h=kk_h, r_h=r_h, bbar=bbar, kbar=kbar, vb=vb, cum_end=cum_end,
            a_ub=[jnp.where(lower_strict, ab[i][:L], 0.0) for i in idx],
            a_rb=[jnp.where(lower_incl, ab[i][L:], 0.0).astype(BF16) for i in idx],
            a_uk=[jnp.where(lower_strict, ak[i][:L], 0.0).astype(BF16) for i in idx],
            a_rk=[jnp.where(lower_incl, ak[i][L:], 0.0).astype(BF16) for i in idx])

    def inverse(s):
        a_ub = s["a_ub"]
        idx = range(len(a_ub))
        n1 = mk([jnp.where(blk_diag, -a_ub[i], 0.0) for i in idx])
        n2 = mk(mm(n1, n1))
        n4 = mk(mm(n2, n2))
        n8 = mk(mm(n4, n4))
        acc = mk([eye + n1[i].x for i in idx])
        for nk in (n2, n4):
            prod = mm(acc, nk)
            acc = mk([acc[i].x + prod[i] for i in idx])
        prod = mm(acc, n8)
        d_inv = mk([acc[i].x + prod[i] for i in idx])
        l_off = mk([jnp.where(blk_diag, 0.0, a_ub[i]) for i in idx])
        e1 = mk([-x for x in mm(d_inv, l_off)])
        e2 = mk(mm(e1, e1))
        qm = mk([eye + e1[i].x for i in idx])
        prod = mm(qm, e2)
        qm = mk([qm[i].x + prod[i] for i in idx])
        s["t_inv"] = mk(mm(qm, d_inv))

    def state(grp, s):
        idx = range(len(grp))
        z = [z_scr[p] for p in grp]
        zb = [z[i].astype(BF16) for i in idx]
        vb = s["vb"]
        bdv = [bd(vb[i]) for i in idx]
        rhs_u = mk([_dot_nt(s["kk_h"][i], zb[i]) + _dot(s["a_uk"][i], bdv[i]) for i in idx])
        u = [-x for x in mm(s["t_inv"], rhs_u)]
        ub = [u[i].astype(BF16) for i in idx]
        s["y"] = [_dot_nt(s["r_h"][i], zb[i])
                  + _dot(jnp.concatenate([s["a_rb"][i], s["a_rk"][i]], axis=1),
                         jnp.concatenate([bd(ub[i]), bdv[i]], axis=0)) for i in idx]
        upd = [_dot_tn(jnp.concatenate([ub[i], vb[i]], axis=0),
                       jnp.concatenate([s["bbar"][i], s["kbar"][i]], axis=0)) for i in idx]
        for i, p in enumerate(grp):
            z_scr[p] = z[i] * jnp.exp(s["cum_end"][i]) + jnp.where(z_mask, upd[i], 0.0)

    def post(s, r0):
        y, sl, r, v, km = s["y"], s["sl"], s["r"], s["v"], s["km"]
        idx = range(len(y))
        inv_n = 1.0 / N
        mean = [half_sum(y[i]) * inv_n for i in idx]
        yc = [y[i] - mean[i] for i in idx]
        var = [half_sum(yc[i] * yc[i]) * inv_n for i in idx]
        bonus = [half_sum(r[i] * km[i] * rk_ref[:, sl[i]]) * v[i] for i in idx]
        for i in idx:
            yn = yc[i] * lax.rsqrt(var[i] + RWKV_GN_EPS) * lnw_ref[:, sl[i]] + lnb_ref[:, sl[i]]
            o_ref[pl.ds(r0, L), sl[i]] = ((yn + bonus[i]) * go_ref[pl.ds(r0, L), sl[i]]).astype(o_ref.dtype)

    group = list(pairs)

    n_chunks = tt // L
    cur = prep(group, 0)
    inverse(cur)
    for c in range(1, n_chunks):
        nxt = prep(group, c * L)
        state(group, cur)
        inverse(nxt)
        post(cur, (c - 1) * L)
        cur = nxt
    state(group, cur)
    post(cur, (n_chunks - 1) * L)


def _rwkv_core(r, k, v, lw, a, go, k_k, k_a, r_k, ln_w, ln_b, batch, seq_len, tt=256):
    m, d = r.shape
    nt = seq_len // tt
    npair = d // LANES
    blk = pl.BlockSpec((tt, d), lambda b, i: (b * nt + i, 0))
    par = _const_spec((1, d))
    row1 = lambda z: z.reshape(1, d)
    return pl.pallas_call(
        functools.partial(_rwkv_core_kernel, tt=tt),
        out_shape=jax.ShapeDtypeStruct((m, d), BF16),
        grid=(batch, nt),
        in_specs=[blk] * 6 + [par] * 5,
        out_specs=blk,
        scratch_shapes=[pltpu.VMEM((npair, 2 * RWKV_HEAD_DIM, LANES), F32)],
        compiler_params=_cparams(("arbitrary", "arbitrary")),
        name="rwkv_core",
    )(r, k, v, lw, a, go, row1(k_k), row1(k_a), row1(r_k), row1(ln_w), row1(ln_b))


def _rwkv_layer(x, norm_g, mu, w_r, w_k, w_v, w0, w_w1, w_w2, a0, a_w1, a_w2, g_w1, g_w2,
                k_k, k_a, r_k, ln_w, ln_b, batch, seq_len):
    r, k, v, lw, a, go = _rwkv_proj(x, norm_g, mu, w_r, w_k, w_v, w0, w_w1, w_w2, a0, a_w1, a_w2,
                                    g_w1, g_w2, seq_len)
    return _rwkv_core(r, k, v, lw, a, go, k_k, k_a, r_k, ln_w, ln_b, batch, seq_len)


def kernel(x, norm_mixer, norm_ffn, ffn_w_up, ffn_conv_w, ffn_conv_b, ffn_w_down, nsa_w_in, nsa_pe_k, nsa_pe_v, nsa_cmp_k_w1, nsa_cmp_k_w2, nsa_cmp_v_w1, nsa_cmp_v_w2, nsa_b_gate, nsa_w_out, mlstm_w_in, mlstm_conv_w, mlstm_conv_b, mlstm_b_gates, mlstm_norm, mlstm_w_out, rwkv_mu, rwkv_w_r, rwkv_w_k, rwkv_w_v, rwkv_w_o, rwkv_w0, rwkv_w_w1, rwkv_w_w2, rwkv_a0, rwkv_a_w1, rwkv_a_w2, rwkv_g_w1, rwkv_g_w2, rwkv_k_k, rwkv_k_a, rwkv_r_k, rwkv_ln_w, rwkv_ln_b, final_norm):
    batch, seq_len, d = x.shape
    depth = norm_mixer.shape[0]
    rope = _rope_tables(seq_len)
    xf = x.reshape(batch * seq_len, d)
    for i in range(depth):
        kind, j = i % 3, i // 3
        if kind == 0:
            w_o = nsa_w_out[j]
            o = _nsa_layer(xf, norm_mixer[i], nsa_w_in[j], nsa_pe_k[j], nsa_pe_v[j], nsa_cmp_k_w1[j],
                           nsa_cmp_k_w2[j], nsa_cmp_v_w1[j], nsa_cmp_v_w2[j], nsa_b_gate[j], rope, batch, seq_len)
        elif kind == 1:
            w_o = mlstm_w_out[j]
            o = _mlstm_layer(xf, norm_mixer[i], mlstm_w_in[j], mlstm_conv_w[j], mlstm_conv_b[j],
                             mlstm_b_gates[j], mlstm_norm[j], batch, seq_len)
        else:
            w_o = rwkv_w_o[j]
            o = _rwkv_layer(xf, norm_mixer[i], rwkv_mu[j], rwkv_w_r[j], rwkv_w_k[j], rwkv_w_v[j],
                            rwkv_w0[j], rwkv_w_w1[j], rwkv_w_w2[j], rwkv_a0[j],
                            rwkv_a_w1[j], rwkv_a_w2[j], rwkv_g_w1[j], rwkv_g_w2[j], rwkv_k_k[j],
                            rwkv_k_a[j], rwkv_r_k[j], rwkv_ln_w[j], rwkv_ln_b[j], batch, seq_len)
        xf = _ffn(xf, o, w_o, norm_ffn[i], ffn_w_up[i], ffn_conv_w[i], ffn_conv_b[i], ffn_w_down[i],
                  seq_len, final_g=final_norm if i == depth - 1 else None)
    return xf.reshape(batch, seq_len, d)
```

```python
import functools
import math

import jax
import jax.numpy as jnp
import numpy as np
from jax import lax
from jax.experimental import pallas as pl
from jax.experimental.pallas import tpu as pltpu

F32 = jnp.float32
BF16 = jnp.bfloat16

D_MODEL = 1024
NORM_EPS = 1e-6
ROPE_THETA = 500000.0

NSA_HEAD_DIM = 64
NSA_HEADS = 16
NSA_GROUPS = 4
NSA_REP = NSA_HEADS // NSA_GROUPS
NSA_ROT_DIM = 16
CMP_BLOCK = 32
CMP_STRIDE = 16
CMP_HIDDEN = 256
SEL_BLOCK = 64
SEL_TOPK = 16
WINDOW = 512
NSA_KV = NSA_GROUPS * NSA_HEAD_DIM
BF16_SUBLANES = 16
NSA_VT_ROWS = NSA_HEAD_DIM + BF16_SUBLANES
NSA_GATE_ROWS = 16
NSA_WINDOW_QUERIES = 256

MLSTM_HEADS = 8
MLSTM_QK_DIM = 64
MLSTM_V_DIM = 128
MLSTM_TILE = 128
MLSTM_CONV = 4

RWKV_HEAD_DIM = 64
RWKV_GN_EPS = 64e-5
RWKV_CHUNK = 64
RWKV_INV_BLOCK = 16

FFN_DIM = 2816
FFN_CONV = 3
FFN_CHUNK = 256
FFN_ROW_BLOCK = 64
FFN_DOWN_GROUP = 6

LOG2E = math.log2(math.e)
MASKED = -1e30

LANES = 128
SUBLANES = 8
VMEM_LIMIT = 56 * 1024 * 1024


def _dot(a, b):
    return jnp.dot(a, b, preferred_element_type=F32)


def _dot_nt(a, b):
    return lax.dot_general(a, b, (((1,), (1,)), ((), ())), preferred_element_type=F32)


def _dot_tn(a, b):
    return lax.dot_general(a, b, (((0,), (0,)), ((), ())), preferred_element_type=F32)


def _split2(x):
    hi = x.astype(BF16)
    return hi, (x - hi.astype(F32)).astype(BF16)


def _split3(x):
    hi = x.astype(BF16)
    r1 = x - hi.astype(F32)
    mid = r1.astype(BF16)
    return hi, mid, (r1 - mid.astype(F32)).astype(BF16)


def _cumsum_rows(tri3, x):
    return _dot(tri3, jnp.concatenate(_split3(x), axis=0))


def _cumsum_lanes(x, triu3):
    return _dot(jnp.concatenate(_split3(x), axis=1), triu3)


def _rms(x, g):
    ms = jnp.mean(x * x, axis=-1, keepdims=True)
    return x * lax.rsqrt(ms + NORM_EPS) * g


def _sigmoid(x):
    return 1.0 / (1.0 + jnp.exp(-x))


def _cparams(sem):
    return pltpu.CompilerParams(dimension_semantics=sem, vmem_limit_bytes=VMEM_LIMIT)


def _const_spec(shape):
    n = len(shape)
    return pl.BlockSpec(shape, lambda *_: (0,) * n, pipeline_mode=pl.Buffered(1))


def _ffn_kernel(res_ref, a_ref, wo_ref, g_ref, wu_ref, cw_ref, cb_ref, wd_ref, fg_ref, o_ref,
                h_scr, carry_scr, acta_scr, actb_scr, ga_scr, va_scr, gb_scr, vb_scr,
                *, tiles_per_seq, n_chunks, final_norm):
    tm = res_ref.shape[0]
    fc = FFN_CHUNK
    halo = SUBLANES
    x = res_ref[...] + _dot(a_ref[...], wo_ref[...])
    h_scr[...] = _rms(x, g_ref[...]).astype(BF16)
    o_ref[...] = x
    seq_start = (pl.program_id(0) % tiles_per_seq) == 0
    cols = lambda c, base=0: pl.ds(pl.multiple_of(base + c * fc, LANES), fc)

    def up(c, g_scr, v_scr):
        h = h_scr[...]
        g_scr[halo:, :] = _dot(h, wu_ref[:, cols(c)])
        v_scr[...] = _dot(h, wu_ref[:, cols(c, FFN_DIM)])

    def activate(c, g_scr, v_scr, act_scr, slot):
        g_scr[0:halo, :] = jnp.where(seq_start, 0.0, carry_scr[:, cols(c)])
        carry_scr[:, cols(c)] = g_scr[tm:tm + halo, :]
        cw = cw_ref[:, cols(c)]
        cb = cb_ref[:, cols(c)]
        for r0 in range(0, tm, FFN_ROW_BLOCK):
            y = cb + sum(cw[FFN_CONV - 1 - s:FFN_CONV - s, :] * g_scr[halo - s + r0:halo - s + r0 + FFN_ROW_BLOCK, :]
                         for s in range(FFN_CONV))
            act = y * _sigmoid(y) * v_scr[r0:r0 + FFN_ROW_BLOCK, :]
            act_scr[r0:r0 + FFN_ROW_BLOCK, slot * fc:(slot + 1) * fc] = act.astype(BF16)

    def down(c0, n, act_scr):
        o_ref[...] += _dot(act_scr[:, 0:n * fc], wd_ref[c0 * fc:(c0 + n) * fc, :])

    bufs = ((ga_scr, va_scr), (gb_scr, vb_scr))
    acts = (acta_scr, actb_scr)
    up(0, *bufs[0])
    for c in range(n_chunks):
        if c + 1 < n_chunks:
            up(c + 1, *bufs[(c + 1) % 2])
        grp, slot = divmod(c, FFN_DOWN_GROUP)
        activate(c, *bufs[c % 2], acts[grp % 2], slot)
        if slot == FFN_DOWN_GROUP - 1 or c == n_chunks - 1:
            down(grp * FFN_DOWN_GROUP, slot + 1, acts[grp % 2])
    if final_norm:
        o_ref[...] = _rms(o_ref[...], fg_ref[...])


def _ffn(res, a, w_o, g, w_up, conv_w, conv_b, w_down, seq_len, final_g=None, tm=1024):
    m, d = res.shape
    nc = FFN_DIM // FFN_CHUNK
    cw = jnp.pad(conv_w, ((0, SUBLANES - FFN_CONV), (0, 0)))
    row = lambda i: (i, 0)
    fg = jnp.ones((d,), F32) if final_g is None else final_g
    kern = functools.partial(_ffn_kernel, tiles_per_seq=seq_len // tm, n_chunks=nc,
                             final_norm=final_g is not None)
    return pl.pallas_call(
        kern,
        out_shape=jax.ShapeDtypeStruct((m, d), F32),
        grid=(m // tm,),
        in_specs=[pl.BlockSpec((tm, d), row), pl.BlockSpec((tm, d), row), _const_spec((d, d)),
                  _const_spec((1, d)), _const_spec((d, 2 * FFN_DIM)),
                  _const_spec((SUBLANES, FFN_DIM)), _const_spec((1, FFN_DIM)),
                  _const_spec((FFN_DIM, d)), _const_spec((1, d))],
        out_specs=pl.BlockSpec((tm, d), row),
        scratch_shapes=[pltpu.VMEM((tm, d), BF16),
                        pltpu.VMEM((SUBLANES, FFN_DIM), F32),
                        pltpu.VMEM((tm, FFN_DOWN_GROUP * FFN_CHUNK), BF16),
                        pltpu.VMEM((tm, FFN_DOWN_GROUP * FFN_CHUNK), BF16)]
                       + [pltpu.VMEM((tm + SUBLANES, FFN_CHUNK), F32), pltpu.VMEM((tm, FFN_CHUNK), F32)] * 2,
        compiler_params=_cparams(("arbitrary",)),
        name="conv_ffn",
    )(res, a, w_o.astype(BF16), g.reshape(1, d), w_up.astype(BF16), cw, conv_b.reshape(1, FFN_DIM),
      w_down.astype(BF16), fg.reshape(1, d))


def _nsa_proj_kernel(x_ref, g_ref, w_ref, wvt_ref, bg_ref, rc_ref, rs1_ref, rs2_ref,
                     qp_ref, qr_ref, kc_ref, vc_ref, ks_ref, vs_ref, kw_ref, vw_ref, gate_ref,
                     cmp_scr, *, tiles_per_seq):
    hn = _rms(x_ref[...], g_ref[...]).astype(BF16)
    y = _dot(hn, w_ref[...])
    vt = _dot_nt(wvt_ref[...], hn)
    rc, rs1, rs2 = rc_ref[...], rs1_ref[...], rs2_ref[...]
    dh = NSA_HEAD_DIM

    def rope(z):
        half = NSA_ROT_DIM // 2
        return z * rc + pltpu.roll(z, half, 1) * rs1 + pltpu.roll(z, LANES - half, 1) * rs2

    scale = dh ** -0.5 * LOG2E
    for j in range(D_MODEL // LANES):
        q = y[:, j * LANES:(j + 1) * LANES] * scale
        qp_ref[:, j * LANES:(j + 1) * LANES] = q.astype(BF16)
        qr_ref[:, j * LANES:(j + 1) * LANES] = rope(q).astype(BF16)

    def kv_chunk(idx):
        return y[:, D_MODEL + idx * NSA_KV:D_MODEL + (idx + 1) * NSA_KV]

    def split_groups(z, ref, dtype):
        for g in range(NSA_GROUPS):
            ref[g] = z[:, g * dh:(g + 1) * dh].astype(dtype)

    def rope256(z):
        return jnp.concatenate([rope(z[:, :LANES]), rope(z[:, LANES:])], axis=1)

    tm = y.shape[0]
    nrow = tm // CMP_STRIDE
    for j in range(2 * NSA_KV // LANES):
        cmp_scr[j] = y[:, D_MODEL + j * LANES:D_MODEL + (j + 1) * LANES]
    for j in range(2 * NSA_KV // LANES):
        ref = kc_ref if j < NSA_KV // LANES else vc_ref
        toks = [cmp_scr[j, pl.ds(tok, nrow, stride=CMP_STRIDE), :] for tok in range(CMP_STRIDE)]
        for half in range(LANES // dh):
            g = (j % (NSA_KV // LANES)) * (LANES // dh) + half
            ref[g] = jnp.concatenate([t[:, half * dh:(half + 1) * dh] for t in toks], axis=1)
    t_pos = (pl.program_id(0) % tiles_per_seq) * tm + lax.broadcasted_iota(jnp.int32, (tm, LANES), 0)
    onehot = (lax.broadcasted_iota(jnp.int32, (tm, LANES), 1) == t_pos // SEL_BLOCK).astype(F32)
    ksel = rope256(kv_chunk(2))
    zpad = jnp.zeros((tm, LANES - dh), F32)
    for g in range(NSA_GROUPS):
        ks_ref[g] = jnp.concatenate([ksel[:, g * dh:(g + 1) * dh], zpad, onehot], axis=1).astype(BF16)
    split_groups(rope256(kv_chunk(3)), kw_ref, BF16)
    ones_pad = (lax.broadcasted_iota(jnp.int32, (NSA_VT_ROWS - dh, tm), 0) == 0).astype(F32)
    for g in range(NSA_GROUPS):
        vs_ref[g] = jnp.concatenate([vt[g * dh:(g + 1) * dh], ones_pad], axis=0).astype(BF16)
        vw_ref[g] = jnp.concatenate([vt[NSA_KV + g * dh:NSA_KV + (g + 1) * dh], ones_pad], axis=0).astype(BF16)
    gate = _sigmoid(vt[2 * NSA_KV:] + bg_ref[...])
    for g in range(NSA_GROUPS):
        gate_ref[g] = gate[g * NSA_GATE_ROWS:(g + 1) * NSA_GATE_ROWS]


def _rope_tables(seq_len):
    half = NSA_ROT_DIM // 2
    inv_freq = ROPE_THETA ** (-jnp.arange(half, dtype=F32) / half)
    ang = jnp.arange(seq_len, dtype=F32)[:, None] * inv_freq[None, :]
    cos, sin = jnp.cos(ang), jnp.sin(ang)
    zeros = jnp.zeros((seq_len, NSA_HEAD_DIM - NSA_ROT_DIM), F32)
    z8 = jnp.zeros((seq_len, half), F32)
    rc = jnp.concatenate([cos, cos, zeros + 1.0], axis=1)
    rs1 = jnp.concatenate([z8, sin, zeros], axis=1)
    rs2 = jnp.concatenate([-sin, z8, zeros], axis=1)
    two = lambda t: jnp.concatenate([t, t], axis=1)
    return two(rc), two(rs1), two(rs2)


def _nsa_proj(x, g, w_in, b_gate, rope, seq_len, tm=512):
    m, d = x.shape
    n_kv = 6 * NSA_KV
    kv = lambda idx: w_in[:, D_MODEL + idx * NSA_KV:D_MODEL + (idx + 1) * NSA_KV]
    pad_g = NSA_GATE_ROWS - NSA_REP * 3
    wg = w_in[:, D_MODEL + n_kv:].reshape(d, NSA_GROUPS, NSA_REP * 3)
    wg = jnp.pad(wg, ((0, 0), (0, 0), (0, pad_g))).reshape(d, NSA_GROUPS * NSA_GATE_ROWS)
    w = jnp.concatenate([w_in[:, :D_MODEL], kv(0), kv(1), kv(2), kv(4)], axis=1).astype(BF16)
    wvt = jnp.concatenate([kv(3), kv(5), wg], axis=1).T.astype(BF16)
    bg = jnp.pad(b_gate.reshape(NSA_GROUPS, NSA_REP * 3), ((0, 0), (0, pad_g)))
    bg = bg.reshape(NSA_GROUPS * NSA_GATE_ROWS, 1)
    n = w.shape[1]
    tps = seq_len // tm
    row = lambda i: (i, 0)
    rope_spec = pl.BlockSpec((tm, LANES), lambda i: (i % tps, 0))
    assert seq_len // SEL_BLOCK <= LANES
    g_out = lambda dt, w=NSA_HEAD_DIM: jax.ShapeDtypeStruct((NSA_GROUPS, m, w), dt)
    g_spec = pl.BlockSpec((NSA_GROUPS, tm, NSA_HEAD_DIM), lambda i: (0, i, 0))
    ks_spec = pl.BlockSpec((NSA_GROUPS, tm, 2 * LANES), lambda i: (0, i, 0))
    cmp_out = jax.ShapeDtypeStruct((NSA_GROUPS, m // CMP_STRIDE, CMP_STRIDE * NSA_HEAD_DIM), F32)
    cmp_spec = pl.BlockSpec((NSA_GROUPS, tm // CMP_STRIDE, CMP_STRIDE * NSA_HEAD_DIM), lambda i: (0, i, 0))
    vt_out = jax.ShapeDtypeStruct((NSA_GROUPS, NSA_VT_ROWS, m), BF16)
    vt_spec = pl.BlockSpec((NSA_GROUPS, NSA_VT_ROWS, tm), lambda i: (0, 0, i))
    return pl.pallas_call(
        functools.partial(_nsa_proj_kernel, tiles_per_seq=tps),
        out_shape=(jax.ShapeDtypeStruct((m, d), BF16), jax.ShapeDtypeStruct((m, d), BF16),
                   cmp_out, cmp_out, g_out(BF16, 2 * LANES), vt_out, g_out(BF16), vt_out,
                   jax.ShapeDtypeStruct((NSA_GROUPS, NSA_GATE_ROWS, m), F32)),
        grid=(m // tm,),
        in_specs=[pl.BlockSpec((tm, d), row), _const_spec((1, d)), _const_spec((d, n)),
                  _const_spec((2 * NSA_KV + NSA_GROUPS * NSA_GATE_ROWS, d)),
                  _const_spec((NSA_GROUPS * NSA_GATE_ROWS, 1)), rope_spec, rope_spec, rope_spec],
        out_specs=(pl.BlockSpec((tm, d), row), pl.BlockSpec((tm, d), row),
                   cmp_spec, cmp_spec, ks_spec, vt_spec, g_spec, vt_spec,
                   pl.BlockSpec((NSA_GROUPS, NSA_GATE_ROWS, tm), lambda i: (0, 0, i))),
        scratch_shapes=[pltpu.VMEM((2 * NSA_KV // LANES, tm, LANES), F32)],
        compiler_params=_cparams(("parallel",)),
        name="nsa_proj",
    )(x, g.reshape(1, d), w, wvt, bg, *rope)


def _gelu_tanh(x):
    return 0.5 * x * (1.0 + jnp.tanh(math.sqrt(2.0 / math.pi) * (x + 0.044715 * (x * x * x))))


def _compress_kernel(zk_ref, zv_ref, pek_ref, pev_ref, w1k_ref, w2k_ref, w1v_ref, w2v_ref,
                     kc_ref, vc_ref):
    nrow = zk_ref.shape[0]
    rows = lax.broadcasted_iota(jnp.int32, (nrow, NSA_HEAD_DIM), 0)

    def one(z_ref, pe_ref, w1_ref, w2_ref, o_ref):
        z = z_ref[...]
        a = _dot((z + pe_ref[0:1, :]).astype(BF16), w1_ref[0])
        b = _dot((z + pe_ref[1:2, :]).astype(BF16), w1_ref[1])
        hid = a + pltpu.roll(b, nrow - 1, 0)
        out = _dot(_gelu_tanh(hid).astype(BF16), w2_ref[...])
        o_ref[...] = jnp.where(rows == nrow - 1, 0.0, out).astype(o_ref.dtype)

    one(zk_ref, pek_ref, w1k_ref, w2k_ref, kc_ref)
    one(zv_ref, pev_ref, w1v_ref, w2v_ref, vc_ref)


def _compress(kc_raw, vc_raw, pe_k, pe_v, w1k, w2k, w1v, w2v, seq_len):
    g, nrows, half = kc_raw.shape
    dh = half // CMP_STRIDE
    nchunk = seq_len // CMP_STRIDE
    zk = kc_raw.reshape(g * nrows, half)
    zv = vc_raw.reshape(g * nrows, half)
    pe2 = lambda pe: pe.reshape(2, half)
    w1 = lambda w: w.astype(BF16).reshape(2, half, CMP_HIDDEN)
    nblk = zk.shape[0] // nchunk
    row = lambda i: (i, 0)
    return pl.pallas_call(
        _compress_kernel,
        out_shape=(jax.ShapeDtypeStruct((zk.shape[0], dh), BF16),
                   jax.ShapeDtypeStruct((zk.shape[0], dh), BF16)),
        grid=(nblk,),
        in_specs=[pl.BlockSpec((nchunk, half), row), pl.BlockSpec((nchunk, half), row),
                  _const_spec((2, half)), _const_spec((2, half)),
                  _const_spec((2, half, CMP_HIDDEN)), _const_spec((CMP_HIDDEN, dh)),
                  _const_spec((2, half, CMP_HIDDEN)), _const_spec((CMP_HIDDEN, dh))],
        out_specs=(pl.BlockSpec((nchunk, dh), row), pl.BlockSpec((nchunk, dh), row)),
        compiler_params=_cparams(("parallel",)),
        name="nsa_compress",
    )(zk, zv, pe2(pe_k), pe2(pe_v), w1(w1k), w2k.astype(BF16), w1(w1v), w2v.astype(BF16))


def _nsa_cmp_kernel(q_ref, kc_ref, vc_ref, gate_ref, ovt_ref, oc_ref, selt_ref, imp_scr, *, tq, n_classes):
    qi = pl.program_id(2)
    nq = pl.num_programs(2)
    dh = NSA_HEAD_DIM
    ncmp = kc_ref.shape[0]
    nsel = imp_scr.shape[0]
    heads = range(NSA_REP)

    def attend(nrows):
        q = q_ref[...]
        qs = jnp.concatenate([q[:, r * dh:(r + 1) * dh] for r in heads], axis=0)
        kc, vc = kc_ref[0:nrows, :], vc_ref[0:nrows, :]
        t = qi * tq + lax.broadcasted_iota(jnp.int32, (nrows, tq), 1)
        cmp_end = lax.broadcasted_iota(jnp.int32, (nrows, tq), 0) * CMP_STRIDE + (CMP_BLOCK - 1)
        neg = jnp.where(cmp_end <= t, 0.0, -jnp.inf)
        st = [_dot_nt(kc, qs[r * tq:(r + 1) * tq]) + neg for r in heads]
        mx = [jnp.max(st[r], axis=0, keepdims=True) for r in heads]
        mx = [jnp.where(mx[r] == -jnp.inf, 0.0, mx[r]) for r in heads]
        e = [jnp.exp2(st[r] - mx[r]) for r in heads]
        p = [e[r] / jnp.maximum(jnp.sum(e[r], axis=0, keepdims=True), 1e-30) for r in heads]
        o_t = [_dot_tn(vc, p[r].astype(BF16)) for r in heads]
        gate = gate_ref[...]
        o_t = [o_t[r] * gate[3 * r:3 * r + 1, :] for r in heads]
        for j in range(NSA_REP // 2):
            pair = jnp.concatenate([o_t[2 * j], o_t[2 * j + 1]], axis=0).T
            oc_ref[:, 2 * j * dh:2 * (j + 1) * dh] = pair
        psum = sum(p)
        ovt = jnp.concatenate([ovt_ref[:, j * ncmp:j * ncmp + nrows] for j in range(3)], axis=1)
        imp_scr[...] = _dot(ovt, jnp.concatenate(_split3(psum), axis=0))

    def select(nblk):
        blk = lax.broadcasted_iota(jnp.int32, (nblk, LANES), 0)
        blk_f = blk.astype(F32)
        if nblk < nsel:
            selt_ref[nblk:, :] = jnp.zeros((nsel - nblk, tq), selt_ref.dtype)
        for cb in range(tq // LANES):
            csl = slice(cb * LANES, (cb + 1) * LANES)
            tb = (qi * tq + cb * LANES + lax.broadcasted_iota(jnp.int32, (nblk, LANES), 1)) // SEL_BLOCK
            forced = (blk == 0) | (blk == tb) | (blk == tb - 1)
            vals = jnp.where(forced, -jnp.inf, jnp.where(blk <= tb, imp_scr[0:nblk, csl], -1.0))
            for _ in range(SEL_TOPK - 3):
                top = jnp.max(vals, axis=0, keepdims=True)
                first = jnp.min(jnp.where(vals == top, blk_f, float(nsel)), axis=0, keepdims=True)
                vals = jnp.where(blk_f == first, -jnp.inf, vals)
            selt_ref[0:nblk, csl] = jnp.where(vals == -jnp.inf, 1.0, 0.0).astype(selt_ref.dtype)

    for cls in range(n_classes):
        @pl.when((qi * n_classes) // nq == cls)
        def _(cls=cls):
            attend((cls + 1) * ncmp // n_classes)
            select((cls + 1) * nsel // n_classes)


def _overlap_matrix_t3(ncmp_pad, nsel):
    c = np.arange(ncmp_pad)[None, :]
    s = np.arange(nsel)[:, None]
    cmp_start = c * CMP_STRIDE
    cmp_end = cmp_start + CMP_BLOCK - 1
    blk_start = s * SEL_BLOCK
    ov = ((cmp_end >= blk_start) & (cmp_start <= blk_start + SEL_BLOCK - 1)).astype(np.float32)
    return jnp.asarray(np.concatenate([ov, ov, ov], axis=1), dtype=BF16)


def _nsa_cmp(qp, kc, vc, gates, batch, seq_len, tq=512):
    m, d = qp.shape
    nq = seq_len // tq
    ncmp = seq_len // CMP_STRIDE
    nsel = seq_len // SEL_BLOCK
    ovt3 = _overlap_matrix_t3(ncmp, nsel)
    qmap = lambda b, g, i: (b * nq + i, g)
    kmap = lambda b, g, i: (g * batch + b, 0)
    n_classes = math.gcd(nq, 4)
    return pl.pallas_call(
        functools.partial(_nsa_cmp_kernel, tq=tq, n_classes=n_classes),
        out_shape=(jax.ShapeDtypeStruct((m, d), F32),
                   jax.ShapeDtypeStruct((NSA_GROUPS, nsel, m), BF16)),
        grid=(batch, NSA_GROUPS, nq),
        in_specs=[pl.BlockSpec((tq, NSA_KV), qmap),
                  pl.BlockSpec((ncmp, NSA_HEAD_DIM), kmap), pl.BlockSpec((ncmp, NSA_HEAD_DIM), kmap),
                  pl.BlockSpec((None, NSA_GATE_ROWS, tq), lambda b, g, i: (g, 0, b * nq + i)),
                  _const_spec((nsel, 3 * ncmp))],
        out_specs=(pl.BlockSpec((tq, NSA_KV), qmap),
                   pl.BlockSpec((None, nsel, tq), lambda b, g, i: (g, 0, b * nq + i))),
        scratch_shapes=[pltpu.VMEM((nsel, tq), F32)],
        compiler_params=_cparams(("parallel", "parallel", "parallel")),
        name="nsa_cmp_topk",
    )(qp, kc, vc, gates, ovt3)


def _nsa_sel_kernel(q_ref, ks_ref, vs_ref, kw_ref, vw_ref, sel_ref, gate_ref, oc_ref, o_ref,
                    m_scr, acc_scr, ow_scr, sta_scr, stb_scr, *, tq, tk):
    qi = pl.program_id(2)
    dh = NSA_HEAD_DIM
    cols = NSA_REP * tq
    q = q_ref[...]
    qs = jnp.concatenate([q[:, r * dh:(r + 1) * dh] for r in range(NSA_REP)], axis=0)
    selt = sel_ref[...].astype(F32)
    nsel = selt.shape[0]
    bmask_t = jnp.where(selt > 0.5, 0.0, MASKED)
    if nsel < LANES:
        bmask_t = jnp.concatenate([bmask_t, jnp.zeros((LANES - nsel, tq), F32)], axis=0)
    bmask = bmask_t.T.astype(BF16)
    zpad = jnp.zeros((tq, LANES - dh), BF16)
    qa = jnp.concatenate([jnp.concatenate([q[:, r * dh:(r + 1) * dh], zpad, bmask], axis=1)
                          for r in range(NSA_REP)], axis=0)
    q0 = qi * tq
    key_iota = lax.broadcasted_iota(jnp.int32, (tk, tq), 0)
    t_pos = q0 + lax.broadcasted_iota(jnp.int32, (tk, tq), 1)

    m_scr[...] = jnp.full((1, cols), -jnp.inf, F32)
    acc_scr[...] = jnp.zeros((NSA_VT_ROWS, cols), F32)

    heads = range(NSA_REP)
    hsl = [slice(r * tq, (r + 1) * tq) for r in heads]

    def put_scores(scr, ki):
        k = ks_ref[pl.ds(pl.multiple_of(ki * tk, tk), tk), :]
        for r in heads:
            scr[r] = _dot_nt(k, qa[hsl[r]])

    def get_scores(scr):
        return [scr[r] for r in heads]

    def consume(ki, st, causal):
        k0 = pl.multiple_of(ki * tk, tk)
        vt = vs_ref[:, pl.ds(k0, tk)]
        if causal:
            neg = jnp.where(k0 + key_iota <= t_pos, 0.0, -jnp.inf)
            st = [s + neg for s in st]
        m_old = [m_scr[:, hsl[r]] for r in heads]
        m_new = [jnp.maximum(m_old[r], jnp.max(st[r], axis=0, keepdims=True)) for r in heads]
        alpha = [jnp.exp2(m_old[r] - m_new[r]) for r in heads]
        p = [jnp.exp2(st[r] - m_new[r]).astype(BF16) for r in heads]
        pv = [_dot(vt, p[r]) for r in heads]
        for r in heads:
            acc_scr[:, hsl[r]] = alpha[r] * acc_scr[:, hsl[r]] + pv[r]
            m_scr[:, hsl[r]] = m_new[r]

    def tile_pair(j, carry):
        even = get_scores(sta_scr)
        put_scores(stb_scr, 2 * j + 1)
        consume(2 * j, even, False)
        odd = get_scores(stb_scr)
        put_scores(sta_scr, 2 * j + 2)
        consume(2 * j + 1, odd, False)
        return carry

    last = (q0 + tq - 1) // tk

    gate = gate_ref[...]
    wq = min(tq, NSA_WINDOW_QUERIES)
    wsub = wq + WINDOW
    nsub = tq // wq
    rel_iota = (lax.broadcasted_iota(jnp.int32, (wsub, wq), 0)
                - lax.broadcasted_iota(jnp.int32, (wsub, wq), 1))
    starts = [pl.multiple_of(jnp.maximum(q0 + wq * u - WINDOW, 0), wq) for u in range(nsub)]
    kw = [kw_ref[pl.ds(starts[u], wsub), :] for u in range(nsub)]
    vwt = [vw_ref[:, pl.ds(starts[u], wsub)] for u in range(nsub)]
    rel = [starts[u] - (q0 + wq * u) + rel_iota for u in range(nsub)]
    neg_w = [jnp.where((rel[u] <= 0) & (rel[u] > -WINDOW), 0.0, -jnp.inf) for u in range(nsub)]
    subs = [(r, u) for r in heads for u in range(nsub)]
    sw = [_dot_nt(kw[u], qs[r * tq + wq * u:r * tq + wq * (u + 1)]) + neg_w[u] for r, u in subs]
    put_scores(sta_scr, 0)
    mw = [jnp.max(s, axis=0, keepdims=True) for s in sw]
    mw = [jnp.where(m == -jnp.inf, 0.0, m) for m in mw]
    ew = [jnp.exp2(s - m).astype(BF16) for s, m in zip(sw, mw)]
    pvw = [_dot(vwt[u], e) for (r, u), e in zip(subs, ew)]
    pvw = [p[0:dh] / jnp.maximum(p[dh:dh + 1], 1e-30) for p in pvw]
    for r in heads:
        o_win = jnp.concatenate(pvw[r * nsub:(r + 1) * nsub], axis=1)
        ow_scr[:, hsl[r]] = o_win * gate[3 * r + 2:3 * r + 3, :]

    n_pairs = last // 2

    def tile_quad(jq, carry):
        tile_pair(2 * jq, carry)
        tile_pair(2 * jq + 1, carry)
        return carry

    lax.fori_loop(0, n_pairs // 2, tile_quad, 0)

    @pl.when(n_pairs % 2 == 1)
    def _():
        tile_pair(n_pairs - 1, 0)

    @pl.when(last % 2 == 0)
    def _():
        consume(last, get_scores(sta_scr), True)

    @pl.when(last % 2 == 1)
    def _():
        even = get_scores(sta_scr)
        put_scores(stb_scr, last)
        consume(last - 1, even, False)
        consume(last, get_scores(stb_scr), True)
    o_sel = acc_scr[0:dh, :] / jnp.maximum(acc_scr[dh:dh + 1, :], 1e-30)

    mix = [o_sel[:, hsl[r]] * gate[3 * r + 1:3 * r + 2, :] + ow_scr[:, hsl[r]] for r in heads]
    for j in range(NSA_REP // 2):
        psl = slice(2 * j * dh, 2 * (j + 1) * dh)
        pair = jnp.concatenate([mix[2 * j], mix[2 * j + 1]], axis=0).T
        o_ref[:, psl] = (oc_ref[:, psl] + pair).astype(o_ref.dtype)


def _nsa_sel(qr, ks, vs, kw, vw, sel, gates, oc, batch, seq_len, tq=512, tk=512):
    m, d = qr.shape
    nq = seq_len // tq
    nsel = seq_len // SEL_BLOCK
    qmap = lambda b, g, i: (b * nq + i, g)
    kvmap = lambda b, g, i: (g, b, 0)
    kv_spec = pl.BlockSpec((None, seq_len, NSA_HEAD_DIM), kvmap)
    ks_spec = pl.BlockSpec((None, seq_len, 2 * LANES), kvmap)
    vt_spec = pl.BlockSpec((None, NSA_VT_ROWS, seq_len), lambda b, g, i: (g, 0, b))
    cols = NSA_REP * tq
    assert tk % tq == 0
    return pl.pallas_call(
        functools.partial(_nsa_sel_kernel, tq=tq, tk=tk),
        out_shape=jax.ShapeDtypeStruct((m, d), BF16),
        grid=(batch, NSA_GROUPS, nq),
        in_specs=[pl.BlockSpec((tq, NSA_KV), qmap), ks_spec, vt_spec, kv_spec, vt_spec,
                  pl.BlockSpec((None, nsel, tq), lambda b, g, i: (g, 0, b * nq + i)),
                  pl.BlockSpec((None, NSA_GATE_ROWS, tq), lambda b, g, i: (g, 0, b * nq + i)),
                  pl.BlockSpec((tq, NSA_KV), qmap)],
        out_specs=pl.BlockSpec((tq, NSA_KV), qmap),
        scratch_shapes=[pltpu.VMEM((1, cols), F32), pltpu.VMEM((NSA_VT_ROWS, cols), F32),
                        pltpu.VMEM((NSA_HEAD_DIM, cols), F32),
                        pltpu.VMEM((NSA_REP, tk, tq), F32), pltpu.VMEM((NSA_REP, tk, tq), F32)],
        compiler_params=_cparams(("parallel", "parallel", "parallel")),
        name="nsa_sel_win",
    )(qr, ks, vs, kw, vw, sel, gates, oc)


def _nsa_layer(x, norm_g, w_in, pe_k, pe_v, w1k, w2k, w1v, w2v, b_gate, rope, batch, seq_len):
    qp, qr, kc_raw, vc_raw, ks, vs, kw, vw, gates = _nsa_proj(x, norm_g, w_in, b_gate, rope, seq_len)
    kc, vc = _compress(kc_raw, vc_raw, pe_k, pe_v, w1k, w2k, w1v, w2v, seq_len)
    oc, sel = _nsa_cmp(qp, kc, vc, gates, batch, seq_len)
    return _nsa_sel(qr, ks, vs, kw, vw, sel, gates, oc, batch, seq_len)


def _mlstm_proj_kernel(x_ref, g_ref, w_ref, wot_ref, wt_ref, bcol_ref, brow_ref,
                       qk_ref, v_ref, ot_ref, gc_ref, gr_ref):
    hn = _rms(x_ref[...], g_ref[...]).astype(BF16)
    y = _dot(hn, w_ref[...])
    d = D_MODEL
    qk_ref[...] = y[:, :d]
    v_ref[...] = y[:, d:2 * d].astype(BF16)
    ot_ref[...] = _sigmoid(_dot_nt(wot_ref[...], hn))
    gc_ref[...] = y[:, 2 * d:] + bcol_ref[...]
    gr_ref[...] = _dot_nt(wt_ref[...], hn) + brow_ref[...]


def _mlstm_proj(x, g, w_in, b_gates, tm=512):
    m, d = x.shape
    h, dk = MLSTM_HEADS, MLSTM_QK_DIM
    wq = w_in[:, :h * dk].reshape(d, h, dk)
    wk = w_in[:, h * dk:2 * h * dk].reshape(d, h, dk)
    wqk = jnp.concatenate([wq, wk], axis=2).reshape(d, 2 * h * dk)
    wv = w_in[:, d:2 * d]
    wif = w_in[:, 2 * d:2 * d + 2 * h]
    wo = w_in[:, 2 * d + 2 * h:]
    w = jnp.concatenate([wqk, wv, jnp.pad(wif, ((0, 0), (0, LANES - 2 * h)))], axis=1).astype(BF16)
    wot = wo.T.astype(BF16)
    wt = wif.T.astype(BF16)
    bcol = jnp.pad(b_gates, (0, LANES - 2 * h)).reshape(1, LANES)
    brow = b_gates.reshape(2 * h, 1)
    n = w.shape[1]
    row = lambda i: (i, 0)
    col = lambda i: (0, i)
    return pl.pallas_call(
        _mlstm_proj_kernel,
        out_shape=(jax.ShapeDtypeStruct((m, d), F32), jax.ShapeDtypeStruct((m, d), BF16),
                   jax.ShapeDtypeStruct((d, m), F32), jax.ShapeDtypeStruct((m, LANES), F32),
                   jax.ShapeDtypeStruct((2 * h, m), F32)),
        grid=(m // tm,),
        in_specs=[pl.BlockSpec((tm, d), row), _const_spec((1, d)), _const_spec((d, n)), _const_spec((d, d)),
                  _const_spec((2 * h, d)), _const_spec((1, LANES)), _const_spec((2 * h, 1))],
        out_specs=(pl.BlockSpec((tm, d), row), pl.BlockSpec((tm, d), row), pl.BlockSpec((d, tm), col),
                   pl.BlockSpec((tm, LANES), row), pl.BlockSpec((2 * h, tm), col)),
        compiler_params=_cparams(("parallel",)),
        name="mlstm_proj",
    )(x, g.reshape(1, d), w, wot, wt, bcol, brow)


def _log_sigmoid(x):
    return jnp.minimum(x, 0.0) - jnp.log(1.0 + jnp.exp(-jnp.abs(x)))


def _mlstm_core_kernel(qk_ref, v_ref, og_ref, gc_ref, gr_ref, cw_ref, cb_ref, ng_ref, o_ref,
                       qkc_scr, ext_scr, c_scr, n_scr, m_scr, *, tt):
    L = MLSTM_TILE
    H, dk, dv = MLSTM_HEADS, MLSTM_QK_DIM, MLSTM_V_DIM
    seq_start = pl.program_id(1) == 0

    @pl.when(seq_start)
    def _():
        c_scr[...] = jnp.zeros_like(c_scr)
        n_scr[...] = jnp.zeros_like(n_scr)
        m_scr[...] = jnp.zeros_like(m_scr)
        ext_scr[0:SUBLANES, :] = jnp.zeros((SUBLANES, ext_scr.shape[1]), F32)

    ext_scr[SUBLANES:, :] = qk_ref[...]
    is_k = lax.broadcasted_iota(jnp.int32, (tt, 2 * dk), 1) >= dk
    for h in range(H):
        hsl = slice(h * 2 * dk, (h + 1) * 2 * dk)
        cw = cw_ref[:, hsl]
        acc = cb_ref[:, hsl] + cw[MLSTM_CONV - 1:MLSTM_CONV, :] * ext_scr[SUBLANES:, hsl]
        for s in range(1, MLSTM_CONV):
            acc = acc + cw[MLSTM_CONV - 1 - s:MLSTM_CONV - s, :] * ext_scr[SUBLANES - s:SUBLANES - s + tt, hsl]
        act = acc * _sigmoid(acc)
        qkc_scr[:, hsl] = jnp.where(is_k, act * dk ** -0.5, act)
    ext_scr[0:SUBLANES, :] = ext_scr[tt:tt + SUBLANES, :]

    src = lax.broadcasted_iota(jnp.int32, (L, L), 0)
    tgt = lax.broadcasted_iota(jnp.int32, (L, L), 1)
    causal = src <= tgt
    tri3 = jnp.concatenate([(src >= tgt).astype(BF16)] * 3, axis=1)
    triu3 = jnp.concatenate([causal.astype(BF16)] * 3, axis=0)
    n_pad = jnp.zeros((SUBLANES - 3, dk), BF16)

    def chunk(c, carry):
        r0 = pl.multiple_of(c * L, L)
        gcol = gc_ref[pl.ds(r0, L), :]
        grow = gr_ref[c]
        b_col = _cumsum_rows(tri3, _log_sigmoid(gcol))
        b_row = _cumsum_lanes(_log_sigmoid(grow), triu3)
        hs = range(H)
        vsl = [slice(h * dv, (h + 1) * dv) for h in hs]
        qk = [qkc_scr[pl.ds(r0, L), h * 2 * dk:(h + 1) * 2 * dk] for h in hs]
        q = [qk[h][:, :dk].astype(BF16) for h in hs]
        k = [qk[h][:, dk:] for h in hs]
        v = [v_ref[pl.ds(r0, L), vsl[h]] for h in hs]
        col = [gcol[:, h:h + 1] - b_col[:, H + h:H + h + 1] for h in hs]
        li_r = [grow[h:h + 1, :] for h in hs]
        b_r = [b_row[H + h:H + h + 1, :] for h in hs]
        b_end = [b_r[h][:, L - 1:L] for h in hs]
        dmat = [jnp.where(causal, b_r[h] + col[h], -jnp.inf) for h in hs]
        d_max = [jnp.max(dmat[h], axis=0, keepdims=True) for h in hs]
        att = [jnp.exp(dmat[h] - d_max[h]) * _dot_nt(k[h].astype(BF16), q[h]) for h in hs]
        a_sum = [jnp.sum(att[h], axis=0, keepdims=True) for h in hs]
        intra = [_dot_tn(v[h], att[h].astype(BF16)) for h in hs]
        g_max = [jnp.max(b_end[h] - b_r[h] + li_r[h], axis=-1, keepdims=True) for h in hs]
        kw = [k[h] * jnp.exp(b_end[h] + col[h] - g_max[h]) for h in hs]
        c_loc = [_dot_tn(v[h], kw[h].astype(BF16)) for h in hs]
        n_loc = [jnp.sum(kw[h], axis=0, keepdims=True) for h in hs]
        m_prev = [m_scr[h:h + 1, 0:1] for h in hs]
        c_prev = [c_scr[h] for h in hs]
        n_prev = [n_scr[h:h + 1, :] for h in hs]
        m_inter = [b_r[h] + m_prev[h] for h in hs]
        m_t = [jnp.maximum(m_inter[h], d_max[h]) for h in hs]
        w_loc = [jnp.exp(d_max[h] - m_t[h]) for h in hs]
        w_int = [jnp.exp(m_inter[h] - m_t[h]) for h in hs]
        q_c = [_dot_nt(c_prev[h].astype(BF16), q[h]) for h in hs]
        q_n = [_dot_nt(jnp.concatenate(list(_split3(n_prev[h])) + [n_pad], axis=0), q[h]) for h in hs]
        q_n = [q_n[h][0:1] + q_n[h][1:2] + q_n[h][2:3] for h in hs]
        num = [w_loc[h] * intra[h] + w_int[h] * q_c[h] for h in hs]
        den = [w_loc[h] * a_sum[h] + w_int[h] * q_n[h] for h in hs]
        h_t = [num[h] / jnp.maximum(jnp.abs(den[h]), jnp.exp(-m_t[h])) for h in hs]
        h_t = [h_t[h] * lax.rsqrt(jnp.mean(h_t[h] * h_t[h], axis=0, keepdims=True) + NORM_EPS) for h in hs]
        m_new = [jnp.maximum(b_end[h] + m_prev[h], g_max[h]) for h in hs]
        a = [jnp.exp(b_end[h] + m_prev[h] - m_new[h]) for h in hs]
        sc = [jnp.exp(g_max[h] - m_new[h]) for h in hs]
        for h in hs:
            out = h_t[h] * ng_ref[vsl[h], :] * og_ref[vsl[h], pl.ds(r0, L)]
            o_ref[pl.ds(r0, L), vsl[h]] = out.T.astype(o_ref.dtype)
            c_scr[h] = a[h] * c_prev[h] + sc[h] * c_loc[h]
            n_scr[h:h + 1, :] = a[h] * n_prev[h] + sc[h] * n_loc[h]
            m_scr[h:h + 1, :] = jnp.broadcast_to(m_new[h], (1, LANES))
        return carry

    lax.fori_loop(0, tt // L, chunk, 0)


def _mlstm_core(qk, v, ogt, gcol, grow3, conv_w, conv_b, norm_gb, batch, seq_len, tt=256):
    m, d = qk.shape
    H, dk, dv = MLSTM_HEADS, MLSTM_QK_DIM, MLSTM_V_DIM
    nt = seq_len // tt
    ncs = tt // MLSTM_TILE
    row = lambda b, i: (b * nt + i, 0)
    return pl.pallas_call(
        functools.partial(_mlstm_core_kernel, tt=tt),
        out_shape=jax.ShapeDtypeStruct((m, d), BF16),
        grid=(batch, nt),
        in_specs=[pl.BlockSpec((tt, d), row), pl.BlockSpec((tt, d), row),
                  pl.BlockSpec((d, tt), lambda b, i: (0, b * nt + i)),
                  pl.BlockSpec((tt, LANES), row),
                  pl.BlockSpec((ncs, 2 * H, MLSTM_TILE), lambda b, i: (b * nt + i, 0, 0)),
                  _const_spec((SUBLANES, d)), _const_spec((1, d)), _const_spec((d, LANES))],
        out_specs=pl.BlockSpec((tt, d), row),
        scratch_shapes=[pltpu.VMEM((tt, d), F32), pltpu.VMEM((tt + SUBLANES, d), F32),
                        pltpu.VMEM((H, dv, dk), F32), pltpu.VMEM((H, dk), F32),
                        pltpu.VMEM((H, LANES), F32)],
        compiler_params=_cparams(("arbitrary", "arbitrary")),
        name="mlstm_core",
    )(qk, v, ogt, gcol, grow3, conv_w, conv_b, norm_gb)


def _mlstm_layer(x, norm_g, w_in, conv_w, conv_b, b_gates, hnorm, batch, seq_len):
    H, dk = MLSTM_HEADS, MLSTM_QK_DIM
    qk, v, ogt, gcol, grow = _mlstm_proj(x, norm_g, w_in, b_gates)
    m = x.shape[0]
    grow3 = grow.reshape(2 * H, m // MLSTM_TILE, MLSTM_TILE).transpose(1, 0, 2)
    norm_gb = jnp.broadcast_to(hnorm.reshape(-1, 1), (hnorm.shape[0], LANES))

    def perm(z):
        lead = z.shape[:-1]
        zq = z[..., :H * dk].reshape(*lead, H, dk)
        zk = z[..., H * dk:].reshape(*lead, H, dk)
        return jnp.concatenate([zq, zk], axis=-1).reshape(*lead, 2 * H * dk)

    cw = jnp.pad(perm(conv_w), ((0, SUBLANES - MLSTM_CONV), (0, 0)))
    return _mlstm_core(qk, v, ogt, gcol, grow3, cw, perm(conv_b).reshape(1, -1), norm_gb,
                       batch, seq_len)


def _softplus(x):
    return jnp.maximum(x, 0.0) + jnp.log(1.0 + jnp.exp(-jnp.abs(x)))


def _rwkv_proj_kernel(x_ref, g_ref, mu_ref, wr_ref, wk_ref, wv_ref, ww1_ref, ww2_ref, w0_ref,
                      aw1_ref, aw2_ref, a0_ref, gw1_ref, gw2_ref,
                      r_ref, k_ref, v_ref, lw_ref, a_ref, go_ref, prev_scr, *, tiles_per_seq):
    tm = x_ref.shape[0]
    h = _rms(x_ref[...], g_ref[...])
    seq_start = (pl.program_id(0) % tiles_per_seq) == 0
    prev = jnp.where(seq_start, 0.0, prev_scr[SUBLANES - 1:SUBLANES, :])
    prev_scr[...] = h[tm - SUBLANES:, :]
    rows = lax.broadcasted_iota(jnp.int32, h.shape, 0)
    xx = jnp.where(rows == 0, prev, pltpu.roll(h, 1, 0)) - h
    mix = lambda j: (h + xx * mu_ref[j:j + 1, :]).astype(BF16)
    r_ref[...] = _dot(mix(0), wr_ref[...])
    k_ref[...] = _dot(mix(2), wk_ref[...])
    v_ref[...] = _dot(mix(3), wv_ref[...])
    wl = _dot(jnp.tanh(_dot(mix(1), ww1_ref[...])).astype(BF16), ww2_ref[...]) + w0_ref[...]
    w_log = -_softplus(-wl) - 0.5
    lw_ref[...] = -jnp.exp(w_log)
    al = _dot(_dot(mix(4), aw1_ref[...]).astype(BF16), aw2_ref[...]) + a0_ref[...]
    a_ref[...] = _sigmoid(al)
    go_ref[...] = _dot(_sigmoid(_dot(mix(5), gw1_ref[...])).astype(BF16), gw2_ref[...])


def _rwkv_proj(x, g, mu, w_r, w_k, w_v, w0, w_w1, w_w2, a0, a_w1, a_w2, g_w1, g_w2, seq_len, tm=512):
    m, d = x.shape
    row = lambda i: (i, 0)
    bf = lambda w: w.astype(BF16)
    mu8 = jnp.pad(mu, ((0, SUBLANES - mu.shape[0]), (0, 0)))
    consts = [g.reshape(1, d), mu8, bf(w_r), bf(w_k), bf(w_v), bf(w_w1), bf(w_w2), w0.reshape(1, d),
              bf(a_w1), bf(a_w2), a0.reshape(1, d), bf(g_w1), bf(g_w2)]
    out = jax.ShapeDtypeStruct((m, d), F32)
    return pl.pallas_call(
        functools.partial(_rwkv_proj_kernel, tiles_per_seq=seq_len // tm),
        out_shape=(out,) * 6,
        grid=(m // tm,),
        in_specs=[pl.BlockSpec((tm, d), row)] + [_const_spec(c.shape) for c in consts],
        out_specs=(pl.BlockSpec((tm, d), row),) * 6,
        scratch_shapes=[pltpu.VMEM((SUBLANES, d), F32)],
        compiler_params=_cparams(("arbitrary",)),
        name="rwkv_proj",
    )(x, *consts)


def _block_diag(x, lo):
    zero = jnp.zeros_like(x)
    return jnp.concatenate([jnp.where(lo, x, zero), jnp.where(lo, zero, x)], axis=0)


class _PairMat:
    def __init__(self, x, lo):
        self.x, self.lo = x, lo
        self._lhs = self._rhs = None

    def lhs(self):
        if self._lhs is None:
            hi, lo = _split2(self.x)
            self._lhs = jnp.concatenate([hi, lo, hi], axis=1)
        return self._lhs

    def rhs(self):
        if self._rhs is None:
            hi, lo = _split2(self.x)
            bh = _block_diag(hi, self.lo)
            self._rhs = jnp.concatenate([bh, bh, _block_diag(lo, self.lo)], axis=0)
        return self._rhs


def _pair_mm(p, q):
    return _dot(p.lhs(), q.rhs())


def _rwkv_core_kernel(r_ref, k_ref, v_ref, lw_ref, a_ref, go_ref, kk_ref, ka_ref, rk_ref,
                      lnw_ref, lnb_ref, o_ref, z_scr, *, tt):
    L, N = RWKV_CHUNK, RWKV_HEAD_DIM
    npair = z_scr.shape[0]
    pairs = range(npair)

    @pl.when(pl.program_id(1) == 0)
    def _():
        z_scr[...] = jnp.zeros_like(z_scr)

    ri = lax.broadcasted_iota(jnp.int32, (L, LANES), 0)
    ln = lax.broadcasted_iota(jnp.int32, (L, LANES), 1)
    si = ln % N
    lo = ln < N
    lower_incl = ri >= si
    lower_strict = ri > si
    blk_diag = (ri // RWKV_INV_BLOCK) == (si // RWKV_INV_BLOCK)
    eye = (ri == si).astype(F32)
    tri = (lax.broadcasted_iota(jnp.int32, (L, L), 0) >= lax.broadcasted_iota(jnp.int32, (L, L), 1))
    tri3 = jnp.concatenate([tri.astype(BF16)] * 3, axis=1)
    z_mask = ((lax.broadcasted_iota(jnp.int32, (2 * N, LANES), 0) // N)
              == (lax.broadcasted_iota(jnp.int32, (2 * N, LANES), 1) // N))

    def half_sum(x):
        s0 = jnp.sum(jnp.where(lo, x, 0.0), axis=-1, keepdims=True)
        s1 = jnp.sum(jnp.where(lo, 0.0, x), axis=-1, keepdims=True)
        return jnp.where(lo, s0, s1)

    bd = lambda x: _block_diag(x, lo)
    mk = lambda xs: [_PairMat(x, lo) for x in xs]
    mm = lambda ps, qs: [_pair_mm(p, q) for p, q in zip(ps, qs)]

    def prep(grp, r0):
        idx = range(len(grp))
        sl = [slice(p * LANES, (p + 1) * LANES) for p in grp]
        ld = lambda ref: [ref[pl.ds(r0, L), s] for s in sl]
        r, k, v, lw, a = ld(r_ref), ld(k_ref), ld(v_ref), ld(lw_ref), ld(a_ref)
        kk = [k[i] * kk_ref[:, sl[i]] for i in idx]
        kk = [kk[i] / jnp.maximum(jnp.sqrt(half_sum(kk[i] * kk[i])), 1e-12) for i in idx]
        km = [k[i] * (1.0 + (a[i] - 1.0) * ka_ref[:, sl[i]]) for i in idx]
        bv = [kk[i] * a[i] for i in idx]
        cum = [_cumsum_rows(tri3, lw[i]) for i in idx]
        cum_end = [cum[i][L - 1:L, :] for i in idx]
        w_inv = [jnp.exp(-cum[i]) for i in idx]
        w_out = [jnp.exp(cum_end[i] - cum[i]) for i in idx]
        kk_h = [(kk[i] * jnp.exp(cum[i] - lw[i])).astype(BF16) for i in idx]
        r_h = [(r[i] * jnp.exp(cum[i])).astype(BF16) for i in idx]
        b_t = [(bv[i] * w_inv[i]).astype(BF16) for i in idx]
        k_t = [(km[i] * w_inv[i]).astype(BF16) for i in idx]
        bbar = [(bv[i] * w_out[i]).astype(BF16) for i in idx]
        kbar = [(km[i] * w_out[i]).astype(BF16) for i in idx]
        vb = [v[i].astype(BF16) for i in idx]
        lhs = [jnp.concatenate([kk_h[i], r_h[i]], axis=0) for i in idx]
        ab = [_dot_nt(lhs[i], bd(b_t[i])) for i in idx]
        ak = [_dot_nt(lhs[i], bd(k_t[i])) for i in idx]
        return dict(
            sl=sl, r=r, v=v, km=km, kk_h=kk_h, r_h=r_h, bbar=bbar, kbar=kbar, vb=vb, cum_end=cum_end,
            a_ub=[jnp.where(lower_strict, ab[i][:L], 0.0) for i in idx],
            a_rb=[jnp.where(lower_incl, ab[i][L:], 0.0).astype(BF16) for i in idx],
            a_uk=[jnp.where(lower_strict, ak[i][:L], 0.0).astype(BF16) for i in idx],
            a_rk=[jnp.where(lower_incl, ak[i][L:], 0.0).astype(BF16) for i in idx])

    def inverse(s):
        a_ub = s["a_ub"]
        idx = range(len(a_ub))
        n1 = mk([jnp.where(blk_diag, -a_ub[i], 0.0) for i in idx])
        n2 = mk(mm(n1, n1))
        n4 = mk(mm(n2, n2))
        n8 = mk(mm(n4, n4))
        acc = mk([eye + n1[i].x for i in idx])
        for nk in (n2, n4):
            prod = mm(acc, nk)
            acc = mk([acc[i].x + prod[i] for i in idx])
        prod = mm(acc, n8)
        d_inv = mk([acc[i].x + prod[i] for i in idx])
        l_off = mk([jnp.where(blk_diag, 0.0, a_ub[i]) for i in idx])
        e1 = mk([-x for x in mm(d_inv, l_off)])
        e2 = mk(mm(e1, e1))
        qm = mk([eye + e1[i].x for i in idx])
        prod = mm(qm, e2)
        qm = mk([qm[i].x + prod[i] for i in idx])
        s["t_inv"] = mk(mm(qm, d_inv))

    def state(grp, s):
        idx = range(len(grp))
        z = [z_scr[p] for p in grp]
        zb = [z[i].astype(BF16) for i in idx]
        vb = s["vb"]
        bdv = [bd(vb[i]) for i in idx]
        rhs_u = mk([_dot_nt(s["kk_h"][i], zb[i]) + _dot(s["a_uk"][i], bdv[i]) for i in idx])
        u = [-x for x in mm(s["t_inv"], rhs_u)]
        ub = [u[i].astype(BF16) for i in idx]
        s["y"] = [_dot_nt(s["r_h"][i], zb[i])
                  + _dot(jnp.concatenate([s["a_rb"][i], s["a_rk"][i]], axis=1),
                         jnp.concatenate([bd(ub[i]), bdv[i]], axis=0)) for i in idx]
        upd = [_dot_tn(jnp.concatenate([ub[i], vb[i]], axis=0),
                       jnp.concatenate([s["bbar"][i], s["kbar"][i]], axis=0)) for i in idx]
        for i, p in enumerate(grp):
            z_scr[p] = z[i] * jnp.exp(s["cum_end"][i]) + jnp.where(z_mask, upd[i], 0.0)

    def post(s, r0):
        y, sl, r, v, km = s["y"], s["sl"], s["r"], s["v"], s["km"]
        idx = range(len(y))
        inv_n = 1.0 / N
        mean = [half_sum(y[i]) * inv_n for i in idx]
        yc = [y[i] - mean[i] for i in idx]
        var = [half_sum(yc[i] * yc[i]) * inv_n for i in idx]
        bonus = [half_sum(r[i] * km[i] * rk_ref[:, sl[i]]) * v[i] for i in idx]
        for i in idx:
            yn = yc[i] * lax.rsqrt(var[i] + RWKV_GN_EPS) * lnw_ref[:, sl[i]] + lnb_ref[:, sl[i]]
            o_ref[pl.ds(r0, L), sl[i]] = ((yn + bonus[i]) * go_ref[pl.ds(r0, L), sl[i]]).astype(o_ref.dtype)

    group = list(pairs)

    n_chunks = tt // L
    cur = prep(group, 0)
    inverse(cur)
    for c in range(1, n_chunks):
        nxt = prep(group, c * L)
        state(group, cur)
        inverse(nxt)
        post(cur, (c - 1) * L)
        cur = nxt
    state(group, cur)
    post(cur, (n_chunks - 1) * L)


def _rwkv_core(r, k, v, lw, a, go, k_k, k_a, r_k, ln_w, ln_b, batch, seq_len, tt=256):
    m, d = r.shape
    nt = seq_len // tt
    npair = d // LANES
    blk = pl.BlockSpec((tt, d), lambda b, i: (b * nt + i, 0))
    par = _const_spec((1, d))
    row1 = lambda z: z.reshape(1, d)
    return pl.pallas_call(
        functools.partial(_rwkv_core_kernel, tt=tt),
        out_shape=jax.ShapeDtypeStruct((m, d), BF16),
        grid=(batch, nt),
        in_specs=[blk] * 6 + [par] * 5,
        out_specs=blk,
        scratch_shapes=[pltpu.VMEM((npair, 2 * RWKV_HEAD_DIM, LANES), F32)],
        compiler_params=_cparams(("arbitrary", "arbitrary")),
        name="rwkv_core",
    )(r, k, v, lw, a, go, row1(k_k), row1(k_a), row1(r_k), row1(ln_w), row1(ln_b))


def _rwkv_layer(x, norm_g, mu, w_r, w_k, w_v, w0, w_w1, w_w2, a0, a_w1, a_w2, g_w1, g_w2,
                k_k, k_a, r_k, ln_w, ln_b, batch, seq_len):
    r, k, v, lw, a, go = _rwkv_proj(x, norm_g, mu, w_r, w_k, w_v, w0, w_w1, w_w2, a0, a_w1, a_w2,
                                    g_w1, g_w2, seq_len)
    return _rwkv_core(r, k, v, lw, a, go, k_k, k_a, r_k, ln_w, ln_b, batch, seq_len)


def kernel(x, norm_mixer, norm_ffn, ffn_w_up, ffn_conv_w, ffn_conv_b, ffn_w_down, nsa_w_in, nsa_pe_k, nsa_pe_v, nsa_cmp_k_w1, nsa_cmp_k_w2, nsa_cmp_v_w1, nsa_cmp_v_w2, nsa_b_gate, nsa_w_out, mlstm_w_in, mlstm_conv_w, mlstm_conv_b, mlstm_b_gates, mlstm_norm, mlstm_w_out, rwkv_mu, rwkv_w_r, rwkv_w_k, rwkv_w_v, rwkv_w_o, rwkv_w0, rwkv_w_w1, rwkv_w_w2, rwkv_a0, rwkv_a_w1, rwkv_a_w2, rwkv_g_w1, rwkv_g_w2, rwkv_k_k, rwkv_k_a, rwkv_r_k, rwkv_ln_w, rwkv_ln_b, final_norm):
    batch, seq_len, d = x.shape
    depth = norm_mixer.shape[0]
    rope = _rope_tables(seq_len)
    xf = x.reshape(batch * seq_len, d)
    for i in range(depth):
        kind, j = i % 3, i // 3
        if kind == 0:
            w_o = nsa_w_out[j]
            o = _nsa_layer(xf, norm_mixer[i], nsa_w_in[j], nsa_pe_k[j], nsa_pe_v[j], nsa_cmp_k_w1[j],
                           nsa_cmp_k_w2[j], nsa_cmp_v_w1[j], nsa_cmp_v_w2[j], nsa_b_gate[j], rope, batch, seq_len)
        elif kind == 1:
            w_o = mlstm_w_out[j]
            o = _mlstm_layer(xf, norm_mixer[i], mlstm_w_in[j], mlstm_conv_w[j], mlstm_conv_b[j],
                             mlstm_b_gates[j], mlstm_norm[j], batch, seq_len)
        else:
            w_o = rwkv_w_o[j]
            o = _rwkv_layer(xf, norm_mixer[i], rwkv_mu[j], rwkv_w_r[j], rwkv_w_k[j], rwkv_w_v[j],
                            rwkv_w0[j], rwkv_w_w1[j], rwkv_w_w2[j], rwkv_a0[j],
                            rwkv_a_w1[j], rwkv_a_w2[j], rwkv_g_w1[j], rwkv_g_w2[j], rwkv_k_k[j],
                            rwkv_k_a[j], rwkv_r_k[j], rwkv_ln_w[j], rwkv_ln_b[j], batch, seq_len)
        xf = _ffn(xf, o, w_o, norm_ffn[i], ffn_w_up[i], ffn_conv_w[i], ffn_conv_b[i], ffn_w_down[i],
                  seq_len, final_g=final_norm if i == depth - 1 else None)
    return xf.reshape(batch, seq_len, d)
```
